```python
import math
import jax
import jax.numpy as jnp
from jax import lax
import numpy as np

D_MODEL = 2048
BATCH = 4
SEQ = 2048
DEPTH = 2
DEC_BATCH = 32
DEC_SEQ = 4
PAST_LEN = 8192
PAGE_SIZE = 128

HEAD_DIM = 128
ROT_DIV = 4
ROPE_THETA = 500000.0
NORM_EPS = 1e-6

NSA_HEADS = 8
NSA_KV = 2
CMP_STRIDE = 16
CMP_LEN = 2 * CMP_STRIDE
CMP_HIDDEN = HEAD_DIM
SLC_BLOCK = 64
N_SEL = 8
N_LOCAL = 2
WINDOW = 512

DSA_HEADS = 8
DSA_KV = 2
IDX_HEADS = 8
IDX_DIM = 64
TOPK_MAX = 256

RWKV_HEADS = 16
RWKV_HEAD = 64
RWKV_DIM = RWKV_HEADS * RWKV_HEAD
DECAY_LORA = 64
AAA_LORA = 64
GATE_LORA = 160
RWKV_GN_EPS = 64e-5

D_FF = ((8 * D_MODEL + 3 * 256 - 1) // (3 * 256)) * 256
Q_BLOCK = 128
NEG = -1e30
FORCE = 1e9

IN_SPLITS = (
    NSA_HEADS * HEAD_DIM,
    6 * NSA_KV * HEAD_DIM,
    3 * NSA_HEADS,
    DSA_HEADS * HEAD_DIM,
    2 * DSA_KV * HEAD_DIM,
    IDX_HEADS * IDX_DIM,
    IDX_DIM,
    IDX_HEADS,
    3 * RWKV_DIM + DECAY_LORA + AAA_LORA + GATE_LORA,
    3 * D_MODEL,
)
D_IN = sum(IN_SPLITS)

kernel_name = 'nsa_dsa_rwkv7_parallel_hybrid_step'


def split_points(sizes):
    pts, acc = [], 0
    for s in sizes[:-1]:
        acc += s
        pts.append(acc)
    return pts


def rms_norm(x, g):
    xf = x.astype(jnp.float32)
    y = xf * lax.rsqrt(jnp.mean(xf * xf, axis=-1, keepdims=True) + NORM_EPS)
    return (y * g.astype(jnp.float32)).astype(x.dtype)


def rope_partial(x, pos):
    rot = x.shape[-1] // ROT_DIV
    half = rot // 2
    freqs = ROPE_THETA ** (-jnp.arange(half, dtype=jnp.float32) / half)
    ang = pos.astype(jnp.float32)[:, None] * freqs[None, :]
    cos, sin = jnp.cos(ang)[:, None, :], jnp.sin(ang)[:, None, :]
    xr = x[..., :rot].astype(jnp.float32)
    x1, x2 = xr[..., :half], xr[..., half:]
    xr = jnp.concatenate([x1 * cos - x2 * sin, x1 * sin + x2 * cos], axis=-1)
    return jnp.concatenate([xr.astype(x.dtype), x[..., rot:]], axis=-1)


def masked_softmax(s, mask):
    s = jnp.where(mask, s.astype(jnp.float32), NEG)
    pr = jax.nn.softmax(s, axis=-1)
    return jnp.where(jnp.any(mask, axis=-1, keepdims=True), pr, 0.0)


def gather_pages(pool, layer, page_table):
    rows = pool[layer, page_table]
    return rows.reshape((page_table.shape[0], -1) + pool.shape[3:])


def sweep_query_blocks(fn, n_q):
    qb = math.gcd(Q_BLOCK, n_q)
    nb = n_q // qb
    out = lax.map(lambda i: fn(i * qb, qb), jnp.arange(nb, dtype=jnp.int32))
    out = jnp.moveaxis(out, 0, 1)
    return out.reshape((out.shape[0], n_q) + out.shape[3:])


def compress_blocks(rows, w1, pos_emb, w2, b2):
    b, L, G, d = rows.shape
    n_sub = L // CMP_STRIDE
    sub = rows[:, :n_sub * CMP_STRIDE].reshape(b, n_sub, CMP_STRIDE, G, d)
    h_lo = jnp.einsum('bnsgd,sde->bnge', sub, w1[:CMP_STRIDE])
    h_hi = jnp.einsum('bnsgd,sde->bnge', sub, w1[CMP_STRIDE:])
    h = h_lo[:, :-1] + h_hi[:, 1:] + jnp.einsum('ld,lde->e', pos_emb, w1)
    return jax.nn.gelu(h) @ w2 + b2


def cmp_to_slc_cover(n_c, n_slc):
    start = jnp.arange(n_c)[:, None] * CMP_STRIDE
    blk = jnp.arange(n_slc)[None, :] * SLC_BLOCK
    return ((start < blk + SLC_BLOCK) & (start + CMP_LEN > blk)).astype(jnp.float32)


def window_attention(q, kv, q0, k0):
    b, T = q.shape[:2]
    qb = math.gcd(Q_BLOCK, T)
    nb = T // qb
    kv_pad = jnp.pad(kv, ((0, 0), (WINDOW, 0), (0, 0), (0, 0), (0, 0)))
    rows = (q0 - k0) + jnp.arange(nb)[:, None] * qb + jnp.arange(WINDOW + qb)[None, :]
    band = kv_pad[:, rows]
    kpos = rows + (k0 - WINDOW)
    t = q0 + jnp.arange(T).reshape(nb, qb)
    m = ((rows[:, None, :] >= WINDOW) & (kpos[:, None, :] <= t[:, :, None])
         & (t[:, :, None] - kpos[:, None, :] < WINDOW))
    qr = q.reshape((b, nb, qb) + q.shape[2:])
    s = jnp.einsum('bnqghd,bnkgd->bnqghk', qr, band[:, :, :, 0]) * HEAD_DIM ** -0.5
    pr = masked_softmax(s, m[None, :, :, None, None, :])
    o = jnp.einsum('bnqghk,bnkgd->bnqghd', pr.astype(band.dtype), band[:, :, :, 1])
    return o.reshape(q.shape)


def nsa_mixer(q, cmp_kv, slc_kv, win_kv, win_pos0, gate_logits, pos0, q_pos, p):
    b, T = q.shape[:2]
    scale = HEAD_DIM ** -0.5
    kc = compress_blocks(cmp_kv[:, :, 0], p['cmp_w1'][0], p['cmp_pos'][0], p['cmp_w2'][0], p['cmp_b2'][0])
    vc = compress_blocks(cmp_kv[:, :, 1], p['cmp_w1'][1], p['cmp_pos'][1], p['cmp_w2'][1], p['cmp_b2'][1])
    n_c = kc.shape[1]
    c_pos = jnp.arange(n_c, dtype=jnp.int32) * CMP_STRIDE + (CMP_LEN - 1)
    kc = rope_partial(rms_norm(kc, p['a_k_norm'][0]), c_pos)
    p_cmp = masked_softmax(jnp.einsum('btghd,bngd->btghn', q, kc) * scale,
                           (c_pos[None, :] <= q_pos[:, None])[None, :, None, None, :])
    o_cmp = jnp.einsum('btghn,bngd->btghd', p_cmp.astype(vc.dtype), vc)
    L = slc_kv.shape[1]
    n_slc = -(-L // SLC_BLOCK)
    n_sel = min(N_SEL, n_slc)
    imp = jnp.einsum('btgn,nj->btgj', jnp.sum(p_cmp, axis=3), cmp_to_slc_cover(n_c, n_slc))
    blk = jnp.arange(n_slc)
    dist = (q_pos // SLC_BLOCK)[:, None] - blk[None, :]
    forced = (blk[None, :] == 0) | ((dist >= 0) & (dist < N_LOCAL))
    score = jnp.where(forced[None, :, None, :], FORCE, jnp.where((dist < 0)[None, :, None, :], -FORCE, imp))
    _, sel = lax.top_k(score, n_sel)
    blocks = jnp.pad(slc_kv, ((0, 0), (0, n_slc * SLC_BLOCK - L), (0, 0), (0, 0), (0, 0)))
    blocks = blocks.reshape(b, n_slc, SLC_BLOCK, 2, NSA_KV, HEAD_DIM).transpose(0, 4, 1, 2, 3, 5)
    bi = jnp.arange(b)[:, None, None, None]
    gi = jnp.arange(NSA_KV)[None, None, :, None]

    def slc_block(start, qb):
        qc = lax.dynamic_slice_in_dim(q, start, qb, axis=1)
        ic = lax.dynamic_slice_in_dim(sel, start, qb, axis=1)
        tc = lax.dynamic_slice_in_dim(q_pos, start, qb, axis=0)
        kv = blocks[bi, gi, ic].reshape(b, qb, NSA_KV, n_sel * SLC_BLOCK, 2, HEAD_DIM)
        kpos = (ic[..., None] * SLC_BLOCK + jnp.arange(SLC_BLOCK)).reshape(b, qb, NSA_KV, n_sel * SLC_BLOCK)
        pr = masked_softmax(jnp.einsum('bqghd,bqgkd->bqghk', qc, kv[..., 0, :]) * scale,
                            (kpos <= tc[None, :, None, None])[:, :, :, None, :])
        return jnp.einsum('bqghk,bqgkd->bqghd', pr.astype(kv.dtype), kv[..., 1, :])

    o_slc = sweep_query_blocks(slc_block, T)
    o_win = window_attention(q, win_kv, pos0, win_pos0)
    g = jax.nn.sigmoid(gate_logits.reshape(b, T, NSA_KV, NSA_HEADS // NSA_KV, 3))
    o = g[..., 0:1] * o_cmp + g[..., 1:2] * o_slc + g[..., 2:3] * o_win
    return o.reshape(b, T, NSA_HEADS * HEAD_DIM)


def dsa_attention(q, kv, idx_k, idx_q, idx_w, q_pos):
    b, T = q.shape[:2]
    L = kv.shape[1]
    k_top = min(TOPK_MAX, L // 4)
    kpos = jnp.arange(L, dtype=jnp.int32)
    bi = jnp.arange(b)[:, None, None]

    def block(start, qb):
        qc = lax.dynamic_slice_in_dim(q, start, qb, axis=1)
        iqc = lax.dynamic_slice_in_dim(idx_q, start, qb, axis=1)
        iwc = lax.dynamic_slice_in_dim(idx_w, start, qb, axis=1)
        tc = lax.dynamic_slice_in_dim(q_pos, start, qb, axis=0)
        logits = jnp.einsum('bqhd,bkd->bqhk', iqc, idx_k).astype(jnp.float32)
        score = jnp.einsum('bqh,bqhk->bqk', iwc.astype(jnp.float32), jax.nn.relu(logits))
        score = jnp.where(kpos[None, None, :] <= tc[None, :, None], score, NEG)
        _, sel = lax.top_k(score, k_top)
        kvg = kv[bi, sel]
        s = jnp.einsum('bqghd,bqkgd->bqghk', qc, kvg[:, :, :, 0]) * HEAD_DIM ** -0.5
        pr = masked_softmax(s, (sel <= tc[None, :, None])[:, :, None, None, :])
        return jnp.einsum('bqghk,bqkgd->bqghd', pr.astype(kvg.dtype), kvg[:, :, :, 1])

    return sweep_query_blocks(block, T).reshape(b, T, DSA_HEADS * HEAD_DIM)


def rwkv7_step(S, inp):
    r, w, k, v, a, bb = inp
    sa = jnp.einsum('bhvk,bhk->bhv', S, a)
    S = S * w[:, :, None, :] + sa[..., None] * bb[:, :, None, :] + v[..., None] * k[:, :, None, :]
    return S, jnp.einsum('bhvk,bhk->bhv', S, r)


def rwkv7_mixer(cur, prev, state0, p):
    b, T, _ = cur.shape
    xm = cur + (prev - cur) * p['rwkv_mu']
    r, k, v, w_lo, a_lo, g_lo = jnp.split(xm, [RWKV_DIM, 2 * RWKV_DIM, 3 * RWKV_DIM, 3 * RWKV_DIM + DECAY_LORA,
                                             3 * RWKV_DIM + DECAY_LORA + AAA_LORA], axis=-1)
    w = -jax.nn.softplus(-(p['rwkv_w0'] + jnp.tanh(w_lo) @ p['rwkv_w2'])) - 0.5
    a = jax.nn.sigmoid(p['rwkv_a0'] + a_lo @ p['rwkv_a2'])
    g = jax.nn.sigmoid(g_lo) @ p['rwkv_g2']
    heads = lambda t: t.reshape(b, T, RWKV_HEADS, RWKV_HEAD).astype(jnp.float32)
    kk = heads(k * p['rwkv_k_k'])
    kk = kk * lax.rsqrt(jnp.sum(kk * kk, axis=-1, keepdims=True) + 1e-12)
    kh = heads(k * (1.0 + (a - 1.0) * p['rwkv_k_a']))
    rh, vh, ah = heads(r), heads(v), heads(a)
    decay = jnp.exp(-jnp.exp(heads(w)))
    tm = lambda t: jnp.moveaxis(t, 1, 0)
    S_T, y = lax.scan(rwkv7_step, state0.astype(jnp.float32),
                      (tm(rh), tm(decay), tm(kh), tm(vh), tm(-kk), tm(kk * ah)))
    y = jnp.moveaxis(y, 0, 1)
    mu = jnp.mean(y, axis=-1, keepdims=True)
    var = jnp.mean(jnp.square(y - mu), axis=-1, keepdims=True)
    y = ((y - mu) * lax.rsqrt(var + RWKV_GN_EPS)).reshape(b, T, RWKV_DIM) * p['rwkv_ln_w'] + p['rwkv_ln_b']
    y = y + (jnp.sum(rh * kh * p['rwkv_r_k'], axis=-1, keepdims=True) * vh).reshape(b, T, RWKV_DIM)
    return (y * g).astype(cur.dtype), S_T.astype(state0.dtype)


def hybrid_layer(x, p, pos0, past):
    b, T, _ = x.shape
    q_pos = pos0 + jnp.arange(T, dtype=jnp.int32)
    xn = rms_norm(x, p['norm_attn'])
    prev = jnp.zeros_like(xn[:, :1]) if past is None else past['shift'][:, None].astype(xn.dtype)
    pieces = jnp.split(jnp.concatenate([prev, xn], axis=1) @ p['w_in'], split_points(IN_SPLITS), axis=-1)
    a_q, a_kv, a_gate, d_q, d_kv, i_q, i_k, i_w, c_cur, merge = [t[:, 1:] for t in pieces]
    c_prev = pieces[8][:, :-1]

    q_a = rope_partial(rms_norm(a_q.reshape(b, T, NSA_HEADS, HEAD_DIM), p['a_q_norm']), q_pos)
    q_a = q_a.reshape(b, T, NSA_KV, NSA_HEADS // NSA_KV, HEAD_DIM)
    kv6 = a_kv.reshape(b, T, 6, NSA_KV, HEAD_DIM)
    cmp_rows = kv6[:, :, 0:2]
    slc_rows = jnp.stack([rope_partial(rms_norm(kv6[:, :, 2], p['a_k_norm'][1]), q_pos), kv6[:, :, 3]], axis=2)
    win_rows = jnp.stack([rope_partial(rms_norm(kv6[:, :, 4], p['a_k_norm'][2]), q_pos), kv6[:, :, 5]], axis=2)
    q_d = rope_partial(rms_norm(d_q.reshape(b, T, DSA_HEADS, HEAD_DIM), p['b_q_norm']), q_pos)
    q_d = q_d.reshape(b, T, DSA_KV, DSA_HEADS // DSA_KV, HEAD_DIM)
    kv2 = d_kv.reshape(b, T, 2, DSA_KV, HEAD_DIM)
    dsa_rows = jnp.stack([rope_partial(rms_norm(kv2[:, :, 0], p['b_k_norm']), q_pos), kv2[:, :, 1]], axis=2)
    idx_rows = rope_partial(rms_norm(i_k, p['idx_k_norm'])[:, :, None], q_pos)[:, :, 0]
    idx_q = rope_partial(i_q.reshape(b, T, IDX_HEADS, IDX_DIM), q_pos)
    idx_w = i_w * (IDX_HEADS * IDX_DIM) ** -0.5

    if past is None:
        cmp_all, slc_all, win_all, dsa_all, idx_all = cmp_rows, slc_rows, win_rows, dsa_rows, idx_rows
        n_keep = min(WINDOW, T)
        win_pos0 = pos0
        rwkv0 = jnp.zeros((b, RWKV_HEADS, RWKV_HEAD, RWKV_HEAD), x.dtype)
    else:
        cmp_all = jnp.concatenate([past['nsa_cmp'], cmp_rows], axis=1)
        slc_all = jnp.concatenate([past['nsa_slc'], slc_rows], axis=1)
        dsa_all = jnp.concatenate([past['dsa_kv'], dsa_rows], axis=1)
        idx_all = jnp.concatenate([past['dsa_idx'], idx_rows], axis=1)
        win_all = jnp.concatenate([past['nsa_win'], win_rows], axis=1)
        n_keep = past['nsa_win'].shape[1]
        win_pos0 = pos0 - n_keep
        rwkv0 = past['rwkv']

    o_a = nsa_mixer(q_a, cmp_all, slc_all, win_all, win_pos0, a_gate, pos0, q_pos, p)
    o_b = dsa_attention(q_d, dsa_all, idx_all, idx_q, idx_w, q_pos)
    o_c, rwkv_new = rwkv7_mixer(c_cur, c_prev, rwkv0, p)

    g_a, g_b, g_c = jnp.split(jax.nn.sigmoid(merge), 3, axis=-1)
    mixed = g_a * (o_a @ p['p_a']) + g_b * (o_b @ p['p_b']) + g_c * (o_c @ p['p_c'])
    x = x + mixed @ p['w_o']
    h = rms_norm(x, p['norm_ffn'])
    gate, up = jnp.split(h @ p['w_ffn_in'], 2, axis=-1)
    x = x + (jax.nn.silu(gate) * up) @ p['w_ffn_out']
    new_state = {'nsa_cmp': cmp_rows, 'nsa_slc': slc_rows, 'dsa_kv': dsa_rows, 'dsa_idx': idx_rows,
                 'nsa_win': win_all[:, -n_keep:], 'rwkv': rwkv_new, 'shift': xn[:, -1]}
    return x, new_state


def setup_inputs(seed: int = 0) -> dict:
    key = jax.random.key(seed)
    ks = iter(jax.random.split(key, 64))
    nrm = lambda shape, s: jax.random.normal(next(ks), shape, jnp.float32) * s
    gain = lambda shape: 1.0 + nrm(shape, 0.02)
    unif = lambda shape, lo, hi: jax.random.uniform(next(ks), shape, jnp.float32, minval=lo, maxval=hi)
    n_pages = PAST_LEN // PAGE_SIZE
    n_used = DEC_BATCH * n_pages
    n_pool = n_used + n_used // 4
    w_buf = min(WINDOW, PAST_LEN)
    page_table = jax.random.permutation(next(ks), n_pool)[:n_used].reshape(DEC_BATCH, n_pages).astype(jnp.int32)
    c_cols = IN_SPLITS[8]
    return {
        'x_prompt': nrm((BATCH, SEQ, D_MODEL), 1.0),
        'x_sample': nrm((DEC_BATCH, DEC_SEQ, D_MODEL), 1.0),
        'cache_nsa_cmp': nrm((DEPTH, n_pool, PAGE_SIZE, 2, NSA_KV, HEAD_DIM), 1.0),
        'cache_nsa_slc': nrm((DEPTH, n_pool, PAGE_SIZE, 2, NSA_KV, HEAD_DIM), 1.0),
        'cache_dsa_kv': nrm((DEPTH, n_pool, PAGE_SIZE, 2, DSA_KV, HEAD_DIM), 1.0),
        'cache_dsa_idx': nrm((DEPTH, n_pool, PAGE_SIZE, IDX_DIM), 1.0),
        'state_nsa_win': nrm((DEPTH, DEC_BATCH, w_buf, 2, NSA_KV, HEAD_DIM), 1.0),
        'state_rwkv': nrm((DEPTH, DEC_BATCH, RWKV_HEADS, RWKV_HEAD, RWKV_HEAD), 0.5),
        'state_shift': nrm((DEPTH, DEC_BATCH, D_MODEL), 1.0),
        'page_table': page_table,
        'norm_attn': gain((DEPTH, D_MODEL)),
        'w_in': nrm((DEPTH, D_MODEL, D_IN), D_MODEL ** -0.5),
        'a_q_norm': gain((DEPTH, HEAD_DIM)),
        'a_k_norm': gain((DEPTH, 3, HEAD_DIM)),
        'cmp_w1': nrm((DEPTH, 2, CMP_LEN, HEAD_DIM, CMP_HIDDEN), (CMP_LEN * HEAD_DIM) ** -0.5),
        'cmp_pos': nrm((DEPTH, 2, CMP_LEN, HEAD_DIM), 0.5),
        'cmp_w2': nrm((DEPTH, 2, CMP_HIDDEN, HEAD_DIM), CMP_HIDDEN ** -0.5),
        'cmp_b2': nrm((DEPTH, 2, HEAD_DIM), 0.02),
        'b_q_norm': gain((DEPTH, HEAD_DIM)),
        'b_k_norm': gain((DEPTH, HEAD_DIM)),
        'idx_k_norm': gain((DEPTH, IDX_DIM)),
        'rwkv_mu': unif((DEPTH, c_cols), 0.0, 1.0),
        'rwkv_w0': unif((DEPTH, RWKV_DIM), -5.0, 1.0),
        'rwkv_w2': nrm((DEPTH, DECAY_LORA, RWKV_DIM), 0.1),
        'rwkv_a0': nrm((DEPTH, RWKV_DIM), 0.1),
        'rwkv_a2': nrm((DEPTH, AAA_LORA, RWKV_DIM), AAA_LORA ** -0.5),
        'rwkv_g2': nrm((DEPTH, GATE_LORA, RWKV_DIM), GATE_LORA ** -0.5),
        'rwkv_k_k': 0.85 + nrm((DEPTH, RWKV_DIM), 0.02),
        'rwkv_k_a': gain((DEPTH, RWKV_DIM)),
        'rwkv_r_k': nrm((DEPTH, RWKV_HEADS, RWKV_HEAD), 0.1),
        'rwkv_ln_w': gain((DEPTH, RWKV_DIM)),
        'rwkv_ln_b': nrm((DEPTH, RWKV_DIM), 0.02),
        'p_a': nrm((DEPTH, NSA_HEADS * HEAD_DIM, D_MODEL), (NSA_HEADS * HEAD_DIM) ** -0.5),
        'p_b': nrm((DEPTH, DSA_HEADS * HEAD_DIM, D_MODEL), (DSA_HEADS * HEAD_DIM) ** -0.5),
        'p_c': nrm((DEPTH, RWKV_DIM, D_MODEL), RWKV_DIM ** -0.5),
        'w_o': nrm((DEPTH, D_MODEL, D_MODEL), D_MODEL ** -0.5),
        'norm_ffn': gain((DEPTH, D_MODEL)),
        'w_ffn_in': nrm((DEPTH, D_MODEL, 2 * D_FF), D_MODEL ** -0.5),
        'w_ffn_out': nrm((DEPTH, D_FF, D_MODEL), D_FF ** -0.5),
    }


def reference(x_prompt, x_sample, cache_nsa_cmp, cache_nsa_slc, cache_dsa_kv, cache_dsa_idx,
              state_nsa_win, state_rwkv, state_shift, page_table,
              norm_attn, w_in, a_q_norm, a_k_norm, cmp_w1, cmp_pos, cmp_w2, cmp_b2,
              b_q_norm, b_k_norm, idx_k_norm, rwkv_mu, rwkv_w0, rwkv_w2, rwkv_a0, rwkv_a2, rwkv_g2,
              rwkv_k_k, rwkv_k_a, rwkv_r_k, rwkv_ln_w, rwkv_ln_b, p_a, p_b, p_c, w_o,
              norm_ffn, w_ffn_in, w_ffn_out):
    past_len = page_table.shape[1] * cache_nsa_cmp.shape[2]
    yp, ys = x_prompt, x_sample
    st_p, st_s = [], []
    for l in range(DEPTH):
        p = {'norm_attn': norm_attn[l], 'w_in': w_in[l], 'a_q_norm': a_q_norm[l], 'a_k_norm': a_k_norm[l],
             'cmp_w1': cmp_w1[l], 'cmp_pos': cmp_pos[l], 'cmp_w2': cmp_w2[l], 'cmp_b2': cmp_b2[l],
             'b_q_norm': b_q_norm[l], 'b_k_norm': b_k_norm[l], 'idx_k_norm': idx_k_norm[l],
             'rwkv_mu': rwkv_mu[l], 'rwkv_w0': rwkv_w0[l], 'rwkv_w2': rwkv_w2[l], 'rwkv_a0': rwkv_a0[l],
             'rwkv_a2': rwkv_a2[l], 'rwkv_g2': rwkv_g2[l], 'rwkv_k_k': rwkv_k_k[l], 'rwkv_k_a': rwkv_k_a[l],
             'rwkv_r_k': rwkv_r_k[l], 'rwkv_ln_w': rwkv_ln_w[l], 'rwkv_ln_b': rwkv_ln_b[l],
             'p_a': p_a[l], 'p_b': p_b[l], 'p_c': p_c[l], 'w_o': w_o[l],
             'norm_ffn': norm_ffn[l], 'w_ffn_in': w_ffn_in[l], 'w_ffn_out': w_ffn_out[l]}
        past = {'nsa_cmp': gather_pages(cache_nsa_cmp, l, page_table),
                'nsa_slc': gather_pages(cache_nsa_slc, l, page_table),
                'dsa_kv': gather_pages(cache_dsa_kv, l, page_table),
                'dsa_idx': gather_pages(cache_dsa_idx, l, page_table),
                'nsa_win': state_nsa_win[l], 'rwkv': state_rwkv[l], 'shift': state_shift[l]}
        yp, sp = hybrid_layer(yp, p, 0, None)
        ys, ss = hybrid_layer(ys, p, past_len, past)
        st_p.append(sp)
        st_s.append(ss)
    stack = lambda sts, name: jnp.stack([s[name] for s in sts])
    return (yp, ys,
            stack(st_p, 'nsa_cmp'), stack(st_s, 'nsa_cmp'),
            stack(st_p, 'nsa_slc'), stack(st_s, 'nsa_slc'),
            stack(st_p, 'dsa_kv'), stack(st_s, 'dsa_kv'),
            stack(st_p, 'dsa_idx'), stack(st_s, 'dsa_idx'),
            stack(st_p, 'nsa_win'), stack(st_s, 'nsa_win'),
            stack(st_p, 'rwkv'), stack(st_s, 'rwkv'),
            stack(st_p, 'shift'), stack(st_s, 'shift'))
```

```python
import functools
import math

import jax
import jax.numpy as jnp
from jax import lax
from jax.experimental import pallas as pl
from jax.experimental.pallas import tpu as pltpu

D_MODEL = 2048
HEAD_DIM = 128
ROT_DIV = 4
ROPE_THETA = 500000.0
NORM_EPS = 1e-6

NSA_HEADS = 8
NSA_KV = 2
CMP_STRIDE = 16
CMP_LEN = 2 * CMP_STRIDE
SLC_BLOCK = 64
N_SEL = 8
N_LOCAL = 2
WINDOW = 512

DSA_HEADS = 8
DSA_KV = 2
IDX_HEADS = 8
IDX_DIM = 64
TOPK_MAX = 256

RWKV_HEADS = 16
RWKV_HEAD = 64
RWKV_DIM = RWKV_HEADS * RWKV_HEAD
DECAY_LORA = 64
AAA_LORA = 64
GATE_LORA = 160
RWKV_GN_EPS = 64e-5

Q_BLOCK = 128
NEG = -1e30
FORCE = 1e9

IN_SPLITS = (
    NSA_HEADS * HEAD_DIM,
    6 * NSA_KV * HEAD_DIM,
    3 * NSA_HEADS,
    DSA_HEADS * HEAD_DIM,
    2 * DSA_KV * HEAD_DIM,
    IDX_HEADS * IDX_DIM,
    IDX_DIM,
    IDX_HEADS,
    3 * RWKV_DIM + DECAY_LORA + AAA_LORA + GATE_LORA,
    3 * D_MODEL,
)

LANES = 128
VMEM_LIMIT_BYTES = 56 * 1024 * 1024


def _mm_kernel(x_ref, w_ref, o_ref, xb_ref):
    @pl.when(pl.program_id(1) == 0)
    def _():
        xb_ref[...] = x_ref[...].astype(jnp.bfloat16)

    o_ref[...] = jnp.dot(xb_ref[...], w_ref[...].astype(jnp.bfloat16),
                         preferred_element_type=jnp.float32)


def _pick_tile(n, prefs):
    for t in prefs:
        if n % t == 0:
            return t
    return n


def _mm(x, w):
    m, k = x.shape
    n = w.shape[1]
    tm = _pick_tile(m, (1024, 512, 256, 128)) if k <= 2048 else _pick_tile(m, (256, 128))
    tn = _pick_tile(n, (512, 384, 256, 128))
    return pl.pallas_call(
        _mm_kernel,
        grid=(m // tm, n // tn),
        in_specs=[pl.BlockSpec((tm, k), lambda i, j: (i, 0)),
                  pl.BlockSpec((k, tn), lambda i, j: (0, j))],
        out_specs=pl.BlockSpec((tm, tn), lambda i, j: (i, j)),
        out_shape=jax.ShapeDtypeStruct((m, n), jnp.float32),
        scratch_shapes=[pltpu.VMEM((tm, k), jnp.bfloat16)],
        compiler_params=pltpu.CompilerParams(
            dimension_semantics=("arbitrary", "arbitrary"),
            vmem_limit_bytes=VMEM_LIMIT_BYTES),
        name="dense_proj",
    )(x, w)


def _proj(x, w):
    lead = x.shape[:-1]
    return _mm(x.reshape(-1, x.shape[-1]), w).reshape(lead + (w.shape[1],))


RWKV_HEADS_PER_BLOCK = LANES // 2
RWKV_V_HALF = RWKV_HEAD // 2


def _rwkv_scan_kernel(r_ref, w_ref, k_ref, a_ref, b_ref, v_ref, s0_ref, y_ref, s_out_ref, s_ref):
    c = pl.program_id(1)

    @pl.when(c == 0)
    def _():
        s_ref[...] = s0_ref[0]

    steps = r_ref.shape[1]

    def step(t, carry):
        parts = [jnp.zeros((RWKV_V_HALF, LANES), jnp.float32) for _ in range(4)]
        for k in range(RWKV_HEAD):
            parts[k % 4] = parts[k % 4] + s_ref[k] * a_ref[0, t, k:k + 1, :]
        sa = (parts[0] + parts[1]) + (parts[2] + parts[3])
        vt = v_ref[0, t]
        ys = [jnp.zeros((RWKV_V_HALF, LANES), jnp.float32) for _ in range(4)]
        for k in range(RWKV_HEAD):
            s_new = (s_ref[k] * w_ref[0, t, k:k + 1, :] + sa * b_ref[0, t, k:k + 1, :]
                     + vt * k_ref[0, t, k:k + 1, :])
            s_ref[k] = s_new
            ys[k % 4] = ys[k % 4] + s_new * r_ref[0, t, k:k + 1, :]
        y_ref[0, t] = (ys[0] + ys[1]) + (ys[2] + ys[3])
        return carry

    lax.fori_loop(0, steps, step, 0)

    @pl.when(c == pl.num_programs(1) - 1)
    def _():
        s_out_ref[0] = s_ref[...]


def _rwkv_scan(state0, r, w, k, v, a, b):
    bsz, t_len, n_h, n = r.shape
    heads = bsz * n_h
    hb = RWKV_HEADS_PER_BLOCK
    nblk = heads // hb
    tc = _pick_tile(t_len, (64, 32, 16, 8, 4))

    def key_rows(x):
        x = jnp.transpose(x, (1, 3, 0, 2)).reshape(t_len, n, nblk, 1, hb)
        x = jnp.broadcast_to(x, (t_len, n, nblk, 2, hb))
        return jnp.transpose(x, (2, 0, 1, 3, 4)).reshape(nblk, t_len, n, LANES)

    vv = jnp.transpose(v, (1, 3, 0, 2)).reshape(t_len, 2, RWKV_V_HALF, nblk, hb)
    vv = jnp.transpose(vv, (3, 0, 2, 1, 4)).reshape(nblk, t_len, RWKV_V_HALF, LANES)
    s0 = state0.reshape(nblk, hb, 2, RWKV_V_HALF, n)
    s0 = jnp.transpose(s0, (0, 4, 3, 2, 1)).reshape(nblk, n, RWKV_V_HALF, LANES)

    row_spec = pl.BlockSpec((1, tc, n, LANES), lambda i, c: (i, c, 0, 0))
    val_spec = pl.BlockSpec((1, tc, RWKV_V_HALF, LANES), lambda i, c: (i, c, 0, 0))
    st_spec = pl.BlockSpec((1, n, RWKV_V_HALF, LANES), lambda i, c: (i, 0, 0, 0))
    y, s_out = pl.pallas_call(
        _rwkv_scan_kernel,
        grid=(nblk, t_len // tc),
        in_specs=[row_spec] * 5 + [val_spec, st_spec],
        out_specs=[val_spec, st_spec],
        out_shape=[jax.ShapeDtypeStruct((nblk, t_len, RWKV_V_HALF, LANES), jnp.float32),
                   jax.ShapeDtypeStruct((nblk, n, RWKV_V_HALF, LANES), jnp.float32)],
        scratch_shapes=[pltpu.VMEM((n, RWKV_V_HALF, LANES), jnp.float32)],
        compiler_params=pltpu.CompilerParams(
            dimension_semantics=("arbitrary", "arbitrary"),
            vmem_limit_bytes=VMEM_LIMIT_BYTES),
        name="rwkv7_scan",
    )(key_rows(r), key_rows(w), key_rows(k), key_rows(a), key_rows(b), vv, s0)

    y = y.reshape(nblk, t_len, RWKV_V_HALF, 2, hb)
    y = jnp.transpose(y, (0, 4, 1, 3, 2)).reshape(bsz, n_h, t_len, n)
    y = jnp.transpose(y, (0, 2, 1, 3))
    s_out = s_out.reshape(nblk, n, RWKV_V_HALF, 2, hb)
    s_out = jnp.transpose(s_out, (0, 4, 3, 2, 1)).reshape(bsz, n_h, n, n)
    return y, s_out


def _split_points(sizes):
    pts, acc = [], 0
    for s in sizes[:-1]:
        acc += s
        pts.append(acc)
    return pts


def _rms_norm(x, g):
    xf = x.astype(jnp.float32)
    y = xf * lax.rsqrt(jnp.mean(xf * xf, axis=-1, keepdims=True) + NORM_EPS)
    return (y * g.astype(jnp.float32)).astype(x.dtype)


def _rope_partial(x, pos):
    rot = x.shape[-1] // ROT_DIV
    half = rot // 2
    freqs = ROPE_THETA ** (-jnp.arange(half, dtype=jnp.float32) / half)
    ang = pos.astype(jnp.float32)[:, None] * freqs[None, :]
    cos, sin = jnp.cos(ang)[:, None, :], jnp.sin(ang)[:, None, :]
    xr = x[..., :rot].astype(jnp.float32)
    x1, x2 = xr[..., :half], xr[..., half:]
    xr = jnp.concatenate([x1 * cos - x2 * sin, x1 * sin + x2 * cos], axis=-1)
    return jnp.concatenate([xr.astype(x.dtype), x[..., rot:]], axis=-1)


def _masked_softmax(s, mask):
    s = jnp.where(mask, s.astype(jnp.float32), NEG)
    pr = jax.nn.softmax(s, axis=-1)
    return jnp.where(jnp.any(mask, axis=-1, keepdims=True), pr, 0.0)


def _gather_pages(pool, layer, page_table):
    rows = pool[layer, page_table]
    return rows.reshape((page_table.shape[0], -1) + pool.shape[3:])


def _sweep_query_blocks(fn, n_q):
    qb = math.gcd(Q_BLOCK, n_q)
    nb = n_q // qb
    out = lax.map(lambda i: fn(i * qb, qb), jnp.arange(nb, dtype=jnp.int32))
    out = jnp.moveaxis(out, 0, 1)
    return out.reshape((out.shape[0], n_q) + out.shape[3:])


def _compress_blocks(rows, w1, pos_emb, w2, b2):
    b, L, G, d = rows.shape
    n_sub = L // CMP_STRIDE
    sub = rows[:, :n_sub * CMP_STRIDE].reshape(b, n_sub, CMP_STRIDE, G, d)
    h_lo = jnp.einsum('bnsgd,sde->bnge', sub, w1[:CMP_STRIDE])
    h_hi = jnp.einsum('bnsgd,sde->bnge', sub, w1[CMP_STRIDE:])
    h = h_lo[:, :-1] + h_hi[:, 1:] + jnp.einsum('ld,lde->e', pos_emb, w1)
    return jax.nn.gelu(h) @ w2 + b2


def _cmp_to_slc_cover(n_c, n_slc):
    start = jnp.arange(n_c)[:, None] * CMP_STRIDE
    blk = jnp.arange(n_slc)[None, :] * SLC_BLOCK
    return ((start < blk + SLC_BLOCK) & (start + CMP_LEN > blk)).astype(jnp.float32)


def _window_attention(q, kv, q0, k0):
    b, T = q.shape[:2]
    qb = math.gcd(Q_BLOCK, T)
    nb = T // qb
    kv_pad = jnp.pad(kv, ((0, 0), (WINDOW, 0), (0, 0), (0, 0), (0, 0)))
    rows = (q0 - k0) + jnp.arange(nb)[:, None] * qb + jnp.arange(WINDOW + qb)[None, :]
    band = kv_pad[:, rows]
    kpos = rows + (k0 - WINDOW)
    t = q0 + jnp.arange(T).reshape(nb, qb)
    m = ((rows[:, None, :] >= WINDOW) & (kpos[:, None, :] <= t[:, :, None])
         & (t[:, :, None] - kpos[:, None, :] < WINDOW))
    qr = q.reshape((b, nb, qb) + q.shape[2:])
    s = jnp.einsum('bnqghd,bnkgd->bnqghk', qr, band[:, :, :, 0]) * HEAD_DIM ** -0.5
    pr = _masked_softmax(s, m[None, :, :, None, None, :])
    o = jnp.einsum('bnqghk,bnkgd->bnqghd', pr.astype(band.dtype), band[:, :, :, 1])
    return o.reshape(q.shape)


def _nsa_mixer(q, cmp_kv, slc_kv, win_kv, win_pos0, gate_logits, pos0, q_pos, p):
    b, T = q.shape[:2]
    scale = HEAD_DIM ** -0.5
    kc = _compress_blocks(cmp_kv[:, :, 0], p['cmp_w1'][0], p['cmp_pos'][0], p['cmp_w2'][0], p['cmp_b2'][0])
    vc = _compress_blocks(cmp_kv[:, :, 1], p['cmp_w1'][1], p['cmp_pos'][1], p['cmp_w2'][1], p['cmp_b2'][1])
    n_c = kc.shape[1]
    c_pos = jnp.arange(n_c, dtype=jnp.int32) * CMP_STRIDE + (CMP_LEN - 1)
    kc = _rope_partial(_rms_norm(kc, p['a_k_norm'][0]), c_pos)
    p_cmp = _masked_softmax(jnp.einsum('btghd,bngd->btghn', q, kc) * scale,
                            (c_pos[None, :] <= q_pos[:, None])[None, :, None, None, :])
    o_cmp = jnp.einsum('btghn,bngd->btghd', p_cmp.astype(vc.dtype), vc)
    L = slc_kv.shape[1]
    n_slc = -(-L // SLC_BLOCK)
    n_sel = min(N_SEL, n_slc)
    imp = jnp.einsum('btgn,nj->btgj', jnp.sum(p_cmp, axis=3), _cmp_to_slc_cover(n_c, n_slc))
    blk = jnp.arange(n_slc)
    dist = (q_pos // SLC_BLOCK)[:, None] - blk[None, :]
    forced = (blk[None, :] == 0) | ((dist >= 0) & (dist < N_LOCAL))
    score = jnp.where(forced[None, :, None, :], FORCE, jnp.where((dist < 0)[None, :, None, :], -FORCE, imp))
    _, sel = lax.top_k(score, n_sel)
    blocks = jnp.pad(slc_kv, ((0, 0), (0, n_slc * SLC_BLOCK - L), (0, 0), (0, 0), (0, 0)))
    blocks = blocks.reshape(b, n_slc, SLC_BLOCK, 2, NSA_KV, HEAD_DIM).transpose(0, 4, 1, 2, 3, 5)
    bi = jnp.arange(b)[:, None, None, None]
    gi = jnp.arange(NSA_KV)[None, None, :, None]

    def slc_block(start, qb):
        qc = lax.dynamic_slice_in_dim(q, start, qb, axis=1)
        ic = lax.dynamic_slice_in_dim(sel, start, qb, axis=1)
        tc = lax.dynamic_slice_in_dim(q_pos, start, qb, axis=0)
        kv = blocks[bi, gi, ic].reshape(b, qb, NSA_KV, n_sel * SLC_BLOCK, 2, HEAD_DIM)
        kpos = (ic[..., None] * SLC_BLOCK + jnp.arange(SLC_BLOCK)).reshape(b, qb, NSA_KV, n_sel * SLC_BLOCK)
        pr = _masked_softmax(jnp.einsum('bqghd,bqgkd->bqghk', qc, kv[..., 0, :]) * scale,
                             (kpos <= tc[None, :, None, None])[:, :, :, None, :])
        return jnp.einsum('bqghk,bqgkd->bqghd', pr.astype(kv.dtype), kv[..., 1, :])

    o_slc = _sweep_query_blocks(slc_block, T)
    o_win = _window_attention(q, win_kv, pos0, win_pos0)
    g = jax.nn.sigmoid(gate_logits.reshape(b, T, NSA_KV, NSA_HEADS // NSA_KV, 3))
    o = g[..., 0:1] * o_cmp + g[..., 1:2] * o_slc + g[..., 2:3] * o_win
    return o.reshape(b, T, NSA_HEADS * HEAD_DIM)


def _dsa_attention(q, kv, idx_k, idx_q, idx_w, q_pos):
    b, T = q.shape[:2]
    L = kv.shape[1]
    k_top = min(TOPK_MAX, L // 4)
    kpos = jnp.arange(L, dtype=jnp.int32)
    bi = jnp.arange(b)[:, None, None]

    def block(start, qb):
        qc = lax.dynamic_slice_in_dim(q, start, qb, axis=1)
        iqc = lax.dynamic_slice_in_dim(idx_q, start, qb, axis=1)
        iwc = lax.dynamic_slice_in_dim(idx_w, start, qb, axis=1)
        tc = lax.dynamic_slice_in_dim(q_pos, start, qb, axis=0)
        logits = jnp.einsum('bqhd,bkd->bqhk', iqc, idx_k).astype(jnp.float32)
        score = jnp.einsum('bqh,bqhk->bqk', iwc.astype(jnp.float32), jax.nn.relu(logits))
        score = jnp.where(kpos[None, None, :] <= tc[None, :, None], score, NEG)
        _, sel = lax.top_k(score, k_top)
        kvg = kv[bi, sel]
        s = jnp.einsum('bqghd,bqkgd->bqghk', qc, kvg[:, :, :, 0]) * HEAD_DIM ** -0.5
        pr = _masked_softmax(s, (sel <= tc[None, :, None])[:, :, None, None, :])
        return jnp.einsum('bqghk,bqkgd->bqghd', pr.astype(kvg.dtype), kvg[:, :, :, 1])

    return _sweep_query_blocks(block, T).reshape(b, T, DSA_HEADS * HEAD_DIM)


def _rwkv7_mixer(cur, prev, state0, p):
    b, T, _ = cur.shape
    xm = cur + (prev - cur) * p['rwkv_mu']
    r, k, v, w_lo, a_lo, g_lo = jnp.split(xm, [RWKV_DIM, 2 * RWKV_DIM, 3 * RWKV_DIM, 3 * RWKV_DIM + DECAY_LORA,
                                               3 * RWKV_DIM + DECAY_LORA + AAA_LORA], axis=-1)
    w = -jax.nn.softplus(-(p['rwkv_w0'] + jnp.tanh(w_lo) @ p['rwkv_w2'])) - 0.5
    a = jax.nn.sigmoid(p['rwkv_a0'] + a_lo @ p['rwkv_a2'])
    g = jax.nn.sigmoid(g_lo) @ p['rwkv_g2']
    heads = lambda t: t.reshape(b, T, RWKV_HEADS, RWKV_HEAD).astype(jnp.float32)
    kk = heads(k * p['rwkv_k_k'])
    kk = kk * lax.rsqrt(jnp.sum(kk * kk, axis=-1, keepdims=True) + 1e-12)
    kh = heads(k * (1.0 + (a - 1.0) * p['rwkv_k_a']))
    rh, vh, ah = heads(r), heads(v), heads(a)
    decay = jnp.exp(-jnp.exp(heads(w)))
    y, s_t = _rwkv_scan(state0.astype(jnp.float32), rh, decay, kh, vh, -kk, kk * ah)
    mu = jnp.mean(y, axis=-1, keepdims=True)
    var = jnp.mean(jnp.square(y - mu), axis=-1, keepdims=True)
    y = ((y - mu) * lax.rsqrt(var + RWKV_GN_EPS)).reshape(b, T, RWKV_DIM) * p['rwkv_ln_w'] + p['rwkv_ln_b']
    y = y + (jnp.sum(rh * kh * p['rwkv_r_k'], axis=-1, keepdims=True) * vh).reshape(b, T, RWKV_DIM)
    return (y * g).astype(cur.dtype), s_t.astype(state0.dtype)


def _hybrid_layer(x, p, pos0, past):
    b, T, _ = x.shape
    q_pos = pos0 + jnp.arange(T, dtype=jnp.int32)
    xn = _rms_norm(x, p['norm_attn'])
    c_lo, c_hi = sum(IN_SPLITS[:8]), sum(IN_SPLITS[:9])
    if past is None:
        pieces = jnp.split(_proj(xn, p['w_in']), _split_points(IN_SPLITS), axis=-1)
        c_cur = pieces[8]
        c_prev = jnp.concatenate([jnp.zeros_like(c_cur[:, :1]), c_cur[:, :-1]], axis=1)
    else:
        prev = past['shift'][:, None].astype(xn.dtype)
        full = _proj(jnp.concatenate([prev, xn], axis=1), p['w_in'])
        pieces = jnp.split(full[:, 1:], _split_points(IN_SPLITS), axis=-1)
        c_cur = pieces[8]
        c_prev = full[:, :-1, c_lo:c_hi]
    a_q, a_kv, a_gate, d_q, d_kv, i_q, i_k, i_w, _, merge = pieces

    q_a = _rope_partial(_rms_norm(a_q.reshape(b, T, NSA_HEADS, HEAD_DIM), p['a_q_norm']), q_pos)
    q_a = q_a.reshape(b, T, NSA_KV, NSA_HEADS // NSA_KV, HEAD_DIM)
    kv6 = a_kv.reshape(b, T, 6, NSA_KV, HEAD_DIM)
    cmp_rows = kv6[:, :, 0:2]
    slc_rows = jnp.stack([_rope_partial(_rms_norm(kv6[:, :, 2], p['a_k_norm'][1]), q_pos), kv6[:, :, 3]], axis=2)
    win_rows = jnp.stack([_rope_partial(_rms_norm(kv6[:, :, 4], p['a_k_norm'][2]), q_pos), kv6[:, :, 5]], axis=2)
    q_d = _rope_partial(_rms_norm(d_q.reshape(b, T, DSA_HEADS, HEAD_DIM), p['b_q_norm']), q_pos)
    q_d = q_d.reshape(b, T, DSA_KV, DSA_HEADS // DSA_KV, HEAD_DIM)
    kv2 = d_kv.reshape(b, T, 2, DSA_KV, HEAD_DIM)
    dsa_rows = jnp.stack([_rope_partial(_rms_norm(kv2[:, :, 0], p['b_k_norm']), q_pos), kv2[:, :, 1]], axis=2)
    idx_rows = _rope_partial(_rms_norm(i_k, p['idx_k_norm'])[:, :, None], q_pos)[:, :, 0]
    idx_q = _rope_partial(i_q.reshape(b, T, IDX_HEADS, IDX_DIM), q_pos)
    idx_w = i_w * (IDX_HEADS * IDX_DIM) ** -0.5

    if past is None:
        cmp_all, slc_all, win_all, dsa_all, idx_all = cmp_rows, slc_rows, win_rows, dsa_rows, idx_rows
        n_keep = min(WINDOW, T)
        win_pos0 = pos0
        rwkv0 = jnp.zeros((b, RWKV_HEADS, RWKV_HEAD, RWKV_HEAD), x.dtype)
    else:
        cmp_all = jnp.concatenate([past['nsa_cmp'], cmp_rows], axis=1)
        slc_all = jnp.concatenate([past['nsa_slc'], slc_rows], axis=1)
        dsa_all = jnp.concatenate([past['dsa_kv'], dsa_rows], axis=1)
        idx_all = jnp.concatenate([past['dsa_idx'], idx_rows], axis=1)
        win_all = jnp.concatenate([past['nsa_win'], win_rows], axis=1)
        n_keep = past['nsa_win'].shape[1]
        win_pos0 = pos0 - n_keep
        rwkv0 = past['rwkv']

    o_a = _nsa_mixer(q_a, cmp_all, slc_all, win_all, win_pos0, a_gate, pos0, q_pos, p)
    o_b = _dsa_attention(q_d, dsa_all, idx_all, idx_q, idx_w, q_pos)
    o_c, rwkv_new = _rwkv7_mixer(c_cur, c_prev, rwkv0, p)

    g_a, g_b, g_c = jnp.split(jax.nn.sigmoid(merge), 3, axis=-1)
    mixed = g_a * _proj(o_a, p['p_a']) + g_b * _proj(o_b, p['p_b']) + g_c * _proj(o_c, p['p_c'])
    x = x + _proj(mixed, p['w_o'])
    h = _rms_norm(x, p['norm_ffn'])
    gate, up = jnp.split(_proj(h, p['w_ffn_in']), 2, axis=-1)
    x = x + _proj(jax.nn.silu(gate) * up, p['w_ffn_out'])
    new_state = {'nsa_cmp': cmp_rows, 'nsa_slc': slc_rows, 'dsa_kv': dsa_rows, 'dsa_idx': idx_rows,
                 'nsa_win': win_all[:, -n_keep:], 'rwkv': rwkv_new, 'shift': xn[:, -1]}
    return x, new_state


_PARAM_NAMES = ('norm_attn', 'w_in', 'a_q_norm', 'a_k_norm', 'cmp_w1', 'cmp_pos', 'cmp_w2', 'cmp_b2',
                'b_q_norm', 'b_k_norm', 'idx_k_norm', 'rwkv_mu', 'rwkv_w0', 'rwkv_w2', 'rwkv_a0',
                'rwkv_a2', 'rwkv_g2', 'rwkv_k_k', 'rwkv_k_a', 'rwkv_r_k', 'rwkv_ln_w', 'rwkv_ln_b',
                'p_a', 'p_b', 'p_c', 'w_o', 'norm_ffn', 'w_ffn_in', 'w_ffn_out')


def kernel(x_prompt, x_sample, cache_nsa_cmp, cache_nsa_slc, cache_dsa_kv, cache_dsa_idx, state_nsa_win, state_rwkv, state_shift, page_table, norm_attn, w_in, a_q_norm, a_k_norm, cmp_w1, cmp_pos, cmp_w2, cmp_b2, b_q_norm, b_k_norm, idx_k_norm, rwkv_mu, rwkv_w0, rwkv_w2, rwkv_a0, rwkv_a2, rwkv_g2, rwkv_k_k, rwkv_k_a, rwkv_r_k, rwkv_ln_w, rwkv_ln_b, p_a, p_b, p_c, w_o, norm_ffn, w_ffn_in, w_ffn_out):
    weights = dict(zip(_PARAM_NAMES, (norm_attn, w_in, a_q_norm, a_k_norm, cmp_w1, cmp_pos, cmp_w2, cmp_b2,
                                      b_q_norm, b_k_norm, idx_k_norm, rwkv_mu, rwkv_w0, rwkv_w2, rwkv_a0,
                                      rwkv_a2, rwkv_g2, rwkv_k_k, rwkv_k_a, rwkv_r_k, rwkv_ln_w, rwkv_ln_b,
                                      p_a, p_b, p_c, w_o, norm_ffn, w_ffn_in, w_ffn_out)))
    depth = w_in.shape[0]
    past_len = page_table.shape[1] * cache_nsa_cmp.shape[2]
    yp, ys = x_prompt, x_sample
    st_p, st_s = [], []
    for l in range(depth):
        p = {name: w[l] for name, w in weights.items()}
        past = {'nsa_cmp': _gather_pages(cache_nsa_cmp, l, page_table),
                'nsa_slc': _gather_pages(cache_nsa_slc, l, page_table),
                'dsa_kv': _gather_pages(cache_dsa_kv, l, page_table),
                'dsa_idx': _gather_pages(cache_dsa_idx, l, page_table),
                'nsa_win': state_nsa_win[l], 'rwkv': state_rwkv[l], 'shift': state_shift[l]}
        yp, sp = _hybrid_layer(yp, p, 0, None)
        ys, ss = _hybrid_layer(ys, p, past_len, past)
        st_p.append(sp)
        st_s.append(ss)
    stack = lambda sts, name: jnp.stack([s[name] for s in sts])
    return (yp, ys,
            stack(st_p, 'nsa_cmp'), stack(st_s, 'nsa_cmp'),
            stack(st_p, 'nsa_slc'), stack(st_s, 'nsa_slc'),
            stack(st_p, 'dsa_kv'), stack(st_s, 'dsa_kv'),
            stack(st_p, 'dsa_idx'), stack(st_s, 'dsa_idx'),
            stack(st_p, 'nsa_win'), stack(st_s, 'nsa_win'),
            stack(st_p, 'rwkv'), stack(st_s, 'rwkv'),
            stack(st_p, 'shift'), stack(st_s, 'shift'))
```

```python
import functools
import math

import jax
import jax.numpy as jnp
from jax import lax
from jax.experimental import pallas as pl
from jax.experimental.pallas import tpu as pltpu

D_MODEL = 2048
HEAD_DIM = 128
ROT_DIV = 4
ROPE_THETA = 500000.0
NORM_EPS = 1e-6

NSA_HEADS = 8
NSA_KV = 2
CMP_STRIDE = 16
CMP_LEN = 2 * CMP_STRIDE
SLC_BLOCK = 64
N_SEL = 8
N_LOCAL = 2
WINDOW = 512

DSA_HEADS = 8
DSA_KV = 2
IDX_HEADS = 8
IDX_DIM = 64
TOPK_MAX = 256

RWKV_HEADS = 16
RWKV_HEAD = 64
RWKV_DIM = RWKV_HEADS * RWKV_HEAD
DECAY_LORA = 64
AAA_LORA = 64
GATE_LORA = 160
RWKV_GN_EPS = 64e-5

NEG = -1e30
FORCE = 1e9
LOWEST = -3e38

IN_SPLITS = (
    NSA_HEADS * HEAD_DIM,
    6 * NSA_KV * HEAD_DIM,
    3 * NSA_HEADS,
    DSA_HEADS * HEAD_DIM,
    2 * DSA_KV * HEAD_DIM,
    IDX_HEADS * IDX_DIM,
    IDX_DIM,
    IDX_HEADS,
    3 * RWKV_DIM + DECAY_LORA + AAA_LORA + GATE_LORA,
    3 * D_MODEL,
)

LANES = 128
SUBLANES = 8
VMEM_LIMIT_BYTES = 56 * 1024 * 1024

HEADS_PER_GROUP = NSA_HEADS // NSA_KV
ATTN_SCALE = HEAD_DIM ** -0.5
Q_TILE = 128

_NT = (((1,), (1,)), ((), ()))


def _round_up(n, m):
    return -(-n // m) * m


def _mm_kernel(x_ref, w_ref, o_ref, xb_ref):
    @pl.when(pl.program_id(1) == 0)
    def _():
        xb_ref[...] = x_ref[...].astype(jnp.bfloat16)

    o_ref[...] = jnp.dot(xb_ref[...], w_ref[...].astype(jnp.bfloat16),
                         preferred_element_type=jnp.float32)


def _pick_tile(n, prefs):
    for t in prefs:
        if n % t == 0:
            return t
    return n


def _mm(x, w):
    m, k = x.shape
    n = w.shape[1]
    tm = _pick_tile(m, (1024, 512, 256, 128)) if k <= 2048 else _pick_tile(m, (256, 128))
    tn = _pick_tile(n, (512, 384, 256, 128))
    return pl.pallas_call(
        _mm_kernel,
        grid=(m // tm, n // tn),
        in_specs=[pl.BlockSpec((tm, k), lambda i, j: (i, 0)),
                  pl.BlockSpec((k, tn), lambda i, j: (0, j))],
        out_specs=pl.BlockSpec((tm, tn), lambda i, j: (i, j)),
        out_shape=jax.ShapeDtypeStruct((m, n), jnp.float32),
        scratch_shapes=[pltpu.VMEM((tm, k), jnp.bfloat16)],
        compiler_params=pltpu.CompilerParams(
            dimension_semantics=("arbitrary", "arbitrary"),
            vmem_limit_bytes=VMEM_LIMIT_BYTES),
        name="dense_proj",
    )(x, w)


def _proj(x, w):
    lead = x.shape[:-1]
    return _mm(x.reshape(-1, x.shape[-1]), w).reshape(lead + (w.shape[1],))


RWKV_HEADS_PER_BLOCK = LANES // 2
RWKV_V_HALF = RWKV_HEAD // 2


def _rwkv_scan_kernel(r_ref, w_ref, k_ref, a_ref, b_ref, v_ref, s0_ref, y_ref, s_out_ref, s_ref):
    c = pl.program_id(1)

    @pl.when(c == 0)
    def _():
        s_ref[...] = s0_ref[0]

    steps = r_ref.shape[1]

    def step(t, carry):
        parts = [jnp.zeros((RWKV_V_HALF, LANES), jnp.float32) for _ in range(4)]
        for k in range(RWKV_HEAD):
            parts[k % 4] = parts[k % 4] + s_ref[k] * a_ref[0, t, k:k + 1, :]
        sa = (parts[0] + parts[1]) + (parts[2] + parts[3])
        vt = v_ref[0, t]
        ys = [jnp.zeros((RWKV_V_HALF, LANES), jnp.float32) for _ in range(4)]
        for k in range(RWKV_HEAD):
            s_new = (s_ref[k] * w_ref[0, t, k:k + 1, :] + sa * b_ref[0, t, k:k + 1, :]
                     + vt * k_ref[0, t, k:k + 1, :])
            s_ref[k] = s_new
            ys[k % 4] = ys[k % 4] + s_new * r_ref[0, t, k:k + 1, :]
        y_ref[0, t] = (ys[0] + ys[1]) + (ys[2] + ys[3])
        return carry

    lax.fori_loop(0, steps, step, 0)

    @pl.when(c == pl.num_programs(1) - 1)
    def _():
        s_out_ref[0] = s_ref[...]


def _rwkv_scan(state0, r, w, k, v, a, b):
    bsz, t_len, n_h, n = r.shape
    heads = bsz * n_h
    hb = RWKV_HEADS_PER_BLOCK
    nblk = heads // hb
    tc = _pick_tile(t_len, (64, 32, 16, 8, 4))

    def key_rows(x):
        x = jnp.transpose(x, (1, 3, 0, 2)).reshape(t_len, n, nblk, 1, hb)
        x = jnp.broadcast_to(x, (t_len, n, nblk, 2, hb))
        return jnp.transpose(x, (2, 0, 1, 3, 4)).reshape(nblk, t_len, n, LANES)

    vv = jnp.transpose(v, (1, 3, 0, 2)).reshape(t_len, 2, RWKV_V_HALF, nblk, hb)
    vv = jnp.transpose(vv, (3, 0, 2, 1, 4)).reshape(nblk, t_len, RWKV_V_HALF, LANES)
    s0 = state0.reshape(nblk, hb, 2, RWKV_V_HALF, n)
    s0 = jnp.transpose(s0, (0, 4, 3, 2, 1)).reshape(nblk, n, RWKV_V_HALF, LANES)

    row_spec = pl.BlockSpec((1, tc, n, LANES), lambda i, c: (i, c, 0, 0))
    val_spec = pl.BlockSpec((1, tc, RWKV_V_HALF, LANES), lambda i, c: (i, c, 0, 0))
    st_spec = pl.BlockSpec((1, n, RWKV_V_HALF, LANES), lambda i, c: (i, 0, 0, 0))
    y, s_out = pl.pallas_call(
        _rwkv_scan_kernel,
        grid=(nblk, t_len // tc),
        in_specs=[row_spec] * 5 + [val_spec, st_spec],
        out_specs=[val_spec, st_spec],
        out_shape=[jax.ShapeDtypeStruct((nblk, t_len, RWKV_V_HALF, LANES), jnp.float32),
                   jax.ShapeDtypeStruct((nblk, n, RWKV_V_HALF, LANES), jnp.float32)],
        scratch_shapes=[pltpu.VMEM((n, RWKV_V_HALF, LANES), jnp.float32)],
        compiler_params=pltpu.CompilerParams(
            dimension_semantics=("arbitrary", "arbitrary"),
            vmem_limit_bytes=VMEM_LIMIT_BYTES),
        name="rwkv7_scan",
    )(key_rows(r), key_rows(w), key_rows(k), key_rows(a), key_rows(b), vv, s0)

    y = y.reshape(nblk, t_len, RWKV_V_HALF, 2, hb)
    y = jnp.transpose(y, (0, 4, 1, 3, 2)).reshape(bsz, n_h, t_len, n)
    y = jnp.transpose(y, (0, 2, 1, 3))
    s_out = s_out.reshape(nblk, n, RWKV_V_HALF, 2, hb)
    s_out = jnp.transpose(s_out, (0, 4, 3, 2, 1)).reshape(bsz, n_h, n, n)
    return y, s_out


def _group_queries(q, g):
    h0 = g * HEADS_PER_GROUP
    return jnp.concatenate([q[:, (h0 + h) * HEAD_DIM:(h0 + h + 1) * HEAD_DIM] for h in range(HEADS_PER_GROUP)],
                           axis=0).astype(jnp.bfloat16)


def _attend(qg, k, v, mask, want_psum=False):
    tq = mask.shape[0]
    s = lax.dot_general(qg, k, _NT, preferred_element_type=jnp.float32) * ATTN_SCALE
    ps = []
    for h in range(HEADS_PER_GROUP):
        sh = jnp.where(mask, s[h * tq:(h + 1) * tq], NEG)
        e = jnp.where(mask, jnp.exp(sh - jnp.max(sh, axis=-1, keepdims=True)), 0.0)
        den = jnp.sum(e, axis=-1, keepdims=True)
        ps.append(e / jnp.where(den > 0.0, den, 1.0))
    o = jnp.dot(jnp.concatenate(ps, axis=0).astype(jnp.bfloat16), v, preferred_element_type=jnp.float32)
    if want_psum:
        return o, (ps[0] + ps[1]) + (ps[2] + ps[3])
    return o


def _nsa_kernel(q_ref, kc_ref, vc_ref, slc_ref, win_ref, gate_ref, cover_ref, expand_ref, o_ref,
                *, pos0, win_pos0, n_slc, n_sel):
    tq = q_ref.shape[1]
    row = lax.broadcasted_iota(jnp.int32, (tq, 1), 0)
    qpos = pos0 + pl.program_id(1) * tq + row
    q = q_ref[0]
    gates = jax.nn.sigmoid(gate_ref[0])

    cpos = lax.broadcasted_iota(jnp.int32, (1, kc_ref.shape[2]), 1) * CMP_STRIDE + (CMP_LEN - 1)
    cmask = cpos <= qpos
    kpos = lax.broadcasted_iota(jnp.int32, (1, slc_ref.shape[1]), 1)
    causal = kpos <= qpos
    wpos = win_pos0 + lax.broadcasted_iota(jnp.int32, (1, win_ref.shape[1]), 1)
    wmask = (wpos <= qpos) & (qpos - wpos < WINDOW)
    blk = lax.broadcasted_iota(jnp.int32, (1, cover_ref.shape[1]), 1)
    dist = jnp.right_shift(qpos, int(math.log2(SLC_BLOCK))) - blk
    forced = (blk == 0) | ((dist >= 0) & (dist < N_LOCAL))
    cover = cover_ref[...]

    for g in range(NSA_KV):
        qg = _group_queries(q, g)
        o_cmp, p_sum = _attend(qg, kc_ref[0, g], vc_ref[0, g], cmask, want_psum=True)
        p_hi = p_sum.astype(jnp.bfloat16)
        p_lo = (p_sum - p_hi.astype(jnp.float32)).astype(jnp.bfloat16)
        imp = (jnp.dot(p_hi, cover, preferred_element_type=jnp.float32)
               + jnp.dot(p_lo, cover, preferred_element_type=jnp.float32))
        score = jnp.where(forced, FORCE, jnp.where(dist < 0, -FORCE, imp))
        score = jnp.where(blk < n_slc, score, LOWEST)
        sel = jnp.zeros(score.shape, jnp.float32)
        for _ in range(n_sel):
            best = jnp.max(score, axis=-1, keepdims=True)
            first = jnp.min(jnp.where(score == best, blk, cover_ref.shape[1]), axis=-1, keepdims=True)
            hit = blk == first
            sel = jnp.where(hit, 1.0, sel)
            score = jnp.where(hit, LOWEST, score)
        smask = jnp.dot(sel.astype(jnp.bfloat16), expand_ref[...], preferred_element_type=jnp.float32) > 0.5
        c0 = g * HEAD_DIM
        o_slc = _attend(qg, slc_ref[0, :, c0:c0 + HEAD_DIM],
                        slc_ref[0, :, NSA_KV * HEAD_DIM + c0:NSA_KV * HEAD_DIM + c0 + HEAD_DIM], smask & causal)
        o_win = _attend(qg, win_ref[0, :, c0:c0 + HEAD_DIM],
                        win_ref[0, :, NSA_KV * HEAD_DIM + c0:NSA_KV * HEAD_DIM + c0 + HEAD_DIM], wmask)
        for h in range(HEADS_PER_GROUP):
            head = g * HEADS_PER_GROUP + h
            rows = slice(h * tq, (h + 1) * tq)
            o_ref[0, :, head * HEAD_DIM:(head + 1) * HEAD_DIM] = (
                gates[:, 3 * head:3 * head + 1] * o_cmp[rows]
                + gates[:, 3 * head + 1:3 * head + 2] * o_slc[rows]
                + gates[:, 3 * head + 2:3 * head + 3] * o_win[rows])


def _pad_rows(x, n):
    return jnp.pad(x, ((0, 0), (0, n - x.shape[1])) + ((0, 0),) * (x.ndim - 2))


def _nsa_attention(q, kc, vc, slc_kv, win_kv, gate_logits, pos0, win_pos0, n_c):
    b, t_len, _ = q.shape
    seq = slc_kv.shape[1]
    tq = Q_TILE if t_len % Q_TILE == 0 else _round_up(t_len, SUBLANES)
    tp = _round_up(t_len, tq)
    lp = _round_up(seq, LANES)
    lw = _round_up(win_kv.shape[1], LANES)
    n_cp = kc.shape[2]
    n_slc = -(-seq // SLC_BLOCK)
    n_sp = _round_up(n_slc, LANES)
    slc = _pad_rows(slc_kv.reshape(b, seq, -1), lp).astype(jnp.bfloat16)
    win = _pad_rows(win_kv.reshape(b, win_kv.shape[1], -1), lw).astype(jnp.bfloat16)
    start = jnp.arange(n_cp)[:, None] * CMP_STRIDE
    lo = jnp.arange(n_sp)[None, :] * SLC_BLOCK
    cover = ((start < lo + SLC_BLOCK) & (start + CMP_LEN > lo) & (jnp.arange(n_cp)[:, None] < n_c)
             & (jnp.arange(n_sp)[None, :] < n_slc)).astype(jnp.bfloat16)
    expand = (jnp.arange(lp)[None, :] // SLC_BLOCK == jnp.arange(n_sp)[:, None]).astype(jnp.bfloat16)
    kern = functools.partial(_nsa_kernel, pos0=pos0, win_pos0=win_pos0, n_slc=n_slc, n_sel=min(N_SEL, n_slc))
    width = NSA_HEADS * HEAD_DIM
    kv_w = 2 * NSA_KV * HEAD_DIM
    out = pl.pallas_call(
        kern,
        grid=(b, tp // tq),
        in_specs=[pl.BlockSpec((1, tq, width), lambda i, j: (i, j, 0)),
                  pl.BlockSpec((1, NSA_KV, n_cp, HEAD_DIM), lambda i, j: (i, 0, 0, 0)),
                  pl.BlockSpec((1, NSA_KV, n_cp, HEAD_DIM), lambda i, j: (i, 0, 0, 0)),
                  pl.BlockSpec((1, lp, kv_w), lambda i, j: (i, 0, 0)),
                  pl.BlockSpec((1, lw, kv_w), lambda i, j: (i, 0, 0)),
                  pl.BlockSpec((1, tq, 3 * NSA_HEADS), lambda i, j: (i, j, 0)),
                  pl.BlockSpec((n_cp, n_sp), lambda i, j: (0, 0)),
                  pl.BlockSpec((n_sp, lp), lambda i, j: (0, 0))],
        out_specs=pl.BlockSpec((1, tq, width), lambda i, j: (i, j, 0)),
        out_shape=jax.ShapeDtypeStruct((b, tp, width), jnp.float32),
        compiler_params=pltpu.CompilerParams(
            dimension_semantics=("arbitrary", "arbitrary"),
            vmem_limit_bytes=VMEM_LIMIT_BYTES),
        name="nsa_attention",
    )(_pad_rows(q, tp), kc.astype(jnp.bfloat16), vc.astype(jnp.bfloat16), slc, win,
      _pad_rows(gate_logits, tp), cover, expand)
    return out[:, :t_len]


def _compress_kernel(x_ref, w1_ref, pos_ref, w2_ref, b2_ref, o_ref):
    n_sub = o_ref.shape[3]
    half = CMP_STRIDE * HEAD_DIM
    x = jnp.concatenate([x_ref[0, pl.ds(s, n_sub, stride=CMP_STRIDE), :].astype(jnp.bfloat16)
                         for s in range(CMP_STRIDE)], axis=1)
    h_lo = jnp.dot(x, w1_ref[0, :half].astype(jnp.bfloat16), preferred_element_type=jnp.float32)
    h_hi = jnp.dot(x, w1_ref[0, half:].astype(jnp.bfloat16), preferred_element_type=jnp.float32)
    h = h_lo + pltpu.roll(h_hi, n_sub - 1, 0) + pos_ref[0]
    o_ref[0, 0, 0] = jnp.dot(jax.nn.gelu(h).astype(jnp.bfloat16), w2_ref[0].astype(jnp.bfloat16),
                             preferred_element_type=jnp.float32) + b2_ref[0]


def _compress(rows, w1, pos_emb, w2, b2):
    b, seq = rows.shape[:2]
    n_sub = seq // CMP_STRIDE
    x = rows[:, :n_sub * CMP_STRIDE].reshape(b, n_sub * CMP_STRIDE, 2 * NSA_KV * HEAD_DIM)
    pos_bias = jnp.einsum('kld,klde->ke', pos_emb, w1)[:, None, :]
    return pl.pallas_call(
        _compress_kernel,
        grid=(b, 2, NSA_KV),
        in_specs=[pl.BlockSpec((1, n_sub * CMP_STRIDE, HEAD_DIM), lambda i, s, g: (i, 0, s * NSA_KV + g)),
                  pl.BlockSpec((1, CMP_LEN * HEAD_DIM, HEAD_DIM), lambda i, s, g: (s, 0, 0)),
                  pl.BlockSpec((1, 1, HEAD_DIM), lambda i, s, g: (s, 0, 0)),
                  pl.BlockSpec((1, HEAD_DIM, HEAD_DIM), lambda i, s, g: (s, 0, 0)),
                  pl.BlockSpec((1, 1, HEAD_DIM), lambda i, s, g: (s, 0, 0))],
        out_specs=pl.BlockSpec((1, 1, 1, n_sub, HEAD_DIM), lambda i, s, g: (i, s, g, 0, 0)),
        out_shape=jax.ShapeDtypeStruct((b, 2, NSA_KV, n_sub, HEAD_DIM), jnp.float32),
        compiler_params=pltpu.CompilerParams(
            dimension_semantics=("arbitrary", "arbitrary", "arbitrary"),
            vmem_limit_bytes=VMEM_LIMIT_BYTES),
        name="nsa_compress",
    )(x, w1.reshape(2, CMP_LEN * HEAD_DIM, HEAD_DIM), pos_bias, w2, b2[:, None, :])


INT_MIN = -2 ** 31


def _count(pred):
    return jnp.sum(jnp.where(pred, 1.0, 0.0), axis=-1, keepdims=True)


def _dsa_kernel(q_ref, kv_ref, iq_ref, iw_ref, ik_ref, o_ref, *, pos0, k_top):
    tq = q_ref.shape[1]
    lp = kv_ref.shape[1]
    row = lax.broadcasted_iota(jnp.int32, (tq, 1), 0)
    qpos = pos0 + pl.program_id(1) * tq + row
    kpos = lax.broadcasted_iota(jnp.int32, (1, lp), 1)
    causal = kpos <= qpos

    iq = iq_ref[0].astype(jnp.bfloat16)
    iw = iw_ref[0]
    ik = ik_ref[0]
    score = jnp.zeros((tq, lp), jnp.float32)
    for h in range(IDX_HEADS):
        logits = lax.dot_general(iq[:, h * IDX_DIM:(h + 1) * IDX_DIM], ik, _NT,
                                 preferred_element_type=jnp.float32)
        score = score + iw[:, h:h + 1] * jnp.maximum(logits, 0.0)
    score = jnp.where(causal, score, NEG)
    score = jnp.where(score == 0.0, 0.0, score)
    bits = lax.bitcast_convert_type(score, jnp.int32)
    key = bits ^ (jnp.right_shift(bits, 31) & 0x7FFFFFFF)

    k_f = float(k_top)
    thr = jnp.where(_count(key >= 0) >= k_f, 0, INT_MIN).astype(jnp.int32)

    def refine(i, thr):
        cand = thr | jnp.left_shift(jnp.int32(1), 30 - i)
        return jnp.where(_count(key >= cand) >= k_f, cand, thr)

    thr = lax.fori_loop(0, 31, refine, thr)
    above = key > thr
    tied = key == thr
    need = k_f - _count(above)

    def widen(i, c):
        cand = c | jnp.left_shift(jnp.int32(1), 14 - i)
        return jnp.where(_count(tied & (kpos < cand)) < need, cand, c)

    bound = lax.fori_loop(0, 15, widen, jnp.zeros((tq, 1), jnp.int32))
    mask = causal & (above | (tied & (kpos <= bound)))

    q = q_ref[0]
    for g in range(DSA_KV):
        c0 = g * HEAD_DIM
        o = _attend(_group_queries(q, g), kv_ref[0, :, c0:c0 + HEAD_DIM],
                    kv_ref[0, :, DSA_KV * HEAD_DIM + c0:DSA_KV * HEAD_DIM + c0 + HEAD_DIM], mask)
        for h in range(HEADS_PER_GROUP):
            head = g * HEADS_PER_GROUP + h
            o_ref[0, :, head * HEAD_DIM:(head + 1) * HEAD_DIM] = o[h * tq:(h + 1) * tq]


def _dsa_attention(q, kv, idx_k, idx_q, idx_w, pos0):
    b, t_len, _ = q.shape
    seq = kv.shape[1]
    assert seq < 2 ** 15
    tq = Q_TILE if t_len % Q_TILE == 0 else _round_up(t_len, SUBLANES)
    tp = _round_up(t_len, tq)
    lp = _round_up(seq, LANES)
    kern = functools.partial(_dsa_kernel, pos0=pos0, k_top=min(TOPK_MAX, seq // 4))
    width = DSA_HEADS * HEAD_DIM
    kv_w = 2 * DSA_KV * HEAD_DIM
    out = pl.pallas_call(
        kern,
        grid=(b, tp // tq),
        in_specs=[pl.BlockSpec((1, tq, width), lambda i, j: (i, j, 0)),
                  pl.BlockSpec((1, lp, kv_w), lambda i, j: (i, 0, 0)),
                  pl.BlockSpec((1, tq, IDX_HEADS * IDX_DIM), lambda i, j: (i, j, 0)),
                  pl.BlockSpec((1, tq, IDX_HEADS), lambda i, j: (i, j, 0)),
                  pl.BlockSpec((1, lp, IDX_DIM), lambda i, j: (i, 0, 0))],
        out_specs=pl.BlockSpec((1, tq, width), lambda i, j: (i, j, 0)),
        out_shape=jax.ShapeDtypeStruct((b, tp, width), jnp.float32),
        compiler_params=pltpu.CompilerParams(
            dimension_semantics=("arbitrary", "arbitrary"),
            vmem_limit_bytes=VMEM_LIMIT_BYTES),
        name="dsa_attention",
    )(_pad_rows(q, tp), _pad_rows(kv.reshape(b, seq, -1), lp).astype(jnp.bfloat16),
      _pad_rows(idx_q, tp), _pad_rows(idx_w, tp), _pad_rows(idx_k, lp).astype(jnp.bfloat16))
    return out[:, :t_len]


def _split_points(sizes):
    pts, acc = [], 0
    for s in sizes[:-1]:
        acc += s
        pts.append(acc)
    return pts


def _rms_norm(x, g):
    xf = x.astype(jnp.float32)
    y = xf * lax.rsqrt(jnp.mean(xf * xf, axis=-1, keepdims=True) + NORM_EPS)
    return (y * g.astype(jnp.float32)).astype(x.dtype)


def _rope_partial(x, pos):
    rot = x.shape[-1] // ROT_DIV
    half = rot // 2
    freqs = ROPE_THETA ** (-jnp.arange(half, dtype=jnp.float32) / half)
    ang = pos.astype(jnp.float32)[:, None] * freqs[None, :]
    cos, sin = jnp.cos(ang)[:, None, :], jnp.sin(ang)[:, None, :]
    xr = x[..., :rot].astype(jnp.float32)
    x1, x2 = xr[..., :half], xr[..., half:]
    xr = jnp.concatenate([x1 * cos - x2 * sin, x1 * sin + x2 * cos], axis=-1)
    return jnp.concatenate([xr.astype(x.dtype), x[..., rot:]], axis=-1)


def _gather_pages(pool, layer, page_table):
    rows = pool[layer, page_table]
    return rows.reshape((page_table.shape[0], -1) + pool.shape[3:])


def _nsa_mixer(q, cmp_kv, slc_kv, win_kv, win_pos0, gate_logits, pos0, p):
    cmp = _compress(cmp_kv, p['cmp_w1'], p['cmp_pos'], p['cmp_w2'], p['cmp_b2'])
    n_sub = cmp.shape[3]
    c_pos = jnp.arange(n_sub, dtype=jnp.int32) * CMP_STRIDE + (CMP_LEN - 1)
    kc = jnp.swapaxes(_rope_partial(_rms_norm(jnp.swapaxes(cmp[:, 0], 1, 2), p['a_k_norm'][0]), c_pos), 1, 2)
    return _nsa_attention(q, kc, cmp[:, 1], slc_kv, win_kv, gate_logits, pos0, win_pos0, n_sub - 1)


def _rwkv7_mixer(cur, prev, state0, p):
    b, T, _ = cur.shape
    xm = cur + (prev - cur) * p['rwkv_mu']
    r, k, v, w_lo, a_lo, g_lo = jnp.split(xm, [RWKV_DIM, 2 * RWKV_DIM, 3 * RWKV_DIM, 3 * RWKV_DIM + DECAY_LORA,
                                               3 * RWKV_DIM + DECAY_LORA + AAA_LORA], axis=-1)
    w = -jax.nn.softplus(-(p['rwkv_w0'] + jnp.tanh(w_lo) @ p['rwkv_w2'])) - 0.5
    a = jax.nn.sigmoid(p['rwkv_a0'] + a_lo @ p['rwkv_a2'])
    g = jax.nn.sigmoid(g_lo) @ p['rwkv_g2']
    heads = lambda t: t.reshape(b, T, RWKV_HEADS, RWKV_HEAD).astype(jnp.float32)
    kk = heads(k * p['rwkv_k_k'])
    kk = kk * lax.rsqrt(jnp.sum(kk * kk, axis=-1, keepdims=True) + 1e-12)
    kh = heads(k * (1.0 + (a - 1.0) * p['rwkv_k_a']))
    rh, vh, ah = heads(r), heads(v), heads(a)
    decay = jnp.exp(-jnp.exp(heads(w)))
    y, s_t = _rwkv_scan(state0.astype(jnp.float32), rh, decay, kh, vh, -kk, kk * ah)
    mu = jnp.mean(y, axis=-1, keepdims=True)
    var = jnp.mean(jnp.square(y - mu), axis=-1, keepdims=True)
    y = ((y - mu) * lax.rsqrt(var + RWKV_GN_EPS)).reshape(b, T, RWKV_DIM) * p['rwkv_ln_w'] + p['rwkv_ln_b']
    y = y + (jnp.sum(rh * kh * p['rwkv_r_k'], axis=-1, keepdims=True) * vh).reshape(b, T, RWKV_DIM)
    return (y * g).astype(cur.dtype), s_t.astype(state0.dtype)


def _hybrid_layer(x, p, pos0, past):
    b, T, _ = x.shape
    q_pos = pos0 + jnp.arange(T, dtype=jnp.int32)
    xn = _rms_norm(x, p['norm_attn'])
    c_lo, c_hi = sum(IN_SPLITS[:8]), sum(IN_SPLITS[:9])
    if past is None:
        pieces = jnp.split(_proj(xn, p['w_in']), _split_points(IN_SPLITS), axis=-1)
        c_cur = pieces[8]
        c_prev = jnp.concatenate([jnp.zeros_like(c_cur[:, :1]), c_cur[:, :-1]], axis=1)
    else:
        prev = past['shift'][:, None].astype(xn.dtype)
        full = _proj(jnp.concatenate([prev, xn], axis=1), p['w_in'])
        pieces = jnp.split(full[:, 1:], _split_points(IN_SPLITS), axis=-1)
        c_cur = pieces[8]
        c_prev = full[:, :-1, c_lo:c_hi]
    a_q, a_kv, a_gate, d_q, d_kv, i_q, i_k, i_w, _, merge = pieces

    q_a = _rope_partial(_rms_norm(a_q.reshape(b, T, NSA_HEADS, HEAD_DIM), p['a_q_norm']), q_pos)
    kv6 = a_kv.reshape(b, T, 6, NSA_KV, HEAD_DIM)
    cmp_rows = kv6[:, :, 0:2]
    slc_rows = jnp.stack([_rope_partial(_rms_norm(kv6[:, :, 2], p['a_k_norm'][1]), q_pos), kv6[:, :, 3]], axis=2)
    win_rows = jnp.stack([_rope_partial(_rms_norm(kv6[:, :, 4], p['a_k_norm'][2]), q_pos), kv6[:, :, 5]], axis=2)
    q_d = _rope_partial(_rms_norm(d_q.reshape(b, T, DSA_HEADS, HEAD_DIM), p['b_q_norm']), q_pos)
    kv2 = d_kv.reshape(b, T, 2, DSA_KV, HEAD_DIM)
    dsa_rows = jnp.stack([_rope_partial(_rms_norm(kv2[:, :, 0], p['b_k_norm']), q_pos), kv2[:, :, 1]], axis=2)
    idx_rows = _rope_partial(_rms_norm(i_k, p['idx_k_norm'])[:, :, None], q_pos)[:, :, 0]
    idx_q = _rope_partial(i_q.reshape(b, T, IDX_HEADS, IDX_DIM), q_pos)
    idx_w = i_w * (IDX_HEADS * IDX_DIM) ** -0.5

    if past is None:
        cmp_all, slc_all, win_all, dsa_all, idx_all = cmp_rows, slc_rows, win_rows, dsa_rows, idx_rows
        n_keep = min(WINDOW, T)
        win_pos0 = pos0
        rwkv0 = jnp.zeros((b, RWKV_HEADS, RWKV_HEAD, RWKV_HEAD), x.dtype)
    else:
        cmp_all = jnp.concatenate([past['nsa_cmp'], cmp_rows], axis=1)
        slc_all = jnp.concatenate([past['nsa_slc'], slc_rows], axis=1)
        dsa_all = jnp.concatenate([past['dsa_kv'], dsa_rows], axis=1)
        idx_all = jnp.concatenate([past['dsa_idx'], idx_rows], axis=1)
        win_all = jnp.concatenate([past['nsa_win'], win_rows], axis=1)
        n_keep = past['nsa_win'].shape[1]
        win_pos0 = pos0 - n_keep
        rwkv0 = past['rwkv']

    o_a = _nsa_mixer(q_a.reshape(b, T, -1), cmp_all, slc_all, win_all, win_pos0, a_gate, pos0, p)
    o_b = _dsa_attention(q_d.reshape(b, T, -1), dsa_all, idx_all, idx_q.reshape(b, T, -1), idx_w, pos0)
    o_c, rwkv_new = _rwkv7_mixer(c_cur, c_prev, rwkv0, p)

    g_a, g_b, g_c = jnp.split(jax.nn.sigmoid(merge), 3, axis=-1)
    mixed = g_a * _proj(o_a, p['p_a']) + g_b * _proj(o_b, p['p_b']) + g_c * _proj(o_c, p['p_c'])
    x = x + _proj(mixed, p['w_o'])
    h = _rms_norm(x, p['norm_ffn'])
    gate, up = jnp.split(_proj(h, p['w_ffn_in']), 2, axis=-1)
    x = x + _proj(jax.nn.silu(gate) * up, p['w_ffn_out'])
    new_state = {'nsa_cmp': cmp_rows, 'nsa_slc': slc_rows, 'dsa_kv': dsa_rows, 'dsa_idx': idx_rows,
                 'nsa_win': win_all[:, -n_keep:], 'rwkv': rwkv_new, 'shift': xn[:, -1]}
    return x, new_state


_PARAM_NAMES = ('norm_attn', 'w_in', 'a_q_norm', 'a_k_norm', 'cmp_w1', 'cmp_pos', 'cmp_w2', 'cmp_b2',
                'b_q_norm', 'b_k_norm', 'idx_k_norm', 'rwkv_mu', 'rwkv_w0', 'rwkv_w2', 'rwkv_a0',
                'rwkv_a2', 'rwkv_g2', 'rwkv_k_k', 'rwkv_k_a', 'rwkv_r_k', 'rwkv_ln_w', 'rwkv_ln_b',
                'p_a', 'p_b', 'p_c', 'w_o', 'norm_ffn', 'w_ffn_in', 'w_ffn_out')


def kernel(x_prompt, x_sample, cache_nsa_cmp, cache_nsa_slc, cache_dsa_kv, cache_dsa_idx, state_nsa_win, state_rwkv, state_shift, page_table, norm_attn, w_in, a_q_norm, a_k_norm, cmp_w1, cmp_pos, cmp_w2, cmp_b2, b_q_norm, b_k_norm, idx_k_norm, rwkv_mu, rwkv_w0, rwkv_w2, rwkv_a0, rwkv_a2, rwkv_g2, rwkv_k_k, rwkv_k_a, rwkv_r_k, rwkv_ln_w, rwkv_ln_b, p_a, p_b, p_c, w_o, norm_ffn, w_ffn_in, w_ffn_out):
    weights = dict(zip(_PARAM_NAMES, (norm_attn, w_in, a_q_norm, a_k_norm, cmp_w1, cmp_pos, cmp_w2, cmp_b2,
                                      b_q_norm, b_k_norm, idx_k_norm, rwkv_mu, rwkv_w0, rwkv_w2, rwkv_a0,
                                      rwkv_a2, rwkv_g2, rwkv_k_k, rwkv_k_a, rwkv_r_k, rwkv_ln_w, rwkv_ln_b,
                                      p_a, p_b, p_c, w_o, norm_ffn, w_ffn_in, w_ffn_out)))
    depth = w_in.shape[0]
    past_len = page_table.shape[1] * cache_nsa_cmp.shape[2]
    yp, ys = x_prompt, x_sample
    st_p, st_s = [], []
    for l in range(depth):
        p = {name: w[l] for name, w in weights.items()}
        past = {'nsa_cmp': _gather_pages(cache_nsa_cmp, l, page_table),
                'nsa_slc': _gather_pages(cache_nsa_slc, l, page_table),
                'dsa_kv': _gather_pages(cache_dsa_kv, l, page_table),
                'dsa_idx': _gather_pages(cache_dsa_idx, l, page_table),
                'nsa_win': state_nsa_win[l], 'rwkv': state_rwkv[l], 'shift': state_shift[l]}
        yp, sp = _hybrid_layer(yp, p, 0, None)
        ys, ss = _hybrid_layer(ys, p, past_len, past)
        st_p.append(sp)
        st_s.append(ss)
    stack = lambda sts, name: jnp.stack([s[name] for s in sts])
    return (yp, ys,
            stack(st_p, 'nsa_cmp'), stack(st_s, 'nsa_cmp'),
            stack(st_p, 'nsa_slc'), stack(st_s, 'nsa_slc'),
            stack(st_p, 'dsa_kv'), stack(st_s, 'dsa_kv'),
            stack(st_p, 'dsa_idx'), stack(st_s, 'dsa_idx'),
            stack(st_p, 'nsa_win'), stack(st_s, 'nsa_win'),
            stack(st_p, 'rwkv'), stack(st_s, 'rwkv'),
            stack(st_p, 'shift'), stack(st_s, 'shift'))
```

```python
import functools
import math

import jax
import jax.numpy as jnp
from jax import lax
from jax.experimental import pallas as pl
from jax.experimental.pallas import tpu as pltpu

D_MODEL = 2048
HEAD_DIM = 128
ROT_DIV = 4
ROPE_THETA = 500000.0
NORM_EPS = 1e-6

NSA_HEADS = 8
NSA_KV = 2
CMP_STRIDE = 16
CMP_LEN = 2 * CMP_STRIDE
SLC_BLOCK = 64
N_SEL = 8
N_LOCAL = 2
WINDOW = 512

DSA_HEADS = 8
DSA_KV = 2
IDX_HEADS = 8
IDX_DIM = 64
TOPK_MAX = 256

RWKV_HEADS = 16
RWKV_HEAD = 64
RWKV_DIM = RWKV_HEADS * RWKV_HEAD
DECAY_LORA = 64
AAA_LORA = 64
GATE_LORA = 160
RWKV_GN_EPS = 64e-5

NEG = -1e30
FORCE = 1e9
LOWEST = -3e38

IN_SPLITS = (
    NSA_HEADS * HEAD_DIM,
    6 * NSA_KV * HEAD_DIM,
    3 * NSA_HEADS,
    DSA_HEADS * HEAD_DIM,
    2 * DSA_KV * HEAD_DIM,
    IDX_HEADS * IDX_DIM,
    IDX_DIM,
    IDX_HEADS,
    3 * RWKV_DIM + DECAY_LORA + AAA_LORA + GATE_LORA,
    3 * D_MODEL,
)

LANES = 128
SUBLANES = 8
VMEM_LIMIT_BYTES = 56 * 1024 * 1024

HEADS_PER_GROUP = NSA_HEADS // NSA_KV
ATTN_SCALE = HEAD_DIM ** -0.5
Q_TILE = 128

_NT = (((1,), (1,)), ((), ()))


def _round_up(n, m):
    return -(-n // m) * m


def _mm_kernel(x_ref, w_ref, o_ref, xb_ref):
    @pl.when(pl.program_id(1) == 0)
    def _():
        xb_ref[...] = x_ref[...].astype(jnp.bfloat16)

    o_ref[...] = jnp.dot(xb_ref[...], w_ref[...].astype(jnp.bfloat16),
                         preferred_element_type=jnp.float32)


def _pick_tile(n, prefs):
    for t in prefs:
        if n % t == 0:
            return t
    return n


def _mm(x, w):
    m, k = x.shape
    n = w.shape[1]
    tm = _pick_tile(m, (1024, 512, 256, 128)) if k <= 2048 else _pick_tile(m, (256, 128))
    tn = _pick_tile(n, (512, 384, 256, 128))
    return pl.pallas_call(
        _mm_kernel,
        grid=(m // tm, n // tn),
        in_specs=[pl.BlockSpec((tm, k), lambda i, j: (i, 0)),
                  pl.BlockSpec((k, tn), lambda i, j: (0, j))],
        out_specs=pl.BlockSpec((tm, tn), lambda i, j: (i, j)),
        out_shape=jax.ShapeDtypeStruct((m, n), jnp.float32),
        scratch_shapes=[pltpu.VMEM((tm, k), jnp.bfloat16)],
        compiler_params=pltpu.CompilerParams(
            dimension_semantics=("arbitrary", "arbitrary"),
            vmem_limit_bytes=VMEM_LIMIT_BYTES),
        name="dense_proj",
    )(x, w)


def _proj(x, w):
    lead = x.shape[:-1]
    return _mm(x.reshape(-1, x.shape[-1]), w).reshape(lead + (w.shape[1],))


RWKV_HEADS_PER_BLOCK = LANES // 2
RWKV_V_HALF = RWKV_HEAD // 2


def _rwkv_scan_kernel(r_ref, w_ref, k_ref, a_ref, b_ref, v_ref, s0_ref, y_ref, s_out_ref, s_ref):
    c = pl.program_id(1)

    @pl.when(c == 0)
    def _():
        s_ref[...] = s0_ref[0]

    steps = r_ref.shape[1]

    def step(t, carry):
        parts = [jnp.zeros((RWKV_V_HALF, LANES), jnp.float32) for _ in range(4)]
        for k in range(RWKV_HEAD):
            parts[k % 4] = parts[k % 4] + s_ref[k] * a_ref[0, t, k:k + 1, :]
        sa = (parts[0] + parts[1]) + (parts[2] + parts[3])
        vt = v_ref[0, t]
        ys = [jnp.zeros((RWKV_V_HALF, LANES), jnp.float32) for _ in range(4)]
        for k in range(RWKV_HEAD):
            s_new = (s_ref[k] * w_ref[0, t, k:k + 1, :] + sa * b_ref[0, t, k:k + 1, :]
                     + vt * k_ref[0, t, k:k + 1, :])
            s_ref[k] = s_new
            ys[k % 4] = ys[k % 4] + s_new * r_ref[0, t, k:k + 1, :]
        y_ref[0, t] = (ys[0] + ys[1]) + (ys[2] + ys[3])
        return carry

    lax.fori_loop(0, steps, step, 0)

    @pl.when(c == pl.num_programs(1) - 1)
    def _():
        s_out_ref[0] = s_ref[...]


def _rwkv_scan(state0, r, w, k, v, a, b):
    bsz, t_len, n_h, n = r.shape
    heads = bsz * n_h
    hb = RWKV_HEADS_PER_BLOCK
    nblk = heads // hb
    tc = _pick_tile(t_len, (64, 32, 16, 8, 4))

    def key_rows(x):
        x = jnp.transpose(x, (1, 3, 0, 2)).reshape(t_len, n, nblk, 1, hb)
        x = jnp.broadcast_to(x, (t_len, n, nblk, 2, hb))
        return jnp.transpose(x, (2, 0, 1, 3, 4)).reshape(nblk, t_len, n, LANES)

    vv = jnp.transpose(v, (1, 3, 0, 2)).reshape(t_len, 2, RWKV_V_HALF, nblk, hb)
    vv = jnp.transpose(vv, (3, 0, 2, 1, 4)).reshape(nblk, t_len, RWKV_V_HALF, LANES)
    s0 = state0.reshape(nblk, hb, 2, RWKV_V_HALF, n)
    s0 = jnp.transpose(s0, (0, 4, 3, 2, 1)).reshape(nblk, n, RWKV_V_HALF, LANES)

    row_spec = pl.BlockSpec((1, tc, n, LANES), lambda i, c: (i, c, 0, 0))
    val_spec = pl.BlockSpec((1, tc, RWKV_V_HALF, LANES), lambda i, c: (i, c, 0, 0))
    st_spec = pl.BlockSpec((1, n, RWKV_V_HALF, LANES), lambda i, c: (i, 0, 0, 0))
    y, s_out = pl.pallas_call(
        _rwkv_scan_kernel,
        grid=(nblk, t_len // tc),
        in_specs=[row_spec] * 5 + [val_spec, st_spec],
        out_specs=[val_spec, st_spec],
        out_shape=[jax.ShapeDtypeStruct((nblk, t_len, RWKV_V_HALF, LANES), jnp.float32),
                   jax.ShapeDtypeStruct((nblk, n, RWKV_V_HALF, LANES), jnp.float32)],
        scratch_shapes=[pltpu.VMEM((n, RWKV_V_HALF, LANES), jnp.float32)],
        compiler_params=pltpu.CompilerParams(
            dimension_semantics=("arbitrary", "arbitrary"),
            vmem_limit_bytes=VMEM_LIMIT_BYTES),
        name="rwkv7_scan",
    )(key_rows(r), key_rows(w), key_rows(k), key_rows(a), key_rows(b), vv, s0)

    y = y.reshape(nblk, t_len, RWKV_V_HALF, 2, hb)
    y = jnp.transpose(y, (0, 4, 1, 3, 2)).reshape(bsz, n_h, t_len, n)
    y = jnp.transpose(y, (0, 2, 1, 3))
    s_out = s_out.reshape(nblk, n, RWKV_V_HALF, 2, hb)
    s_out = jnp.transpose(s_out, (0, 4, 3, 2, 1)).reshape(bsz, n_h, n, n)
    return y, s_out


def _group_queries(q, g):
    h0 = g * HEADS_PER_GROUP
    return jnp.concatenate([q[:, (h0 + h) * HEAD_DIM:(h0 + h + 1) * HEAD_DIM] for h in range(HEADS_PER_GROUP)],
                           axis=0).astype(jnp.bfloat16)


def _attend(qg, k, v, mask, want_psum=False):
    tq = mask.shape[0]
    s = lax.dot_general(qg, k, _NT, preferred_element_type=jnp.float32) * ATTN_SCALE
    ps = []
    for h in range(HEADS_PER_GROUP):
        sh = jnp.where(mask, s[h * tq:(h + 1) * tq], NEG)
        e = jnp.where(mask, jnp.exp(sh - jnp.max(sh, axis=-1, keepdims=True)), 0.0)
        den = jnp.sum(e, axis=-1, keepdims=True)
        ps.append(e / jnp.where(den > 0.0, den, 1.0))
    o = jnp.dot(jnp.concatenate(ps, axis=0).astype(jnp.bfloat16), v, preferred_element_type=jnp.float32)
    if want_psum:
        return o, (ps[0] + ps[1]) + (ps[2] + ps[3])
    return o


def _kv_columns(ref, n_kv):
    def get(s, g, rows):
        c0 = (s * n_kv + g) * HEAD_DIM
        return ref[0, rows, c0:c0 + HEAD_DIM]
    return get


def _kv_planes(ref, n_kv):
    return lambda s, g, rows: ref[s * n_kv + g, rows, :]


KEY_GROUP = 512


def _for_key_range(qpos0, tq, n_keys, body):
    lengths = list(range(KEY_GROUP, n_keys, KEY_GROUP)) + [n_keys]
    group = (qpos0 + tq - 1) // KEY_GROUP
    for i, n in enumerate(lengths):
        @pl.when((group == i) if i + 1 < len(lengths) else (group >= i))
        def _(n=n):
            body(n)


PAGES_PER_STEP = 8


def _page_specs(pool, layer, pages_per_seq):
    assert pages_per_seq % PAGES_PER_STEP == 0
    block = (1, 1) + pool.shape[2:]
    return [pl.BlockSpec(block, functools.partial(
        lambda i, j, pt, k: (layer, pt[i, j * PAGES_PER_STEP + k], 0, 0), k=k)) for k in range(PAGES_PER_STEP)]


def _load_kv_pages(page_refs, new_ref, kv_scr, page_rows, past_len):
    j = pl.program_id(1)
    planes = kv_scr.shape[0]
    for k, page in enumerate(page_refs):
        row0 = pl.multiple_of((j * len(page_refs) + k) * page_rows, page_rows)
        for c in range(planes):
            kv_scr[c, pl.ds(row0, page_rows), :] = (
                page[0, 0, pl.ds(c, page_rows, stride=planes), :].astype(jnp.bfloat16))

    @pl.when(j == 0)
    def _():
        tail = kv_scr.shape[1] - past_len
        for c in range(planes):
            fresh = new_ref[0, :, c * HEAD_DIM:(c + 1) * HEAD_DIM]
            kv_scr[c, past_len:, :] = jnp.concatenate(
                [fresh, jnp.zeros((tail - fresh.shape[0], HEAD_DIM), jnp.float32)], axis=0).astype(jnp.bfloat16)


def _pad_rows(x, n):
    return jnp.pad(x, ((0, 0), (0, n - x.shape[1])) + ((0, 0),) * (x.ndim - 2))


def _nsa_body(q_ref, kc_ref, vc_ref, slc, slc_len, win_ref, gate_ref, cover_ref, expand_ref, o_ref,
              *, qpos0, win_pos0, n_slc, n_sel):
    tq = q_ref.shape[1]
    qpos = qpos0 + lax.broadcasted_iota(jnp.int32, (tq, 1), 0)
    q = q_ref[0]
    gates = jax.nn.sigmoid(gate_ref[0])
    win = _kv_columns(win_ref, NSA_KV)
    win_len = min(win_ref.shape[1], _round_up(WINDOW + tq, LANES))
    win0 = jnp.clip(qpos0 - win_pos0 - WINDOW, 0, win_ref.shape[1] - win_len)
    win_rows = pl.ds(pl.multiple_of(win0, LANES), win_len)
    slc_rows = slice(0, slc_len)

    cpos = lax.broadcasted_iota(jnp.int32, (1, kc_ref.shape[2]), 1) * CMP_STRIDE + (CMP_LEN - 1)
    cmask = cpos <= qpos
    kpos = lax.broadcasted_iota(jnp.int32, (1, slc_len), 1)
    causal = kpos <= qpos
    wpos = win_pos0 + win0 + lax.broadcasted_iota(jnp.int32, (1, win_len), 1)
    wmask = (wpos <= qpos) & (qpos - wpos < WINDOW)
    blk = lax.broadcasted_iota(jnp.int32, (1, cover_ref.shape[1]), 1)
    dist = jnp.right_shift(qpos, int(math.log2(SLC_BLOCK))) - blk
    forced = (blk == 0) | ((dist >= 0) & (dist < N_LOCAL))
    cover = cover_ref[...]

    for g in range(NSA_KV):
        qg = _group_queries(q, g)
        o_cmp, p_sum = _attend(qg, kc_ref[0, g], vc_ref[0, g], cmask, want_psum=True)
        p_hi = p_sum.astype(jnp.bfloat16)
        p_lo = (p_sum - p_hi.astype(jnp.float32)).astype(jnp.bfloat16)
        imp = (jnp.dot(p_hi, cover, preferred_element_type=jnp.float32)
               + jnp.dot(p_lo, cover, preferred_element_type=jnp.float32))
        score = jnp.where(forced, FORCE, jnp.where(dist < 0, -FORCE, imp))
        score = jnp.where(blk < n_slc, score, LOWEST)
        sel = jnp.zeros(score.shape, jnp.float32)
        for _ in range(n_sel):
            best = jnp.max(score, axis=-1, keepdims=True)
            first = jnp.min(jnp.where(score == best, blk, cover_ref.shape[1]), axis=-1, keepdims=True)
            hit = blk == first
            sel = jnp.where(hit, 1.0, sel)
            score = jnp.where(hit, LOWEST, score)
        smask = jnp.dot(sel.astype(jnp.bfloat16), expand_ref[:, :slc_len],
                        preferred_element_type=jnp.float32) > 0.5
        o_slc = _attend(qg, slc(0, g, slc_rows), slc(1, g, slc_rows), smask & causal)
        o_win = _attend(qg, win(0, g, win_rows), win(1, g, win_rows), wmask)
        for h in range(HEADS_PER_GROUP):
            head = g * HEADS_PER_GROUP + h
            rows = slice(h * tq, (h + 1) * tq)
            o_ref[0, :, head * HEAD_DIM:(head + 1) * HEAD_DIM] = (
                gates[:, 3 * head:3 * head + 1] * o_cmp[rows]
                + gates[:, 3 * head + 1:3 * head + 2] * o_slc[rows]
                + gates[:, 3 * head + 2:3 * head + 3] * o_win[rows])


def _nsa_kernel(q_ref, kc_ref, vc_ref, slc_ref, win_ref, gate_ref, cover_ref, expand_ref, o_ref,
                *, pos0, win_pos0, n_slc, n_sel):
    tq = q_ref.shape[1]
    qpos0 = pos0 + pl.program_id(1) * tq
    _for_key_range(qpos0, tq, slc_ref.shape[1], lambda n: _nsa_body(
        q_ref, kc_ref, vc_ref, _kv_columns(slc_ref, NSA_KV), n, win_ref, gate_ref, cover_ref, expand_ref, o_ref,
        qpos0=qpos0, win_pos0=win_pos0, n_slc=n_slc, n_sel=n_sel))


def _nsa_paged_kernel(pt_ref, q_ref, kc_ref, vc_ref, new_ref, win_ref, gate_ref, cover_ref, expand_ref, *rest,
                      pos0, win_pos0, n_slc, n_sel, page_rows):
    page_refs, (o_ref, kv_scr) = rest[:PAGES_PER_STEP], rest[PAGES_PER_STEP:]
    _load_kv_pages(page_refs, new_ref, kv_scr, page_rows, pos0)

    @pl.when(pl.program_id(1) == pl.num_programs(1) - 1)
    def _():
        _nsa_body(q_ref, kc_ref, vc_ref, _kv_planes(kv_scr, NSA_KV), kv_scr.shape[1], win_ref, gate_ref,
                  cover_ref, expand_ref, o_ref, qpos0=pos0, win_pos0=win_pos0, n_slc=n_slc, n_sel=n_sel)


def _nsa_attention(q, kc, vc, slc_kv, win_kv, gate_logits, pos0, win_pos0, n_c, paged=None):
    b, t_len, _ = q.shape
    seq = slc_kv.shape[1] + (pos0 if paged else 0)
    tq = Q_TILE if t_len % Q_TILE == 0 else _round_up(t_len, SUBLANES)
    tp = _round_up(t_len, tq)
    lp = _round_up(seq, LANES)
    lw = _round_up(win_kv.shape[1], LANES)
    n_cp = kc.shape[2]
    n_slc = -(-seq // SLC_BLOCK)
    n_sp = _round_up(n_slc, LANES)
    win = _pad_rows(win_kv.reshape(b, win_kv.shape[1], -1), lw).astype(jnp.bfloat16)
    start = jnp.arange(n_cp)[:, None] * CMP_STRIDE
    lo = jnp.arange(n_sp)[None, :] * SLC_BLOCK
    cover = ((start < lo + SLC_BLOCK) & (start + CMP_LEN > lo) & (jnp.arange(n_cp)[:, None] < n_c)
             & (jnp.arange(n_sp)[None, :] < n_slc)).astype(jnp.bfloat16)
    expand = (jnp.arange(lp)[None, :] // SLC_BLOCK == jnp.arange(n_sp)[:, None]).astype(jnp.bfloat16)
    static = dict(pos0=pos0, win_pos0=win_pos0, n_slc=n_slc, n_sel=min(N_SEL, n_slc))
    width = NSA_HEADS * HEAD_DIM
    kv_w = 2 * NSA_KV * HEAD_DIM
    out_shape = jax.ShapeDtypeStruct((b, tp, width), jnp.float32)
    params = pltpu.CompilerParams(dimension_semantics=("arbitrary", "arbitrary"),
                                  vmem_limit_bytes=VMEM_LIMIT_BYTES)
    if paged is None:
        slc = _pad_rows(slc_kv.reshape(b, seq, -1), lp).astype(jnp.bfloat16)
        out = pl.pallas_call(
            functools.partial(_nsa_kernel, **static),
            grid=(b, tp // tq),
            in_specs=[pl.BlockSpec((1, tq, width), lambda i, j: (i, j, 0)),
                      pl.BlockSpec((1, NSA_KV, n_cp, HEAD_DIM), lambda i, j: (i, 0, 0, 0)),
                      pl.BlockSpec((1, NSA_KV, n_cp, HEAD_DIM), lambda i, j: (i, 0, 0, 0)),
                      pl.BlockSpec((1, lp, kv_w), lambda i, j: (i, 0, 0)),
                      pl.BlockSpec((1, lw, kv_w), lambda i, j: (i, 0, 0)),
                      pl.BlockSpec((1, tq, 3 * NSA_HEADS), lambda i, j: (i, j, 0)),
                      pl.BlockSpec((n_cp, n_sp), lambda i, j: (0, 0)),
                      pl.BlockSpec((n_sp, lp), lambda i, j: (0, 0))],
            out_specs=pl.BlockSpec((1, tq, width), lambda i, j: (i, j, 0)),
            out_shape=out_shape, compiler_params=params, name="nsa_attention",
        )(_pad_rows(q, tp), kc.astype(jnp.bfloat16), vc.astype(jnp.bfloat16), slc, win,
          _pad_rows(gate_logits, tp), cover, expand)
        return out[:, :t_len]

    pool, layer, page_table = paged
    assert tp == tq
    page_rows = pool.shape[2] // (2 * NSA_KV)
    new_rows = _pad_rows(slc_kv.reshape(b, t_len, -1), tq)
    out = pl.pallas_call(
        functools.partial(_nsa_paged_kernel, page_rows=page_rows, **static),
        grid_spec=pltpu.PrefetchScalarGridSpec(
            num_scalar_prefetch=1,
            grid=(b, page_table.shape[1] // PAGES_PER_STEP),
            in_specs=[pl.BlockSpec((1, tq, width), lambda i, j, pt: (i, 0, 0)),
                      pl.BlockSpec((1, NSA_KV, n_cp, HEAD_DIM), lambda i, j, pt: (i, 0, 0, 0)),
                      pl.BlockSpec((1, NSA_KV, n_cp, HEAD_DIM), lambda i, j, pt: (i, 0, 0, 0)),
                      pl.BlockSpec((1, tq, kv_w), lambda i, j, pt: (i, 0, 0)),
                      pl.BlockSpec((1, lw, kv_w), lambda i, j, pt: (i, 0, 0)),
                      pl.BlockSpec((1, tq, 3 * NSA_HEADS), lambda i, j, pt: (i, 0, 0)),
                      pl.BlockSpec((n_cp, n_sp), lambda i, j, pt: (0, 0)),
                      pl.BlockSpec((n_sp, lp), lambda i, j, pt: (0, 0))]
                     + _page_specs(pool, layer, page_table.shape[1]),
            out_specs=pl.BlockSpec((1, tq, width), lambda i, j, pt: (i, 0, 0)),
            scratch_shapes=[pltpu.VMEM((2 * NSA_KV, lp, HEAD_DIM), jnp.bfloat16)]),
        out_shape=out_shape, compiler_params=params, name="nsa_attention_paged",
    )(page_table, _pad_rows(q, tp), kc.astype(jnp.bfloat16), vc.astype(jnp.bfloat16), new_rows, win,
      _pad_rows(gate_logits, tp), cover, expand, *([pool] * PAGES_PER_STEP))
    return out[:, :t_len]


def _compress_mlp(x, w1_ref, pos_ref, w2_ref, b2_ref, s):
    n_sub = x.shape[0]
    half = CMP_STRIDE * HEAD_DIM
    h_lo = jnp.dot(x, w1_ref[s, :half].astype(jnp.bfloat16), preferred_element_type=jnp.float32)
    h_hi = jnp.dot(x, w1_ref[s, half:].astype(jnp.bfloat16), preferred_element_type=jnp.float32)
    h = h_lo + pltpu.roll(h_hi, n_sub - 1, 0) + pos_ref[s]
    return jnp.dot(jax.nn.gelu(h).astype(jnp.bfloat16), w2_ref[s].astype(jnp.bfloat16),
                   preferred_element_type=jnp.float32) + b2_ref[s]


def _compress_kernel(x_ref, w1_ref, pos_ref, w2_ref, b2_ref, o_ref):
    n_sub = o_ref.shape[3]
    x = jnp.concatenate([x_ref[0, pl.ds(s, n_sub, stride=CMP_STRIDE), :].astype(jnp.bfloat16)
                         for s in range(CMP_STRIDE)], axis=1)
    o_ref[0, 0, 0] = _compress_mlp(x, w1_ref, pos_ref, w2_ref, b2_ref, 0)


def _compress_paged_kernel(pt_ref, w1_ref, pos_ref, w2_ref, b2_ref, *rest, page_rows):
    page_refs, (o_ref, x_scr) = rest[:PAGES_PER_STEP], rest[PAGES_PER_STEP:]
    j = pl.program_id(1)
    planes = x_scr.shape[0]
    per_page = page_rows // CMP_STRIDE
    for k, page in enumerate(page_refs):
        row0 = pl.multiple_of((j * len(page_refs) + k) * per_page, per_page)
        for c in range(planes):
            for s in range(CMP_STRIDE):
                x_scr[c, pl.ds(row0, per_page), s * HEAD_DIM:(s + 1) * HEAD_DIM] = (
                    page[0, 0, pl.ds(s * planes + c, per_page, stride=CMP_STRIDE * planes), :])

    @pl.when(j == pl.num_programs(1) - 1)
    def _():
        for c in range(planes):
            s, g = divmod(c, NSA_KV)
            o_ref[0, s, g] = _compress_mlp(x_scr[c].astype(jnp.bfloat16), w1_ref, pos_ref, w2_ref, b2_ref, s)


def _compress(rows, w1, pos_emb, w2, b2, paged=None):
    w1 = w1.reshape(2, CMP_LEN * HEAD_DIM, HEAD_DIM)
    pos_bias = jnp.einsum('kld,klde->ke', pos_emb, w1.reshape(2, CMP_LEN, HEAD_DIM, HEAD_DIM))[:, None, :]
    params = dict(vmem_limit_bytes=VMEM_LIMIT_BYTES)
    if paged is None:
        b, seq = rows.shape[:2]
        n_sub = seq // CMP_STRIDE
        x = rows[:, :n_sub * CMP_STRIDE].reshape(b, n_sub * CMP_STRIDE, 2 * NSA_KV * HEAD_DIM)
        return pl.pallas_call(
            _compress_kernel,
            grid=(b, 2, NSA_KV),
            in_specs=[pl.BlockSpec((1, n_sub * CMP_STRIDE, HEAD_DIM), lambda i, s, g: (i, 0, s * NSA_KV + g)),
                      pl.BlockSpec((1, CMP_LEN * HEAD_DIM, HEAD_DIM), lambda i, s, g: (s, 0, 0)),
                      pl.BlockSpec((1, 1, HEAD_DIM), lambda i, s, g: (s, 0, 0)),
                      pl.BlockSpec((1, HEAD_DIM, HEAD_DIM), lambda i, s, g: (s, 0, 0)),
                      pl.BlockSpec((1, 1, HEAD_DIM), lambda i, s, g: (s, 0, 0))],
            out_specs=pl.BlockSpec((1, 1, 1, n_sub, HEAD_DIM), lambda i, s, g: (i, s, g, 0, 0)),
            out_shape=jax.ShapeDtypeStruct((b, 2, NSA_KV, n_sub, HEAD_DIM), jnp.float32),
            compiler_params=pltpu.CompilerParams(dimension_semantics=("arbitrary",) * 3, **params),
            name="nsa_compress",
        )(x, w1, pos_bias, w2, b2[:, None, :])

    pool, layer, page_table = paged
    b, n_pages = page_table.shape
    planes = 2 * NSA_KV
    page_rows = pool.shape[2] // planes
    n_sub = n_pages * page_rows // CMP_STRIDE
    const = lambda *shape: pl.BlockSpec(shape, lambda i, j, pt: (0,) * len(shape))
    return pl.pallas_call(
        functools.partial(_compress_paged_kernel, page_rows=page_rows),
        grid_spec=pltpu.PrefetchScalarGridSpec(
            num_scalar_prefetch=1,
            grid=(b, n_pages // PAGES_PER_STEP),
            in_specs=[const(2, CMP_LEN * HEAD_DIM, HEAD_DIM), const(2, 1, HEAD_DIM),
                      const(2, HEAD_DIM, HEAD_DIM), const(2, 1, HEAD_DIM)]
                     + _page_specs(pool, layer, n_pages),
            out_specs=pl.BlockSpec((1, 2, NSA_KV, n_sub, HEAD_DIM), lambda i, j, pt: (i, 0, 0, 0, 0)),
            scratch_shapes=[pltpu.VMEM((planes, n_sub, CMP_STRIDE * HEAD_DIM), jnp.float32)]),
        out_shape=jax.ShapeDtypeStruct((b, 2, NSA_KV, n_sub, HEAD_DIM), jnp.float32),
        compiler_params=pltpu.CompilerParams(dimension_semantics=("arbitrary",) * 2, **params),
        name="nsa_compress_paged",
    )(page_table, w1, pos_bias, w2, b2[:, None, :], *([pool] * PAGES_PER_STEP))


INT_MIN = -2 ** 31


def _count(pred):
    return jnp.sum(jnp.where(pred, 1.0, 0.0), axis=-1, keepdims=True)


def _dsa_body(q_ref, kv, iq_ref, iw_ref, ik, o_ref, *, qpos0, k_top):
    tq = q_ref.shape[1]
    lp = ik.shape[0]
    rows = slice(0, lp)
    qpos = qpos0 + lax.broadcasted_iota(jnp.int32, (tq, 1), 0)
    kpos = lax.broadcasted_iota(jnp.int32, (1, lp), 1)
    causal = kpos <= qpos

    iq = iq_ref[0].astype(jnp.bfloat16)
    iw = iw_ref[0]
    score = jnp.zeros((tq, lp), jnp.float32)
    for h in range(IDX_HEADS):
        logits = lax.dot_general(iq[:, h * IDX_DIM:(h + 1) * IDX_DIM], ik, _NT,
                                 preferred_element_type=jnp.float32)
        score = score + iw[:, h:h + 1] * jnp.maximum(logits, 0.0)
    score = jnp.where(causal, score, NEG)
    score = jnp.where(score == 0.0, 0.0, score)
    bits = lax.bitcast_convert_type(score, jnp.int32)
    key = bits ^ (jnp.right_shift(bits, 31) & 0x7FFFFFFF)

    k_f = float(k_top)
    thr = jnp.where(_count(key >= 0) >= k_f, 0, INT_MIN).astype(jnp.int32)

    def refine(i, thr):
        cand = thr | jnp.left_shift(jnp.int32(1), 30 - i)
        return jnp.where(_count(key >= cand) >= k_f, cand, thr)

    thr = lax.fori_loop(0, 31, refine, thr)
    above = key > thr
    tied = key == thr
    need = k_f - _count(above)

    def widen(i, c):
        cand = c | jnp.left_shift(jnp.int32(1), 14 - i)
        return jnp.where(_count(tied & (kpos < cand)) < need, cand, c)

    bound = lax.fori_loop(0, 15, widen, jnp.zeros((tq, 1), jnp.int32))
    mask = causal & (above | (tied & (kpos <= bound)))

    q = q_ref[0]
    for g in range(DSA_KV):
        o = _attend(_group_queries(q, g), kv(0, g, rows), kv(1, g, rows), mask)
        for h in range(HEADS_PER_GROUP):
            head = g * HEADS_PER_GROUP + h
            o_ref[0, :, head * HEAD_DIM:(head + 1) * HEAD_DIM] = o[h * tq:(h + 1) * tq]


def _dsa_kernel(q_ref, kv_ref, iq_ref, iw_ref, ik_ref, o_ref, *, pos0, k_top):
    tq = q_ref.shape[1]
    qpos0 = pos0 + pl.program_id(1) * tq
    _for_key_range(qpos0, tq, kv_ref.shape[1], lambda n: _dsa_body(
        q_ref, _kv_columns(kv_ref, DSA_KV), iq_ref, iw_ref, ik_ref[0, :n], o_ref, qpos0=qpos0, k_top=k_top))


def _dsa_paged_kernel(pt_ref, q_ref, iq_ref, iw_ref, new_kv_ref, new_ik_ref, *rest, pos0, k_top, page_rows):
    n = PAGES_PER_STEP
    kv_pages, ik_pages, (o_ref, kv_scr, ik_scr) = rest[:n], rest[n:2 * n], rest[2 * n:]
    _load_kv_pages(kv_pages, new_kv_ref, kv_scr, page_rows, pos0)
    j = pl.program_id(1)
    for k, page in enumerate(ik_pages):
        row0 = pl.multiple_of((j * n + k) * page_rows, page_rows)
        ik_scr[pl.ds(row0, page_rows), :] = page[0, 0].astype(jnp.bfloat16)

    @pl.when(j == 0)
    def _():
        fresh = new_ik_ref[0]
        ik_scr[pos0:, :] = jnp.concatenate(
            [fresh, jnp.zeros((ik_scr.shape[0] - pos0 - fresh.shape[0], IDX_DIM), jnp.float32)],
            axis=0).astype(jnp.bfloat16)

    @pl.when(j == pl.num_programs(1) - 1)
    def _():
        _dsa_body(q_ref, _kv_planes(kv_scr, DSA_KV), iq_ref, iw_ref, ik_scr[...], o_ref, qpos0=pos0, k_top=k_top)


def _dsa_attention(q, kv, idx_k, idx_q, idx_w, pos0, paged=None):
    b, t_len, _ = q.shape
    seq = kv.shape[1] + (pos0 if paged else 0)
    assert seq < 2 ** 15
    tq = Q_TILE if t_len % Q_TILE == 0 else _round_up(t_len, SUBLANES)
    tp = _round_up(t_len, tq)
    lp = _round_up(seq, LANES)
    static = dict(pos0=pos0, k_top=min(TOPK_MAX, seq // 4))
    width = DSA_HEADS * HEAD_DIM
    kv_w = 2 * DSA_KV * HEAD_DIM
    out_shape = jax.ShapeDtypeStruct((b, tp, width), jnp.float32)
    params = pltpu.CompilerParams(dimension_semantics=("arbitrary", "arbitrary"),
                                  vmem_limit_bytes=VMEM_LIMIT_BYTES)
    if paged is None:
        out = pl.pallas_call(
            functools.partial(_dsa_kernel, **static),
            grid=(b, tp // tq),
            in_specs=[pl.BlockSpec((1, tq, width), lambda i, j: (i, j, 0)),
                      pl.BlockSpec((1, lp, kv_w), lambda i, j: (i, 0, 0)),
                      pl.BlockSpec((1, tq, IDX_HEADS * IDX_DIM), lambda i, j: (i, j, 0)),
                      pl.BlockSpec((1, tq, IDX_HEADS), lambda i, j: (i, j, 0)),
                      pl.BlockSpec((1, lp, IDX_DIM), lambda i, j: (i, 0, 0))],
            out_specs=pl.BlockSpec((1, tq, width), lambda i, j: (i, j, 0)),
            out_shape=out_shape, compiler_params=params, name="dsa_attention",
        )(_pad_rows(q, tp), _pad_rows(kv.reshape(b, seq, -1), lp).astype(jnp.bfloat16),
          _pad_rows(idx_q, tp), _pad_rows(idx_w, tp), _pad_rows(idx_k, lp).astype(jnp.bfloat16))
        return out[:, :t_len]

    kv_pool, idx_pool, layer, page_table = paged
    assert tp == tq
    page_rows = idx_pool.shape[2]
    n_pages = page_table.shape[1]
    out = pl.pallas_call(
        functools.partial(_dsa_paged_kernel, page_rows=page_rows, **static),
        grid_spec=pltpu.PrefetchScalarGridSpec(
            num_scalar_prefetch=1,
            grid=(b, n_pages // PAGES_PER_STEP),
            in_specs=[pl.BlockSpec((1, tq, width), lambda i, j, pt: (i, 0, 0)),
                      pl.BlockSpec((1, tq, IDX_HEADS * IDX_DIM), lambda i, j, pt: (i, 0, 0)),
                      pl.BlockSpec((1, tq, IDX_HEADS), lambda i, j, pt: (i, 0, 0)),
                      pl.BlockSpec((1, tq, kv_w), lambda i, j, pt: (i, 0, 0)),
                      pl.BlockSpec((1, tq, IDX_DIM), lambda i, j, pt: (i, 0, 0))]
                     + _page_specs(kv_pool, layer, n_pages) + _page_specs(idx_pool, layer, n_pages),
            out_specs=pl.BlockSpec((1, tq, width), lambda i, j, pt: (i, 0, 0)),
            scratch_shapes=[pltpu.VMEM((2 * DSA_KV, lp, HEAD_DIM), jnp.bfloat16),
                            pltpu.VMEM((lp, IDX_DIM), jnp.bfloat16)]),
        out_shape=out_shape, compiler_params=params, name="dsa_attention_paged",
    )(page_table, _pad_rows(q, tp), _pad_rows(idx_q, tp), _pad_rows(idx_w, tp),
      _pad_rows(kv.reshape(b, t_len, -1), tq), _pad_rows(idx_k, tq),
      *([kv_pool] * PAGES_PER_STEP), *([idx_pool] * PAGES_PER_STEP))
    return out[:, :t_len]


def _split_points(sizes):
    pts, acc = [], 0
    for s in sizes[:-1]:
        acc += s
        pts.append(acc)
    return pts


def _rms_norm(x, g):
    xf = x.astype(jnp.float32)
    y = xf * lax.rsqrt(jnp.mean(xf * xf, axis=-1, keepdims=True) + NORM_EPS)
    return (y * g.astype(jnp.float32)).astype(x.dtype)


def _rope_partial(x, pos):
    rot = x.shape[-1] // ROT_DIV
    half = rot // 2
    freqs = ROPE_THETA ** (-jnp.arange(half, dtype=jnp.float32) / half)
    ang = pos.astype(jnp.float32)[:, None] * freqs[None, :]
    cos, sin = jnp.cos(ang)[:, None, :], jnp.sin(ang)[:, None, :]
    xr = x[..., :rot].astype(jnp.float32)
    x1, x2 = xr[..., :half], xr[..., half:]
    xr = jnp.concatenate([x1 * cos - x2 * sin, x1 * sin + x2 * cos], axis=-1)
    return jnp.concatenate([xr.astype(x.dtype), x[..., rot:]], axis=-1)


def _nsa_mixer(q, cmp_kv, slc_kv, win_kv, win_pos0, gate_logits, pos0, p, paged=None):
    if paged is None:
        cmp = _compress(cmp_kv, p['cmp_w1'], p['cmp_pos'], p['cmp_w2'], p['cmp_b2'])
    else:
        cmp = _compress(None, p['cmp_w1'], p['cmp_pos'], p['cmp_w2'], p['cmp_b2'], paged=(paged[0],) + paged[2:])
    n_sub = cmp.shape[3]
    c_pos = jnp.arange(n_sub, dtype=jnp.int32) * CMP_STRIDE + (CMP_LEN - 1)
    kc = jnp.swapaxes(_rope_partial(_rms_norm(jnp.swapaxes(cmp[:, 0], 1, 2), p['a_k_norm'][0]), c_pos), 1, 2)
    return _nsa_attention(q, kc, cmp[:, 1], slc_kv, win_kv, gate_logits, pos0, win_pos0, n_sub - 1,
                          paged=None if paged is None else paged[1:])


def _rwkv7_mixer(cur, prev, state0, p):
    b, T, _ = cur.shape
    xm = cur + (prev - cur) * p['rwkv_mu']
    r, k, v, w_lo, a_lo, g_lo = jnp.split(xm, [RWKV_DIM, 2 * RWKV_DIM, 3 * RWKV_DIM, 3 * RWKV_DIM + DECAY_LORA,
                                               3 * RWKV_DIM + DECAY_LORA + AAA_LORA], axis=-1)
    w = -jax.nn.softplus(-(p['rwkv_w0'] + jnp.tanh(w_lo) @ p['rwkv_w2'])) - 0.5
    a = jax.nn.sigmoid(p['rwkv_a0'] + a_lo @ p['rwkv_a2'])
    g = jax.nn.sigmoid(g_lo) @ p['rwkv_g2']
    heads = lambda t: t.reshape(b, T, RWKV_HEADS, RWKV_HEAD).astype(jnp.float32)
    kk = heads(k * p['rwkv_k_k'])
    kk = kk * lax.rsqrt(jnp.sum(kk * kk, axis=-1, keepdims=True) + 1e-12)
    kh = heads(k * (1.0 + (a - 1.0) * p['rwkv_k_a']))
    rh, vh, ah = heads(r), heads(v), heads(a)
    decay = jnp.exp(-jnp.exp(heads(w)))
    y, s_t = _rwkv_scan(state0.astype(jnp.float32), rh, decay, kh, vh, -kk, kk * ah)
    mu = jnp.mean(y, axis=-1, keepdims=True)
    var = jnp.mean(jnp.square(y - mu), axis=-1, keepdims=True)
    y = ((y - mu) * lax.rsqrt(var + RWKV_GN_EPS)).reshape(b, T, RWKV_DIM) * p['rwkv_ln_w'] + p['rwkv_ln_b']
    y = y + (jnp.sum(rh * kh * p['rwkv_r_k'], axis=-1, keepdims=True) * vh).reshape(b, T, RWKV_DIM)
    return (y * g).astype(cur.dtype), s_t.astype(state0.dtype)


def _hybrid_layer(x, p, pos0, past):
    b, T, _ = x.shape
    q_pos = pos0 + jnp.arange(T, dtype=jnp.int32)
    xn = _rms_norm(x, p['norm_attn'])
    c_lo, c_hi = sum(IN_SPLITS[:8]), sum(IN_SPLITS[:9])
    if past is None:
        pieces = jnp.split(_proj(xn, p['w_in']), _split_points(IN_SPLITS), axis=-1)
        c_cur = pieces[8]
        c_prev = jnp.concatenate([jnp.zeros_like(c_cur[:, :1]), c_cur[:, :-1]], axis=1)
    else:
        prev = past['shift'][:, None].astype(xn.dtype)
        full = _proj(jnp.concatenate([prev, xn], axis=1), p['w_in'])
        pieces = jnp.split(full[:, 1:], _split_points(IN_SPLITS), axis=-1)
        c_cur = pieces[8]
        c_prev = full[:, :-1, c_lo:c_hi]
    a_q, a_kv, a_gate, d_q, d_kv, i_q, i_k, i_w, _, merge = pieces

    q_a = _rope_partial(_rms_norm(a_q.reshape(b, T, NSA_HEADS, HEAD_DIM), p['a_q_norm']), q_pos)
    kv6 = a_kv.reshape(b, T, 6, NSA_KV, HEAD_DIM)
    cmp_rows = kv6[:, :, 0:2]
    slc_rows = jnp.stack([_rope_partial(_rms_norm(kv6[:, :, 2], p['a_k_norm'][1]), q_pos), kv6[:, :, 3]], axis=2)
    win_rows = jnp.stack([_rope_partial(_rms_norm(kv6[:, :, 4], p['a_k_norm'][2]), q_pos), kv6[:, :, 5]], axis=2)
    q_d = _rope_partial(_rms_norm(d_q.reshape(b, T, DSA_HEADS, HEAD_DIM), p['b_q_norm']), q_pos)
    kv2 = d_kv.reshape(b, T, 2, DSA_KV, HEAD_DIM)
    dsa_rows = jnp.stack([_rope_partial(_rms_norm(kv2[:, :, 0], p['b_k_norm']), q_pos), kv2[:, :, 1]], axis=2)
    idx_rows = _rope_partial(_rms_norm(i_k, p['idx_k_norm'])[:, :, None], q_pos)[:, :, 0]
    idx_q = _rope_partial(i_q.reshape(b, T, IDX_HEADS, IDX_DIM), q_pos)
    idx_w = i_w * (IDX_HEADS * IDX_DIM) ** -0.5

    q_a, q_d, idx_q = q_a.reshape(b, T, -1), q_d.reshape(b, T, -1), idx_q.reshape(b, T, -1)
    if past is None:
        win_all = win_rows
        n_keep = min(WINDOW, T)
        win_pos0 = pos0
        rwkv0 = jnp.zeros((b, RWKV_HEADS, RWKV_HEAD, RWKV_HEAD), x.dtype)
        o_a = _nsa_mixer(q_a, cmp_rows, slc_rows, win_all, win_pos0, a_gate, pos0, p)
        o_b = _dsa_attention(q_d, dsa_rows, idx_rows, idx_q, idx_w, pos0)
    else:
        assert (pos0 + T) // CMP_STRIDE * CMP_STRIDE <= pos0
        win_all = jnp.concatenate([past['nsa_win'], win_rows], axis=1)
        n_keep = past['nsa_win'].shape[1]
        win_pos0 = pos0 - n_keep
        rwkv0 = past['rwkv']
        layer, page_table = past['layer'], past['page_table']
        o_a = _nsa_mixer(q_a, None, slc_rows, win_all, win_pos0, a_gate, pos0, p,
                         paged=(past['cmp_pool'], past['slc_pool'], layer, page_table))
        o_b = _dsa_attention(q_d, dsa_rows, idx_rows, idx_q, idx_w, pos0,
                             paged=(past['dsa_pool'], past['idx_pool'], layer, page_table))
    o_c, rwkv_new = _rwkv7_mixer(c_cur, c_prev, rwkv0, p)

    g_a, g_b, g_c = jnp.split(jax.nn.sigmoid(merge), 3, axis=-1)
    mixed = g_a * _proj(o_a, p['p_a']) + g_b * _proj(o_b, p['p_b']) + g_c * _proj(o_c, p['p_c'])
    x = x + _proj(mixed, p['w_o'])
    h = _rms_norm(x, p['norm_ffn'])
    gate, up = jnp.split(_proj(h, p['w_ffn_in']), 2, axis=-1)
    x = x + _proj(jax.nn.silu(gate) * up, p['w_ffn_out'])
    new_state = {'nsa_cmp': cmp_rows, 'nsa_slc': slc_rows, 'dsa_kv': dsa_rows, 'dsa_idx': idx_rows,
                 'nsa_win': win_all[:, -n_keep:], 'rwkv': rwkv_new, 'shift': xn[:, -1]}
    return x, new_state


_PARAM_NAMES = ('norm_attn', 'w_in', 'a_q_norm', 'a_k_norm', 'cmp_w1', 'cmp_pos', 'cmp_w2', 'cmp_b2',
                'b_q_norm', 'b_k_norm', 'idx_k_norm', 'rwkv_mu', 'rwkv_w0', 'rwkv_w2', 'rwkv_a0',
                'rwkv_a2', 'rwkv_g2', 'rwkv_k_k', 'rwkv_k_a', 'rwkv_r_k', 'rwkv_ln_w', 'rwkv_ln_b',
                'p_a', 'p_b', 'p_c', 'w_o', 'norm_ffn', 'w_ffn_in', 'w_ffn_out')


def kernel(x_prompt, x_sample, cache_nsa_cmp, cache_nsa_slc, cache_dsa_kv, cache_dsa_idx, state_nsa_win, state_rwkv, state_shift, page_table, norm_attn, w_in, a_q_norm, a_k_norm, cmp_w1, cmp_pos, cmp_w2, cmp_b2, b_q_norm, b_k_norm, idx_k_norm, rwkv_mu, rwkv_w0, rwkv_w2, rwkv_a0, rwkv_a2, rwkv_g2, rwkv_k_k, rwkv_k_a, rwkv_r_k, rwkv_ln_w, rwkv_ln_b, p_a, p_b, p_c, w_o, norm_ffn, w_ffn_in, w_ffn_out):
    weights = dict(zip(_PARAM_NAMES, (norm_attn, w_in, a_q_norm, a_k_norm, cmp_w1, cmp_pos, cmp_w2, cmp_b2,
                                      b_q_norm, b_k_norm, idx_k_norm, rwkv_mu, rwkv_w0, rwkv_w2, rwkv_a0,
                                      rwkv_a2, rwkv_g2, rwkv_k_k, rwkv_k_a, rwkv_r_k, rwkv_ln_w, rwkv_ln_b,
                                      p_a, p_b, p_c, w_o, norm_ffn, w_ffn_in, w_ffn_out)))
    depth = w_in.shape[0]
    past_len = page_table.shape[1] * cache_nsa_cmp.shape[2]
    yp, ys = x_prompt, x_sample
    st_p, st_s = [], []
    kv_pool = lambda c: c.reshape(c.shape[:2] + (-1, HEAD_DIM))
    for l in range(depth):
        p = {name: w[l] for name, w in weights.items()}
        past = {'cmp_pool': kv_pool(cache_nsa_cmp), 'slc_pool': kv_pool(cache_nsa_slc),
                'dsa_pool': kv_pool(cache_dsa_kv), 'idx_pool': cache_dsa_idx, 'layer': l, 'page_table': page_table,
                'nsa_win': state_nsa_win[l], 'rwkv': state_rwkv[l], 'shift': state_shift[l]}
        yp, sp = _hybrid_layer(yp, p, 0, None)
        ys, ss = _hybrid_layer(ys, p, past_len, past)
        st_p.append(sp)
        st_s.append(ss)
    stack = lambda sts, name: jnp.stack([s[name] for s in sts])
    return (yp, ys,
            stack(st_p, 'nsa_cmp'), stack(st_s, 'nsa_cmp'),
            stack(st_p, 'nsa_slc'), stack(st_s, 'nsa_slc'),
            stack(st_p, 'dsa_kv'), stack(st_s, 'dsa_kv'),
            stack(st_p, 'dsa_idx'), stack(st_s, 'dsa_idx'),
            stack(st_p, 'nsa_win'), stack(st_s, 'nsa_win'),
            stack(st_p, 'rwkv'), stack(st_s, 'rwkv'),
            stack(st_p, 'shift'), stack(st_s, 'shift'))
```

```python
import functools
import math

import jax
import jax.numpy as jnp
from jax import lax
from jax.experimental import pallas as pl
from jax.experimental.pallas import tpu as pltpu

D_MODEL = 2048
HEAD_DIM = 128
ROT_DIV = 4
ROPE_THETA = 500000.0
NORM_EPS = 1e-6

NSA_HEADS = 8
NSA_KV = 2
CMP_STRIDE = 16
CMP_LEN = 2 * CMP_STRIDE
SLC_BLOCK = 64
N_SEL = 8
N_LOCAL = 2
WINDOW = 512

DSA_HEADS = 8
DSA_KV = 2
IDX_HEADS = 8
IDX_DIM = 64
TOPK_MAX = 256

RWKV_HEADS = 16
RWKV_HEAD = 64
RWKV_DIM = RWKV_HEADS * RWKV_HEAD
DECAY_LORA = 64
AAA_LORA = 64
GATE_LORA = 160
RWKV_GN_EPS = 64e-5

NEG = -1e30
FORCE = 1e9
LOWEST = -3e38

IN_SPLITS = (
    NSA_HEADS * HEAD_DIM,
    6 * NSA_KV * HEAD_DIM,
    3 * NSA_HEADS,
    DSA_HEADS * HEAD_DIM,
    2 * DSA_KV * HEAD_DIM,
    IDX_HEADS * IDX_DIM,
    IDX_DIM,
    IDX_HEADS,
    3 * RWKV_DIM + DECAY_LORA + AAA_LORA + GATE_LORA,
    3 * D_MODEL,
)

LANES = 128
SUBLANES = 8
VMEM_LIMIT_BYTES = 56 * 1024 * 1024

HEADS_PER_GROUP = NSA_HEADS // NSA_KV
ATTN_SCALE = HEAD_DIM ** -0.5
Q_TILE = 128

_NT = (((1,), (1,)), ((), ()))


def _round_up(n, m):
    return -(-n // m) * m


def _mm_kernel(x_ref, w_ref, *rest):
    r_ref = rest[0] if len(rest) == 3 else None
    o_ref, xb_ref = rest[-2:]

    @pl.when(pl.program_id(1) == 0)
    def _():
        xb_ref[...] = x_ref[...].astype(jnp.bfloat16)

    acc = jnp.dot(xb_ref[...], w_ref[0], preferred_element_type=jnp.float32)
    o_ref[...] = acc if r_ref is None else r_ref[...] + acc


def _pick_tile(n, prefs):
    for t in prefs:
        if n % t == 0:
            return t
    return n


def _mm(x, w, layer, residual=None):
    m, k = x.shape
    n = w.shape[2]
    tm = _pick_tile(m, (1024, 512, 256, 128))
    tn = _pick_tile(n, (512, 384, 256, 128))
    tile = pl.BlockSpec((tm, tn), lambda i, j: (i, j))
    return pl.pallas_call(
        _mm_kernel,
        grid=(m // tm, n // tn),
        in_specs=[pl.BlockSpec((tm, k), lambda i, j: (i, 0)),
                  pl.BlockSpec((1, k, tn), lambda i, j: (layer, 0, j))] + ([] if residual is None else [tile]),
        out_specs=tile,
        out_shape=jax.ShapeDtypeStruct((m, n), jnp.float32),
        scratch_shapes=[pltpu.VMEM((tm, k), jnp.bfloat16)],
        compiler_params=pltpu.CompilerParams(
            dimension_semantics=("arbitrary", "arbitrary"),
            vmem_limit_bytes=VMEM_LIMIT_BYTES),
        name="dense_proj",
    )(x, w, *([] if residual is None else [residual]))


def _proj(x, w, layer, residual=None):
    lead = x.shape[:-1]
    res = None if residual is None else residual.reshape(-1, residual.shape[-1])
    return _mm(x.reshape(-1, x.shape[-1]), w, layer, res).reshape(lead + (w.shape[2],))


FFN_TILE = 512


def _ffn_kernel(x_ref, gain_ref, wg_ref, wu_ref, wo_ref, o_ref, hb_ref, acc_ref):
    f = pl.program_id(1)

    @pl.when(f == 0)
    def _():
        x = x_ref[...]
        y = x * lax.rsqrt(jnp.mean(x * x, axis=-1, keepdims=True) + NORM_EPS)
        hb_ref[...] = (y * gain_ref[...]).astype(jnp.bfloat16)
        acc_ref[...] = jnp.zeros_like(acc_ref)

    hb = hb_ref[...]
    gate = jnp.dot(hb, wg_ref[0], preferred_element_type=jnp.float32)
    up = jnp.dot(hb, wu_ref[0], preferred_element_type=jnp.float32)
    act = (jax.nn.silu(gate) * up).astype(jnp.bfloat16)
    acc_ref[...] += jnp.dot(act, wo_ref[0], preferred_element_type=jnp.float32)

    @pl.when(f == pl.num_programs(1) - 1)
    def _():
        o_ref[...] = x_ref[...] + acc_ref[...]


def _ffn(x, gain, w_in, w_out, layer):
    lead = x.shape[:-1]
    x2 = x.reshape(-1, x.shape[-1])
    m, d = x2.shape
    n_f = w_out.shape[1] // FFN_TILE
    tm = _pick_tile(m, (512, 256, 128))
    out = pl.pallas_call(
        _ffn_kernel,
        grid=(m // tm, n_f),
        in_specs=[pl.BlockSpec((tm, d), lambda i, f: (i, 0)),
                  pl.BlockSpec((1, d), lambda i, f: (0, 0)),
                  pl.BlockSpec((1, d, FFN_TILE), lambda i, f: (layer, 0, f)),
                  pl.BlockSpec((1, d, FFN_TILE), lambda i, f: (layer, 0, n_f + f)),
                  pl.BlockSpec((1, FFN_TILE, d), lambda i, f: (layer, f, 0))],
        out_specs=pl.BlockSpec((tm, d), lambda i, f: (i, 0)),
        out_shape=jax.ShapeDtypeStruct((m, d), jnp.float32),
        scratch_shapes=[pltpu.VMEM((tm, d), jnp.bfloat16), pltpu.VMEM((tm, d), jnp.float32)],
        compiler_params=pltpu.CompilerParams(
            dimension_semantics=("arbitrary", "arbitrary"),
            vmem_limit_bytes=VMEM_LIMIT_BYTES),
        name="ffn_block",
    )(x2, gain[None, :], w_in, w_in, w_out)
    return out.reshape(lead + (d,))


RWKV_HEADS_PER_BLOCK = LANES // 2
RWKV_V_HALF = RWKV_HEAD // 2


def _rwkv_scan_kernel(r_ref, w_ref, k_ref, a_ref, b_ref, v_ref, s0_ref, y_ref, s_out_ref, s_ref, rows_ref):
    c = pl.program_id(1)

    @pl.when(c == 0)
    def _():
        s_ref[...] = s0_ref[0]

    steps = r_ref.shape[1]
    low_half = lax.broadcasted_iota(jnp.int32, (RWKV_V_HALF, LANES), 1) < RWKV_HEADS_PER_BLOCK

    def step(t, carry):
        for n, ref in enumerate((r_ref, w_ref, k_ref, a_ref, b_ref)):
            tile = ref[0, t]
            swapped = pltpu.roll(tile, RWKV_HEADS_PER_BLOCK, 1)
            rows_ref[n, :RWKV_V_HALF] = jnp.where(low_half, tile, swapped)
            rows_ref[n, RWKV_V_HALF:] = jnp.where(low_half, swapped, tile)
        parts = [jnp.zeros((RWKV_V_HALF, LANES), jnp.float32) for _ in range(4)]
        for k in range(RWKV_HEAD):
            parts[k % 4] = parts[k % 4] + s_ref[k] * rows_ref[3, k:k + 1, :]
        sa = (parts[0] + parts[1]) + (parts[2] + parts[3])
        vt = v_ref[0, t]
        ys = [jnp.zeros((RWKV_V_HALF, LANES), jnp.float32) for _ in range(4)]
        for k in range(RWKV_HEAD):
            s_new = (s_ref[k] * rows_ref[1, k:k + 1, :] + sa * rows_ref[4, k:k + 1, :]
                     + vt * rows_ref[2, k:k + 1, :])
            s_ref[k] = s_new
            ys[k % 4] = ys[k % 4] + s_new * rows_ref[0, k:k + 1, :]
        y_ref[0, t] = (ys[0] + ys[1]) + (ys[2] + ys[3])
        return carry

    lax.fori_loop(0, steps, step, 0)

    @pl.when(c == pl.num_programs(1) - 1)
    def _():
        s_out_ref[0] = s_ref[...]


def _rwkv_scan(state0, r, w, k, v, a, b):
    bsz, t_len, n_h, n = r.shape
    heads = bsz * n_h
    hb = RWKV_HEADS_PER_BLOCK
    nblk = heads // hb
    tc = _pick_tile(t_len, (128, 64, 32, 16, 8, 4))

    def pack(x):
        x = jnp.transpose(x.reshape(bsz, t_len, n_h, 2, RWKV_V_HALF), (1, 4, 3, 0, 2))
        x = x.reshape(t_len, RWKV_V_HALF, 2, nblk, hb)
        return jnp.transpose(x, (3, 0, 1, 2, 4)).reshape(nblk, t_len, RWKV_V_HALF, LANES)

    s0 = state0.reshape(nblk, hb, 2, RWKV_V_HALF, n)
    s0 = jnp.transpose(s0, (0, 4, 3, 2, 1)).reshape(nblk, n, RWKV_V_HALF, LANES)

    val_spec = pl.BlockSpec((1, tc, RWKV_V_HALF, LANES), lambda i, c: (i, c, 0, 0))
    st_spec = pl.BlockSpec((1, n, RWKV_V_HALF, LANES), lambda i, c: (i, 0, 0, 0))
    y, s_out = pl.pallas_call(
        _rwkv_scan_kernel,
        grid=(nblk, t_len // tc),
        in_specs=[val_spec] * 6 + [st_spec],
        out_specs=[val_spec, st_spec],
        out_shape=[jax.ShapeDtypeStruct((nblk, t_len, RWKV_V_HALF, LANES), jnp.float32),
                   jax.ShapeDtypeStruct((nblk, n, RWKV_V_HALF, LANES), jnp.float32)],
        scratch_shapes=[pltpu.VMEM((n, RWKV_V_HALF, LANES), jnp.float32),
                        pltpu.VMEM((5, n, LANES), jnp.float32)],
        compiler_params=pltpu.CompilerParams(
            dimension_semantics=("arbitrary", "arbitrary"),
            vmem_limit_bytes=VMEM_LIMIT_BYTES),
        name="rwkv7_scan",
    )(pack(r), pack(w), pack(k), pack(a), pack(b), pack(v), s0)

    y = y.reshape(nblk, t_len, RWKV_V_HALF, 2, hb)
    y = jnp.transpose(y, (0, 4, 1, 3, 2)).reshape(bsz, n_h, t_len, n)
    y = jnp.transpose(y, (0, 2, 1, 3))
    s_out = s_out.reshape(nblk, n, RWKV_V_HALF, 2, hb)
    s_out = jnp.transpose(s_out, (0, 4, 3, 2, 1)).reshape(bsz, n_h, n, n)
    return y, s_out


def _group_queries(q, g):
    h0 = g * HEADS_PER_GROUP
    return jnp.concatenate([q[:, (h0 + h) * HEAD_DIM:(h0 + h + 1) * HEAD_DIM] for h in range(HEADS_PER_GROUP)],
                           axis=0).astype(jnp.bfloat16)


def _attend(qg, k, v, mask, want_psum=False):
    tq = mask.shape[0]
    s = lax.dot_general(qg, k, _NT, preferred_element_type=jnp.float32) * ATTN_SCALE
    ps = []
    for h in range(HEADS_PER_GROUP):
        sh = jnp.where(mask, s[h * tq:(h + 1) * tq], NEG)
        e = jnp.where(mask, jnp.exp(sh - jnp.max(sh, axis=-1, keepdims=True)), 0.0)
        den = jnp.sum(e, axis=-1, keepdims=True)
        ps.append(e / jnp.where(den > 0.0, den, 1.0))
    o = jnp.dot(jnp.concatenate(ps, axis=0).astype(jnp.bfloat16), v, preferred_element_type=jnp.float32)
    if want_psum:
        return o, (ps[0] + ps[1]) + (ps[2] + ps[3])
    return o


def _kv_columns(ref, n_kv):
    def get(s, g, rows):
        c0 = (s * n_kv + g) * HEAD_DIM
        return ref[0, rows, c0:c0 + HEAD_DIM]
    return get


def _kv_planes(ref, n_kv):
    return lambda s, g, rows: ref[s * n_kv + g, rows, :]


KEY_GROUP = 512


def _for_key_range(qpos0, tq, n_keys, body):
    lengths = list(range(KEY_GROUP, n_keys, KEY_GROUP)) + [n_keys]
    group = (qpos0 + tq - 1) // KEY_GROUP
    for i, n in enumerate(lengths):
        @pl.when((group == i) if i + 1 < len(lengths) else (group >= i))
        def _(n=n):
            body(n)


PAGES_PER_STEP = 8


def _page_specs(pool, layer, pages_per_seq):
    assert pages_per_seq % PAGES_PER_STEP == 0
    block = (1, 1) + pool.shape[2:]
    return [pl.BlockSpec(block, functools.partial(
        lambda i, j, pt, k: (layer, pt[i, j * PAGES_PER_STEP + k], 0, 0), k=k)) for k in range(PAGES_PER_STEP)]


def _load_kv_pages(page_refs, new_ref, kv_scr, page_rows, past_len):
    j = pl.program_id(1)
    planes = kv_scr.shape[0]
    for k, page in enumerate(page_refs):
        row0 = pl.multiple_of((j * len(page_refs) + k) * page_rows, page_rows)
        for c in range(planes):
            kv_scr[c, pl.ds(row0, page_rows), :] = (
                page[0, 0, pl.ds(c, page_rows, stride=planes), :].astype(jnp.bfloat16))

    @pl.when(j == 0)
    def _():
        for c in range(planes):
            kv_scr[c, past_len:, :] = new_ref[0, :, c * HEAD_DIM:(c + 1) * HEAD_DIM].astype(jnp.bfloat16)


def _pad_rows(x, n):
    return jnp.pad(x, ((0, 0), (0, n - x.shape[1])) + ((0, 0),) * (x.ndim - 2))


def _nsa_body(q_ref, kc_ref, vc_ref, slc, slc_len, win_ref, gate_ref, cover_ref, expand_ref, o_ref,
              *, qpos0, win_pos0, n_slc, n_sel):
    tq = q_ref.shape[1]
    qpos = qpos0 + lax.broadcasted_iota(jnp.int32, (tq, 1), 0)
    q = q_ref[0]
    gates = jax.nn.sigmoid(gate_ref[0])
    win = _kv_columns(win_ref, NSA_KV)
    win_len = min(win_ref.shape[1], _round_up(WINDOW + tq, LANES))
    win0 = jnp.clip(qpos0 - win_pos0 - WINDOW, 0, win_ref.shape[1] - win_len)
    win_rows = pl.ds(pl.multiple_of(win0, LANES), win_len)
    slc_rows = slice(0, slc_len)

    cpos = lax.broadcasted_iota(jnp.int32, (1, kc_ref.shape[2]), 1) * CMP_STRIDE + (CMP_LEN - 1)
    cmask = cpos <= qpos
    kpos = lax.broadcasted_iota(jnp.int32, (1, slc_len), 1)
    causal = kpos <= qpos
    wpos = win_pos0 + win0 + lax.broadcasted_iota(jnp.int32, (1, win_len), 1)
    wmask = (wpos <= qpos) & (qpos - wpos < WINDOW)
    blk = lax.broadcasted_iota(jnp.int32, (1, cover_ref.shape[1]), 1)
    dist = jnp.right_shift(qpos, int(math.log2(SLC_BLOCK))) - blk
    forced = (blk == 0) | ((dist >= 0) & (dist < N_LOCAL))
    cover = cover_ref[...]

    for g in range(NSA_KV):
        qg = _group_queries(q, g)
        o_cmp, p_sum = _attend(qg, kc_ref[0, g], vc_ref[0, g], cmask, want_psum=True)
        p_hi = p_sum.astype(jnp.bfloat16)
        p_lo = (p_sum - p_hi.astype(jnp.float32)).astype(jnp.bfloat16)
        imp = (jnp.dot(p_hi, cover, preferred_element_type=jnp.float32)
               + jnp.dot(p_lo, cover, preferred_element_type=jnp.float32))
        score = jnp.where(forced, FORCE, jnp.where(dist < 0, -FORCE, imp))
        score = jnp.where(blk < n_slc, score, LOWEST)
        sel = jnp.zeros(score.shape, jnp.float32)
        for _ in range(n_sel):
            best = jnp.max(score, axis=-1, keepdims=True)
            first = jnp.min(jnp.where(score == best, blk, cover_ref.shape[1]), axis=-1, keepdims=True)
            hit = blk == first
            sel = jnp.where(hit, 1.0, sel)
            score = jnp.where(hit, LOWEST, score)
        smask = jnp.dot(sel.astype(jnp.bfloat16), expand_ref[:, :slc_len],
                        preferred_element_type=jnp.float32) > 0.5
        o_slc = _attend(qg, slc(0, g, slc_rows), slc(1, g, slc_rows), smask & causal)
        o_win = _attend(qg, win(0, g, win_rows), win(1, g, win_rows), wmask)
        for h in range(HEADS_PER_GROUP):
            head = g * HEADS_PER_GROUP + h
            rows = slice(h * tq, (h + 1) * tq)
            o_ref[0, :, head * HEAD_DIM:(head + 1) * HEAD_DIM] = (
                gates[:, 3 * head:3 * head + 1] * o_cmp[rows]
                + gates[:, 3 * head + 1:3 * head + 2] * o_slc[rows]
                + gates[:, 3 * head + 2:3 * head + 3] * o_win[rows])


def _nsa_kernel(q_ref, kc_ref, vc_ref, slc_ref, win_ref, gate_ref, cover_ref, expand_ref, o_ref,
                *, pos0, win_pos0, n_slc, n_sel):
    tq = q_ref.shape[1]
    qpos0 = pos0 + pl.program_id(1) * tq
    _for_key_range(qpos0, tq, slc_ref.shape[1], lambda n: _nsa_body(
        q_ref, kc_ref, vc_ref, _kv_columns(slc_ref, NSA_KV), n, win_ref, gate_ref, cover_ref, expand_ref, o_ref,
        qpos0=qpos0, win_pos0=win_pos0, n_slc=n_slc, n_sel=n_sel))


def _nsa_paged_kernel(pt_ref, q_ref, kc_ref, vc_ref, new_ref, win_ref, gate_ref, cover_ref, expand_ref, *rest,
                      pos0, win_pos0, n_slc, n_sel, page_rows):
    page_refs, (o_ref, kv_scr) = rest[:PAGES_PER_STEP], rest[PAGES_PER_STEP:]
    _load_kv_pages(page_refs, new_ref, kv_scr, page_rows, pos0)

    @pl.when(pl.program_id(1) == pl.num_programs(1) - 1)
    def _():
        _nsa_body(q_ref, kc_ref, vc_ref, _kv_planes(kv_scr, NSA_KV), kv_scr.shape[1], win_ref, gate_ref,
                  cover_ref, expand_ref, o_ref, qpos0=pos0, win_pos0=win_pos0, n_slc=n_slc, n_sel=n_sel)


def _nsa_attention(q, kc, vc, slc_kv, win_kv, gate_logits, pos0, win_pos0, n_c, paged=None):
    b, t_len, _ = q.shape
    seq = slc_kv.shape[1] + (pos0 if paged else 0)
    tq = Q_TILE if t_len % Q_TILE == 0 else _round_up(t_len, SUBLANES)
    tp = _round_up(t_len, tq)
    lp = _round_up(seq, LANES)
    lw = _round_up(win_kv.shape[1], LANES)
    n_cp = kc.shape[2]
    n_slc = -(-seq // SLC_BLOCK)
    n_sp = _round_up(n_slc, LANES)
    win = _pad_rows(win_kv.reshape(b, win_kv.shape[1], -1), lw).astype(jnp.bfloat16)
    start = jnp.arange(n_cp)[:, None] * CMP_STRIDE
    lo = jnp.arange(n_sp)[None, :] * SLC_BLOCK
    cover = ((start < lo + SLC_BLOCK) & (start + CMP_LEN > lo) & (jnp.arange(n_cp)[:, None] < n_c)
             & (jnp.arange(n_sp)[None, :] < n_slc)).astype(jnp.bfloat16)
    expand = (jnp.arange(lp)[None, :] // SLC_BLOCK == jnp.arange(n_sp)[:, None]).astype(jnp.bfloat16)
    static = dict(pos0=pos0, win_pos0=win_pos0, n_slc=n_slc, n_sel=min(N_SEL, n_slc))
    width = NSA_HEADS * HEAD_DIM
    kv_w = 2 * NSA_KV * HEAD_DIM
    out_shape = jax.ShapeDtypeStruct((b, tp, width), jnp.float32)
    params = pltpu.CompilerParams(dimension_semantics=("arbitrary", "arbitrary"),
                                  vmem_limit_bytes=VMEM_LIMIT_BYTES)
    if paged is None:
        slc = _pad_rows(slc_kv.reshape(b, seq, -1), lp).astype(jnp.bfloat16)
        out = pl.pallas_call(
            functools.partial(_nsa_kernel, **static),
            grid=(b, tp // tq),
            in_specs=[pl.BlockSpec((1, tq, width), lambda i, j: (i, j, 0)),
                      pl.BlockSpec((1, NSA_KV, n_cp, HEAD_DIM), lambda i, j: (i, 0, 0, 0)),
                      pl.BlockSpec((1, NSA_KV, n_cp, HEAD_DIM), lambda i, j: (i, 0, 0, 0)),
                      pl.BlockSpec((1, lp, kv_w), lambda i, j: (i, 0, 0)),
                      pl.BlockSpec((1, lw, kv_w), lambda i, j: (i, 0, 0)),
                      pl.BlockSpec((1, tq, 3 * NSA_HEADS), lambda i, j: (i, j, 0)),
                      pl.BlockSpec((n_cp, n_sp), lambda i, j: (0, 0)),
                      pl.BlockSpec((n_sp, lp), lambda i, j: (0, 0))],
            out_specs=pl.BlockSpec((1, tq, width), lambda i, j: (i, j, 0)),
            out_shape=out_shape, compiler_params=params, name="nsa_attention",
        )(_pad_rows(q, tp), kc.astype(jnp.bfloat16), vc.astype(jnp.bfloat16), slc, win,
          _pad_rows(gate_logits, tp), cover, expand)
        return out[:, :t_len]

    pool, layer, page_table = paged
    assert tp == tq
    page_rows = pool.shape[2] // (2 * NSA_KV)
    new_rows = _pad_rows(slc_kv.reshape(b, t_len, -1), lp - pos0)
    out = pl.pallas_call(
        functools.partial(_nsa_paged_kernel, page_rows=page_rows, **static),
        grid_spec=pltpu.PrefetchScalarGridSpec(
            num_scalar_prefetch=1,
            grid=(b, page_table.shape[1] // PAGES_PER_STEP),
            in_specs=[pl.BlockSpec((1, tq, width), lambda i, j, pt: (i, 0, 0)),
                      pl.BlockSpec((1, NSA_KV, n_cp, HEAD_DIM), lambda i, j, pt: (i, 0, 0, 0)),
                      pl.BlockSpec((1, NSA_KV, n_cp, HEAD_DIM), lambda i, j, pt: (i, 0, 0, 0)),
                      pl.BlockSpec((1, lp - pos0, kv_w), lambda i, j, pt: (i, 0, 0)),
                      pl.BlockSpec((1, lw, kv_w), lambda i, j, pt: (i, 0, 0)),
                      pl.BlockSpec((1, tq, 3 * NSA_HEADS), lambda i, j, pt: (i, 0, 0)),
                      pl.BlockSpec((n_cp, n_sp), lambda i, j, pt: (0, 0)),
                      pl.BlockSpec((n_sp, lp), lambda i, j, pt: (0, 0))]
                     + _page_specs(pool, layer, page_table.shape[1]),
            out_specs=pl.BlockSpec((1, tq, width), lambda i, j, pt: (i, 0, 0)),
            scratch_shapes=[pltpu.VMEM((2 * NSA_KV, lp, HEAD_DIM), jnp.bfloat16)]),
        out_shape=out_shape, compiler_params=params, name="nsa_attention_paged",
    )(page_table, _pad_rows(q, tp), kc.astype(jnp.bfloat16), vc.astype(jnp.bfloat16), new_rows, win,
      _pad_rows(gate_logits, tp), cover, expand, *([pool] * PAGES_PER_STEP))
    return out[:, :t_len]


def _compress_mlp(x, w1_ref, pos_ref, w2_ref, b2_ref, s):
    n_sub = x.shape[0]
    half = CMP_STRIDE * HEAD_DIM
    h_lo = jnp.dot(x, w1_ref[s, :half].astype(jnp.bfloat16), preferred_element_type=jnp.float32)
    h_hi = jnp.dot(x, w1_ref[s, half:].astype(jnp.bfloat16), preferred_element_type=jnp.float32)
    h = h_lo + pltpu.roll(h_hi, n_sub - 1, 0) + pos_ref[s]
    return jnp.dot(jax.nn.gelu(h).astype(jnp.bfloat16), w2_ref[s].astype(jnp.bfloat16),
                   preferred_element_type=jnp.float32) + b2_ref[s]


def _compress_kernel(x_ref, w1_ref, pos_ref, w2_ref, b2_ref, o_ref):
    n_sub = o_ref.shape[3]
    x = jnp.concatenate([x_ref[0, pl.ds(s, n_sub, stride=CMP_STRIDE), :].astype(jnp.bfloat16)
                         for s in range(CMP_STRIDE)], axis=1)
    o_ref[0, 0, 0] = _compress_mlp(x, w1_ref, pos_ref, w2_ref, b2_ref, 0)


def _compress_paged_kernel(pt_ref, w1_ref, pos_ref, w2_ref, b2_ref, *rest, page_rows):
    page_refs, (o_ref, x_scr) = rest[:PAGES_PER_STEP], rest[PAGES_PER_STEP:]
    j = pl.program_id(1)
    planes = x_scr.shape[0]
    per_page = page_rows // CMP_STRIDE
    for k, page in enumerate(page_refs):
        row0 = pl.multiple_of((j * len(page_refs) + k) * per_page, per_page)
        for c in range(planes):
            for s in range(CMP_STRIDE):
                x_scr[c, pl.ds(row0, per_page), s * HEAD_DIM:(s + 1) * HEAD_DIM] = (
                    page[0, 0, pl.ds(s * planes + c, per_page, stride=CMP_STRIDE * planes), :])

    @pl.when(j == pl.num_programs(1) - 1)
    def _():
        for c in range(planes):
            s, g = divmod(c, NSA_KV)
            o_ref[0, s, g] = _compress_mlp(x_scr[c].astype(jnp.bfloat16), w1_ref, pos_ref, w2_ref, b2_ref, s)


def _compress(rows, w1, pos_emb, w2, b2, paged=None):
    w1 = w1.reshape(2, CMP_LEN * HEAD_DIM, HEAD_DIM)
    pos_bias = jnp.einsum('kld,klde->ke', pos_emb, w1.reshape(2, CMP_LEN, HEAD_DIM, HEAD_DIM))[:, None, :]
    params = dict(vmem_limit_bytes=VMEM_LIMIT_BYTES)
    if paged is None:
        b, seq = rows.shape[:2]
        n_sub = seq // CMP_STRIDE
        x = rows[:, :n_sub * CMP_STRIDE].reshape(b, n_sub * CMP_STRIDE, 2 * NSA_KV * HEAD_DIM)
        return pl.pallas_call(
            _compress_kernel,
            grid=(b, 2, NSA_KV),
            in_specs=[pl.BlockSpec((1, n_sub * CMP_STRIDE, HEAD_DIM), lambda i, s, g: (i, 0, s * NSA_KV + g)),
                      pl.BlockSpec((1, CMP_LEN * HEAD_DIM, HEAD_DIM), lambda i, s, g: (s, 0, 0)),
                      pl.BlockSpec((1, 1, HEAD_DIM), lambda i, s, g: (s, 0, 0)),
                      pl.BlockSpec((1, HEAD_DIM, HEAD_DIM), lambda i, s, g: (s, 0, 0)),
                      pl.BlockSpec((1, 1, HEAD_DIM), lambda i, s, g: (s, 0, 0))],
            out_specs=pl.BlockSpec((1, 1, 1, n_sub, HEAD_DIM), lambda i, s, g: (i, s, g, 0, 0)),
            out_shape=jax.ShapeDtypeStruct((b, 2, NSA_KV, n_sub, HEAD_DIM), jnp.float32),
            compiler_params=pltpu.CompilerParams(dimension_semantics=("arbitrary",) * 3, **params),
            name="nsa_compress",
        )(x, w1, pos_bias, w2, b2[:, None, :])

    pool, layer, page_table = paged
    b, n_pages = page_table.shape
    planes = 2 * NSA_KV
    page_rows = pool.shape[2] // planes
    n_sub = n_pages * page_rows // CMP_STRIDE
    const = lambda *shape: pl.BlockSpec(shape, lambda i, j, pt: (0,) * len(shape))
    return pl.pallas_call(
        functools.partial(_compress_paged_kernel, page_rows=page_rows),
        grid_spec=pltpu.PrefetchScalarGridSpec(
            num_scalar_prefetch=1,
            grid=(b, n_pages // PAGES_PER_STEP),
            in_specs=[const(2, CMP_LEN * HEAD_DIM, HEAD_DIM), const(2, 1, HEAD_DIM),
                      const(2, HEAD_DIM, HEAD_DIM), const(2, 1, HEAD_DIM)]
                     + _page_specs(pool, layer, n_pages),
            out_specs=pl.BlockSpec((1, 2, NSA_KV, n_sub, HEAD_DIM), lambda i, j, pt: (i, 0, 0, 0, 0)),
            scratch_shapes=[pltpu.VMEM((planes, n_sub, CMP_STRIDE * HEAD_DIM), jnp.float32)]),
        out_shape=jax.ShapeDtypeStruct((b, 2, NSA_KV, n_sub, HEAD_DIM), jnp.float32),
        compiler_params=pltpu.CompilerParams(dimension_semantics=("arbitrary",) * 2, **params),
        name="nsa_compress_paged",
    )(page_table, w1, pos_bias, w2, b2[:, None, :], *([pool] * PAGES_PER_STEP))


INT_MIN = -2 ** 31


def _count(pred):
    return jnp.sum(jnp.where(pred, 1.0, 0.0), axis=-1, keepdims=True)


def _dsa_body(q_ref, kv, iq_ref, iw_ref, ik, o_ref, *, qpos0, k_top):
    tq = q_ref.shape[1]
    lp = ik.shape[1]
    rows = slice(0, lp)
    qpos = qpos0 + lax.broadcasted_iota(jnp.int32, (tq, 1), 0)
    kpos = lax.broadcasted_iota(jnp.int32, (1, lp), 1)
    causal = kpos <= qpos

    iq = iq_ref[0].astype(jnp.bfloat16)
    iw = iw_ref[0]
    score = jnp.zeros((tq, lp), jnp.float32)
    for h in range(IDX_HEADS):
        logits = jnp.dot(iq[:, h * IDX_DIM:(h + 1) * IDX_DIM], ik, preferred_element_type=jnp.float32)
        score = score + iw[:, h:h + 1] * jnp.maximum(logits, 0.0)
    score = jnp.where(causal, score, NEG)
    score = jnp.where(score == 0.0, 0.0, score)
    bits = lax.bitcast_convert_type(score, jnp.int32)
    key = bits ^ (jnp.right_shift(bits, 31) & 0x7FFFFFFF)

    k_f = float(k_top)
    thr = jnp.where(_count(key >= 0) >= k_f, 0, INT_MIN).astype(jnp.int32)

    def refine(i, thr):
        cand = thr | jnp.left_shift(jnp.int32(1), 30 - i)
        return jnp.where(_count(key >= cand) >= k_f, cand, thr)

    thr = lax.fori_loop(0, 31, refine, thr)
    above = key > thr
    tied = key == thr
    need = k_f - _count(above)

    def widen(i, c):
        cand = c | jnp.left_shift(jnp.int32(1), 14 - i)
        return jnp.where(_count(tied & (kpos < cand)) < need, cand, c)

    bound = lax.fori_loop(0, 15, widen, jnp.zeros((tq, 1), jnp.int32))
    mask = causal & (above | (tied & (kpos <= bound)))

    q = q_ref[0]
    for g in range(DSA_KV):
        o = _attend(_group_queries(q, g), kv(0, g, rows), kv(1, g, rows), mask)
        for h in range(HEADS_PER_GROUP):
            head = g * HEADS_PER_GROUP + h
            o_ref[0, :, head * HEAD_DIM:(head + 1) * HEAD_DIM] = o[h * tq:(h + 1) * tq]


def _dsa_kernel(q_ref, kv_ref, iq_ref, iw_ref, ik_ref, o_ref, *, pos0, k_top):
    tq = q_ref.shape[1]
    qpos0 = pos0 + pl.program_id(1) * tq
    _for_key_range(qpos0, tq, kv_ref.shape[1], lambda n: _dsa_body(
        q_ref, _kv_columns(kv_ref, DSA_KV), iq_ref, iw_ref, ik_ref[0, :, :n], o_ref, qpos0=qpos0, k_top=k_top))


def _dsa_paged_kernel(pt_ref, q_ref, iq_ref, iw_ref, new_kv_ref, new_ik_ref, *rest, pos0, k_top, page_rows):
    n = PAGES_PER_STEP
    kv_pages, ik_pages, (o_ref, kv_scr, ik_scr) = rest[:n], rest[n:2 * n], rest[2 * n:]
    _load_kv_pages(kv_pages, new_kv_ref, kv_scr, page_rows, pos0)
    j = pl.program_id(1)
    for k, page in enumerate(ik_pages):
        row0 = pl.multiple_of((j * n + k) * page_rows, page_rows)
        ik_scr[:, pl.ds(row0, page_rows)] = page[0, 0].astype(jnp.bfloat16)

    @pl.when(j == 0)
    def _():
        ik_scr[:, pos0:] = new_ik_ref[0].astype(jnp.bfloat16)

    @pl.when(j == pl.num_programs(1) - 1)
    def _():
        _dsa_body(q_ref, _kv_planes(kv_scr, DSA_KV), iq_ref, iw_ref, ik_scr[...], o_ref, qpos0=pos0, k_top=k_top)


def _dsa_attention(q, kv, idx_k, idx_q, idx_w, pos0, paged=None):
    b, t_len, _ = q.shape
    seq = kv.shape[1] + (pos0 if paged else 0)
    assert seq < 2 ** 15
    tq = Q_TILE if t_len % Q_TILE == 0 else _round_up(t_len, SUBLANES)
    tp = _round_up(t_len, tq)
    lp = _round_up(seq, LANES)
    static = dict(pos0=pos0, k_top=min(TOPK_MAX, seq // 4))
    width = DSA_HEADS * HEAD_DIM
    kv_w = 2 * DSA_KV * HEAD_DIM
    out_shape = jax.ShapeDtypeStruct((b, tp, width), jnp.float32)
    params = pltpu.CompilerParams(dimension_semantics=("arbitrary", "arbitrary"),
                                  vmem_limit_bytes=VMEM_LIMIT_BYTES)
    if paged is None:
        out = pl.pallas_call(
            functools.partial(_dsa_kernel, **static),
            grid=(b, tp // tq),
            in_specs=[pl.BlockSpec((1, tq, width), lambda i, j: (i, j, 0)),
                      pl.BlockSpec((1, lp, kv_w), lambda i, j: (i, 0, 0)),
                      pl.BlockSpec((1, tq, IDX_HEADS * IDX_DIM), lambda i, j: (i, j, 0)),
                      pl.BlockSpec((1, tq, IDX_HEADS), lambda i, j: (i, j, 0)),
                      pl.BlockSpec((1, IDX_DIM, lp), lambda i, j: (i, 0, 0))],
            out_specs=pl.BlockSpec((1, tq, width), lambda i, j: (i, j, 0)),
            out_shape=out_shape, compiler_params=params, name="dsa_attention",
        )(_pad_rows(q, tp), _pad_rows(kv.reshape(b, seq, -1), lp).astype(jnp.bfloat16),
          _pad_rows(idx_q, tp), _pad_rows(idx_w, tp),
          jnp.swapaxes(_pad_rows(idx_k, lp), 1, 2).astype(jnp.bfloat16))
        return out[:, :t_len]

    kv_pool, idx_pool, layer, page_table = paged
    assert tp == tq
    page_rows = idx_pool.shape[3]
    n_pages = page_table.shape[1]
    out = pl.pallas_call(
        functools.partial(_dsa_paged_kernel, page_rows=page_rows, **static),
        grid_spec=pltpu.PrefetchScalarGridSpec(
            num_scalar_prefetch=1,
            grid=(b, n_pages // PAGES_PER_STEP),
            in_specs=[pl.BlockSpec((1, tq, width), lambda i, j, pt: (i, 0, 0)),
                      pl.BlockSpec((1, tq, IDX_HEADS * IDX_DIM), lambda i, j, pt: (i, 0, 0)),
                      pl.BlockSpec((1, tq, IDX_HEADS), lambda i, j, pt: (i, 0, 0)),
                      pl.BlockSpec((1, lp - pos0, kv_w), lambda i, j, pt: (i, 0, 0)),
                      pl.BlockSpec((1, IDX_DIM, lp - pos0), lambda i, j, pt: (i, 0, 0))]
                     + _page_specs(kv_pool, layer, n_pages) + _page_specs(idx_pool, layer, n_pages),
            out_specs=pl.BlockSpec((1, tq, width), lambda i, j, pt: (i, 0, 0)),
            scratch_shapes=[pltpu.VMEM((2 * DSA_KV, lp, HEAD_DIM), jnp.bfloat16),
                            pltpu.VMEM((IDX_DIM, lp), jnp.bfloat16)]),
        out_shape=out_shape, compiler_params=params, name="dsa_attention_paged",
    )(page_table, _pad_rows(q, tp), _pad_rows(idx_q, tp), _pad_rows(idx_w, tp),
      _pad_rows(kv.reshape(b, t_len, -1), lp - pos0), jnp.swapaxes(_pad_rows(idx_k, lp - pos0), 1, 2),
      *([kv_pool] * PAGES_PER_STEP), *([idx_pool] * PAGES_PER_STEP))
    return out[:, :t_len]


def _split_points(sizes):
    pts, acc = [], 0
    for s in sizes[:-1]:
        acc += s
        pts.append(acc)
    return pts


def _rms_norm(x, g):
    xf = x.astype(jnp.float32)
    y = xf * lax.rsqrt(jnp.mean(xf * xf, axis=-1, keepdims=True) + NORM_EPS)
    return (y * g.astype(jnp.float32)).astype(x.dtype)


def _rope_partial(x, pos):
    rot = x.shape[-1] // ROT_DIV
    half = rot // 2
    freqs = ROPE_THETA ** (-jnp.arange(half, dtype=jnp.float32) / half)
    ang = pos.astype(jnp.float32)[:, None] * freqs[None, :]
    cos, sin = jnp.cos(ang)[:, None, :], jnp.sin(ang)[:, None, :]
    xr = x[..., :rot].astype(jnp.float32)
    x1, x2 = xr[..., :half], xr[..., half:]
    xr = jnp.concatenate([x1 * cos - x2 * sin, x1 * sin + x2 * cos], axis=-1)
    return jnp.concatenate([xr.astype(x.dtype), x[..., rot:]], axis=-1)


def _nsa_mixer(q, cmp_kv, slc_kv, win_kv, win_pos0, gate_logits, pos0, p, paged=None):
    if paged is None:
        cmp = _compress(cmp_kv, p['cmp_w1'], p['cmp_pos'], p['cmp_w2'], p['cmp_b2'])
    else:
        cmp = _compress(None, p['cmp_w1'], p['cmp_pos'], p['cmp_w2'], p['cmp_b2'], paged=(paged[0],) + paged[2:])
    n_sub = cmp.shape[3]
    c_pos = jnp.arange(n_sub, dtype=jnp.int32) * CMP_STRIDE + (CMP_LEN - 1)
    kc = jnp.swapaxes(_rope_partial(_rms_norm(jnp.swapaxes(cmp[:, 0], 1, 2), p['a_k_norm'][0]), c_pos), 1, 2)
    return _nsa_attention(q, kc, cmp[:, 1], slc_kv, win_kv, gate_logits, pos0, win_pos0, n_sub - 1,
                          paged=None if paged is None else paged[1:])


def _rwkv7_mixer(cur, prev, state0, p):
    b, T, _ = cur.shape
    xm = cur + (prev - cur) * p['rwkv_mu']
    r, k, v, w_lo, a_lo, g_lo = jnp.split(xm, [RWKV_DIM, 2 * RWKV_DIM, 3 * RWKV_DIM, 3 * RWKV_DIM + DECAY_LORA,
                                               3 * RWKV_DIM + DECAY_LORA + AAA_LORA], axis=-1)
    w = -jax.nn.softplus(-(p['rwkv_w0'] + jnp.tanh(w_lo) @ p['rwkv_w2'])) - 0.5
    a = jax.nn.sigmoid(p['rwkv_a0'] + a_lo @ p['rwkv_a2'])
    g = jax.nn.sigmoid(g_lo) @ p['rwkv_g2']
    heads = lambda t: t.reshape(b, T, RWKV_HEADS, RWKV_HEAD).astype(jnp.float32)
    kk = heads(k * p['rwkv_k_k'])
    kk = kk * lax.rsqrt(jnp.sum(kk * kk, axis=-1, keepdims=True) + 1e-12)
    kh = heads(k * (1.0 + (a - 1.0) * p['rwkv_k_a']))
    rh, vh, ah = heads(r), heads(v), heads(a)
    decay = jnp.exp(-jnp.exp(heads(w)))
    y, s_t = _rwkv_scan(state0.astype(jnp.float32), rh, decay, kh, vh, -kk, kk * ah)
    mu = jnp.mean(y, axis=-1, keepdims=True)
    var = jnp.mean(jnp.square(y - mu), axis=-1, keepdims=True)
    y = ((y - mu) * lax.rsqrt(var + RWKV_GN_EPS)).reshape(b, T, RWKV_DIM) * p['rwkv_ln_w'] + p['rwkv_ln_b']
    y = y + (jnp.sum(rh * kh * p['rwkv_r_k'], axis=-1, keepdims=True) * vh).reshape(b, T, RWKV_DIM)
    return (y * g).astype(cur.dtype), s_t.astype(state0.dtype)


def _hybrid_layer(x, p, big, layer, pos0, past):
    b, T, _ = x.shape
    q_pos = pos0 + jnp.arange(T, dtype=jnp.int32)
    xn = _rms_norm(x, p['norm_attn'])
    c_lo, c_hi = sum(IN_SPLITS[:8]), sum(IN_SPLITS[:9])
    if past is None:
        pieces = jnp.split(_proj(xn, big['w_in'], layer), _split_points(IN_SPLITS), axis=-1)
        c_cur = pieces[8]
        c_prev = jnp.concatenate([jnp.zeros_like(c_cur[:, :1]), c_cur[:, :-1]], axis=1)
    else:
        prev = past['shift'][:, None].astype(xn.dtype)
        full = _proj(jnp.concatenate([prev, xn], axis=1), big['w_in'], layer)
        pieces = jnp.split(full[:, 1:], _split_points(IN_SPLITS), axis=-1)
        c_cur = pieces[8]
        c_prev = full[:, :-1, c_lo:c_hi]
    a_q, a_kv, a_gate, d_q, d_kv, i_q, i_k, i_w, _, merge = pieces

    q_a = _rope_partial(_rms_norm(a_q.reshape(b, T, NSA_HEADS, HEAD_DIM), p['a_q_norm']), q_pos)
    kv6 = a_kv.reshape(b, T, 6, NSA_KV, HEAD_DIM)
    cmp_rows = kv6[:, :, 0:2]
    slc_rows = jnp.stack([_rope_partial(_rms_norm(kv6[:, :, 2], p['a_k_norm'][1]), q_pos), kv6[:, :, 3]], axis=2)
    win_rows = jnp.stack([_rope_partial(_rms_norm(kv6[:, :, 4], p['a_k_norm'][2]), q_pos), kv6[:, :, 5]], axis=2)
    q_d = _rope_partial(_rms_norm(d_q.reshape(b, T, DSA_HEADS, HEAD_DIM), p['b_q_norm']), q_pos)
    kv2 = d_kv.reshape(b, T, 2, DSA_KV, HEAD_DIM)
    dsa_rows = jnp.stack([_rope_partial(_rms_norm(kv2[:, :, 0], p['b_k_norm']), q_pos), kv2[:, :, 1]], axis=2)
    idx_rows = _rope_partial(_rms_norm(i_k, p['idx_k_norm'])[:, :, None], q_pos)[:, :, 0]
    idx_q = _rope_partial(i_q.reshape(b, T, IDX_HEADS, IDX_DIM), q_pos)
    idx_w = i_w * (IDX_HEADS * IDX_DIM) ** -0.5

    q_a, q_d, idx_q = q_a.reshape(b, T, -1), q_d.reshape(b, T, -1), idx_q.reshape(b, T, -1)
    if past is None:
        win_all = win_rows
        n_keep = min(WINDOW, T)
        win_pos0 = pos0
        rwkv0 = jnp.zeros((b, RWKV_HEADS, RWKV_HEAD, RWKV_HEAD), x.dtype)
        o_a = _nsa_mixer(q_a, cmp_rows, slc_rows, win_all, win_pos0, a_gate, pos0, p)
        o_b = _dsa_attention(q_d, dsa_rows, idx_rows, idx_q, idx_w, pos0)
    else:
        assert (pos0 + T) // CMP_STRIDE * CMP_STRIDE <= pos0
        win_all = jnp.concatenate([past['nsa_win'], win_rows], axis=1)
        n_keep = past['nsa_win'].shape[1]
        win_pos0 = pos0 - n_keep
        rwkv0 = past['rwkv']
        page_table = past['page_table']
        o_a = _nsa_mixer(q_a, None, slc_rows, win_all, win_pos0, a_gate, pos0, p,
                         paged=(past['cmp_pool'], past['slc_pool'], layer, page_table))
        o_b = _dsa_attention(q_d, dsa_rows, idx_rows, idx_q, idx_w, pos0,
                             paged=(past['dsa_pool'], past['idx_pool'], layer, page_table))
    o_c, rwkv_new = _rwkv7_mixer(c_cur, c_prev, rwkv0, p)

    g_a, g_b, g_c = jnp.split(jax.nn.sigmoid(merge), 3, axis=-1)
    mixed = (g_a * _proj(o_a, big['p_a'], layer) + g_b * _proj(o_b, big['p_b'], layer)
             + g_c * _proj(o_c, big['p_c'], layer))
    x = _proj(mixed, big['w_o'], layer, residual=x)
    x = _ffn(x, p['norm_ffn'], big['w_ffn_in'], big['w_ffn_out'], layer)
    new_state = {'nsa_cmp': cmp_rows, 'nsa_slc': slc_rows, 'dsa_kv': dsa_rows, 'dsa_idx': idx_rows,
                 'nsa_win': win_all[:, -n_keep:], 'rwkv': rwkv_new, 'shift': xn[:, -1]}
    return x, new_state


_PARAM_NAMES = ('norm_attn', 'w_in', 'a_q_norm', 'a_k_norm', 'cmp_w1', 'cmp_pos', 'cmp_w2', 'cmp_b2',
                'b_q_norm', 'b_k_norm', 'idx_k_norm', 'rwkv_mu', 'rwkv_w0', 'rwkv_w2', 'rwkv_a0',
                'rwkv_a2', 'rwkv_g2', 'rwkv_k_k', 'rwkv_k_a', 'rwkv_r_k', 'rwkv_ln_w', 'rwkv_ln_b',
                'p_a', 'p_b', 'p_c', 'w_o', 'norm_ffn', 'w_ffn_in', 'w_ffn_out')


def kernel(x_prompt, x_sample, cache_nsa_cmp, cache_nsa_slc, cache_dsa_kv, cache_dsa_idx, state_nsa_win, state_rwkv, state_shift, page_table, norm_attn, w_in, a_q_norm, a_k_norm, cmp_w1, cmp_pos, cmp_w2, cmp_b2, b_q_norm, b_k_norm, idx_k_norm, rwkv_mu, rwkv_w0, rwkv_w2, rwkv_a0, rwkv_a2, rwkv_g2, rwkv_k_k, rwkv_k_a, rwkv_r_k, rwkv_ln_w, rwkv_ln_b, p_a, p_b, p_c, w_o, norm_ffn, w_ffn_in, w_ffn_out):
    weights = dict(zip(_PARAM_NAMES, (norm_attn, w_in, a_q_norm, a_k_norm, cmp_w1, cmp_pos, cmp_w2, cmp_b2,
                                      b_q_norm, b_k_norm, idx_k_norm, rwkv_mu, rwkv_w0, rwkv_w2, rwkv_a0,
                                      rwkv_a2, rwkv_g2, rwkv_k_k, rwkv_k_a, rwkv_r_k, rwkv_ln_w, rwkv_ln_b,
                                      p_a, p_b, p_c, w_o, norm_ffn, w_ffn_in, w_ffn_out)))
    depth = w_in.shape[0]
    past_len = page_table.shape[1] * cache_nsa_cmp.shape[2]
    yp, ys = x_prompt, x_sample
    st_p, st_s = [], []
    kv_pool = lambda c: c.reshape(c.shape[:2] + (-1, HEAD_DIM))
    big_names = ('w_in', 'p_a', 'p_b', 'p_c', 'w_o', 'w_ffn_in', 'w_ffn_out')
    big = {name: weights[name].astype(jnp.bfloat16) for name in big_names}
    for l in range(depth):
        p = {name: w[l] for name, w in weights.items() if name not in big_names}
        past = {'cmp_pool': kv_pool(cache_nsa_cmp), 'slc_pool': kv_pool(cache_nsa_slc),
                'dsa_pool': kv_pool(cache_dsa_kv), 'idx_pool': jnp.swapaxes(cache_dsa_idx, 2, 3), 'page_table': page_table,
                'nsa_win': state_nsa_win[l], 'rwkv': state_rwkv[l], 'shift': state_shift[l]}
        yp, sp = _hybrid_layer(yp, p, big, l, 0, None)
        ys, ss = _hybrid_layer(ys, p, big, l, past_len, past)
        st_p.append(sp)
        st_s.append(ss)
    stack = lambda sts, name: jnp.stack([s[name] for s in sts])
    return (yp, ys,
            stack(st_p, 'nsa_cmp'), stack(st_s, 'nsa_cmp'),
            stack(st_p, 'nsa_slc'), stack(st_s, 'nsa_slc'),
            stack(st_p, 'dsa_kv'), stack(st_s, 'dsa_kv'),
            stack(st_p, 'dsa_idx'), stack(st_s, 'dsa_idx'),
            stack(st_p, 'nsa_win'), stack(st_s, 'nsa_win'),
            stack(st_p, 'rwkv'), stack(st_s, 'rwkv'),
            stack(st_p, 'shift'), stack(st_s, 'shift'))
```

```python
import functools
import math

import jax
import jax.numpy as jnp
from jax import lax
from jax.experimental import pallas as pl
from jax.experimental.pallas import tpu as pltpu

D_MODEL = 2048
HEAD_DIM = 128
ROT_DIV = 4
ROPE_THETA = 500000.0
NORM_EPS = 1e-6

NSA_HEADS = 8
NSA_KV = 2
CMP_STRIDE = 16
CMP_LEN = 2 * CMP_STRIDE
SLC_BLOCK = 64
N_SEL = 8
N_LOCAL = 2
WINDOW = 512

DSA_HEADS = 8
DSA_KV = 2
IDX_HEADS = 8
IDX_DIM = 64
TOPK_MAX = 256

RWKV_HEADS = 16
RWKV_HEAD = 64
RWKV_DIM = RWKV_HEADS * RWKV_HEAD
DECAY_LORA = 64
AAA_LORA = 64
GATE_LORA = 160
RWKV_GN_EPS = 64e-5

NEG = -1e30
FORCE = 1e9
LOWEST = -3e38

IN_SPLITS = (
    NSA_HEADS * HEAD_DIM,
    6 * NSA_KV * HEAD_DIM,
    3 * NSA_HEADS,
    DSA_HEADS * HEAD_DIM,
    2 * DSA_KV * HEAD_DIM,
    IDX_HEADS * IDX_DIM,
    IDX_DIM,
    IDX_HEADS,
    3 * RWKV_DIM + DECAY_LORA + AAA_LORA + GATE_LORA,
    3 * D_MODEL,
)

LANES = 128
SUBLANES = 8
VMEM_LIMIT_BYTES = 56 * 1024 * 1024

HEADS_PER_GROUP = NSA_HEADS // NSA_KV
ATTN_SCALE = HEAD_DIM ** -0.5
Q_TILE = 128

_NT = (((1,), (1,)), ((), ()))


def _round_up(n, m):
    return -(-n // m) * m


def _mm_kernel(x_ref, w_ref, *rest):
    r_ref = rest[0] if len(rest) == 3 else None
    o_ref, xb_ref = rest[-2:]

    @pl.when(pl.program_id(1) == 0)
    def _():
        xb_ref[...] = x_ref[...].astype(jnp.bfloat16)

    acc = jnp.dot(xb_ref[...], w_ref[0], preferred_element_type=jnp.float32)
    o_ref[...] = acc if r_ref is None else r_ref[...] + acc


def _pick_tile(n, prefs):
    for t in prefs:
        if n % t == 0:
            return t
    return n


def _mm(x, w, layer, residual=None):
    m, k = x.shape
    n = w.shape[2]
    tm = _pick_tile(m, (1024, 512, 256, 128))
    tn = _pick_tile(n, (512, 384, 256, 128))
    tile = pl.BlockSpec((tm, tn), lambda i, j: (i, j))
    return pl.pallas_call(
        _mm_kernel,
        grid=(m // tm, n // tn),
        in_specs=[pl.BlockSpec((tm, k), lambda i, j: (i, 0)),
                  pl.BlockSpec((1, k, tn), lambda i, j: (layer, 0, j))] + ([] if residual is None else [tile]),
        out_specs=tile,
        out_shape=jax.ShapeDtypeStruct((m, n), jnp.float32),
        scratch_shapes=[pltpu.VMEM((tm, k), jnp.bfloat16)],
        compiler_params=pltpu.CompilerParams(
            dimension_semantics=("arbitrary", "arbitrary"),
            vmem_limit_bytes=VMEM_LIMIT_BYTES),
        name="dense_proj",
    )(x, w, *([] if residual is None else [residual]))


def _proj(x, w, layer, residual=None):
    lead = x.shape[:-1]
    res = None if residual is None else residual.reshape(-1, residual.shape[-1])
    return _mm(x.reshape(-1, x.shape[-1]), w, layer, res).reshape(lead + (w.shape[2],))


FFN_TILE = 512


def _ffn_kernel(x_ref, gain_ref, wg_ref, wu_ref, wo_ref, o_ref, hb_ref, acc_ref):
    f = pl.program_id(1)

    @pl.when(f == 0)
    def _():
        x = x_ref[...]
        y = x * lax.rsqrt(jnp.mean(x * x, axis=-1, keepdims=True) + NORM_EPS)
        hb_ref[...] = (y * gain_ref[...]).astype(jnp.bfloat16)
        acc_ref[...] = jnp.zeros_like(acc_ref)

    hb = hb_ref[...]
    gate = jnp.dot(hb, wg_ref[0], preferred_element_type=jnp.float32)
    up = jnp.dot(hb, wu_ref[0], preferred_element_type=jnp.float32)
    act = (jax.nn.silu(gate) * up).astype(jnp.bfloat16)
    acc_ref[...] += jnp.dot(act, wo_ref[0], preferred_element_type=jnp.float32)

    @pl.when(f == pl.num_programs(1) - 1)
    def _():
        o_ref[...] = x_ref[...] + acc_ref[...]


def _ffn(x, gain, w_in, w_out, layer):
    lead = x.shape[:-1]
    x2 = x.reshape(-1, x.shape[-1])
    m, d = x2.shape
    n_f = w_out.shape[1] // FFN_TILE
    tm = _pick_tile(m, (512, 256, 128))
    out = pl.pallas_call(
        _ffn_kernel,
        grid=(m // tm, n_f),
        in_specs=[pl.BlockSpec((tm, d), lambda i, f: (i, 0)),
                  pl.BlockSpec((1, d), lambda i, f: (0, 0)),
                  pl.BlockSpec((1, d, FFN_TILE), lambda i, f: (layer, 0, f)),
                  pl.BlockSpec((1, d, FFN_TILE), lambda i, f: (layer, 0, n_f + f)),
                  pl.BlockSpec((1, FFN_TILE, d), lambda i, f: (layer, f, 0))],
        out_specs=pl.BlockSpec((tm, d), lambda i, f: (i, 0)),
        out_shape=jax.ShapeDtypeStruct((m, d), jnp.float32),
        scratch_shapes=[pltpu.VMEM((tm, d), jnp.bfloat16), pltpu.VMEM((tm, d), jnp.float32)],
        compiler_params=pltpu.CompilerParams(
            dimension_semantics=("arbitrary", "arbitrary"),
            vmem_limit_bytes=VMEM_LIMIT_BYTES),
        name="ffn_block",
    )(x2, gain[None, :], w_in, w_in, w_out)
    return out.reshape(lead + (d,))


RWKV_HEADS_PER_BLOCK = LANES // 2
RWKV_V_HALF = RWKV_HEAD // 2


def _rwkv_scan_kernel(r_ref, w_ref, k_ref, a_ref, b_ref, v_ref, s0_ref, y_ref, s_out_ref, s_ref, rows_ref):
    c = pl.program_id(1)

    @pl.when(c == 0)
    def _():
        s_ref[...] = s0_ref[0]

    steps = r_ref.shape[1]
    low_half = lax.broadcasted_iota(jnp.int32, (RWKV_V_HALF, LANES), 1) < RWKV_HEADS_PER_BLOCK

    def step(t, carry):
        for n, ref in enumerate((r_ref, w_ref, k_ref, a_ref, b_ref)):
            tile = ref[0, t]
            swapped = pltpu.roll(tile, RWKV_HEADS_PER_BLOCK, 1)
            rows_ref[n, :RWKV_V_HALF] = jnp.where(low_half, tile, swapped)
            rows_ref[n, RWKV_V_HALF:] = jnp.where(low_half, swapped, tile)
        parts = [jnp.zeros((RWKV_V_HALF, LANES), jnp.float32) for _ in range(4)]
        for k in range(RWKV_HEAD):
            parts[k % 4] = parts[k % 4] + s_ref[k] * rows_ref[3, k:k + 1, :]
        sa = (parts[0] + parts[1]) + (parts[2] + parts[3])
        vt = v_ref[0, t]
        ys = [jnp.zeros((RWKV_V_HALF, LANES), jnp.float32) for _ in range(4)]
        for k in range(RWKV_HEAD):
            s_new = (s_ref[k] * rows_ref[1, k:k + 1, :] + sa * rows_ref[4, k:k + 1, :]
                     + vt * rows_ref[2, k:k + 1, :])
            s_ref[k] = s_new
            ys[k % 4] = ys[k % 4] + s_new * rows_ref[0, k:k + 1, :]
        y_ref[0, t] = (ys[0] + ys[1]) + (ys[2] + ys[3])
        return carry

    lax.fori_loop(0, steps, step, 0)

    @pl.when(c == pl.num_programs(1) - 1)
    def _():
        s_out_ref[0] = s_ref[...]


def _rwkv_scan(state0, r, w, k, v, a, b):
    bsz, t_len, n_h, n = r.shape
    heads = bsz * n_h
    hb = RWKV_HEADS_PER_BLOCK
    nblk = heads // hb
    tc = _pick_tile(t_len, (128, 64, 32, 16, 8, 4))

    def pack(x):
        x = jnp.transpose(x.reshape(bsz, t_len, n_h, 2, RWKV_V_HALF), (1, 4, 3, 0, 2))
        x = x.reshape(t_len, RWKV_V_HALF, 2, nblk, hb)
        return jnp.transpose(x, (3, 0, 1, 2, 4)).reshape(nblk, t_len, RWKV_V_HALF, LANES)

    s0 = state0.reshape(nblk, hb, 2, RWKV_V_HALF, n)
    s0 = jnp.transpose(s0, (0, 4, 3, 2, 1)).reshape(nblk, n, RWKV_V_HALF, LANES)

    val_spec = pl.BlockSpec((1, tc, RWKV_V_HALF, LANES), lambda i, c: (i, c, 0, 0))
    st_spec = pl.BlockSpec((1, n, RWKV_V_HALF, LANES), lambda i, c: (i, 0, 0, 0))
    y, s_out = pl.pallas_call(
        _rwkv_scan_kernel,
        grid=(nblk, t_len // tc),
        in_specs=[val_spec] * 6 + [st_spec],
        out_specs=[val_spec, st_spec],
        out_shape=[jax.ShapeDtypeStruct((nblk, t_len, RWKV_V_HALF, LANES), jnp.float32),
                   jax.ShapeDtypeStruct((nblk, n, RWKV_V_HALF, LANES), jnp.float32)],
        scratch_shapes=[pltpu.VMEM((n, RWKV_V_HALF, LANES), jnp.float32),
                        pltpu.VMEM((5, n, LANES), jnp.float32)],
        compiler_params=pltpu.CompilerParams(
            dimension_semantics=("arbitrary", "arbitrary"),
            vmem_limit_bytes=VMEM_LIMIT_BYTES),
        name="rwkv7_scan",
    )(pack(r), pack(w), pack(k), pack(a), pack(b), pack(v), s0)

    y = y.reshape(nblk, t_len, RWKV_V_HALF, 2, hb)
    y = jnp.transpose(y, (0, 4, 1, 3, 2)).reshape(bsz, n_h, t_len, n)
    y = jnp.transpose(y, (0, 2, 1, 3))
    s_out = s_out.reshape(nblk, n, RWKV_V_HALF, 2, hb)
    s_out = jnp.transpose(s_out, (0, 4, 3, 2, 1)).reshape(bsz, n_h, n, n)
    return y, s_out


_ORIG = dict(zip(('a_q', 'a_kv', 'a_gate', 'd_q', 'd_kv', 'i_q', 'i_k', 'i_w', 'c', 'merge'),
                 zip([sum(IN_SPLITS[:i]) for i in range(len(IN_SPLITS))], IN_SPLITS)))
C_COLS = IN_SPLITS[8]
C_PAD = _round_up(C_COLS, LANES) - C_COLS
TAIL_PAD = LANES - (IDX_DIM + 3 * NSA_HEADS + IDX_HEADS)
COL = {}
_off = 0
for _name, _width in (('merge', 3 * D_MODEL), ('a_q', NSA_HEADS * HEAD_DIM), ('d_q', DSA_HEADS * HEAD_DIM),
                      ('a_kv', 6 * NSA_KV * HEAD_DIM), ('d_kv', 2 * DSA_KV * HEAD_DIM),
                      ('i_q', IDX_HEADS * IDX_DIM), ('c', C_COLS + C_PAD), ('tail', LANES)):
    COL[_name] = (_off, _width)
    _off += _width
PROJ_COLS = _off
TAIL_IK = (0, IDX_DIM)
TAIL_GATE = (IDX_DIM, IDX_DIM + 3 * NSA_HEADS)
TAIL_IW = (IDX_DIM + 3 * NSA_HEADS, IDX_DIM + 3 * NSA_HEADS + IDX_HEADS)


def _regroup_w_in(w_in):
    cut = lambda name: w_in[..., _ORIG[name][0]:_ORIG[name][0] + _ORIG[name][1]]
    zeros = lambda n: jnp.zeros(w_in.shape[:2] + (n,), w_in.dtype)
    parts = [cut('merge'), cut('a_q'), cut('d_q'), cut('a_kv'), cut('d_kv'), cut('i_q'), cut('c'), zeros(C_PAD),
             cut('i_k'), cut('a_gate'), cut('i_w'), zeros(TAIL_PAD)]
    return jnp.concatenate(parts, axis=-1).astype(jnp.bfloat16)


def _rope_tables(pos, width, head_dim):
    half = head_dim // ROT_DIV // 2
    freqs = ROPE_THETA ** (-jnp.arange(half, dtype=jnp.float32) / half)
    ang = pos.astype(jnp.float32)[:, None] * freqs[None, :]
    cos, sin = jnp.cos(ang), jnp.sin(ang)
    rest = head_dim - 2 * half
    ones, zeros = jnp.ones((pos.shape[0], rest), jnp.float32), jnp.zeros((pos.shape[0], rest), jnp.float32)
    z_half = jnp.zeros_like(sin)
    reps = width // head_dim
    c = jnp.tile(jnp.concatenate([cos, cos, ones], axis=1), (1, reps))
    s_up = jnp.tile(jnp.concatenate([z_half, sin, zeros], axis=1), (1, reps))
    s_down = jnp.tile(jnp.concatenate([-sin, z_half, zeros], axis=1), (1, reps))
    return c, s_up, s_down


def _rotate(y, tables, half):
    c, s_up, s_down = tables
    width = y.shape[-1]
    return y * c + pltpu.roll(y, half, 1) * s_up + pltpu.roll(y, width - half, 1) * s_down


def _head_norm(x, gain):
    return x * lax.rsqrt(jnp.mean(x * x, axis=-1, keepdims=True) + NORM_EPS) * gain


ROPE_HALF = HEAD_DIM // ROT_DIV // 2
IDX_ROPE_HALF = IDX_DIM // ROT_DIV // 2
KV_PLANES = 2 * NSA_KV


def _post_proj_kernel(aq_ref, dq_ref, cmp_ref, slc_ref, win_ref, dkv_ref, iq_ref, tail_ref,
                      c_ref, su_ref, sd_ref, ic_ref, isu_ref, isd_ref, gains_ref,
                      qa_ref, qd_ref, iqo_ref, cmp_o, slc_o, win_o, dsa_o, ik_o):
    tm = aq_ref.shape[0]
    rope = (c_ref[...], su_ref[...], sd_ref[...])
    irope = (ic_ref[...], isu_ref[...], isd_ref[...])
    gain = lambda i: gains_ref[i:i + 1, :]

    def head(ref, h):
        return ref[:, h * HEAD_DIM:(h + 1) * HEAD_DIM]

    for h in range(NSA_HEADS):
        qa_ref[:, h * HEAD_DIM:(h + 1) * HEAD_DIM] = _rotate(
            _head_norm(head(aq_ref, h), gain(0)), rope, ROPE_HALF).astype(qa_ref.dtype)
    for h in range(DSA_HEADS):
        qd_ref[:, h * HEAD_DIM:(h + 1) * HEAD_DIM] = _rotate(
            _head_norm(head(dq_ref, h), gain(3)), rope, ROPE_HALF).astype(qd_ref.dtype)
    for pair in range(IDX_HEADS * IDX_DIM // LANES):
        iqo_ref[:, pair * LANES:(pair + 1) * LANES] = _rotate(
            iq_ref[:, pair * LANES:(pair + 1) * LANES], irope, IDX_ROPE_HALF).astype(iqo_ref.dtype)

    def planes(src, dst, key_gain):
        for c in range(KV_PLANES):
            x = head(src, c)
            if key_gain is not None and c < NSA_KV:
                x = _rotate(_head_norm(x, gain(key_gain)), rope, ROPE_HALF)
            dst[pl.ds(c, tm, stride=KV_PLANES), :] = x

    planes(cmp_ref, cmp_o, None)
    planes(slc_ref, slc_o, 1)
    planes(win_ref, win_o, 2)
    planes(dkv_ref, dsa_o, 4)

    lane = lax.broadcasted_iota(jnp.int32, (tm, LANES), 1)
    ik = jnp.where(lane < IDX_DIM, tail_ref[...], 0.0)
    ik = ik * lax.rsqrt(jnp.sum(ik * ik, axis=-1, keepdims=True) * (1.0 / IDX_DIM) + NORM_EPS) * gain(5)
    ik_o[...] = _rotate(ik, irope, IDX_ROPE_HALF)[:, :IDX_DIM]


def _post_proj(proj, tables, itables, gains):
    m = proj.shape[0]
    tm = _pick_tile(m, (256, 128))
    col = lambda name, width: pl.BlockSpec((tm, width), functools.partial(
        lambda i, blk: (i, blk), blk=COL[name][0] // width))
    kv_w = KV_PLANES * HEAD_DIM
    akv0 = COL['a_kv'][0] // kv_w
    kv_col = lambda n: pl.BlockSpec((tm, kv_w), functools.partial(lambda i, blk: (i, blk), blk=akv0 + n))
    row = lambda width: pl.BlockSpec((tm, width), lambda i: (i, 0))
    planes = pl.BlockSpec((tm * KV_PLANES, HEAD_DIM), lambda i: (i, 0))
    f32 = jnp.float32
    return pl.pallas_call(
        _post_proj_kernel,
        grid=(m // tm,),
        in_specs=[col('a_q', NSA_HEADS * HEAD_DIM), col('d_q', DSA_HEADS * HEAD_DIM),
                  kv_col(0), kv_col(1), kv_col(2), col('d_kv', kv_w), col('i_q', IDX_HEADS * IDX_DIM),
                  col('tail', LANES)] + [row(LANES)] * 6 + [pl.BlockSpec((8, LANES), lambda i: (0, 0))],
        out_specs=[row(NSA_HEADS * HEAD_DIM), row(DSA_HEADS * HEAD_DIM), row(IDX_HEADS * IDX_DIM),
                   planes, planes, planes, planes, row(IDX_DIM)],
        out_shape=[jax.ShapeDtypeStruct((m, NSA_HEADS * HEAD_DIM), jnp.bfloat16),
                   jax.ShapeDtypeStruct((m, DSA_HEADS * HEAD_DIM), jnp.bfloat16),
                   jax.ShapeDtypeStruct((m, IDX_HEADS * IDX_DIM), jnp.bfloat16)]
                  + [jax.ShapeDtypeStruct((m * KV_PLANES, HEAD_DIM), f32)] * 4
                  + [jax.ShapeDtypeStruct((m, IDX_DIM), f32)],
        compiler_params=pltpu.CompilerParams(dimension_semantics=("arbitrary",),
                                             vmem_limit_bytes=VMEM_LIMIT_BYTES),
        name="post_proj",
    )(*([proj] * 8), *tables, *itables, gains)


def _merge_kernel(oa_ref, ob_ref, oc_ref, pa_ref, pb_ref, pc_ref, ga_ref, gb_ref, gc_ref, o_ref, xb_ref):
    @pl.when(pl.program_id(1) == 0)
    def _():
        for n, ref in enumerate((oa_ref, ob_ref, oc_ref)):
            xb_ref[n] = ref[...].astype(jnp.bfloat16)

    acc = None
    for n, (w_ref, g_ref) in enumerate(((pa_ref, ga_ref), (pb_ref, gb_ref), (pc_ref, gc_ref))):
        term = jax.nn.sigmoid(g_ref[...]) * jnp.dot(xb_ref[n], w_ref[0], preferred_element_type=jnp.float32)
        acc = term if acc is None else acc + term
    o_ref[...] = acc


def _merge(o_a, o_b, o_c, proj, big, layer):
    m, k = o_a.shape
    tm = _pick_tile(m, (512, 256, 128))
    tn = 512
    n_blk = D_MODEL // tn
    x_spec = pl.BlockSpec((tm, k), lambda i, j: (i, 0))
    w_spec = pl.BlockSpec((1, k, tn), lambda i, j: (layer, 0, j))
    gate = lambda n: pl.BlockSpec((tm, tn), functools.partial(lambda i, j, n: (i, n * n_blk + j), n=n))
    assert COL['merge'][0] == 0
    return pl.pallas_call(
        _merge_kernel,
        grid=(m // tm, n_blk),
        in_specs=[x_spec] * 3 + [w_spec] * 3 + [gate(0), gate(1), gate(2)],
        out_specs=pl.BlockSpec((tm, tn), lambda i, j: (i, j)),
        out_shape=jax.ShapeDtypeStruct((m, D_MODEL), jnp.float32),
        scratch_shapes=[pltpu.VMEM((3, tm, k), jnp.bfloat16)],
        compiler_params=pltpu.CompilerParams(dimension_semantics=("arbitrary", "arbitrary"),
                                             vmem_limit_bytes=VMEM_LIMIT_BYTES),
        name="mixer_merge",
    )(o_a, o_b, o_c, big['p_a'], big['p_b'], big['p_c'], proj, proj, proj)


def _group_queries(q, g):
    h0 = g * HEADS_PER_GROUP
    return jnp.concatenate([q[:, (h0 + h) * HEAD_DIM:(h0 + h + 1) * HEAD_DIM] for h in range(HEADS_PER_GROUP)],
                           axis=0).astype(jnp.bfloat16)


def _attend(qg, k, v, mask, want_psum=False):
    tq = mask.shape[0]
    s = lax.dot_general(qg, k, _NT, preferred_element_type=jnp.float32) * ATTN_SCALE
    ps = []
    for h in range(HEADS_PER_GROUP):
        sh = jnp.where(mask, s[h * tq:(h + 1) * tq], NEG)
        e = jnp.where(mask, jnp.exp(sh - jnp.max(sh, axis=-1, keepdims=True)), 0.0)
        den = jnp.sum(e, axis=-1, keepdims=True)
        ps.append(e / jnp.where(den > 0.0, den, 1.0))
    o = jnp.dot(jnp.concatenate(ps, axis=0).astype(jnp.bfloat16), v, preferred_element_type=jnp.float32)
    if want_psum:
        return o, (ps[0] + ps[1]) + (ps[2] + ps[3])
    return o


def _kv_columns(ref, n_kv):
    def get(s, g, rows):
        c0 = (s * n_kv + g) * HEAD_DIM
        return ref[0, rows, c0:c0 + HEAD_DIM]
    return get


def _kv_planes(ref, n_kv):
    return lambda s, g, rows: ref[s * n_kv + g, rows, :]


def _unpack_planes(rows_ref, scr):
    planes, length = scr.shape[0], scr.shape[1]
    for c in range(planes):
        scr[c] = rows_ref[0, pl.ds(c, length, stride=planes), :].astype(jnp.bfloat16)


KEY_GROUP = 512


def _for_key_range(qpos0, tq, n_keys, body):
    lengths = list(range(KEY_GROUP, n_keys, KEY_GROUP)) + [n_keys]
    group = (qpos0 + tq - 1) // KEY_GROUP
    for i, n in enumerate(lengths):
        @pl.when((group == i) if i + 1 < len(lengths) else (group >= i))
        def _(n=n):
            body(n)


PAGES_PER_STEP = 8


def _page_specs(pool, layer, pages_per_seq):
    assert pages_per_seq % PAGES_PER_STEP == 0
    block = (1, 1) + pool.shape[2:]
    return [pl.BlockSpec(block, functools.partial(
        lambda i, j, pt, k: (layer, pt[i, j * PAGES_PER_STEP + k], 0, 0), k=k)) for k in range(PAGES_PER_STEP)]


def _load_kv_pages(page_refs, new_ref, kv_scr, page_rows, past_len):
    j = pl.program_id(1)
    planes = kv_scr.shape[0]
    for k, page in enumerate(page_refs):
        row0 = pl.multiple_of((j * len(page_refs) + k) * page_rows, page_rows)
        for c in range(planes):
            kv_scr[c, pl.ds(row0, page_rows), :] = (
                page[0, 0, pl.ds(c, page_rows, stride=planes), :].astype(jnp.bfloat16))

    @pl.when(j == 0)
    def _():
        tail = kv_scr.shape[1] - past_len
        for c in range(planes):
            kv_scr[c, past_len:, :] = new_ref[0, pl.ds(c, tail, stride=planes), :].astype(jnp.bfloat16)


def _pad_rows(x, n):
    return jnp.pad(x, ((0, 0), (0, n - x.shape[1])) + ((0, 0),) * (x.ndim - 2))


def _nsa_body(q_ref, tail_ref, kc_ref, vc_ref, slc, slc_len, win, win_total, cover_ref, expand_ref, o_ref,
              *, qpos0, win_pos0, n_slc, n_sel):
    tq = q_ref.shape[1]
    qpos = qpos0 + lax.broadcasted_iota(jnp.int32, (tq, 1), 0)
    q = q_ref[0]
    gates = jax.nn.sigmoid(tail_ref[0][:, TAIL_GATE[0]:TAIL_GATE[1]])
    win_len = min(win_total, _round_up(WINDOW + tq, LANES))
    win0 = jnp.clip(qpos0 - win_pos0 - WINDOW, 0, win_total - win_len)
    win_rows = pl.ds(pl.multiple_of(win0, LANES), win_len)
    slc_rows = slice(0, slc_len)

    cpos = lax.broadcasted_iota(jnp.int32, (1, kc_ref.shape[2]), 1) * CMP_STRIDE + (CMP_LEN - 1)
    cmask = cpos <= qpos
    kpos = lax.broadcasted_iota(jnp.int32, (1, slc_len), 1)
    causal = kpos <= qpos
    wpos = win_pos0 + win0 + lax.broadcasted_iota(jnp.int32, (1, win_len), 1)
    wmask = (wpos <= qpos) & (qpos - wpos < WINDOW)
    blk = lax.broadcasted_iota(jnp.int32, (1, cover_ref.shape[1]), 1)
    dist = jnp.right_shift(qpos, int(math.log2(SLC_BLOCK))) - blk
    forced = (blk == 0) | ((dist >= 0) & (dist < N_LOCAL))
    cover = cover_ref[...]

    for g in range(NSA_KV):
        qg = _group_queries(q, g)
        o_cmp, p_sum = _attend(qg, kc_ref[0, g], vc_ref[0, g], cmask, want_psum=True)
        p_hi = p_sum.astype(jnp.bfloat16)
        p_lo = (p_sum - p_hi.astype(jnp.float32)).astype(jnp.bfloat16)
        imp = (jnp.dot(p_hi, cover, preferred_element_type=jnp.float32)
               + jnp.dot(p_lo, cover, preferred_element_type=jnp.float32))
        score = jnp.where(forced, FORCE, jnp.where(dist < 0, -FORCE, imp))
        score = jnp.where(blk < n_slc, score, LOWEST)
        sel = jnp.zeros(score.shape, jnp.float32)
        for _ in range(n_sel):
            best = jnp.max(score, axis=-1, keepdims=True)
            first = jnp.min(jnp.where(score == best, blk, cover_ref.shape[1]), axis=-1, keepdims=True)
            hit = blk == first
            sel = jnp.where(hit, 1.0, sel)
            score = jnp.where(hit, LOWEST, score)
        smask = jnp.dot(sel.astype(jnp.bfloat16), expand_ref[:, :slc_len],
                        preferred_element_type=jnp.float32) > 0.5
        o_slc = _attend(qg, slc(0, g, slc_rows), slc(1, g, slc_rows), smask & causal)
        o_win = _attend(qg, win(0, g, win_rows), win(1, g, win_rows), wmask)
        for h in range(HEADS_PER_GROUP):
            head = g * HEADS_PER_GROUP + h
            rows = slice(h * tq, (h + 1) * tq)
            o_ref[0, :, head * HEAD_DIM:(head + 1) * HEAD_DIM] = (
                gates[:, 3 * head:3 * head + 1] * o_cmp[rows]
                + gates[:, 3 * head + 1:3 * head + 2] * o_slc[rows]
                + gates[:, 3 * head + 2:3 * head + 3] * o_win[rows])


def _nsa_kernel(q_ref, tail_ref, kc_ref, vc_ref, slc_ref, win_ref, cover_ref, expand_ref, o_ref, slc_scr, win_scr,
                *, pos0, win_pos0, n_slc, n_sel):
    @pl.when(pl.program_id(1) == 0)
    def _():
        _unpack_planes(slc_ref, slc_scr)
        _unpack_planes(win_ref, win_scr)

    tq = q_ref.shape[1]
    qpos0 = pos0 + pl.program_id(1) * tq
    _for_key_range(qpos0, tq, slc_scr.shape[1], lambda n: _nsa_body(
        q_ref, tail_ref, kc_ref, vc_ref, _kv_planes(slc_scr, NSA_KV), n, _kv_planes(win_scr, NSA_KV),
        win_scr.shape[1], cover_ref, expand_ref, o_ref, qpos0=qpos0, win_pos0=win_pos0, n_slc=n_slc, n_sel=n_sel))


def _nsa_paged_kernel(pt_ref, q_ref, tail_ref, kc_ref, vc_ref, new_ref, win_ref, cover_ref, expand_ref, *rest,
                      pos0, win_pos0, n_slc, n_sel, page_rows):
    page_refs, (o_ref, kv_scr) = rest[:PAGES_PER_STEP], rest[PAGES_PER_STEP:]
    _load_kv_pages(page_refs, new_ref, kv_scr, page_rows, pos0)

    @pl.when(pl.program_id(1) == pl.num_programs(1) - 1)
    def _():
        _nsa_body(q_ref, tail_ref, kc_ref, vc_ref, _kv_planes(kv_scr, NSA_KV), kv_scr.shape[1],
                  _kv_columns(win_ref, NSA_KV), win_ref.shape[1], cover_ref, expand_ref, o_ref,
                  qpos0=pos0, win_pos0=win_pos0, n_slc=n_slc, n_sel=n_sel)


def _nsa_attention(q, tail, kc, vc, slc_rows, win_kv, pos0, win_pos0, n_c, paged=None):
    b, t_len, _ = q.shape
    seq = slc_rows.shape[1] // KV_PLANES + (pos0 if paged else 0)
    tq = Q_TILE if t_len % Q_TILE == 0 else _round_up(t_len, SUBLANES)
    tp = _round_up(t_len, tq)
    lp = _round_up(seq, LANES)
    n_cp = kc.shape[2]
    n_slc = -(-seq // SLC_BLOCK)
    n_sp = _round_up(n_slc, LANES)
    start = jnp.arange(n_cp)[:, None] * CMP_STRIDE
    lo = jnp.arange(n_sp)[None, :] * SLC_BLOCK
    cover = ((start < lo + SLC_BLOCK) & (start + CMP_LEN > lo) & (jnp.arange(n_cp)[:, None] < n_c)
             & (jnp.arange(n_sp)[None, :] < n_slc)).astype(jnp.bfloat16)
    expand = (jnp.arange(lp)[None, :] // SLC_BLOCK == jnp.arange(n_sp)[:, None]).astype(jnp.bfloat16)
    static = dict(pos0=pos0, win_pos0=win_pos0, n_slc=n_slc, n_sel=min(N_SEL, n_slc))
    width = NSA_HEADS * HEAD_DIM
    kv_w = KV_PLANES * HEAD_DIM
    out_shape = jax.ShapeDtypeStruct((b, tp, width), jnp.float32)
    params = pltpu.CompilerParams(dimension_semantics=("arbitrary", "arbitrary"),
                                  vmem_limit_bytes=VMEM_LIMIT_BYTES)
    kc, vc = kc.astype(jnp.bfloat16), vc.astype(jnp.bfloat16)
    if paged is None:
        assert seq == lp and tp == t_len
        lw = win_kv.shape[1] // KV_PLANES
        assert lw % LANES == 0
        return pl.pallas_call(
            functools.partial(_nsa_kernel, **static),
            grid=(b, tp // tq),
            in_specs=[pl.BlockSpec((1, tq, width), lambda i, j: (i, j, 0)),
                      pl.BlockSpec((1, tq, LANES), lambda i, j: (i, j, 0)),
                      pl.BlockSpec((1, NSA_KV, n_cp, HEAD_DIM), lambda i, j: (i, 0, 0, 0)),
                      pl.BlockSpec((1, NSA_KV, n_cp, HEAD_DIM), lambda i, j: (i, 0, 0, 0)),
                      pl.BlockSpec((1, lp * KV_PLANES, HEAD_DIM), lambda i, j: (i, 0, 0)),
                      pl.BlockSpec((1, lw * KV_PLANES, HEAD_DIM), lambda i, j: (i, 0, 0)),
                      pl.BlockSpec((n_cp, n_sp), lambda i, j: (0, 0)),
                      pl.BlockSpec((n_sp, lp), lambda i, j: (0, 0))],
            out_specs=pl.BlockSpec((1, tq, width), lambda i, j: (i, j, 0)),
            out_shape=out_shape,
            scratch_shapes=[pltpu.VMEM((KV_PLANES, lp, HEAD_DIM), jnp.bfloat16),
                            pltpu.VMEM((KV_PLANES, lw, HEAD_DIM), jnp.bfloat16)],
            compiler_params=params, name="nsa_attention",
        )(q, tail, kc, vc, slc_rows, win_kv, cover, expand)

    pool, layer, page_table = paged
    assert tp == tq
    page_rows = pool.shape[2] // KV_PLANES
    lw = _round_up(win_kv.shape[1], LANES)
    win = _pad_rows(win_kv.reshape(b, win_kv.shape[1], -1), lw).astype(jnp.bfloat16)
    new_rows = _pad_rows(slc_rows, (lp - pos0) * KV_PLANES)
    out = pl.pallas_call(
        functools.partial(_nsa_paged_kernel, page_rows=page_rows, **static),
        grid_spec=pltpu.PrefetchScalarGridSpec(
            num_scalar_prefetch=1,
            grid=(b, page_table.shape[1] // PAGES_PER_STEP),
            in_specs=[pl.BlockSpec((1, tq, width), lambda i, j, pt: (i, 0, 0)),
                      pl.BlockSpec((1, tq, LANES), lambda i, j, pt: (i, 0, 0)),
                      pl.BlockSpec((1, NSA_KV, n_cp, HEAD_DIM), lambda i, j, pt: (i, 0, 0, 0)),
                      pl.BlockSpec((1, NSA_KV, n_cp, HEAD_DIM), lambda i, j, pt: (i, 0, 0, 0)),
                      pl.BlockSpec((1, (lp - pos0) * KV_PLANES, HEAD_DIM), lambda i, j, pt: (i, 0, 0)),
                      pl.BlockSpec((1, lw, kv_w), lambda i, j, pt: (i, 0, 0)),
                      pl.BlockSpec((n_cp, n_sp), lambda i, j, pt: (0, 0)),
                      pl.BlockSpec((n_sp, lp), lambda i, j, pt: (0, 0))]
                     + _page_specs(pool, layer, page_table.shape[1]),
            out_specs=pl.BlockSpec((1, tq, width), lambda i, j, pt: (i, 0, 0)),
            scratch_shapes=[pltpu.VMEM((KV_PLANES, lp, HEAD_DIM), jnp.bfloat16)]),
        out_shape=out_shape, compiler_params=params, name="nsa_attention_paged",
    )(page_table, _pad_rows(q, tp), _pad_rows(tail, tp), kc, vc, new_rows, win, cover, expand,
      *([pool] * PAGES_PER_STEP))
    return out[:, :t_len]


def _compress_mlp(x, w1_ref, pos_ref, w2_ref, b2_ref, s):
    n_sub = x.shape[0]
    half = CMP_STRIDE * HEAD_DIM
    h_lo = jnp.dot(x, w1_ref[s, :half].astype(jnp.bfloat16), preferred_element_type=jnp.float32)
    h_hi = jnp.dot(x, w1_ref[s, half:].astype(jnp.bfloat16), preferred_element_type=jnp.float32)
    h = h_lo + pltpu.roll(h_hi, n_sub - 1, 0) + pos_ref[s]
    return jnp.dot(jax.nn.gelu(h).astype(jnp.bfloat16), w2_ref[s].astype(jnp.bfloat16),
                   preferred_element_type=jnp.float32) + b2_ref[s]


def _compress_kernel(x_ref, w1_ref, pos_ref, w2_ref, b2_ref, o_ref):
    n_sub = o_ref.shape[3]
    for c in range(KV_PLANES):
        s, g = divmod(c, NSA_KV)
        x = jnp.concatenate([x_ref[0, pl.ds(r * KV_PLANES + c, n_sub, stride=CMP_STRIDE * KV_PLANES), :]
                             .astype(jnp.bfloat16) for r in range(CMP_STRIDE)], axis=1)
        o_ref[0, s, g] = _compress_mlp(x, w1_ref, pos_ref, w2_ref, b2_ref, s)


def _compress_paged_kernel(pt_ref, w1_ref, pos_ref, w2_ref, b2_ref, *rest, page_rows):
    page_refs, (o_ref, x_scr) = rest[:PAGES_PER_STEP], rest[PAGES_PER_STEP:]
    j = pl.program_id(1)
    planes = x_scr.shape[0]
    per_page = page_rows // CMP_STRIDE
    for k, page in enumerate(page_refs):
        row0 = pl.multiple_of((j * len(page_refs) + k) * per_page, per_page)
        for c in range(planes):
            for s in range(CMP_STRIDE):
                x_scr[c, pl.ds(row0, per_page), s * HEAD_DIM:(s + 1) * HEAD_DIM] = (
                    page[0, 0, pl.ds(s * planes + c, per_page, stride=CMP_STRIDE * planes), :])

    @pl.when(j == pl.num_programs(1) - 1)
    def _():
        for c in range(planes):
            s, g = divmod(c, NSA_KV)
            o_ref[0, s, g] = _compress_mlp(x_scr[c].astype(jnp.bfloat16), w1_ref, pos_ref, w2_ref, b2_ref, s)


def _compress(rows, w1, pos_emb, w2, b2, paged=None):
    w1 = w1.reshape(2, CMP_LEN * HEAD_DIM, HEAD_DIM)
    pos_bias = jnp.einsum('kld,klde->ke', pos_emb, w1.reshape(2, CMP_LEN, HEAD_DIM, HEAD_DIM))[:, None, :]
    params = dict(vmem_limit_bytes=VMEM_LIMIT_BYTES)
    if paged is None:
        b = rows.shape[0]
        seq = rows.shape[1] // KV_PLANES
        assert seq % CMP_STRIDE == 0
        n_sub = seq // CMP_STRIDE
        const = lambda *shape: pl.BlockSpec(shape, lambda i: (0,) * len(shape))
        return pl.pallas_call(
            _compress_kernel,
            grid=(b,),
            in_specs=[pl.BlockSpec((1, seq * KV_PLANES, HEAD_DIM), lambda i: (i, 0, 0)),
                      const(2, CMP_LEN * HEAD_DIM, HEAD_DIM), const(2, 1, HEAD_DIM),
                      const(2, HEAD_DIM, HEAD_DIM), const(2, 1, HEAD_DIM)],
            out_specs=pl.BlockSpec((1, 2, NSA_KV, n_sub, HEAD_DIM), lambda i: (i, 0, 0, 0, 0)),
            out_shape=jax.ShapeDtypeStruct((b, 2, NSA_KV, n_sub, HEAD_DIM), jnp.float32),
            compiler_params=pltpu.CompilerParams(dimension_semantics=("arbitrary",), **params),
            name="nsa_compress",
        )(rows, w1, pos_bias, w2, b2[:, None, :])

    pool, layer, page_table = paged
    b, n_pages = page_table.shape
    planes = KV_PLANES
    page_rows = pool.shape[2] // planes
    n_sub = n_pages * page_rows // CMP_STRIDE
    const = lambda *shape: pl.BlockSpec(shape, lambda i, j, pt: (0,) * len(shape))
    return pl.pallas_call(
        functools.partial(_compress_paged_kernel, page_rows=page_rows),
        grid_spec=pltpu.PrefetchScalarGridSpec(
            num_scalar_prefetch=1,
            grid=(b, n_pages // PAGES_PER_STEP),
            in_specs=[const(2, CMP_LEN * HEAD_DIM, HEAD_DIM), const(2, 1, HEAD_DIM),
                      const(2, HEAD_DIM, HEAD_DIM), const(2, 1, HEAD_DIM)]
                     + _page_specs(pool, layer, n_pages),
            out_specs=pl.BlockSpec((1, 2, NSA_KV, n_sub, HEAD_DIM), lambda i, j, pt: (i, 0, 0, 0, 0)),
            scratch_shapes=[pltpu.VMEM((planes, n_sub, CMP_STRIDE * HEAD_DIM), jnp.float32)]),
        out_shape=jax.ShapeDtypeStruct((b, 2, NSA_KV, n_sub, HEAD_DIM), jnp.float32),
        compiler_params=pltpu.CompilerParams(dimension_semantics=("arbitrary",) * 2, **params),
        name="nsa_compress_paged",
    )(page_table, w1, pos_bias, w2, b2[:, None, :], *([pool] * PAGES_PER_STEP))


INT_MIN = -2 ** 31
IDX_W_SCALE = (IDX_HEADS * IDX_DIM) ** -0.5


def _count(pred):
    return jnp.sum(jnp.where(pred, 1.0, 0.0), axis=-1, keepdims=True)


def _dsa_body(q_ref, kv, iq_ref, tail_ref, ik, o_ref, *, qpos0, k_top):
    tq = q_ref.shape[1]
    lp = ik.shape[1]
    rows = slice(0, lp)
    qpos = qpos0 + lax.broadcasted_iota(jnp.int32, (tq, 1), 0)
    kpos = lax.broadcasted_iota(jnp.int32, (1, lp), 1)
    causal = kpos <= qpos

    iq = iq_ref[0].astype(jnp.bfloat16)
    iw = tail_ref[0][:, TAIL_IW[0]:TAIL_IW[1]] * IDX_W_SCALE
    score = jnp.zeros((tq, lp), jnp.float32)
    for h in range(IDX_HEADS):
        logits = jnp.dot(iq[:, h * IDX_DIM:(h + 1) * IDX_DIM], ik, preferred_element_type=jnp.float32)
        score = score + iw[:, h:h + 1] * jnp.maximum(logits, 0.0)
    score = jnp.where(causal, score, NEG)
    score = jnp.where(score == 0.0, 0.0, score)
    bits = lax.bitcast_convert_type(score, jnp.int32)
    key = bits ^ (jnp.right_shift(bits, 31) & 0x7FFFFFFF)

    k_f = float(k_top)
    thr = jnp.where(_count(key >= 0) >= k_f, 0, INT_MIN).astype(jnp.int32)

    def refine(i, thr):
        cand = thr | jnp.left_shift(jnp.int32(1), 30 - i)
        return jnp.where(_count(key >= cand) >= k_f, cand, thr)

    thr = lax.fori_loop(0, 31, refine, thr)
    above = key > thr
    tied = key == thr
    need = k_f - _count(above)

    def widen(i, c):
        cand = c | jnp.left_shift(jnp.int32(1), 14 - i)
        return jnp.where(_count(tied & (kpos < cand)) < need, cand, c)

    bound = lax.fori_loop(0, 15, widen, jnp.zeros((tq, 1), jnp.int32))
    mask = causal & (above | (tied & (kpos <= bound)))

    q = q_ref[0]
    for g in range(DSA_KV):
        o = _attend(_group_queries(q, g), kv(0, g, rows), kv(1, g, rows), mask)
        for h in range(HEADS_PER_GROUP):
            head = g * HEADS_PER_GROUP + h
            o_ref[0, :, head * HEAD_DIM:(head + 1) * HEAD_DIM] = o[h * tq:(h + 1) * tq]


def _dsa_kernel(q_ref, kv_ref, iq_ref, tail_ref, ik_ref, o_ref, kv_scr, *, pos0, k_top):
    @pl.when(pl.program_id(1) == 0)
    def _():
        _unpack_planes(kv_ref, kv_scr)

    tq = q_ref.shape[1]
    qpos0 = pos0 + pl.program_id(1) * tq
    _for_key_range(qpos0, tq, kv_scr.shape[1], lambda n: _dsa_body(
        q_ref, _kv_planes(kv_scr, DSA_KV), iq_ref, tail_ref, ik_ref[0, :, :n], o_ref, qpos0=qpos0, k_top=k_top))


def _dsa_paged_kernel(pt_ref, q_ref, iq_ref, tail_ref, new_kv_ref, new_ik_ref, *rest, pos0, k_top, page_rows):
    n = PAGES_PER_STEP
    kv_pages, ik_pages, (o_ref, kv_scr, ik_scr) = rest[:n], rest[n:2 * n], rest[2 * n:]
    _load_kv_pages(kv_pages, new_kv_ref, kv_scr, page_rows, pos0)
    j = pl.program_id(1)
    for k, page in enumerate(ik_pages):
        row0 = pl.multiple_of((j * n + k) * page_rows, page_rows)
        ik_scr[:, pl.ds(row0, page_rows)] = page[0, 0].astype(jnp.bfloat16)

    @pl.when(j == 0)
    def _():
        ik_scr[:, pos0:] = new_ik_ref[0].astype(jnp.bfloat16)

    @pl.when(j == pl.num_programs(1) - 1)
    def _():
        _dsa_body(q_ref, _kv_planes(kv_scr, DSA_KV), iq_ref, tail_ref, ik_scr[...], o_ref, qpos0=pos0, k_top=k_top)


def _dsa_attention(q, kv_rows, idx_k, idx_q, tail, pos0, paged=None):
    b, t_len, _ = q.shape
    seq = idx_k.shape[1] + (pos0 if paged else 0)
    assert seq < 2 ** 15
    tq = Q_TILE if t_len % Q_TILE == 0 else _round_up(t_len, SUBLANES)
    tp = _round_up(t_len, tq)
    lp = _round_up(seq, LANES)
    static = dict(pos0=pos0, k_top=min(TOPK_MAX, seq // 4))
    width = DSA_HEADS * HEAD_DIM
    out_shape = jax.ShapeDtypeStruct((b, tp, width), jnp.float32)
    params = pltpu.CompilerParams(dimension_semantics=("arbitrary", "arbitrary"),
                                  vmem_limit_bytes=VMEM_LIMIT_BYTES)
    if paged is None:
        assert seq == lp and tp == t_len
        return pl.pallas_call(
            functools.partial(_dsa_kernel, **static),
            grid=(b, tp // tq),
            in_specs=[pl.BlockSpec((1, tq, width), lambda i, j: (i, j, 0)),
                      pl.BlockSpec((1, lp * KV_PLANES, HEAD_DIM), lambda i, j: (i, 0, 0)),
                      pl.BlockSpec((1, tq, IDX_HEADS * IDX_DIM), lambda i, j: (i, j, 0)),
                      pl.BlockSpec((1, tq, LANES), lambda i, j: (i, j, 0)),
                      pl.BlockSpec((1, IDX_DIM, lp), lambda i, j: (i, 0, 0))],
            out_specs=pl.BlockSpec((1, tq, width), lambda i, j: (i, j, 0)),
            out_shape=out_shape,
            scratch_shapes=[pltpu.VMEM((KV_PLANES, lp, HEAD_DIM), jnp.bfloat16)],
            compiler_params=params, name="dsa_attention",
        )(q, kv_rows, idx_q, tail, jnp.swapaxes(idx_k, 1, 2).astype(jnp.bfloat16))

    kv_pool, idx_pool, layer, page_table = paged
    assert tp == tq
    page_rows = idx_pool.shape[3]
    n_pages = page_table.shape[1]
    out = pl.pallas_call(
        functools.partial(_dsa_paged_kernel, page_rows=page_rows, **static),
        grid_spec=pltpu.PrefetchScalarGridSpec(
            num_scalar_prefetch=1,
            grid=(b, n_pages // PAGES_PER_STEP),
            in_specs=[pl.BlockSpec((1, tq, width), lambda i, j, pt: (i, 0, 0)),
                      pl.BlockSpec((1, tq, IDX_HEADS * IDX_DIM), lambda i, j, pt: (i, 0, 0)),
                      pl.BlockSpec((1, tq, LANES), lambda i, j, pt: (i, 0, 0)),
                      pl.BlockSpec((1, (lp - pos0) * KV_PLANES, HEAD_DIM), lambda i, j, pt: (i, 0, 0)),
                      pl.BlockSpec((1, IDX_DIM, lp - pos0), lambda i, j, pt: (i, 0, 0))]
                     + _page_specs(kv_pool, layer, n_pages) + _page_specs(idx_pool, layer, n_pages),
            out_specs=pl.BlockSpec((1, tq, width), lambda i, j, pt: (i, 0, 0)),
            scratch_shapes=[pltpu.VMEM((KV_PLANES, lp, HEAD_DIM), jnp.bfloat16),
                            pltpu.VMEM((IDX_DIM, lp), jnp.bfloat16)]),
        out_shape=out_shape, compiler_params=params, name="dsa_attention_paged",
    )(page_table, _pad_rows(q, tp), _pad_rows(idx_q, tp), _pad_rows(tail, tp),
      _pad_rows(kv_rows, (lp - pos0) * KV_PLANES), jnp.swapaxes(_pad_rows(idx_k, lp - pos0), 1, 2),
      *([kv_pool] * PAGES_PER_STEP), *([idx_pool] * PAGES_PER_STEP))
    return out[:, :t_len]


def _rms_norm(x, g):
    xf = x.astype(jnp.float32)
    y = xf * lax.rsqrt(jnp.mean(xf * xf, axis=-1, keepdims=True) + NORM_EPS)
    return (y * g.astype(jnp.float32)).astype(x.dtype)


def _rope_partial(x, pos):
    rot = x.shape[-1] // ROT_DIV
    half = rot // 2
    freqs = ROPE_THETA ** (-jnp.arange(half, dtype=jnp.float32) / half)
    ang = pos.astype(jnp.float32)[:, None] * freqs[None, :]
    cos, sin = jnp.cos(ang)[:, None, :], jnp.sin(ang)[:, None, :]
    xr = x[..., :rot].astype(jnp.float32)
    x1, x2 = xr[..., :half], xr[..., half:]
    xr = jnp.concatenate([x1 * cos - x2 * sin, x1 * sin + x2 * cos], axis=-1)
    return jnp.concatenate([xr.astype(x.dtype), x[..., rot:]], axis=-1)


def _nsa_mixer(q, tail, cmp_rows, slc_rows, win_kv, win_pos0, pos0, p, paged=None):
    if paged is None:
        cmp = _compress(cmp_rows, p['cmp_w1'], p['cmp_pos'], p['cmp_w2'], p['cmp_b2'])
    else:
        cmp = _compress(None, p['cmp_w1'], p['cmp_pos'], p['cmp_w2'], p['cmp_b2'], paged=(paged[0],) + paged[2:])
    n_sub = cmp.shape[3]
    c_pos = jnp.arange(n_sub, dtype=jnp.int32) * CMP_STRIDE + (CMP_LEN - 1)
    kc = jnp.swapaxes(_rope_partial(_rms_norm(jnp.swapaxes(cmp[:, 0], 1, 2), p['a_k_norm'][0]), c_pos), 1, 2)
    return _nsa_attention(q, tail, kc, cmp[:, 1], slc_rows, win_kv, pos0, win_pos0, n_sub - 1,
                          paged=None if paged is None else paged[1:])


def _rwkv7_mixer(cur, prev, state0, p):
    b, T, _ = cur.shape
    xm = cur + (prev - cur) * p['rwkv_mu']
    r, k, v, w_lo, a_lo, g_lo = jnp.split(xm, [RWKV_DIM, 2 * RWKV_DIM, 3 * RWKV_DIM, 3 * RWKV_DIM + DECAY_LORA,
                                               3 * RWKV_DIM + DECAY_LORA + AAA_LORA], axis=-1)
    w = -jax.nn.softplus(-(p['rwkv_w0'] + jnp.tanh(w_lo) @ p['rwkv_w2'])) - 0.5
    a = jax.nn.sigmoid(p['rwkv_a0'] + a_lo @ p['rwkv_a2'])
    g = jax.nn.sigmoid(g_lo) @ p['rwkv_g2']
    heads = lambda t: t.reshape(b, T, RWKV_HEADS, RWKV_HEAD).astype(jnp.float32)
    kk = heads(k * p['rwkv_k_k'])
    kk = kk * lax.rsqrt(jnp.sum(kk * kk, axis=-1, keepdims=True) + 1e-12)
    kh = heads(k * (1.0 + (a - 1.0) * p['rwkv_k_a']))
    rh, vh, ah = heads(r), heads(v), heads(a)
    decay = jnp.exp(-jnp.exp(heads(w)))
    y, s_t = _rwkv_scan(state0.astype(jnp.float32), rh, decay, kh, vh, -kk, kk * ah)
    mu = jnp.mean(y, axis=-1, keepdims=True)
    var = jnp.mean(jnp.square(y - mu), axis=-1, keepdims=True)
    y = ((y - mu) * lax.rsqrt(var + RWKV_GN_EPS)).reshape(b, T, RWKV_DIM) * p['rwkv_ln_w'] + p['rwkv_ln_b']
    y = y + (jnp.sum(rh * kh * p['rwkv_r_k'], axis=-1, keepdims=True) * vh).reshape(b, T, RWKV_DIM)
    return (y * g).astype(cur.dtype), s_t.astype(state0.dtype)


def _hybrid_layer(x, p, big, layer, pos0, past, ropes):
    b, T, d = x.shape
    xn = _rms_norm(x, p['norm_attn'])
    c0 = COL['c'][0]
    if past is None:
        proj = _proj(xn, big['w_in'], layer)
        c_cur = proj[..., c0:c0 + C_COLS]
        c_prev = jnp.concatenate([jnp.zeros_like(c_cur[:, :1]), c_cur[:, :-1]], axis=1)
    else:
        prev = past['shift'][:, None].astype(xn.dtype)
        full = _proj(jnp.concatenate([prev, xn], axis=1), big['w_in'], layer)
        proj = full[:, 1:]
        c_cur = proj[..., c0:c0 + C_COLS]
        c_prev = full[:, :-1, c0:c0 + C_COLS]

    proj2 = proj.reshape(b * T, PROJ_COLS)
    gains = jnp.stack([p['a_q_norm'], p['a_k_norm'][1], p['a_k_norm'][2], p['b_q_norm'], p['b_k_norm'],
                       jnp.pad(p['idx_k_norm'], (0, LANES - IDX_DIM)),
                       jnp.zeros((LANES,), jnp.float32), jnp.zeros((LANES,), jnp.float32)])
    q_a, q_d, idx_q, cmp_rows, slc_rows, win_rows, dsa_rows, idx_rows = _post_proj(proj2, *ropes, gains)
    per_seq = lambda t: t.reshape((b, -1) + t.shape[1:])
    q_a, q_d, idx_q, idx_rows = per_seq(q_a), per_seq(q_d), per_seq(idx_q), per_seq(idx_rows)
    cmp_rows, slc_rows, win_rows, dsa_rows = per_seq(cmp_rows), per_seq(slc_rows), per_seq(win_rows), per_seq(dsa_rows)
    t0 = COL['tail'][0]
    tail = proj[..., t0:t0 + LANES]
    as_cache = lambda t: t.reshape(b, T, 2, NSA_KV, HEAD_DIM)

    if past is None:
        n_keep = min(WINDOW, T)
        win_state = as_cache(win_rows)[:, -n_keep:]
        rwkv0 = jnp.zeros((b, RWKV_HEADS, RWKV_HEAD, RWKV_HEAD), x.dtype)
        o_a = _nsa_mixer(q_a, tail, cmp_rows, slc_rows, win_rows, pos0, pos0, p)
        o_b = _dsa_attention(q_d, dsa_rows, idx_rows, idx_q, tail, pos0)
    else:
        assert (pos0 + T) // CMP_STRIDE * CMP_STRIDE <= pos0
        win_all = jnp.concatenate([past['nsa_win'], as_cache(win_rows)], axis=1)
        n_keep = past['nsa_win'].shape[1]
        win_state = win_all[:, -n_keep:]
        rwkv0 = past['rwkv']
        page_table = past['page_table']
        o_a = _nsa_mixer(q_a, tail, None, slc_rows, win_all, pos0 - n_keep, pos0, p,
                         paged=(past['cmp_pool'], past['slc_pool'], layer, page_table))
        o_b = _dsa_attention(q_d, dsa_rows, idx_rows, idx_q, tail, pos0,
                             paged=(past['dsa_pool'], past['idx_pool'], layer, page_table))
    o_c, rwkv_new = _rwkv7_mixer(c_cur, c_prev, rwkv0, p)

    mixed = _merge(o_a.reshape(b * T, -1), o_b.reshape(b * T, -1), o_c.reshape(b * T, -1), proj2, big, layer)
    x = _proj(mixed.reshape(b, T, d), big['w_o'], layer, residual=x)
    x = _ffn(x, p['norm_ffn'], big['w_ffn_in'], big['w_ffn_out'], layer)
    new_state = {'nsa_cmp': as_cache(cmp_rows), 'nsa_slc': as_cache(slc_rows), 'dsa_kv': as_cache(dsa_rows),
                 'dsa_idx': idx_rows, 'nsa_win': win_state, 'rwkv': rwkv_new, 'shift': xn[:, -1]}
    return x, new_state


_PARAM_NAMES = ('norm_attn', 'w_in', 'a_q_norm', 'a_k_norm', 'cmp_w1', 'cmp_pos', 'cmp_w2', 'cmp_b2',
                'b_q_norm', 'b_k_norm', 'idx_k_norm', 'rwkv_mu', 'rwkv_w0', 'rwkv_w2', 'rwkv_a0',
                'rwkv_a2', 'rwkv_g2', 'rwkv_k_k', 'rwkv_k_a', 'rwkv_r_k', 'rwkv_ln_w', 'rwkv_ln_b',
                'p_a', 'p_b', 'p_c', 'w_o', 'norm_ffn', 'w_ffn_in', 'w_ffn_out')


def kernel(x_prompt, x_sample, cache_nsa_cmp, cache_nsa_slc, cache_dsa_kv, cache_dsa_idx, state_nsa_win, state_rwkv, state_shift, page_table, norm_attn, w_in, a_q_norm, a_k_norm, cmp_w1, cmp_pos, cmp_w2, cmp_b2, b_q_norm, b_k_norm, idx_k_norm, rwkv_mu, rwkv_w0, rwkv_w2, rwkv_a0, rwkv_a2, rwkv_g2, rwkv_k_k, rwkv_k_a, rwkv_r_k, rwkv_ln_w, rwkv_ln_b, p_a, p_b, p_c, w_o, norm_ffn, w_ffn_in, w_ffn_out):
    weights = dict(zip(_PARAM_NAMES, (norm_attn, w_in, a_q_norm, a_k_norm, cmp_w1, cmp_pos, cmp_w2, cmp_b2,
                                      b_q_norm, b_k_norm, idx_k_norm, rwkv_mu, rwkv_w0, rwkv_w2, rwkv_a0,
                                      rwkv_a2, rwkv_g2, rwkv_k_k, rwkv_k_a, rwkv_r_k, rwkv_ln_w, rwkv_ln_b,
                                      p_a, p_b, p_c, w_o, norm_ffn, w_ffn_in, w_ffn_out)))
    depth = w_in.shape[0]
    past_len = page_table.shape[1] * cache_nsa_cmp.shape[2]
    yp, ys = x_prompt, x_sample
    st_p, st_s = [], []
    kv_pool = lambda c: c.reshape(c.shape[:2] + (-1, HEAD_DIM))
    big_names = ('w_in', 'p_a', 'p_b', 'p_c', 'w_o', 'w_ffn_in', 'w_ffn_out')
    big = {name: weights[name].astype(jnp.bfloat16) for name in big_names[1:]}
    big['w_in'] = _regroup_w_in(w_in)

    def ropes(pos0, bsz, t_len):
        pos = jnp.tile(pos0 + jnp.arange(t_len, dtype=jnp.int32), bsz)
        return _rope_tables(pos, LANES, HEAD_DIM), _rope_tables(pos, LANES, IDX_DIM)

    ropes_p = ropes(0, *x_prompt.shape[:2])
    ropes_s = ropes(past_len, *x_sample.shape[:2])
    for l in range(depth):
        p = {name: w[l] for name, w in weights.items() if name not in big_names}
        past = {'cmp_pool': kv_pool(cache_nsa_cmp), 'slc_pool': kv_pool(cache_nsa_slc),
                'dsa_pool': kv_pool(cache_dsa_kv), 'idx_pool': jnp.swapaxes(cache_dsa_idx, 2, 3),
                'page_table': page_table,
                'nsa_win': state_nsa_win[l], 'rwkv': state_rwkv[l], 'shift': state_shift[l]}
        yp, sp = _hybrid_layer(yp, p, big, l, 0, None, ropes_p)
        ys, ss = _hybrid_layer(ys, p, big, l, past_len, past, ropes_s)
        st_p.append(sp)
        st_s.append(ss)
    stack = lambda sts, name: jnp.stack([s[name] for s in sts])
    return (yp, ys,
            stack(st_p, 'nsa_cmp'), stack(st_s, 'nsa_cmp'),
            stack(st_p, 'nsa_slc'), stack(st_s, 'nsa_slc'),
            stack(st_p, 'dsa_kv'), stack(st_s, 'dsa_kv'),
            stack(st_p, 'dsa_idx'), stack(st_s, 'dsa_idx'),
            stack(st_p, 'nsa_win'), stack(st_s, 'nsa_win'),
            stack(st_p, 'rwkv'), stack(st_s, 'rwkv'),
            stack(st_p, 'shift'), stack(st_s, 'shift'))
```

```python
import functools
import math

import jax
import jax.numpy as jnp
from jax import lax
from jax.experimental import pallas as pl
from jax.experimental.pallas import tpu as pltpu

D_MODEL = 2048
HEAD_DIM = 128
ROT_DIV = 4
ROPE_THETA = 500000.0
NORM_EPS = 1e-6

NSA_HEADS = 8
NSA_KV = 2
CMP_STRIDE = 16
CMP_LEN = 2 * CMP_STRIDE
SLC_BLOCK = 64
N_SEL = 8
N_LOCAL = 2
WINDOW = 512

DSA_HEADS = 8
DSA_KV = 2
IDX_HEADS = 8
IDX_DIM = 64
TOPK_MAX = 256

RWKV_HEADS = 16
RWKV_HEAD = 64
RWKV_DIM = RWKV_HEADS * RWKV_HEAD
DECAY_LORA = 64
AAA_LORA = 64
GATE_LORA = 160
RWKV_GN_EPS = 64e-5

NEG = -1e30
FORCE = 1e9
LOWEST = -3e38

IN_SPLITS = (
    NSA_HEADS * HEAD_DIM,
    6 * NSA_KV * HEAD_DIM,
    3 * NSA_HEADS,
    DSA_HEADS * HEAD_DIM,
    2 * DSA_KV * HEAD_DIM,
    IDX_HEADS * IDX_DIM,
    IDX_DIM,
    IDX_HEADS,
    3 * RWKV_DIM + DECAY_LORA + AAA_LORA + GATE_LORA,
    3 * D_MODEL,
)

LANES = 128
SUBLANES = 8
VMEM_LIMIT_BYTES = 56 * 1024 * 1024

HEADS_PER_GROUP = NSA_HEADS // NSA_KV
ATTN_SCALE = HEAD_DIM ** -0.5
LOG2_E = math.log2(math.e)
Q_TILE = 128

_NT = (((1,), (1,)), ((), ()))


def _round_up(n, m):
    return -(-n // m) * m


def _mm_kernel(x_ref, w_ref, *rest):
    r_ref = rest[0] if len(rest) == 3 else None
    o_ref, xb_ref = rest[-2:]

    @pl.when(pl.program_id(1) == 0)
    def _():
        xb_ref[...] = x_ref[...].astype(jnp.bfloat16)

    acc = jnp.dot(xb_ref[...], w_ref[0], preferred_element_type=jnp.float32)
    o_ref[...] = acc if r_ref is None else r_ref[...] + acc


def _pick_tile(n, prefs):
    for t in prefs:
        if n % t == 0:
            return t
    return n


def _mm(x, w, layer, residual=None):
    m, k = x.shape
    n = w.shape[2]
    tm = _pick_tile(m, (1024, 512, 256, 128))
    tn = _pick_tile(n, (512, 384, 256, 128))
    tile = pl.BlockSpec((tm, tn), lambda i, j: (i, j))
    return pl.pallas_call(
        _mm_kernel,
        grid=(m // tm, n // tn),
        in_specs=[pl.BlockSpec((tm, k), lambda i, j: (i, 0)),
                  pl.BlockSpec((1, k, tn), lambda i, j: (layer, 0, j))] + ([] if residual is None else [tile]),
        out_specs=tile,
        out_shape=jax.ShapeDtypeStruct((m, n), jnp.float32),
        scratch_shapes=[pltpu.VMEM((tm, k), jnp.bfloat16)],
        compiler_params=pltpu.CompilerParams(
            dimension_semantics=("arbitrary", "arbitrary"),
            vmem_limit_bytes=VMEM_LIMIT_BYTES),
        name="dense_proj",
    )(x, w, *([] if residual is None else [residual]))


def _proj(x, w, layer, residual=None):
    lead = x.shape[:-1]
    res = None if residual is None else residual.reshape(-1, residual.shape[-1])
    return _mm(x.reshape(-1, x.shape[-1]), w, layer, res).reshape(lead + (w.shape[2],))


FFN_TILE = 512


def _ffn_kernel(x_ref, gain_ref, wg_ref, wu_ref, wo_ref, o_ref, hb_ref, acc_ref):
    f = pl.program_id(1)

    @pl.when(f == 0)
    def _():
        x = x_ref[...]
        y = x * lax.rsqrt(jnp.mean(x * x, axis=-1, keepdims=True) + NORM_EPS)
        hb_ref[...] = (y * gain_ref[...]).astype(jnp.bfloat16)
        acc_ref[...] = jnp.zeros_like(acc_ref)

    hb = hb_ref[...]
    gate = jnp.dot(hb, wg_ref[0], preferred_element_type=jnp.float32)
    up = jnp.dot(hb, wu_ref[0], preferred_element_type=jnp.float32)
    act = (jax.nn.silu(gate) * up).astype(jnp.bfloat16)
    acc_ref[...] += jnp.dot(act, wo_ref[0], preferred_element_type=jnp.float32)

    @pl.when(f == pl.num_programs(1) - 1)
    def _():
        o_ref[...] = x_ref[...] + acc_ref[...]


def _ffn(x, gain, w_in, w_out, layer):
    lead = x.shape[:-1]
    x2 = x.reshape(-1, x.shape[-1])
    m, d = x2.shape
    n_f = w_out.shape[1] // FFN_TILE
    tm = _pick_tile(m, (512, 256, 128))
    out = pl.pallas_call(
        _ffn_kernel,
        grid=(m // tm, n_f),
        in_specs=[pl.BlockSpec((tm, d), lambda i, f: (i, 0)),
                  pl.BlockSpec((1, d), lambda i, f: (0, 0)),
                  pl.BlockSpec((1, d, FFN_TILE), lambda i, f: (layer, 0, f)),
                  pl.BlockSpec((1, d, FFN_TILE), lambda i, f: (layer, 0, n_f + f)),
                  pl.BlockSpec((1, FFN_TILE, d), lambda i, f: (layer, f, 0))],
        out_specs=pl.BlockSpec((tm, d), lambda i, f: (i, 0)),
        out_shape=jax.ShapeDtypeStruct((m, d), jnp.float32),
        scratch_shapes=[pltpu.VMEM((tm, d), jnp.bfloat16), pltpu.VMEM((tm, d), jnp.float32)],
        compiler_params=pltpu.CompilerParams(
            dimension_semantics=("arbitrary", "arbitrary"),
            vmem_limit_bytes=VMEM_LIMIT_BYTES),
        name="ffn_block",
    )(x2, gain[None, :], w_in, w_in, w_out)
    return out.reshape(lead + (d,))


RWKV_HEADS_PER_BLOCK = LANES // 2
RWKV_V_HALF = RWKV_HEAD // 2


def _rwkv_scan_kernel(r_ref, w_ref, k_ref, a_ref, b_ref, v_ref, s0_ref, y_ref, s_out_ref, s_ref, rows_ref):
    c = pl.program_id(1)

    @pl.when(c == 0)
    def _():
        s_ref[...] = s0_ref[0]

    steps = r_ref.shape[1]
    low_half = lax.broadcasted_iota(jnp.int32, (RWKV_V_HALF, LANES), 1) < RWKV_HEADS_PER_BLOCK

    def step(t, carry):
        for n, ref in enumerate((r_ref, w_ref, k_ref, a_ref, b_ref)):
            tile = ref[0, t]
            swapped = pltpu.roll(tile, RWKV_HEADS_PER_BLOCK, 1)
            rows_ref[n, :RWKV_V_HALF] = jnp.where(low_half, tile, swapped)
            rows_ref[n, RWKV_V_HALF:] = jnp.where(low_half, swapped, tile)
        parts = [jnp.zeros((RWKV_V_HALF, LANES), jnp.float32) for _ in range(4)]
        for k in range(RWKV_HEAD):
            parts[k % 4] = parts[k % 4] + s_ref[k] * rows_ref[3, k:k + 1, :]
        sa = (parts[0] + parts[1]) + (parts[2] + parts[3])
        vt = v_ref[0, t]
        ys = [jnp.zeros((RWKV_V_HALF, LANES), jnp.float32) for _ in range(4)]
        for k in range(RWKV_HEAD):
            s_new = (s_ref[k] * rows_ref[1, k:k + 1, :] + sa * rows_ref[4, k:k + 1, :]
                     + vt * rows_ref[2, k:k + 1, :])
            s_ref[k] = s_new
            ys[k % 4] = ys[k % 4] + s_new * rows_ref[0, k:k + 1, :]
        y_ref[0, t] = (ys[0] + ys[1]) + (ys[2] + ys[3])
        return carry

    lax.fori_loop(0, steps, step, 0)

    @pl.when(c == pl.num_programs(1) - 1)
    def _():
        s_out_ref[0] = s_ref[...]


def _rwkv_scan(state0, r, w, k, v, a, b):
    bsz, t_len, n_h, n = r.shape
    heads = bsz * n_h
    hb = RWKV_HEADS_PER_BLOCK
    nblk = heads // hb
    tc = _pick_tile(t_len, (128, 64, 32, 16, 8, 4))

    def pack(x):
        x = jnp.transpose(x.reshape(bsz, t_len, n_h, 2, RWKV_V_HALF), (1, 4, 3, 0, 2))
        x = x.reshape(t_len, RWKV_V_HALF, 2, nblk, hb)
        return jnp.transpose(x, (3, 0, 1, 2, 4)).reshape(nblk, t_len, RWKV_V_HALF, LANES)

    s0 = state0.reshape(nblk, hb, 2, RWKV_V_HALF, n)
    s0 = jnp.transpose(s0, (0, 4, 3, 2, 1)).reshape(nblk, n, RWKV_V_HALF, LANES)

    val_spec = pl.BlockSpec((1, tc, RWKV_V_HALF, LANES), lambda i, c: (i, c, 0, 0))
    st_spec = pl.BlockSpec((1, n, RWKV_V_HALF, LANES), lambda i, c: (i, 0, 0, 0))
    y, s_out = pl.pallas_call(
        _rwkv_scan_kernel,
        grid=(nblk, t_len // tc),
        in_specs=[val_spec] * 6 + [st_spec],
        out_specs=[val_spec, st_spec],
        out_shape=[jax.ShapeDtypeStruct((nblk, t_len, RWKV_V_HALF, LANES), jnp.float32),
                   jax.ShapeDtypeStruct((nblk, n, RWKV_V_HALF, LANES), jnp.float32)],
        scratch_shapes=[pltpu.VMEM((n, RWKV_V_HALF, LANES), jnp.float32),
                        pltpu.VMEM((5, n, LANES), jnp.float32)],
        compiler_params=pltpu.CompilerParams(
            dimension_semantics=("arbitrary", "arbitrary"),
            vmem_limit_bytes=VMEM_LIMIT_BYTES),
        name="rwkv7_scan",
    )(pack(r), pack(w), pack(k), pack(a), pack(b), pack(v), s0)

    y = y.reshape(nblk, t_len, RWKV_V_HALF, 2, hb)
    y = jnp.transpose(y, (0, 4, 1, 3, 2)).reshape(bsz, n_h, t_len, n)
    y = jnp.transpose(y, (0, 2, 1, 3))
    s_out = s_out.reshape(nblk, n, RWKV_V_HALF, 2, hb)
    s_out = jnp.transpose(s_out, (0, 4, 3, 2, 1)).reshape(bsz, n_h, n, n)
    return y, s_out


_ORIG = dict(zip(('a_q', 'a_kv', 'a_gate', 'd_q', 'd_kv', 'i_q', 'i_k', 'i_w', 'c', 'merge'),
                 zip([sum(IN_SPLITS[:i]) for i in range(len(IN_SPLITS))], IN_SPLITS)))
LORA_COLS = DECAY_LORA + AAA_LORA + GATE_LORA
LORA_WIDTH = _round_up(LORA_COLS, LANES)
TAIL_PAD = LANES - (IDX_DIM + 3 * NSA_HEADS + IDX_HEADS)
COL = {}
_off = 0
for _name, _width in (('merge', 3 * D_MODEL), ('a_q', NSA_HEADS * HEAD_DIM), ('d_q', DSA_HEADS * HEAD_DIM),
                      ('rkv', 3 * RWKV_DIM), ('a_kv', 6 * NSA_KV * HEAD_DIM), ('d_kv', 2 * DSA_KV * HEAD_DIM),
                      ('i_q', IDX_HEADS * IDX_DIM), ('lora', LORA_WIDTH), ('tail', LANES)):
    COL[_name] = (_off, _width)
    _off += _width
PROJ_COLS = _off
TAIL_IK = (0, IDX_DIM)
TAIL_GATE = (IDX_DIM, IDX_DIM + 3 * NSA_HEADS)
TAIL_IW = (IDX_DIM + 3 * NSA_HEADS, IDX_DIM + 3 * NSA_HEADS + IDX_HEADS)


def _regroup_w_in(w_in):
    cut = lambda name: w_in[..., _ORIG[name][0]:_ORIG[name][0] + _ORIG[name][1]]
    zeros = lambda n: jnp.zeros(w_in.shape[:2] + (n,), w_in.dtype)
    c0 = _ORIG['c'][0]
    parts = [cut('merge'), cut('a_q'), cut('d_q'), w_in[..., c0:c0 + 3 * RWKV_DIM], cut('a_kv'), cut('d_kv'),
             cut('i_q'), w_in[..., c0 + 3 * RWKV_DIM:c0 + 3 * RWKV_DIM + LORA_COLS], zeros(LORA_WIDTH - LORA_COLS),
             cut('i_k'), cut('a_gate'), cut('i_w'), zeros(TAIL_PAD)]
    return jnp.concatenate(parts, axis=-1).astype(jnp.bfloat16)


def _rope_tables(pos, width, head_dim):
    half = head_dim // ROT_DIV // 2
    freqs = ROPE_THETA ** (-jnp.arange(half, dtype=jnp.float32) / half)
    ang = pos.astype(jnp.float32)[:, None] * freqs[None, :]
    cos, sin = jnp.cos(ang), jnp.sin(ang)
    rest = head_dim - 2 * half
    ones, zeros = jnp.ones((pos.shape[0], rest), jnp.float32), jnp.zeros((pos.shape[0], rest), jnp.float32)
    z_half = jnp.zeros_like(sin)
    reps = width // head_dim
    c = jnp.tile(jnp.concatenate([cos, cos, ones], axis=1), (1, reps))
    s_up = jnp.tile(jnp.concatenate([z_half, sin, zeros], axis=1), (1, reps))
    s_down = jnp.tile(jnp.concatenate([-sin, z_half, zeros], axis=1), (1, reps))
    return c, s_up, s_down


def _rotate(y, tables, half):
    c, s_up, s_down = tables
    width = y.shape[-1]
    return y * c + pltpu.roll(y, half, 1) * s_up + pltpu.roll(y, width - half, 1) * s_down


def _head_norm(x, gain):
    return x * lax.rsqrt(jnp.mean(x * x, axis=-1, keepdims=True) + NORM_EPS) * gain


ROPE_HALF = HEAD_DIM // ROT_DIV // 2
IDX_ROPE_HALF = IDX_DIM // ROT_DIV // 2
KV_PLANES = 2 * NSA_KV


def _post_proj_kernel(aq_ref, dq_ref, cmp_ref, slc_ref, win_ref, dkv_ref, iq_ref, tail_ref,
                      c_ref, su_ref, sd_ref, ic_ref, isu_ref, isd_ref, gains_ref,
                      qa_ref, qd_ref, iqo_ref, cmp_o, slc_o, win_o, dsa_o, ik_o):
    tm = aq_ref.shape[0]
    rope = (c_ref[...], su_ref[...], sd_ref[...])
    irope = (ic_ref[...], isu_ref[...], isd_ref[...])
    gain = lambda i: gains_ref[i:i + 1, :]

    def head(ref, h):
        return ref[:, h * HEAD_DIM:(h + 1) * HEAD_DIM]

    for h in range(NSA_HEADS):
        qa_ref[:, h * HEAD_DIM:(h + 1) * HEAD_DIM] = _rotate(
            _head_norm(head(aq_ref, h), gain(0)), rope, ROPE_HALF).astype(qa_ref.dtype)
    for h in range(DSA_HEADS):
        qd_ref[:, h * HEAD_DIM:(h + 1) * HEAD_DIM] = _rotate(
            _head_norm(head(dq_ref, h), gain(3)), rope, ROPE_HALF).astype(qd_ref.dtype)
    for pair in range(IDX_HEADS * IDX_DIM // LANES):
        iqo_ref[:, pair * LANES:(pair + 1) * LANES] = _rotate(
            iq_ref[:, pair * LANES:(pair + 1) * LANES], irope, IDX_ROPE_HALF).astype(iqo_ref.dtype)

    def planes(src, dst, key_gain):
        for c in range(KV_PLANES):
            x = head(src, c)
            if key_gain is not None and c < NSA_KV:
                x = _rotate(_head_norm(x, gain(key_gain)), rope, ROPE_HALF)
            dst[pl.ds(c, tm, stride=KV_PLANES), :] = x

    planes(cmp_ref, cmp_o, None)
    planes(slc_ref, slc_o, 1)
    planes(win_ref, win_o, 2)
    planes(dkv_ref, dsa_o, 4)

    lane = lax.broadcasted_iota(jnp.int32, (tm, LANES), 1)
    ik = jnp.where(lane < IDX_DIM, tail_ref[...], 0.0)
    ik = ik * lax.rsqrt(jnp.sum(ik * ik, axis=-1, keepdims=True) * (1.0 / IDX_DIM) + NORM_EPS) * gain(5)
    ik_o[...] = _rotate(ik, irope, IDX_ROPE_HALF)[:, :IDX_DIM]


def _post_proj(proj, tables, itables, gains):
    m = proj.shape[0]
    tm = _pick_tile(m, (256, 128))
    col = lambda name, width: pl.BlockSpec((tm, width), functools.partial(
        lambda i, blk: (i, blk), blk=COL[name][0] // width))
    kv_w = KV_PLANES * HEAD_DIM
    akv0 = COL['a_kv'][0] // kv_w
    kv_col = lambda n: pl.BlockSpec((tm, kv_w), functools.partial(lambda i, blk: (i, blk), blk=akv0 + n))
    row = lambda width: pl.BlockSpec((tm, width), lambda i: (i, 0))
    planes = pl.BlockSpec((tm * KV_PLANES, HEAD_DIM), lambda i: (i, 0))
    f32 = jnp.float32
    return pl.pallas_call(
        _post_proj_kernel,
        grid=(m // tm,),
        in_specs=[col('a_q', NSA_HEADS * HEAD_DIM), col('d_q', DSA_HEADS * HEAD_DIM),
                  kv_col(0), kv_col(1), kv_col(2), col('d_kv', kv_w), col('i_q', IDX_HEADS * IDX_DIM),
                  col('tail', LANES)] + [row(LANES)] * 6 + [pl.BlockSpec((8, LANES), lambda i: (0, 0))],
        out_specs=[row(NSA_HEADS * HEAD_DIM), row(DSA_HEADS * HEAD_DIM), row(IDX_HEADS * IDX_DIM),
                   planes, planes, planes, planes, row(IDX_DIM)],
        out_shape=[jax.ShapeDtypeStruct((m, NSA_HEADS * HEAD_DIM), jnp.bfloat16),
                   jax.ShapeDtypeStruct((m, DSA_HEADS * HEAD_DIM), jnp.bfloat16),
                   jax.ShapeDtypeStruct((m, IDX_HEADS * IDX_DIM), jnp.bfloat16)]
                  + [jax.ShapeDtypeStruct((m * KV_PLANES, HEAD_DIM), f32)] * 4
                  + [jax.ShapeDtypeStruct((m, IDX_DIM), f32)],
        compiler_params=pltpu.CompilerParams(dimension_semantics=("arbitrary",),
                                             vmem_limit_bytes=VMEM_LIMIT_BYTES),
        name="post_proj",
    )(*([proj] * 8), *tables, *itables, gains)


def _merge_kernel(oa_ref, ob_ref, oc_ref, pa_ref, pb_ref, pc_ref, ga_ref, gb_ref, gc_ref, o_ref, xb_ref):
    @pl.when(pl.program_id(1) == 0)
    def _():
        for n, ref in enumerate((oa_ref, ob_ref, oc_ref)):
            xb_ref[n] = ref[...].astype(jnp.bfloat16)

    acc = None
    for n, (w_ref, g_ref) in enumerate(((pa_ref, ga_ref), (pb_ref, gb_ref), (pc_ref, gc_ref))):
        term = jax.nn.sigmoid(g_ref[...]) * jnp.dot(xb_ref[n], w_ref[0], preferred_element_type=jnp.float32)
        acc = term if acc is None else acc + term
    o_ref[...] = acc


def _merge(o_a, o_b, o_c, proj, big, layer):
    m, k = o_a.shape
    tm = _pick_tile(m, (512, 256, 128))
    tn = 512
    n_blk = D_MODEL // tn
    x_spec = pl.BlockSpec((tm, k), lambda i, j: (i, 0))
    w_spec = pl.BlockSpec((1, k, tn), lambda i, j: (layer, 0, j))
    gate = lambda n: pl.BlockSpec((tm, tn), functools.partial(lambda i, j, n: (i, n * n_blk + j), n=n))
    assert COL['merge'][0] == 0
    return pl.pallas_call(
        _merge_kernel,
        grid=(m // tm, n_blk),
        in_specs=[x_spec] * 3 + [w_spec] * 3 + [gate(0), gate(1), gate(2)],
        out_specs=pl.BlockSpec((tm, tn), lambda i, j: (i, j)),
        out_shape=jax.ShapeDtypeStruct((m, D_MODEL), jnp.float32),
        scratch_shapes=[pltpu.VMEM((3, tm, k), jnp.bfloat16)],
        compiler_params=pltpu.CompilerParams(dimension_semantics=("arbitrary", "arbitrary"),
                                             vmem_limit_bytes=VMEM_LIMIT_BYTES),
        name="mixer_merge",
    )(o_a, o_b, o_c, big['p_a'], big['p_b'], big['p_c'], proj, proj, proj)


def _head_sums(x, ones_ref):
    hi = x.astype(jnp.bfloat16)
    lo = (x - hi.astype(jnp.float32)).astype(jnp.bfloat16)
    ones = ones_ref[...]
    cols = [jnp.dot(hi[:, j:j + LANES], ones, preferred_element_type=jnp.float32)
            + jnp.dot(lo[:, j:j + LANES], ones, preferred_element_type=jnp.float32)
            for j in range(0, x.shape[1], LANES)]
    return jnp.concatenate(cols, axis=1)


def _softplus(z):
    return jnp.maximum(z, 0.0) + jnp.log(1.0 + jnp.exp(-jnp.abs(z)))


def _rwkv_prep_kernel(*refs, carry_rows):
    cur_refs, rest = refs[:4], refs[4:]
    prev_refs, rest = (None, rest) if carry_rows else (rest[:4], rest[4:])
    (mu_ref, mul_ref, w0_ref, a0_ref, kk_ref, ka_ref, rk_ref, w2_ref, a2_ref, g2_ref, ones_ref,
     r_o, w_o, k_o, v_o, a_o, b_o, g_o, bonus_o) = rest[:19]
    last_refs = rest[19:]
    tm = r_o.shape[0]
    first = lax.broadcasted_iota(jnp.int32, (tm, 1), 0) == 0

    @pl.when(pl.program_id(0) == 0)
    def _():
        for ref in last_refs:
            ref[...] = jnp.zeros_like(ref)

    def shifted(n, cur):
        if prev_refs is not None:
            return prev_refs[n][...]
        at_start = (pl.program_id(0) * tm) % carry_rows == 0
        head = jnp.where(at_start, 0.0, last_refs[n][...])
        last_refs[n][...] = cur[tm - 1:tm, :]
        return jnp.where(first, head, pltpu.roll(cur, 1, 0))

    def mixed(n, mu):
        cur = cur_refs[n][...]
        return cur + (shifted(n, cur) - cur) * mu

    r = mixed(0, mu_ref[0:1, :])
    k = mixed(1, mu_ref[1:2, :])
    v = mixed(2, mu_ref[2:3, :])
    lora = mixed(3, mul_ref[...])
    bf = jnp.bfloat16
    w_lo = jnp.tanh(lora[:, :DECAY_LORA]).astype(bf)
    a_lo = lora[:, DECAY_LORA:DECAY_LORA + AAA_LORA].astype(bf)
    g_lo = jax.nn.sigmoid(lora[:, DECAY_LORA + AAA_LORA:LORA_COLS]).astype(bf)
    w = -_softplus(-(w0_ref[...] + jnp.dot(w_lo, w2_ref[...].astype(bf), preferred_element_type=jnp.float32))) - 0.5
    a = jax.nn.sigmoid(a0_ref[...] + jnp.dot(a_lo, a2_ref[...].astype(bf), preferred_element_type=jnp.float32))
    g_o[...] = jnp.dot(g_lo, g2_ref[...].astype(bf), preferred_element_type=jnp.float32)
    kk = k * kk_ref[...]
    kk = kk * lax.rsqrt(_head_sums(kk * kk, ones_ref) + 1e-12)
    kh = k * (1.0 + (a - 1.0) * ka_ref[...])
    r_o[...] = r
    w_o[...] = jnp.exp(-jnp.exp(w))
    k_o[...] = kh
    v_o[...] = v
    a_o[...] = -kk
    b_o[...] = kk * a
    bonus_o[...] = _head_sums(r * kh * rk_ref[...], ones_ref) * v


def _head_ones():
    seg = jnp.arange(LANES) // RWKV_HEAD
    return (seg[:, None] == seg[None, :]).astype(jnp.bfloat16)


def _rwkv_prep(proj, prev, p, seq_len):
    m = proj.shape[0]
    tm = _pick_tile(m, (256, 128))
    if prev is None:
        assert seq_len % tm == 0
    blk = lambda off, width: pl.BlockSpec((tm, width), functools.partial(lambda i, b: (i, b), b=off // width))
    r0 = COL['rkv'][0]
    cur_specs = [blk(r0, RWKV_DIM), blk(r0 + RWKV_DIM, RWKV_DIM), blk(r0 + 2 * RWKV_DIM, RWKV_DIM),
                 blk(COL['lora'][0], LORA_WIDTH)]
    row = lambda width: pl.BlockSpec((tm, width), lambda i: (i, 0))
    const = lambda *shape: pl.BlockSpec(shape, lambda i: (0,) * len(shape))
    mu = p['rwkv_mu']
    mu_rkv = mu[:3 * RWKV_DIM].reshape(3, RWKV_DIM)
    mu_lora = jnp.pad(mu[3 * RWKV_DIM:], (0, LORA_WIDTH - LORA_COLS))[None, :]
    vec = lambda name: p[name].reshape(1, RWKV_DIM)
    small = [mu_rkv, mu_lora, vec('rwkv_w0'), vec('rwkv_a0'), vec('rwkv_k_k'), vec('rwkv_k_a'), vec('rwkv_r_k'),
             p['rwkv_w2'], p['rwkv_a2'], p['rwkv_g2'], _head_ones()]
    small_specs = [const(*s.shape) for s in small]
    out = jax.ShapeDtypeStruct((m, RWKV_DIM), jnp.float32)
    if prev is None:
        ins, specs = [proj] * 4, cur_specs
        scratch = [pltpu.VMEM((1, RWKV_DIM), jnp.float32)] * 3 + [pltpu.VMEM((1, LORA_WIDTH), jnp.float32)]
    else:
        ins = [proj] * 4 + list(prev)
        specs = cur_specs + [row(RWKV_DIM)] * 3 + [row(LORA_WIDTH)]
        scratch = []
    return pl.pallas_call(
        functools.partial(_rwkv_prep_kernel, carry_rows=seq_len if prev is None else None),
        grid=(m // tm,),
        in_specs=specs + small_specs,
        out_specs=[row(RWKV_DIM)] * 8,
        out_shape=[out] * 8,
        scratch_shapes=scratch,
        compiler_params=pltpu.CompilerParams(dimension_semantics=("arbitrary",),
                                             vmem_limit_bytes=VMEM_LIMIT_BYTES),
        name="rwkv_prep",
    )(*ins, *small)


def _rwkv_post_kernel(y_ref, g_ref, bonus_ref, lnw_ref, lnb_ref, ones_ref, o_ref):
    y = y_ref[...]
    mean = _head_sums(y, ones_ref) * (1.0 / RWKV_HEAD)
    dev = y - mean
    var = _head_sums(dev * dev, ones_ref) * (1.0 / RWKV_HEAD)
    out = dev * lax.rsqrt(var + RWKV_GN_EPS) * lnw_ref[...] + lnb_ref[...] + bonus_ref[...]
    o_ref[...] = out * g_ref[...]


def _rwkv_post(y, gate, bonus, p):
    m = y.shape[0]
    tm = _pick_tile(m, (512, 256, 128))
    row = pl.BlockSpec((tm, RWKV_DIM), lambda i: (i, 0))
    vec = pl.BlockSpec((1, RWKV_DIM), lambda i: (0, 0))
    return pl.pallas_call(
        _rwkv_post_kernel,
        grid=(m // tm,),
        in_specs=[row, row, row, vec, vec, pl.BlockSpec((LANES, LANES), lambda i: (0, 0))],
        out_specs=row,
        out_shape=jax.ShapeDtypeStruct((m, RWKV_DIM), jnp.float32),
        compiler_params=pltpu.CompilerParams(dimension_semantics=("arbitrary",),
                                             vmem_limit_bytes=VMEM_LIMIT_BYTES),
        name="rwkv_post",
    )(y, gate, bonus, p['rwkv_ln_w'][None, :], p['rwkv_ln_b'][None, :], _head_ones())


def _group_queries(q, g):
    h0 = g * HEADS_PER_GROUP
    return jnp.concatenate([q[:, (h0 + h) * HEAD_DIM:(h0 + h + 1) * HEAD_DIM] for h in range(HEADS_PER_GROUP)],
                           axis=0).astype(jnp.bfloat16)


def _attend(qg, k, v, mask, want_psum=False):
    tq = mask.shape[0]
    s = lax.dot_general(qg, k, _NT, preferred_element_type=jnp.float32)
    ps = []
    for h in range(HEADS_PER_GROUP):
        sh = jnp.where(mask, s[h * tq:(h + 1) * tq], NEG)
        top = jnp.max(sh, axis=-1, keepdims=True)
        e = jnp.exp2((sh - top) * (ATTN_SCALE * LOG2_E))
        den = jnp.sum(e, axis=-1, keepdims=True)
        ps.append(e * jnp.where(top > 0.5 * NEG, 1.0 / den, 0.0))
    o = jnp.dot(jnp.concatenate(ps, axis=0).astype(jnp.bfloat16), v, preferred_element_type=jnp.float32)
    if want_psum:
        return o, (ps[0] + ps[1]) + (ps[2] + ps[3])
    return o


def _kv_columns(ref, n_kv):
    def get(s, g, rows):
        c0 = (s * n_kv + g) * HEAD_DIM
        return ref[0, rows, c0:c0 + HEAD_DIM]
    return get


def _kv_planes(ref, n_kv):
    return lambda s, g, rows: ref[s * n_kv + g, rows, :]


def _unpack_planes(rows_ref, scr):
    planes, length = scr.shape[0], scr.shape[1]
    for c in range(planes):
        scr[c] = rows_ref[0, pl.ds(c, length, stride=planes), :].astype(jnp.bfloat16)


KEY_GROUP = 512


def _for_key_range(qpos0, tq, n_keys, body):
    lengths = list(range(KEY_GROUP, n_keys, KEY_GROUP)) + [n_keys]
    group = (qpos0 + tq - 1) // KEY_GROUP
    for i, n in enumerate(lengths):
        @pl.when((group == i) if i + 1 < len(lengths) else (group >= i))
        def _(n=n):
            body(n)


PAGES_PER_STEP = 8


def _page_specs(pool, layer, pages_per_seq):
    assert pages_per_seq % PAGES_PER_STEP == 0
    block = (1, 1) + pool.shape[2:]
    return [pl.BlockSpec(block, functools.partial(
        lambda i, j, pt, k: (layer, pt[i, j * PAGES_PER_STEP + k], 0, 0), k=k)) for k in range(PAGES_PER_STEP)]


def _load_kv_pages(page_refs, new_ref, kv_scr, page_rows, past_len):
    j = pl.program_id(1)
    planes = kv_scr.shape[0]
    for k, page in enumerate(page_refs):
        row0 = pl.multiple_of((j * len(page_refs) + k) * page_rows, page_rows)
        for c in range(planes):
            kv_scr[c, pl.ds(row0, page_rows), :] = (
                page[0, 0, pl.ds(c, page_rows, stride=planes), :].astype(jnp.bfloat16))

    @pl.when(j == 0)
    def _():
        tail = kv_scr.shape[1] - past_len
        for c in range(planes):
            kv_scr[c, past_len:, :] = new_ref[0, pl.ds(c, tail, stride=planes), :].astype(jnp.bfloat16)


def _pad_rows(x, n):
    return jnp.pad(x, ((0, 0), (0, n - x.shape[1])) + ((0, 0),) * (x.ndim - 2))


def _nsa_body(q_ref, tail_ref, kc_ref, vc_ref, slc, slc_len, win, win_total, cover_ref, expand_ref, o_ref,
              *, qpos0, win_pos0, n_slc, n_sel):
    tq = q_ref.shape[1]
    qpos = qpos0 + lax.broadcasted_iota(jnp.int32, (tq, 1), 0)
    q = q_ref[0]
    gates = jax.nn.sigmoid(tail_ref[0][:, TAIL_GATE[0]:TAIL_GATE[1]])
    win_len = min(win_total, _round_up(WINDOW + tq, LANES))
    win0 = jnp.clip(qpos0 - win_pos0 - WINDOW, 0, win_total - win_len)
    win_rows = pl.ds(pl.multiple_of(win0, LANES), win_len)
    slc_rows = slice(0, slc_len)

    cpos = lax.broadcasted_iota(jnp.int32, (1, kc_ref.shape[2]), 1) * CMP_STRIDE + (CMP_LEN - 1)
    cmask = cpos <= qpos
    kpos = lax.broadcasted_iota(jnp.int32, (1, slc_len), 1)
    causal = kpos <= qpos
    wpos = win_pos0 + win0 + lax.broadcasted_iota(jnp.int32, (1, win_len), 1)
    wmask = (wpos <= qpos) & (qpos - wpos < WINDOW)
    blk = lax.broadcasted_iota(jnp.int32, (1, cover_ref.shape[1]), 1)
    dist = jnp.right_shift(qpos, int(math.log2(SLC_BLOCK))) - blk
    forced = (blk == 0) | ((dist >= 0) & (dist < N_LOCAL))
    cover = cover_ref[...]

    for g in range(NSA_KV):
        qg = _group_queries(q, g)
        o_cmp, p_sum = _attend(qg, kc_ref[0, g], vc_ref[0, g], cmask, want_psum=True)
        p_hi = p_sum.astype(jnp.bfloat16)
        p_lo = (p_sum - p_hi.astype(jnp.float32)).astype(jnp.bfloat16)
        imp = (jnp.dot(p_hi, cover, preferred_element_type=jnp.float32)
               + jnp.dot(p_lo, cover, preferred_element_type=jnp.float32))
        score = jnp.where(forced, FORCE, jnp.where(dist < 0, -FORCE, imp))
        score = jnp.where(blk < n_slc, score, LOWEST)
        sel = jnp.zeros(score.shape, jnp.float32)
        for _ in range(n_sel):
            best = jnp.max(score, axis=-1, keepdims=True)
            first = jnp.min(jnp.where(score == best, blk, cover_ref.shape[1]), axis=-1, keepdims=True)
            hit = blk == first
            sel = jnp.where(hit, 1.0, sel)
            score = jnp.where(hit, LOWEST, score)
        smask = jnp.dot(sel.astype(jnp.bfloat16), expand_ref[:, :slc_len],
                        preferred_element_type=jnp.float32) > 0.5
        o_slc = _attend(qg, slc(0, g, slc_rows), slc(1, g, slc_rows), smask & causal)
        o_win = _attend(qg, win(0, g, win_rows), win(1, g, win_rows), wmask)
        for h in range(HEADS_PER_GROUP):
            head = g * HEADS_PER_GROUP + h
            rows = slice(h * tq, (h + 1) * tq)
            o_ref[0, :, head * HEAD_DIM:(head + 1) * HEAD_DIM] = (
                gates[:, 3 * head:3 * head + 1] * o_cmp[rows]
                + gates[:, 3 * head + 1:3 * head + 2] * o_slc[rows]
                + gates[:, 3 * head + 2:3 * head + 3] * o_win[rows])


def _nsa_kernel(q_ref, tail_ref, kc_ref, vc_ref, slc_ref, win_ref, cover_ref, expand_ref, o_ref, slc_scr, win_scr,
                *, pos0, win_pos0, n_slc, n_sel):
    @pl.when(pl.program_id(1) == 0)
    def _():
        _unpack_planes(slc_ref, slc_scr)
        _unpack_planes(win_ref, win_scr)

    tq = q_ref.shape[1]
    qpos0 = pos0 + pl.program_id(1) * tq
    _for_key_range(qpos0, tq, slc_scr.shape[1], lambda n: _nsa_body(
        q_ref, tail_ref, kc_ref, vc_ref, _kv_planes(slc_scr, NSA_KV), n, _kv_planes(win_scr, NSA_KV),
        win_scr.shape[1], cover_ref, expand_ref, o_ref, qpos0=qpos0, win_pos0=win_pos0, n_slc=n_slc, n_sel=n_sel))


def _nsa_paged_kernel(pt_ref, q_ref, tail_ref, kc_ref, vc_ref, new_ref, win_ref, cover_ref, expand_ref, *rest,
                      pos0, win_pos0, n_slc, n_sel, page_rows):
    page_refs, (o_ref, kv_scr) = rest[:PAGES_PER_STEP], rest[PAGES_PER_STEP:]
    _load_kv_pages(page_refs, new_ref, kv_scr, page_rows, pos0)

    @pl.when(pl.program_id(1) == pl.num_programs(1) - 1)
    def _():
        _nsa_body(q_ref, tail_ref, kc_ref, vc_ref, _kv_planes(kv_scr, NSA_KV), kv_scr.shape[1],
                  _kv_columns(win_ref, NSA_KV), win_ref.shape[1], cover_ref, expand_ref, o_ref,
                  qpos0=pos0, win_pos0=win_pos0, n_slc=n_slc, n_sel=n_sel)


def _nsa_attention(q, tail, kc, vc, slc_rows, win_kv, pos0, win_pos0, n_c, paged=None):
    b, t_len, _ = q.shape
    seq = slc_rows.shape[1] // KV_PLANES + (pos0 if paged else 0)
    tq = Q_TILE if t_len % Q_TILE == 0 else _round_up(t_len, SUBLANES)
    tp = _round_up(t_len, tq)
    lp = _round_up(seq, LANES)
    n_cp = kc.shape[2]
    n_slc = -(-seq // SLC_BLOCK)
    n_sp = _round_up(n_slc, LANES)
    start = jnp.arange(n_cp)[:, None] * CMP_STRIDE
    lo = jnp.arange(n_sp)[None, :] * SLC_BLOCK
    cover = ((start < lo + SLC_BLOCK) & (start + CMP_LEN > lo) & (jnp.arange(n_cp)[:, None] < n_c)
             & (jnp.arange(n_sp)[None, :] < n_slc)).astype(jnp.bfloat16)
    expand = (jnp.arange(lp)[None, :] // SLC_BLOCK == jnp.arange(n_sp)[:, None]).astype(jnp.bfloat16)
    static = dict(pos0=pos0, win_pos0=win_pos0, n_slc=n_slc, n_sel=min(N_SEL, n_slc))
    width = NSA_HEADS * HEAD_DIM
    kv_w = KV_PLANES * HEAD_DIM
    out_shape = jax.ShapeDtypeStruct((b, tp, width), jnp.float32)
    params = pltpu.CompilerParams(dimension_semantics=("arbitrary", "arbitrary"),
                                  vmem_limit_bytes=VMEM_LIMIT_BYTES)
    kc, vc = kc.astype(jnp.bfloat16), vc.astype(jnp.bfloat16)
    if paged is None:
        assert seq == lp and tp == t_len
        lw = win_kv.shape[1] // KV_PLANES
        assert lw % LANES == 0
        return pl.pallas_call(
            functools.partial(_nsa_kernel, **static),
            grid=(b, tp // tq),
            in_specs=[pl.BlockSpec((1, tq, width), lambda i, j: (i, j, 0)),
                      pl.BlockSpec((1, tq, LANES), lambda i, j: (i, j, 0)),
                      pl.BlockSpec((1, NSA_KV, n_cp, HEAD_DIM), lambda i, j: (i, 0, 0, 0)),
                      pl.BlockSpec((1, NSA_KV, n_cp, HEAD_DIM), lambda i, j: (i, 0, 0, 0)),
                      pl.BlockSpec((1, lp * KV_PLANES, HEAD_DIM), lambda i, j: (i, 0, 0)),
                      pl.BlockSpec((1, lw * KV_PLANES, HEAD_DIM), lambda i, j: (i, 0, 0)),
                      pl.BlockSpec((n_cp, n_sp), lambda i, j: (0, 0)),
                      pl.BlockSpec((n_sp, lp), lambda i, j: (0, 0))],
            out_specs=pl.BlockSpec((1, tq, width), lambda i, j: (i, j, 0)),
            out_shape=out_shape,
            scratch_shapes=[pltpu.VMEM((KV_PLANES, lp, HEAD_DIM), jnp.bfloat16),
                            pltpu.VMEM((KV_PLANES, lw, HEAD_DIM), jnp.bfloat16)],
            compiler_params=params, name="nsa_attention",
        )(q, tail, kc, vc, slc_rows, win_kv, cover, expand)

    pool, layer, page_table = paged
    assert tp == tq
    page_rows = pool.shape[2] // KV_PLANES
    lw = _round_up(win_kv.shape[1], LANES)
    win = _pad_rows(win_kv.reshape(b, win_kv.shape[1], -1), lw).astype(jnp.bfloat16)
    new_rows = _pad_rows(slc_rows, (lp - pos0) * KV_PLANES)
    out = pl.pallas_call(
        functools.partial(_nsa_paged_kernel, page_rows=page_rows, **static),
        grid_spec=pltpu.PrefetchScalarGridSpec(
            num_scalar_prefetch=1,
            grid=(b, page_table.shape[1] // PAGES_PER_STEP),
            in_specs=[pl.BlockSpec((1, tq, width), lambda i, j, pt: (i, 0, 0)),
                      pl.BlockSpec((1, tq, LANES), lambda i, j, pt: (i, 0, 0)),
                      pl.BlockSpec((1, NSA_KV, n_cp, HEAD_DIM), lambda i, j, pt: (i, 0, 0, 0)),
                      pl.BlockSpec((1, NSA_KV, n_cp, HEAD_DIM), lambda i, j, pt: (i, 0, 0, 0)),
                      pl.BlockSpec((1, (lp - pos0) * KV_PLANES, HEAD_DIM), lambda i, j, pt: (i, 0, 0)),
                      pl.BlockSpec((1, lw, kv_w), lambda i, j, pt: (i, 0, 0)),
                      pl.BlockSpec((n_cp, n_sp), lambda i, j, pt: (0, 0)),
                      pl.BlockSpec((n_sp, lp), lambda i, j, pt: (0, 0))]
                     + _page_specs(pool, layer, page_table.shape[1]),
            out_specs=pl.BlockSpec((1, tq, width), lambda i, j, pt: (i, 0, 0)),
            scratch_shapes=[pltpu.VMEM((KV_PLANES, lp, HEAD_DIM), jnp.bfloat16)]),
        out_shape=out_shape, compiler_params=params, name="nsa_attention_paged",
    )(page_table, _pad_rows(q, tp), _pad_rows(tail, tp), kc, vc, new_rows, win, cover, expand,
      *([pool] * PAGES_PER_STEP))
    return out[:, :t_len]


def _compress_mlp(x, w1_ref, pos_ref, w2_ref, b2_ref, s):
    n_sub = x.shape[0]
    half = CMP_STRIDE * HEAD_DIM
    h_lo = jnp.dot(x, w1_ref[s, :half].astype(jnp.bfloat16), preferred_element_type=jnp.float32)
    h_hi = jnp.dot(x, w1_ref[s, half:].astype(jnp.bfloat16), preferred_element_type=jnp.float32)
    h = h_lo + pltpu.roll(h_hi, n_sub - 1, 0) + pos_ref[s]
    return jnp.dot(jax.nn.gelu(h).astype(jnp.bfloat16), w2_ref[s].astype(jnp.bfloat16),
                   preferred_element_type=jnp.float32) + b2_ref[s]


def _compress_kernel(x_ref, w1_ref, pos_ref, w2_ref, b2_ref, o_ref):
    n_sub = o_ref.shape[3]
    for c in range(KV_PLANES):
        s, g = divmod(c, NSA_KV)
        x = jnp.concatenate([x_ref[0, pl.ds(r * KV_PLANES + c, n_sub, stride=CMP_STRIDE * KV_PLANES), :]
                             .astype(jnp.bfloat16) for r in range(CMP_STRIDE)], axis=1)
        o_ref[0, s, g] = _compress_mlp(x, w1_ref, pos_ref, w2_ref, b2_ref, s)


def _compress_paged_kernel(pt_ref, w1_ref, pos_ref, w2_ref, b2_ref, *rest, page_rows):
    page_refs, (o_ref, x_scr) = rest[:PAGES_PER_STEP], rest[PAGES_PER_STEP:]
    j = pl.program_id(1)
    planes = x_scr.shape[0]
    per_page = page_rows // CMP_STRIDE
    for k, page in enumerate(page_refs):
        row0 = pl.multiple_of((j * len(page_refs) + k) * per_page, per_page)
        for c in range(planes):
            for s in range(CMP_STRIDE):
                x_scr[c, pl.ds(row0, per_page), s * HEAD_DIM:(s + 1) * HEAD_DIM] = (
                    page[0, 0, pl.ds(s * planes + c, per_page, stride=CMP_STRIDE * planes), :])

    @pl.when(j == pl.num_programs(1) - 1)
    def _():
        for c in range(planes):
            s, g = divmod(c, NSA_KV)
            o_ref[0, s, g] = _compress_mlp(x_scr[c].astype(jnp.bfloat16), w1_ref, pos_ref, w2_ref, b2_ref, s)


def _compress(rows, w1, pos_emb, w2, b2, paged=None):
    w1 = w1.reshape(2, CMP_LEN * HEAD_DIM, HEAD_DIM)
    pos_bias = jnp.einsum('kld,klde->ke', pos_emb, w1.reshape(2, CMP_LEN, HEAD_DIM, HEAD_DIM))[:, None, :]
    params = dict(vmem_limit_bytes=VMEM_LIMIT_BYTES)
    if paged is None:
        b = rows.shape[0]
        seq = rows.shape[1] // KV_PLANES
        assert seq % CMP_STRIDE == 0
        n_sub = seq // CMP_STRIDE
        const = lambda *shape: pl.BlockSpec(shape, lambda i: (0,) * len(shape))
        return pl.pallas_call(
            _compress_kernel,
            grid=(b,),
            in_specs=[pl.BlockSpec((1, seq * KV_PLANES, HEAD_DIM), lambda i: (i, 0, 0)),
                      const(2, CMP_LEN * HEAD_DIM, HEAD_DIM), const(2, 1, HEAD_DIM),
                      const(2, HEAD_DIM, HEAD_DIM), const(2, 1, HEAD_DIM)],
            out_specs=pl.BlockSpec((1, 2, NSA_KV, n_sub, HEAD_DIM), lambda i: (i, 0, 0, 0, 0)),
            out_shape=jax.ShapeDtypeStruct((b, 2, NSA_KV, n_sub, HEAD_DIM), jnp.float32),
            compiler_params=pltpu.CompilerParams(dimension_semantics=("arbitrary",), **params),
            name="nsa_compress",
        )(rows, w1, pos_bias, w2, b2[:, None, :])

    pool, layer, page_table = paged
    b, n_pages = page_table.shape
    planes = KV_PLANES
    page_rows = pool.shape[2] // planes
    n_sub = n_pages * page_rows // CMP_STRIDE
    const = lambda *shape: pl.BlockSpec(shape, lambda i, j, pt: (0,) * len(shape))
    return pl.pallas_call(
        functools.partial(_compress_paged_kernel, page_rows=page_rows),
        grid_spec=pltpu.PrefetchScalarGridSpec(
            num_scalar_prefetch=1,
            grid=(b, n_pages // PAGES_PER_STEP),
            in_specs=[const(2, CMP_LEN * HEAD_DIM, HEAD_DIM), const(2, 1, HEAD_DIM),
                      const(2, HEAD_DIM, HEAD_DIM), const(2, 1, HEAD_DIM)]
                     + _page_specs(pool, layer, n_pages),
            out_specs=pl.BlockSpec((1, 2, NSA_KV, n_sub, HEAD_DIM), lambda i, j, pt: (i, 0, 0, 0, 0)),
            scratch_shapes=[pltpu.VMEM((planes, n_sub, CMP_STRIDE * HEAD_DIM), jnp.float32)]),
        out_shape=jax.ShapeDtypeStruct((b, 2, NSA_KV, n_sub, HEAD_DIM), jnp.float32),
        compiler_params=pltpu.CompilerParams(dimension_semantics=("arbitrary",) * 2, **params),
        name="nsa_compress_paged",
    )(page_table, w1, pos_bias, w2, b2[:, None, :], *([pool] * PAGES_PER_STEP))


INT_MIN = -2 ** 31
IDX_W_SCALE = (IDX_HEADS * IDX_DIM) ** -0.5


def _count(pred):
    return jnp.sum(jnp.where(pred, 1.0, 0.0), axis=-1, keepdims=True)


def _dsa_body(q_ref, kv, iq_ref, tail_ref, ik, o_ref, *, qpos0, k_top):
    tq = q_ref.shape[1]
    lp = ik.shape[1]
    rows = slice(0, lp)
    qpos = qpos0 + lax.broadcasted_iota(jnp.int32, (tq, 1), 0)
    kpos = lax.broadcasted_iota(jnp.int32, (1, lp), 1)
    causal = kpos <= qpos

    iq = iq_ref[0].astype(jnp.bfloat16)
    iw = tail_ref[0][:, TAIL_IW[0]:TAIL_IW[1]] * IDX_W_SCALE
    score = jnp.zeros((tq, lp), jnp.float32)
    for h in range(IDX_HEADS):
        logits = jnp.dot(iq[:, h * IDX_DIM:(h + 1) * IDX_DIM], ik, preferred_element_type=jnp.float32)
        score = score + iw[:, h:h + 1] * jnp.maximum(logits, 0.0)
    score = jnp.where(causal, score, NEG)
    score = jnp.where(score == 0.0, 0.0, score)
    bits = lax.bitcast_convert_type(score, jnp.int32)
    key = bits ^ (jnp.right_shift(bits, 31) & 0x7FFFFFFF)

    k_f = float(k_top)
    enough = lambda cand: _count(key >= cand) >= k_f
    thr = jnp.where(enough(0), 0, INT_MIN).astype(jnp.int32)
    top_bit = thr | (1 << 30)
    thr = jnp.where(enough(top_bit), top_bit, thr)

    def refine(i, thr):
        low = 28 - 2 * i
        c1, c2, c3 = (thr | jnp.left_shift(jnp.int32(n), low) for n in (1, 2, 3))
        return jnp.where(enough(c3), c3, jnp.where(enough(c2), c2, jnp.where(enough(c1), c1, thr)))

    thr = lax.fori_loop(0, 15, refine, thr)
    above = key > thr
    tied = key == thr
    need = k_f - _count(above)

    few = lambda cand: _count(tied & (kpos < cand)) < need
    bound = jnp.where(few(1 << 14), 1 << 14, 0).astype(jnp.int32)

    def widen(i, c):
        low = 12 - 2 * i
        c1, c2, c3 = (c | jnp.left_shift(jnp.int32(n), low) for n in (1, 2, 3))
        return jnp.where(few(c3), c3, jnp.where(few(c2), c2, jnp.where(few(c1), c1, c)))

    bound = lax.fori_loop(0, 7, widen, bound)
    mask = causal & (above | (tied & (kpos <= bound)))

    q = q_ref[0]
    for g in range(DSA_KV):
        o = _attend(_group_queries(q, g), kv(0, g, rows), kv(1, g, rows), mask)
        for h in range(HEADS_PER_GROUP):
            head = g * HEADS_PER_GROUP + h
            o_ref[0, :, head * HEAD_DIM:(head + 1) * HEAD_DIM] = o[h * tq:(h + 1) * tq]


def _dsa_kernel(q_ref, kv_ref, iq_ref, tail_ref, ik_ref, o_ref, kv_scr, *, pos0, k_top):
    @pl.when(pl.program_id(1) == 0)
    def _():
        _unpack_planes(kv_ref, kv_scr)

    tq = q_ref.shape[1]
    qpos0 = pos0 + pl.program_id(1) * tq
    _for_key_range(qpos0, tq, kv_scr.shape[1], lambda n: _dsa_body(
        q_ref, _kv_planes(kv_scr, DSA_KV), iq_ref, tail_ref, ik_ref[0, :, :n], o_ref, qpos0=qpos0, k_top=k_top))


def _dsa_paged_kernel(pt_ref, q_ref, iq_ref, tail_ref, new_kv_ref, new_ik_ref, *rest, pos0, k_top, page_rows):
    n = PAGES_PER_STEP
    kv_pages, ik_pages, (o_ref, kv_scr, ik_scr) = rest[:n], rest[n:2 * n], rest[2 * n:]
    _load_kv_pages(kv_pages, new_kv_ref, kv_scr, page_rows, pos0)
    j = pl.program_id(1)
    for k, page in enumerate(ik_pages):
        row0 = pl.multiple_of((j * n + k) * page_rows, page_rows)
        ik_scr[:, pl.ds(row0, page_rows)] = page[0, 0].astype(jnp.bfloat16)

    @pl.when(j == 0)
    def _():
        ik_scr[:, pos0:] = new_ik_ref[0].astype(jnp.bfloat16)

    @pl.when(j == pl.num_programs(1) - 1)
    def _():
        _dsa_body(q_ref, _kv_planes(kv_scr, DSA_KV), iq_ref, tail_ref, ik_scr[...], o_ref, qpos0=pos0, k_top=k_top)


def _dsa_attention(q, kv_rows, idx_k, idx_q, tail, pos0, paged=None):
    b, t_len, _ = q.shape
    seq = idx_k.shape[1] + (pos0 if paged else 0)
    assert seq < 2 ** 15
    tq = Q_TILE if t_len % Q_TILE == 0 else _round_up(t_len, SUBLANES)
    tp = _round_up(t_len, tq)
    lp = _round_up(seq, LANES)
    static = dict(pos0=pos0, k_top=min(TOPK_MAX, seq // 4))
    width = DSA_HEADS * HEAD_DIM
    out_shape = jax.ShapeDtypeStruct((b, tp, width), jnp.float32)
    params = pltpu.CompilerParams(dimension_semantics=("arbitrary", "arbitrary"),
                                  vmem_limit_bytes=VMEM_LIMIT_BYTES)
    if paged is None:
        assert seq == lp and tp == t_len
        return pl.pallas_call(
            functools.partial(_dsa_kernel, **static),
            grid=(b, tp // tq),
            in_specs=[pl.BlockSpec((1, tq, width), lambda i, j: (i, j, 0)),
                      pl.BlockSpec((1, lp * KV_PLANES, HEAD_DIM), lambda i, j: (i, 0, 0)),
                      pl.BlockSpec((1, tq, IDX_HEADS * IDX_DIM), lambda i, j: (i, j, 0)),
                      pl.BlockSpec((1, tq, LANES), lambda i, j: (i, j, 0)),
                      pl.BlockSpec((1, IDX_DIM, lp), lambda i, j: (i, 0, 0))],
            out_specs=pl.BlockSpec((1, tq, width), lambda i, j: (i, j, 0)),
            out_shape=out_shape,
            scratch_shapes=[pltpu.VMEM((KV_PLANES, lp, HEAD_DIM), jnp.bfloat16)],
            compiler_params=params, name="dsa_attention",
        )(q, kv_rows, idx_q, tail, jnp.swapaxes(idx_k, 1, 2).astype(jnp.bfloat16))

    kv_pool, idx_pool, layer, page_table = paged
    assert tp == tq
    page_rows = idx_pool.shape[3]
    n_pages = page_table.shape[1]
    out = pl.pallas_call(
        functools.partial(_dsa_paged_kernel, page_rows=page_rows, **static),
        grid_spec=pltpu.PrefetchScalarGridSpec(
            num_scalar_prefetch=1,
            grid=(b, n_pages // PAGES_PER_STEP),
            in_specs=[pl.BlockSpec((1, tq, width), lambda i, j, pt: (i, 0, 0)),
                      pl.BlockSpec((1, tq, IDX_HEADS * IDX_DIM), lambda i, j, pt: (i, 0, 0)),
                      pl.BlockSpec((1, tq, LANES), lambda i, j, pt: (i, 0, 0)),
                      pl.BlockSpec((1, (lp - pos0) * KV_PLANES, HEAD_DIM), lambda i, j, pt: (i, 0, 0)),
                      pl.BlockSpec((1, IDX_DIM, lp - pos0), lambda i, j, pt: (i, 0, 0))]
                     + _page_specs(kv_pool, layer, n_pages) + _page_specs(idx_pool, layer, n_pages),
            out_specs=pl.BlockSpec((1, tq, width), lambda i, j, pt: (i, 0, 0)),
            scratch_shapes=[pltpu.VMEM((KV_PLANES, lp, HEAD_DIM), jnp.bfloat16),
                            pltpu.VMEM((IDX_DIM, lp), jnp.bfloat16)]),
        out_shape=out_shape, compiler_params=params, name="dsa_attention_paged",
    )(page_table, _pad_rows(q, tp), _pad_rows(idx_q, tp), _pad_rows(tail, tp),
      _pad_rows(kv_rows, (lp - pos0) * KV_PLANES), jnp.swapaxes(_pad_rows(idx_k, lp - pos0), 1, 2),
      *([kv_pool] * PAGES_PER_STEP), *([idx_pool] * PAGES_PER_STEP))
    return out[:, :t_len]


def _rms_norm(x, g):
    xf = x.astype(jnp.float32)
    y = xf * lax.rsqrt(jnp.mean(xf * xf, axis=-1, keepdims=True) + NORM_EPS)
    return (y * g.astype(jnp.float32)).astype(x.dtype)


def _rope_partial(x, pos):
    rot = x.shape[-1] // ROT_DIV
    half = rot // 2
    freqs = ROPE_THETA ** (-jnp.arange(half, dtype=jnp.float32) / half)
    ang = pos.astype(jnp.float32)[:, None] * freqs[None, :]
    cos, sin = jnp.cos(ang)[:, None, :], jnp.sin(ang)[:, None, :]
    xr = x[..., :rot].astype(jnp.float32)
    x1, x2 = xr[..., :half], xr[..., half:]
    xr = jnp.concatenate([x1 * cos - x2 * sin, x1 * sin + x2 * cos], axis=-1)
    return jnp.concatenate([xr.astype(x.dtype), x[..., rot:]], axis=-1)


def _nsa_mixer(q, tail, cmp_rows, slc_rows, win_kv, win_pos0, pos0, p, paged=None):
    if paged is None:
        cmp = _compress(cmp_rows, p['cmp_w1'], p['cmp_pos'], p['cmp_w2'], p['cmp_b2'])
    else:
        cmp = _compress(None, p['cmp_w1'], p['cmp_pos'], p['cmp_w2'], p['cmp_b2'], paged=(paged[0],) + paged[2:])
    n_sub = cmp.shape[3]
    c_pos = jnp.arange(n_sub, dtype=jnp.int32) * CMP_STRIDE + (CMP_LEN - 1)
    kc = jnp.swapaxes(_rope_partial(_rms_norm(jnp.swapaxes(cmp[:, 0], 1, 2), p['a_k_norm'][0]), c_pos), 1, 2)
    return _nsa_attention(q, tail, kc, cmp[:, 1], slc_rows, win_kv, pos0, win_pos0, n_sub - 1,
                          paged=None if paged is None else paged[1:])


def _rwkv7_mixer(proj, prev, state0, p, b, T):
    *vectors, gate, bonus = _rwkv_prep(proj, prev, p, T)
    heads = lambda t: t.reshape(b, T, RWKV_HEADS, RWKV_HEAD)
    y, s_t = _rwkv_scan(state0.astype(jnp.float32), *[heads(t) for t in vectors])
    return _rwkv_post(y.reshape(b * T, RWKV_DIM), gate, bonus, p), s_t.astype(state0.dtype)


def _hybrid_layer(x, p, big, layer, pos0, past, ropes):
    b, T, d = x.shape
    xn = _rms_norm(x, p['norm_attn'])
    if past is None:
        proj = _proj(xn, big['w_in'], layer)
        c_prev = None
    else:
        prev = past['shift'][:, None].astype(xn.dtype)
        full = _proj(jnp.concatenate([prev, xn], axis=1), big['w_in'], layer)
        proj = full[:, 1:]
        r0, l0 = COL['rkv'][0], COL['lora'][0]
        c_prev = tuple(full[:, :-1, lo:lo + width].reshape(b * T, width) for lo, width in
                       ((r0, RWKV_DIM), (r0 + RWKV_DIM, RWKV_DIM), (r0 + 2 * RWKV_DIM, RWKV_DIM), (l0, LORA_WIDTH)))

    proj2 = proj.reshape(b * T, PROJ_COLS)
    gains = jnp.stack([p['a_q_norm'], p['a_k_norm'][1], p['a_k_norm'][2], p['b_q_norm'], p['b_k_norm'],
                       jnp.pad(p['idx_k_norm'], (0, LANES - IDX_DIM)),
                       jnp.zeros((LANES,), jnp.float32), jnp.zeros((LANES,), jnp.float32)])
    q_a, q_d, idx_q, cmp_rows, slc_rows, win_rows, dsa_rows, idx_rows = _post_proj(proj2, *ropes, gains)
    per_seq = lambda t: t.reshape((b, -1) + t.shape[1:])
    q_a, q_d, idx_q, idx_rows = per_seq(q_a), per_seq(q_d), per_seq(idx_q), per_seq(idx_rows)
    cmp_rows, slc_rows, win_rows, dsa_rows = per_seq(cmp_rows), per_seq(slc_rows), per_seq(win_rows), per_seq(dsa_rows)
    t0 = COL['tail'][0]
    tail = proj[..., t0:t0 + LANES]
    as_cache = lambda t: t.reshape(b, T, 2, NSA_KV, HEAD_DIM)

    if past is None:
        n_keep = min(WINDOW, T)
        win_state = as_cache(win_rows)[:, -n_keep:]
        rwkv0 = jnp.zeros((b, RWKV_HEADS, RWKV_HEAD, RWKV_HEAD), x.dtype)
        o_a = _nsa_mixer(q_a, tail, cmp_rows, slc_rows, win_rows, pos0, pos0, p)
        o_b = _dsa_attention(q_d, dsa_rows, idx_rows, idx_q, tail, pos0)
    else:
        assert (pos0 + T) // CMP_STRIDE * CMP_STRIDE <= pos0
        win_all = jnp.concatenate([past['nsa_win'], as_cache(win_rows)], axis=1)
        n_keep = past['nsa_win'].shape[1]
        win_state = win_all[:, -n_keep:]
        rwkv0 = past['rwkv']
        page_table = past['page_table']
        o_a = _nsa_mixer(q_a, tail, None, slc_rows, win_all, pos0 - n_keep, pos0, p,
                         paged=(past['cmp_pool'], past['slc_pool'], layer, page_table))
        o_b = _dsa_attention(q_d, dsa_rows, idx_rows, idx_q, tail, pos0,
                             paged=(past['dsa_pool'], past['idx_pool'], layer, page_table))
    o_c, rwkv_new = _rwkv7_mixer(proj2, c_prev, rwkv0, p, b, T)

    mixed = _merge(o_a.reshape(b * T, -1), o_b.reshape(b * T, -1), o_c, proj2, big, layer)
    x = _proj(mixed.reshape(b, T, d), big['w_o'], layer, residual=x)
    x = _ffn(x, p['norm_ffn'], big['w_ffn_in'], big['w_ffn_out'], layer)
    new_state = {'nsa_cmp': as_cache(cmp_rows), 'nsa_slc': as_cache(slc_rows), 'dsa_kv': as_cache(dsa_rows),
                 'dsa_idx': idx_rows, 'nsa_win': win_state, 'rwkv': rwkv_new, 'shift': xn[:, -1]}
    return x, new_state


_PARAM_NAMES = ('norm_attn', 'w_in', 'a_q_norm', 'a_k_norm', 'cmp_w1', 'cmp_pos', 'cmp_w2', 'cmp_b2',
                'b_q_norm', 'b_k_norm', 'idx_k_norm', 'rwkv_mu', 'rwkv_w0', 'rwkv_w2', 'rwkv_a0',
                'rwkv_a2', 'rwkv_g2', 'rwkv_k_k', 'rwkv_k_a', 'rwkv_r_k', 'rwkv_ln_w', 'rwkv_ln_b',
                'p_a', 'p_b', 'p_c', 'w_o', 'norm_ffn', 'w_ffn_in', 'w_ffn_out')


def kernel(x_prompt, x_sample, cache_nsa_cmp, cache_nsa_slc, cache_dsa_kv, cache_dsa_idx, state_nsa_win, state_rwkv, state_shift, page_table, norm_attn, w_in, a_q_norm, a_k_norm, cmp_w1, cmp_pos, cmp_w2, cmp_b2, b_q_norm, b_k_norm, idx_k_norm, rwkv_mu, rwkv_w0, rwkv_w2, rwkv_a0, rwkv_a2, rwkv_g2, rwkv_k_k, rwkv_k_a, rwkv_r_k, rwkv_ln_w, rwkv_ln_b, p_a, p_b, p_c, w_o, norm_ffn, w_ffn_in, w_ffn_out):
    weights = dict(zip(_PARAM_NAMES, (norm_attn, w_in, a_q_norm, a_k_norm, cmp_w1, cmp_pos, cmp_w2, cmp_b2,
                                      b_q_norm, b_k_norm, idx_k_norm, rwkv_mu, rwkv_w0, rwkv_w2, rwkv_a0,
                                      rwkv_a2, rwkv_g2, rwkv_k_k, rwkv_k_a, rwkv_r_k, rwkv_ln_w, rwkv_ln_b,
                                      p_a, p_b, p_c, w_o, norm_ffn, w_ffn_in, w_ffn_out)))
    depth = w_in.shape[0]
    past_len = page_table.shape[1] * cache_nsa_cmp.shape[2]
    yp, ys = x_prompt, x_sample
    st_p, st_s = [], []
    kv_pool = lambda c: c.reshape(c.shape[:2] + (-1, HEAD_DIM))
    big_names = ('w_in', 'p_a', 'p_b', 'p_c', 'w_o', 'w_ffn_in', 'w_ffn_out')
    big = {name: weights[name].astype(jnp.bfloat16) for name in big_names[1:]}
    big['w_in'] = _regroup_w_in(w_in)

    def ropes(pos0, bsz, t_len):
        pos = jnp.tile(pos0 + jnp.arange(t_len, dtype=jnp.int32), bsz)
        return _rope_tables(pos, LANES, HEAD_DIM), _rope_tables(pos, LANES, IDX_DIM)

    ropes_p = ropes(0, *x_prompt.shape[:2])
    ropes_s = ropes(past_len, *x_sample.shape[:2])
    for l in range(depth):
        p = {name: w[l] for name, w in weights.items() if name not in big_names}
        past = {'cmp_pool': kv_pool(cache_nsa_cmp), 'slc_pool': kv_pool(cache_nsa_slc),
                'dsa_pool': kv_pool(cache_dsa_kv), 'idx_pool': jnp.swapaxes(cache_dsa_idx, 2, 3),
                'page_table': page_table,
                'nsa_win': state_nsa_win[l], 'rwkv': state_rwkv[l], 'shift': state_shift[l]}
        yp, sp = _hybrid_layer(yp, p, big, l, 0, None, ropes_p)
        ys, ss = _hybrid_layer(ys, p, big, l, past_len, past, ropes_s)
        st_p.append(sp)
        st_s.append(ss)
    stack = lambda sts, name: jnp.stack([s[name] for s in sts])
    return (yp, ys,
            stack(st_p, 'nsa_cmp'), stack(st_s, 'nsa_cmp'),
            stack(st_p, 'nsa_slc'), stack(st_s, 'nsa_slc'),
            stack(st_p, 'dsa_kv'), stack(st_s, 'dsa_kv'),
            stack(st_p, 'dsa_idx'), stack(st_s, 'dsa_idx'),
            stack(st_p, 'nsa_win'), stack(st_s, 'nsa_win'),
            stack(st_p, 'rwkv'), stack(st_s, 'rwkv'),
            stack(st_p, 'shift'), stack(st_s, 'shift'))
```

```python
import functools
import math

import jax
import jax.numpy as jnp
from jax import lax
from jax.experimental import pallas as pl
from jax.experimental.pallas import tpu as pltpu

D_MODEL = 2048
HEAD_DIM = 128
ROT_DIV = 4
ROPE_THETA = 500000.0
NORM_EPS = 1e-6

NSA_HEADS = 8
NSA_KV = 2
CMP_STRIDE = 16
CMP_LEN = 2 * CMP_STRIDE
SLC_BLOCK = 64
N_SEL = 8
N_LOCAL = 2
WINDOW = 512

DSA_HEADS = 8
DSA_KV = 2
IDX_HEADS = 8
IDX_DIM = 64
TOPK_MAX = 256

RWKV_HEADS = 16
RWKV_HEAD = 64
RWKV_DIM = RWKV_HEADS * RWKV_HEAD
DECAY_LORA = 64
AAA_LORA = 64
GATE_LORA = 160
RWKV_GN_EPS = 64e-5

NEG = -1e30
FORCE = 1e9
LOWEST = -3e38

IN_SPLITS = (
    NSA_HEADS * HEAD_DIM,
    6 * NSA_KV * HEAD_DIM,
    3 * NSA_HEADS,
    DSA_HEADS * HEAD_DIM,
    2 * DSA_KV * HEAD_DIM,
    IDX_HEADS * IDX_DIM,
    IDX_DIM,
    IDX_HEADS,
    3 * RWKV_DIM + DECAY_LORA + AAA_LORA + GATE_LORA,
    3 * D_MODEL,
)

LANES = 128
SUBLANES = 8
VMEM_LIMIT_BYTES = 56 * 1024 * 1024

HEADS_PER_GROUP = NSA_HEADS // NSA_KV
ATTN_SCALE = HEAD_DIM ** -0.5
LOG2_E = math.log2(math.e)
Q_TILE = 128

_NT = (((1,), (1,)), ((), ()))


def _round_up(n, m):
    return -(-n // m) * m


def _mm_kernel(x_ref, w_ref, *rest):
    r_ref = rest[0] if len(rest) == 3 else None
    o_ref, xb_ref = rest[-2:]

    @pl.when(pl.program_id(1) == 0)
    def _():
        xb_ref[...] = x_ref[...].astype(jnp.bfloat16)

    acc = jnp.dot(xb_ref[...], w_ref[0], preferred_element_type=jnp.float32)
    o_ref[...] = acc if r_ref is None else r_ref[...] + acc


def _pick_tile(n, prefs):
    for t in prefs:
        if n % t == 0:
            return t
    return n


def _mm(x, w, layer, residual=None):
    m, k = x.shape
    n = w.shape[2]
    tm = _pick_tile(m, (1024, 512, 256, 128))
    tn = _pick_tile(n, (512, 384, 256, 128))
    tile = pl.BlockSpec((tm, tn), lambda i, j: (i, j))
    return pl.pallas_call(
        _mm_kernel,
        grid=(m // tm, n // tn),
        in_specs=[pl.BlockSpec((tm, k), lambda i, j: (i, 0)),
                  pl.BlockSpec((1, k, tn), lambda i, j: (layer, 0, j))] + ([] if residual is None else [tile]),
        out_specs=tile,
        out_shape=jax.ShapeDtypeStruct((m, n), jnp.float32),
        scratch_shapes=[pltpu.VMEM((tm, k), jnp.bfloat16)],
        compiler_params=pltpu.CompilerParams(
            dimension_semantics=("arbitrary", "arbitrary"),
            vmem_limit_bytes=VMEM_LIMIT_BYTES),
        name="dense_proj",
    )(x, w, *([] if residual is None else [residual]))


def _proj(x, w, layer, residual=None):
    lead = x.shape[:-1]
    res = None if residual is None else residual.reshape(-1, residual.shape[-1])
    return _mm(x.reshape(-1, x.shape[-1]), w, layer, res).reshape(lead + (w.shape[2],))


FFN_TILE = 512


def _ffn_kernel(x_ref, gain_ref, wg_ref, wu_ref, wo_ref, o_ref, hb_ref, acc_ref):
    f = pl.program_id(1)

    @pl.when(f == 0)
    def _():
        x = x_ref[...]
        y = x * lax.rsqrt(jnp.mean(x * x, axis=-1, keepdims=True) + NORM_EPS)
        hb_ref[...] = (y * gain_ref[...]).astype(jnp.bfloat16)
        acc_ref[...] = jnp.zeros_like(acc_ref)

    hb = hb_ref[...]
    gate = jnp.dot(hb, wg_ref[0], preferred_element_type=jnp.float32)
    up = jnp.dot(hb, wu_ref[0], preferred_element_type=jnp.float32)
    act = (jax.nn.silu(gate) * up).astype(jnp.bfloat16)
    acc_ref[...] += jnp.dot(act, wo_ref[0], preferred_element_type=jnp.float32)

    @pl.when(f == pl.num_programs(1) - 1)
    def _():
        o_ref[...] = x_ref[...] + acc_ref[...]


def _ffn(x, gain, w_in, w_out, layer):
    lead = x.shape[:-1]
    x2 = x.reshape(-1, x.shape[-1])
    m, d = x2.shape
    n_f = w_out.shape[1] // FFN_TILE
    tm = _pick_tile(m, (512, 256, 128))
    out = pl.pallas_call(
        _ffn_kernel,
        grid=(m // tm, n_f),
        in_specs=[pl.BlockSpec((tm, d), lambda i, f: (i, 0)),
                  pl.BlockSpec((1, d), lambda i, f: (0, 0)),
                  pl.BlockSpec((1, d, FFN_TILE), lambda i, f: (layer, 0, f)),
                  pl.BlockSpec((1, d, FFN_TILE), lambda i, f: (layer, 0, n_f + f)),
                  pl.BlockSpec((1, FFN_TILE, d), lambda i, f: (layer, f, 0))],
        out_specs=pl.BlockSpec((tm, d), lambda i, f: (i, 0)),
        out_shape=jax.ShapeDtypeStruct((m, d), jnp.float32),
        scratch_shapes=[pltpu.VMEM((tm, d), jnp.bfloat16), pltpu.VMEM((tm, d), jnp.float32)],
        compiler_params=pltpu.CompilerParams(
            dimension_semantics=("arbitrary", "arbitrary"),
            vmem_limit_bytes=VMEM_LIMIT_BYTES),
        name="ffn_block",
    )(x2, gain[None, :], w_in, w_in, w_out)
    return out.reshape(lead + (d,))


RWKV_HEADS_PER_BLOCK = LANES // 2
RWKV_V_HALF = RWKV_HEAD // 2


def _rwkv_scan_kernel(r_ref, w_ref, k_ref, a_ref, b_ref, v_ref, s0_ref, y_ref, s_out_ref, s_ref, rows_ref):
    c = pl.program_id(1)

    @pl.when(c == 0)
    def _():
        s_ref[...] = s0_ref[0]

    steps = r_ref.shape[1]
    low_half = lax.broadcasted_iota(jnp.int32, (RWKV_V_HALF, LANES), 1) < RWKV_HEADS_PER_BLOCK

    def step(t, carry):
        for n, ref in enumerate((r_ref, w_ref, k_ref, a_ref, b_ref)):
            tile = ref[0, t]
            swapped = pltpu.roll(tile, RWKV_HEADS_PER_BLOCK, 1)
            rows_ref[n, :RWKV_V_HALF] = jnp.where(low_half, tile, swapped)
            rows_ref[n, RWKV_V_HALF:] = jnp.where(low_half, swapped, tile)
        parts = [jnp.zeros((RWKV_V_HALF, LANES), jnp.float32) for _ in range(4)]
        for k in range(RWKV_HEAD):
            parts[k % 4] = parts[k % 4] + s_ref[k] * rows_ref[3, k:k + 1, :]
        sa = (parts[0] + parts[1]) + (parts[2] + parts[3])
        vt = v_ref[0, t]
        ys = [jnp.zeros((RWKV_V_HALF, LANES), jnp.float32) for _ in range(4)]
        for k in range(RWKV_HEAD):
            s_new = (s_ref[k] * rows_ref[1, k:k + 1, :] + sa * rows_ref[4, k:k + 1, :]
                     + vt * rows_ref[2, k:k + 1, :])
            s_ref[k] = s_new
            ys[k % 4] = ys[k % 4] + s_new * rows_ref[0, k:k + 1, :]
        y_ref[0, t] = (ys[0] + ys[1]) + (ys[2] + ys[3])
        return carry

    lax.fori_loop(0, steps, step, 0)

    @pl.when(c == pl.num_programs(1) - 1)
    def _():
        s_out_ref[0] = s_ref[...]


def _rwkv_scan(state0, r, w, k, v, a, b):
    bsz, t_len, n_h, n = r.shape
    heads = bsz * n_h
    hb = RWKV_HEADS_PER_BLOCK
    nblk = heads // hb
    tc = _pick_tile(t_len, (128, 64, 32, 16, 8, 4))

    def pack(x):
        x = jnp.transpose(x.reshape(bsz, t_len, n_h, 2, RWKV_V_HALF), (1, 4, 3, 0, 2))
        x = x.reshape(t_len, RWKV_V_HALF, 2, nblk, hb)
        return jnp.transpose(x, (3, 0, 1, 2, 4)).reshape(nblk, t_len, RWKV_V_HALF, LANES)

    s0 = state0.reshape(nblk, hb, 2, RWKV_V_HALF, n)
    s0 = jnp.transpose(s0, (0, 4, 3, 2, 1)).reshape(nblk, n, RWKV_V_HALF, LANES)

    val_spec = pl.BlockSpec((1, tc, RWKV_V_HALF, LANES), lambda i, c: (i, c, 0, 0))
    st_spec = pl.BlockSpec((1, n, RWKV_V_HALF, LANES), lambda i, c: (i, 0, 0, 0))
    y, s_out = pl.pallas_call(
        _rwkv_scan_kernel,
        grid=(nblk, t_len // tc),
        in_specs=[val_spec] * 6 + [st_spec],
        out_specs=[val_spec, st_spec],
        out_shape=[jax.ShapeDtypeStruct((nblk, t_len, RWKV_V_HALF, LANES), jnp.float32),
                   jax.ShapeDtypeStruct((nblk, n, RWKV_V_HALF, LANES), jnp.float32)],
        scratch_shapes=[pltpu.VMEM((n, RWKV_V_HALF, LANES), jnp.float32),
                        pltpu.VMEM((5, n, LANES), jnp.float32)],
        compiler_params=pltpu.CompilerParams(
            dimension_semantics=("arbitrary", "arbitrary"),
            vmem_limit_bytes=VMEM_LIMIT_BYTES),
        name="rwkv7_scan",
    )(pack(r), pack(w), pack(k), pack(a), pack(b), pack(v), s0)

    y = y.reshape(nblk, t_len, RWKV_V_HALF, 2, hb)
    y = jnp.transpose(y, (0, 4, 1, 3, 2)).reshape(bsz, n_h, t_len, n)
    y = jnp.transpose(y, (0, 2, 1, 3))
    s_out = s_out.reshape(nblk, n, RWKV_V_HALF, 2, hb)
    s_out = jnp.transpose(s_out, (0, 4, 3, 2, 1)).reshape(bsz, n_h, n, n)
    return y, s_out


_ORIG = dict(zip(('a_q', 'a_kv', 'a_gate', 'd_q', 'd_kv', 'i_q', 'i_k', 'i_w', 'c', 'merge'),
                 zip([sum(IN_SPLITS[:i]) for i in range(len(IN_SPLITS))], IN_SPLITS)))
LORA_COLS = DECAY_LORA + AAA_LORA + GATE_LORA
LORA_WIDTH = _round_up(LORA_COLS, LANES)
TAIL_PAD = LANES - (IDX_DIM + 3 * NSA_HEADS + IDX_HEADS)
COL = {}
_off = 0
for _name, _width in (('merge', 3 * D_MODEL), ('a_q', NSA_HEADS * HEAD_DIM), ('d_q', DSA_HEADS * HEAD_DIM),
                      ('rkv', 3 * RWKV_DIM), ('a_kv', 6 * NSA_KV * HEAD_DIM), ('d_kv', 2 * DSA_KV * HEAD_DIM),
                      ('i_q', IDX_HEADS * IDX_DIM), ('lora', LORA_WIDTH), ('tail', LANES)):
    COL[_name] = (_off, _width)
    _off += _width
PROJ_COLS = _off
TAIL_IK = (0, IDX_DIM)
TAIL_GATE = (IDX_DIM, IDX_DIM + 3 * NSA_HEADS)
TAIL_IW = (IDX_DIM + 3 * NSA_HEADS, IDX_DIM + 3 * NSA_HEADS + IDX_HEADS)


def _regroup_w_in(w_in):
    cut = lambda name: w_in[..., _ORIG[name][0]:_ORIG[name][0] + _ORIG[name][1]]
    zeros = lambda n: jnp.zeros(w_in.shape[:2] + (n,), w_in.dtype)
    c0 = _ORIG['c'][0]
    parts = [cut('merge'), cut('a_q'), cut('d_q'), w_in[..., c0:c0 + 3 * RWKV_DIM], cut('a_kv'), cut('d_kv'),
             cut('i_q'), w_in[..., c0 + 3 * RWKV_DIM:c0 + 3 * RWKV_DIM + LORA_COLS], zeros(LORA_WIDTH - LORA_COLS),
             cut('i_k'), cut('a_gate'), cut('i_w'), zeros(TAIL_PAD)]
    return jnp.concatenate(parts, axis=-1).astype(jnp.bfloat16)


def _rope_tables(pos, width, head_dim):
    half = head_dim // ROT_DIV // 2
    freqs = ROPE_THETA ** (-jnp.arange(half, dtype=jnp.float32) / half)
    ang = pos.astype(jnp.float32)[:, None] * freqs[None, :]
    cos, sin = jnp.cos(ang), jnp.sin(ang)
    rest = head_dim - 2 * half
    ones, zeros = jnp.ones((pos.shape[0], rest), jnp.float32), jnp.zeros((pos.shape[0], rest), jnp.float32)
    z_half = jnp.zeros_like(sin)
    reps = width // head_dim
    c = jnp.tile(jnp.concatenate([cos, cos, ones], axis=1), (1, reps))
    s_up = jnp.tile(jnp.concatenate([z_half, sin, zeros], axis=1), (1, reps))
    s_down = jnp.tile(jnp.concatenate([-sin, z_half, zeros], axis=1), (1, reps))
    return c, s_up, s_down


def _rotate(y, tables, half):
    c, s_up, s_down = tables
    width = y.shape[-1]
    return y * c + pltpu.roll(y, half, 1) * s_up + pltpu.roll(y, width - half, 1) * s_down


def _head_norm(x, gain):
    return x * lax.rsqrt(jnp.mean(x * x, axis=-1, keepdims=True) + NORM_EPS) * gain


ROPE_HALF = HEAD_DIM // ROT_DIV // 2
IDX_ROPE_HALF = IDX_DIM // ROT_DIV // 2
KV_PLANES = 2 * NSA_KV


def _post_proj_kernel(aq_ref, dq_ref, cmp_ref, slc_ref, win_ref, dkv_ref, iq_ref, tail_ref,
                      c_ref, su_ref, sd_ref, ic_ref, isu_ref, isd_ref, gains_ref,
                      qa_ref, qd_ref, iqo_ref, cmp_o, slc_o, win_o, dsa_o, ik_o):
    tm = aq_ref.shape[0]
    rope = (c_ref[...], su_ref[...], sd_ref[...])
    irope = (ic_ref[...], isu_ref[...], isd_ref[...])
    gain = lambda i: gains_ref[i:i + 1, :]

    def head(ref, h):
        return ref[:, h * HEAD_DIM:(h + 1) * HEAD_DIM]

    for h in range(NSA_HEADS):
        qa_ref[:, h * HEAD_DIM:(h + 1) * HEAD_DIM] = _rotate(
            _head_norm(head(aq_ref, h), gain(0)), rope, ROPE_HALF).astype(qa_ref.dtype)
    for h in range(DSA_HEADS):
        qd_ref[:, h * HEAD_DIM:(h + 1) * HEAD_DIM] = _rotate(
            _head_norm(head(dq_ref, h), gain(3)), rope, ROPE_HALF).astype(qd_ref.dtype)
    for pair in range(IDX_HEADS * IDX_DIM // LANES):
        iqo_ref[:, pair * LANES:(pair + 1) * LANES] = _rotate(
            iq_ref[:, pair * LANES:(pair + 1) * LANES], irope, IDX_ROPE_HALF).astype(iqo_ref.dtype)

    def planes(src, dst, key_gain):
        for c in range(KV_PLANES):
            x = head(src, c)
            if key_gain is not None and c < NSA_KV:
                x = _rotate(_head_norm(x, gain(key_gain)), rope, ROPE_HALF)
            dst[pl.ds(c, tm, stride=KV_PLANES), :] = x

    planes(cmp_ref, cmp_o, None)
    planes(slc_ref, slc_o, 1)
    planes(win_ref, win_o, 2)
    planes(dkv_ref, dsa_o, 4)

    lane = lax.broadcasted_iota(jnp.int32, (tm, LANES), 1)
    ik = jnp.where(lane < IDX_DIM, tail_ref[...], 0.0)
    ik = ik * lax.rsqrt(jnp.sum(ik * ik, axis=-1, keepdims=True) * (1.0 / IDX_DIM) + NORM_EPS) * gain(5)
    ik_o[...] = _rotate(ik, irope, IDX_ROPE_HALF)[:, :IDX_DIM]


def _post_proj(proj, tables, itables, gains):
    m = proj.shape[0]
    tm = _pick_tile(m, (256, 128))
    col = lambda name, width: pl.BlockSpec((tm, width), functools.partial(
        lambda i, blk: (i, blk), blk=COL[name][0] // width))
    kv_w = KV_PLANES * HEAD_DIM
    akv0 = COL['a_kv'][0] // kv_w
    kv_col = lambda n: pl.BlockSpec((tm, kv_w), functools.partial(lambda i, blk: (i, blk), blk=akv0 + n))
    row = lambda width: pl.BlockSpec((tm, width), lambda i: (i, 0))
    planes = pl.BlockSpec((tm * KV_PLANES, HEAD_DIM), lambda i: (i, 0))
    f32 = jnp.float32
    return pl.pallas_call(
        _post_proj_kernel,
        grid=(m // tm,),
        in_specs=[col('a_q', NSA_HEADS * HEAD_DIM), col('d_q', DSA_HEADS * HEAD_DIM),
                  kv_col(0), kv_col(1), kv_col(2), col('d_kv', kv_w), col('i_q', IDX_HEADS * IDX_DIM),
                  col('tail', LANES)] + [row(LANES)] * 6 + [pl.BlockSpec((8, LANES), lambda i: (0, 0))],
        out_specs=[row(NSA_HEADS * HEAD_DIM), row(DSA_HEADS * HEAD_DIM), row(IDX_HEADS * IDX_DIM),
                   planes, planes, planes, planes, row(IDX_DIM)],
        out_shape=[jax.ShapeDtypeStruct((m, NSA_HEADS * HEAD_DIM), jnp.bfloat16),
                   jax.ShapeDtypeStruct((m, DSA_HEADS * HEAD_DIM), jnp.bfloat16),
                   jax.ShapeDtypeStruct((m, IDX_HEADS * IDX_DIM), jnp.bfloat16)]
                  + [jax.ShapeDtypeStruct((m * KV_PLANES, HEAD_DIM), f32)] * 4
                  + [jax.ShapeDtypeStruct((m, IDX_DIM), f32)],
        compiler_params=pltpu.CompilerParams(dimension_semantics=("arbitrary",),
                                             vmem_limit_bytes=VMEM_LIMIT_BYTES),
        name="post_proj",
    )(*([proj] * 8), *tables, *itables, gains)


def _merge_kernel(oa_ref, ob_ref, oc_ref, pa_ref, pb_ref, pc_ref, ga_ref, gb_ref, gc_ref, o_ref, xb_ref):
    @pl.when(pl.program_id(1) == 0)
    def _():
        for n, ref in enumerate((oa_ref, ob_ref, oc_ref)):
            xb_ref[n] = ref[...].astype(jnp.bfloat16)

    acc = None
    for n, (w_ref, g_ref) in enumerate(((pa_ref, ga_ref), (pb_ref, gb_ref), (pc_ref, gc_ref))):
        term = jax.nn.sigmoid(g_ref[...]) * jnp.dot(xb_ref[n], w_ref[0], preferred_element_type=jnp.float32)
        acc = term if acc is None else acc + term
    o_ref[...] = acc


def _merge(o_a, o_b, o_c, proj, big, layer):
    m, k = o_a.shape
    tm = _pick_tile(m, (512, 256, 128))
    tn = 512
    n_blk = D_MODEL // tn
    x_spec = pl.BlockSpec((tm, k), lambda i, j: (i, 0))
    w_spec = pl.BlockSpec((1, k, tn), lambda i, j: (layer, 0, j))
    gate = lambda n: pl.BlockSpec((tm, tn), functools.partial(lambda i, j, n: (i, n * n_blk + j), n=n))
    assert COL['merge'][0] == 0
    return pl.pallas_call(
        _merge_kernel,
        grid=(m // tm, n_blk),
        in_specs=[x_spec] * 3 + [w_spec] * 3 + [gate(0), gate(1), gate(2)],
        out_specs=pl.BlockSpec((tm, tn), lambda i, j: (i, j)),
        out_shape=jax.ShapeDtypeStruct((m, D_MODEL), jnp.float32),
        scratch_shapes=[pltpu.VMEM((3, tm, k), jnp.bfloat16)],
        compiler_params=pltpu.CompilerParams(dimension_semantics=("arbitrary", "arbitrary"),
                                             vmem_limit_bytes=VMEM_LIMIT_BYTES),
        name="mixer_merge",
    )(o_a, o_b, o_c, big['p_a'], big['p_b'], big['p_c'], proj, proj, proj)


def _head_sums(x, ones_ref):
    hi = x.astype(jnp.bfloat16)
    lo = (x - hi.astype(jnp.float32)).astype(jnp.bfloat16)
    ones = ones_ref[...]
    cols = [jnp.dot(hi[:, j:j + LANES], ones, preferred_element_type=jnp.float32)
            + jnp.dot(lo[:, j:j + LANES], ones, preferred_element_type=jnp.float32)
            for j in range(0, x.shape[1], LANES)]
    return jnp.concatenate(cols, axis=1)


def _softplus(z):
    return jnp.maximum(z, 0.0) + jnp.log(1.0 + jnp.exp(-jnp.abs(z)))


def _rwkv_prep_kernel(*refs, carry_rows):
    cur_refs, rest = refs[:4], refs[4:]
    prev_refs, rest = (None, rest) if carry_rows else (rest[:4], rest[4:])
    (mu_ref, mul_ref, w0_ref, a0_ref, kk_ref, ka_ref, rk_ref, w2_ref, a2_ref, g2_ref, ones_ref,
     r_o, w_o, k_o, v_o, a_o, b_o, g_o, bonus_o) = rest[:19]
    last_refs = rest[19:]
    tm = r_o.shape[0]
    first = lax.broadcasted_iota(jnp.int32, (tm, 1), 0) == 0

    @pl.when(pl.program_id(0) == 0)
    def _():
        for ref in last_refs:
            ref[...] = jnp.zeros_like(ref)

    def shifted(n, cur):
        if prev_refs is not None:
            return prev_refs[n][...]
        at_start = (pl.program_id(0) * tm) % carry_rows == 0
        head = jnp.where(at_start, 0.0, last_refs[n][...])
        last_refs[n][...] = cur[tm - 1:tm, :]
        return jnp.where(first, head, pltpu.roll(cur, 1, 0))

    def mixed(n, mu):
        cur = cur_refs[n][...]
        return cur + (shifted(n, cur) - cur) * mu

    r = mixed(0, mu_ref[0:1, :])
    k = mixed(1, mu_ref[1:2, :])
    v = mixed(2, mu_ref[2:3, :])
    lora = mixed(3, mul_ref[...])
    bf = jnp.bfloat16
    w_lo = jnp.tanh(lora[:, :DECAY_LORA]).astype(bf)
    a_lo = lora[:, DECAY_LORA:DECAY_LORA + AAA_LORA].astype(bf)
    g_lo = jax.nn.sigmoid(lora[:, DECAY_LORA + AAA_LORA:LORA_COLS]).astype(bf)
    w = -_softplus(-(w0_ref[...] + jnp.dot(w_lo, w2_ref[...].astype(bf), preferred_element_type=jnp.float32))) - 0.5
    a = jax.nn.sigmoid(a0_ref[...] + jnp.dot(a_lo, a2_ref[...].astype(bf), preferred_element_type=jnp.float32))
    g_o[...] = jnp.dot(g_lo, g2_ref[...].astype(bf), preferred_element_type=jnp.float32)
    kk = k * kk_ref[...]
    kk = kk * lax.rsqrt(_head_sums(kk * kk, ones_ref) + 1e-12)
    kh = k * (1.0 + (a - 1.0) * ka_ref[...])
    r_o[...] = r
    w_o[...] = jnp.exp(-jnp.exp(w))
    k_o[...] = kh
    v_o[...] = v
    a_o[...] = -kk
    b_o[...] = kk * a
    bonus_o[...] = _head_sums(r * kh * rk_ref[...], ones_ref) * v


def _head_ones():
    seg = jnp.arange(LANES) // RWKV_HEAD
    return (seg[:, None] == seg[None, :]).astype(jnp.bfloat16)


def _rwkv_prep(proj, prev, p, seq_len):
    m = proj.shape[0]
    tm = _pick_tile(m, (256, 128))
    if prev is None:
        assert seq_len % tm == 0
    blk = lambda off, width: pl.BlockSpec((tm, width), functools.partial(lambda i, b: (i, b), b=off // width))
    r0 = COL['rkv'][0]
    cur_specs = [blk(r0, RWKV_DIM), blk(r0 + RWKV_DIM, RWKV_DIM), blk(r0 + 2 * RWKV_DIM, RWKV_DIM),
                 blk(COL['lora'][0], LORA_WIDTH)]
    row = lambda width: pl.BlockSpec((tm, width), lambda i: (i, 0))
    const = lambda *shape: pl.BlockSpec(shape, lambda i: (0,) * len(shape))
    mu = p['rwkv_mu']
    mu_rkv = mu[:3 * RWKV_DIM].reshape(3, RWKV_DIM)
    mu_lora = jnp.pad(mu[3 * RWKV_DIM:], (0, LORA_WIDTH - LORA_COLS))[None, :]
    vec = lambda name: p[name].reshape(1, RWKV_DIM)
    small = [mu_rkv, mu_lora, vec('rwkv_w0'), vec('rwkv_a0'), vec('rwkv_k_k'), vec('rwkv_k_a'), vec('rwkv_r_k'),
             p['rwkv_w2'], p['rwkv_a2'], p['rwkv_g2'], _head_ones()]
    small_specs = [const(*s.shape) for s in small]
    out = jax.ShapeDtypeStruct((m, RWKV_DIM), jnp.float32)
    if prev is None:
        ins, specs = [proj] * 4, cur_specs
        scratch = [pltpu.VMEM((1, RWKV_DIM), jnp.float32)] * 3 + [pltpu.VMEM((1, LORA_WIDTH), jnp.float32)]
    else:
        ins = [proj] * 4 + list(prev)
        specs = cur_specs + [row(RWKV_DIM)] * 3 + [row(LORA_WIDTH)]
        scratch = []
    return pl.pallas_call(
        functools.partial(_rwkv_prep_kernel, carry_rows=seq_len if prev is None else None),
        grid=(m // tm,),
        in_specs=specs + small_specs,
        out_specs=[row(RWKV_DIM)] * 8,
        out_shape=[out] * 8,
        scratch_shapes=scratch,
        compiler_params=pltpu.CompilerParams(dimension_semantics=("arbitrary",),
                                             vmem_limit_bytes=VMEM_LIMIT_BYTES),
        name="rwkv_prep",
    )(*ins, *small)


def _rwkv_post_kernel(y_ref, g_ref, bonus_ref, lnw_ref, lnb_ref, ones_ref, o_ref):
    y = y_ref[...]
    mean = _head_sums(y, ones_ref) * (1.0 / RWKV_HEAD)
    dev = y - mean
    var = _head_sums(dev * dev, ones_ref) * (1.0 / RWKV_HEAD)
    out = dev * lax.rsqrt(var + RWKV_GN_EPS) * lnw_ref[...] + lnb_ref[...] + bonus_ref[...]
    o_ref[...] = out * g_ref[...]


def _rwkv_post(y, gate, bonus, p):
    m = y.shape[0]
    tm = _pick_tile(m, (512, 256, 128))
    row = pl.BlockSpec((tm, RWKV_DIM), lambda i: (i, 0))
    vec = pl.BlockSpec((1, RWKV_DIM), lambda i: (0, 0))
    return pl.pallas_call(
        _rwkv_post_kernel,
        grid=(m // tm,),
        in_specs=[row, row, row, vec, vec, pl.BlockSpec((LANES, LANES), lambda i: (0, 0))],
        out_specs=row,
        out_shape=jax.ShapeDtypeStruct((m, RWKV_DIM), jnp.float32),
        compiler_params=pltpu.CompilerParams(dimension_semantics=("arbitrary",),
                                             vmem_limit_bytes=VMEM_LIMIT_BYTES),
        name="rwkv_post",
    )(y, gate, bonus, p['rwkv_ln_w'][None, :], p['rwkv_ln_b'][None, :], _head_ones())


def _group_queries(q, g):
    h0 = g * HEADS_PER_GROUP
    return jnp.concatenate([q[:, (h0 + h) * HEAD_DIM:(h0 + h + 1) * HEAD_DIM] for h in range(HEADS_PER_GROUP)],
                           axis=0).astype(jnp.bfloat16)


def _attend(qg, k, v, mask, want_psum=False):
    tq = mask.shape[0]
    s = lax.dot_general(qg, k, _NT, preferred_element_type=jnp.float32)
    ps = []
    for h in range(HEADS_PER_GROUP):
        sh = jnp.where(mask, s[h * tq:(h + 1) * tq], NEG)
        top = jnp.max(sh, axis=-1, keepdims=True)
        e = jnp.exp2((sh - top) * (ATTN_SCALE * LOG2_E))
        den = jnp.sum(e, axis=-1, keepdims=True)
        ps.append(e * jnp.where(top > 0.5 * NEG, 1.0 / den, 0.0))
    o = jnp.dot(jnp.concatenate(ps, axis=0).astype(jnp.bfloat16), v, preferred_element_type=jnp.float32)
    if want_psum:
        return o, (ps[0] + ps[1]) + (ps[2] + ps[3])
    return o


def _kv_columns(ref, n_kv):
    def get(s, g, rows):
        c0 = (s * n_kv + g) * HEAD_DIM
        return ref[0, rows, c0:c0 + HEAD_DIM]
    return get


def _kv_planes(ref, n_kv):
    return lambda s, g, rows: ref[s * n_kv + g, rows, :]


def _unpack_planes(rows_ref, scr):
    planes, length = scr.shape[0], scr.shape[1]
    for c in range(planes):
        scr[c] = rows_ref[0, pl.ds(c, length, stride=planes), :].astype(jnp.bfloat16)


KEY_GROUP = 512


def _for_key_range(qpos0, tq, n_keys, body):
    lengths = list(range(KEY_GROUP, n_keys, KEY_GROUP)) + [n_keys]
    group = (qpos0 + tq - 1) // KEY_GROUP
    for i, n in enumerate(lengths):
        @pl.when((group == i) if i + 1 < len(lengths) else (group >= i))
        def _(n=n):
            body(n)


PAGES_PER_STEP = 16


def _page_specs(pool, layer, pages_per_seq):
    assert pages_per_seq % PAGES_PER_STEP == 0
    block = (1, 1) + pool.shape[2:]
    return [pl.BlockSpec(block, functools.partial(
        lambda i, j, pt, k: (layer, pt[i, j * PAGES_PER_STEP + k], 0, 0), k=k)) for k in range(PAGES_PER_STEP)]


def _load_kv_pages(page_refs, new_ref, kv_scr, page_rows, past_len):
    j = pl.program_id(1)
    planes = kv_scr.shape[0]
    for k, page in enumerate(page_refs):
        row0 = pl.multiple_of((j * len(page_refs) + k) * page_rows, page_rows)
        for c in range(planes):
            kv_scr[c, pl.ds(row0, page_rows), :] = (
                page[0, 0, pl.ds(c, page_rows, stride=planes), :].astype(jnp.bfloat16))

    @pl.when(j == 0)
    def _():
        tail = kv_scr.shape[1] - past_len
        for c in range(planes):
            kv_scr[c, past_len:, :] = new_ref[0, pl.ds(c, tail, stride=planes), :].astype(jnp.bfloat16)


def _pad_rows(x, n):
    return jnp.pad(x, ((0, 0), (0, n - x.shape[1])) + ((0, 0),) * (x.ndim - 2))


def _nsa_body(q_ref, tail_ref, kc_ref, vc_ref, slc, slc_len, win, win_total, cover_ref, expand_ref, o_ref,
              *, qpos0, win_pos0, n_slc, n_sel):
    tq = q_ref.shape[1]
    qpos = qpos0 + lax.broadcasted_iota(jnp.int32, (tq, 1), 0)
    q = q_ref[0]
    gates = jax.nn.sigmoid(tail_ref[0][:, TAIL_GATE[0]:TAIL_GATE[1]])
    win_len = min(win_total, _round_up(WINDOW + tq, LANES))
    win0 = jnp.clip(qpos0 - win_pos0 - WINDOW, 0, win_total - win_len)
    win_rows = pl.ds(pl.multiple_of(win0, LANES), win_len)
    slc_rows = slice(0, slc_len)

    cpos = lax.broadcasted_iota(jnp.int32, (1, kc_ref.shape[2]), 1) * CMP_STRIDE + (CMP_LEN - 1)
    cmask = cpos <= qpos
    kpos = lax.broadcasted_iota(jnp.int32, (1, slc_len), 1)
    causal = kpos <= qpos
    wpos = win_pos0 + win0 + lax.broadcasted_iota(jnp.int32, (1, win_len), 1)
    wmask = (wpos <= qpos) & (qpos - wpos < WINDOW)
    blk = lax.broadcasted_iota(jnp.int32, (1, cover_ref.shape[1]), 1)
    cover = cover_ref[...]

    qgs = [_group_queries(q, g) for g in range(NSA_KV)]
    cmp_out = [_attend(qgs[g], kc_ref[0, g], vc_ref[0, g], cmask, want_psum=True) for g in range(NSA_KV)]
    p_sum = jnp.concatenate([p for _, p in cmp_out], axis=0)
    p_hi = p_sum.astype(jnp.bfloat16)
    p_lo = (p_sum - p_hi.astype(jnp.float32)).astype(jnp.bfloat16)
    imp = (jnp.dot(p_hi, cover, preferred_element_type=jnp.float32)
           + jnp.dot(p_lo, cover, preferred_element_type=jnp.float32))
    dist = jnp.right_shift(jnp.concatenate([qpos] * NSA_KV, axis=0), int(math.log2(SLC_BLOCK))) - blk
    forced = (blk == 0) | ((dist >= 0) & (dist < N_LOCAL))
    score = jnp.where(forced, FORCE, jnp.where(dist < 0, -FORCE, imp))
    score = jnp.where(blk < n_slc, score, LOWEST)
    sel = jnp.zeros(score.shape, jnp.float32)
    for _ in range(n_sel):
        best = jnp.max(score, axis=-1, keepdims=True)
        first = jnp.min(jnp.where(score == best, blk, cover_ref.shape[1]), axis=-1, keepdims=True)
        hit = blk == first
        sel = jnp.where(hit, 1.0, sel)
        score = jnp.where(hit, LOWEST, score)

    for g in range(NSA_KV):
        qg, o_cmp = qgs[g], cmp_out[g][0]
        smask = jnp.dot(sel[g * tq:(g + 1) * tq].astype(jnp.bfloat16), expand_ref[:, :slc_len],
                        preferred_element_type=jnp.float32) > 0.5
        o_slc = _attend(qg, slc(0, g, slc_rows), slc(1, g, slc_rows), smask & causal)
        o_win = _attend(qg, win(0, g, win_rows), win(1, g, win_rows), wmask)
        for h in range(HEADS_PER_GROUP):
            head = g * HEADS_PER_GROUP + h
            rows = slice(h * tq, (h + 1) * tq)
            o_ref[0, :, head * HEAD_DIM:(head + 1) * HEAD_DIM] = (
                gates[:, 3 * head:3 * head + 1] * o_cmp[rows]
                + gates[:, 3 * head + 1:3 * head + 2] * o_slc[rows]
                + gates[:, 3 * head + 2:3 * head + 3] * o_win[rows])


def _nsa_kernel(q_ref, tail_ref, kc_ref, vc_ref, slc_ref, win_ref, cover_ref, expand_ref, o_ref, slc_scr, win_scr,
                *, pos0, win_pos0, n_slc, n_sel):
    @pl.when(pl.program_id(1) == 0)
    def _():
        _unpack_planes(slc_ref, slc_scr)
        _unpack_planes(win_ref, win_scr)

    tq = q_ref.shape[1]
    qpos0 = pos0 + pl.program_id(1) * tq
    _for_key_range(qpos0, tq, slc_scr.shape[1], lambda n: _nsa_body(
        q_ref, tail_ref, kc_ref, vc_ref, _kv_planes(slc_scr, NSA_KV), n, _kv_planes(win_scr, NSA_KV),
        win_scr.shape[1], cover_ref, expand_ref, o_ref, qpos0=qpos0, win_pos0=win_pos0, n_slc=n_slc, n_sel=n_sel))


def _nsa_paged_kernel(pt_ref, q_ref, tail_ref, kc_ref, vc_ref, new_ref, win_ref, cover_ref, expand_ref, *rest,
                      pos0, win_pos0, n_slc, n_sel, page_rows):
    page_refs, (o_ref, kv_scr) = rest[:PAGES_PER_STEP], rest[PAGES_PER_STEP:]
    _load_kv_pages(page_refs, new_ref, kv_scr, page_rows, pos0)

    @pl.when(pl.program_id(1) == pl.num_programs(1) - 1)
    def _():
        _nsa_body(q_ref, tail_ref, kc_ref, vc_ref, _kv_planes(kv_scr, NSA_KV), kv_scr.shape[1],
                  _kv_columns(win_ref, NSA_KV), win_ref.shape[1], cover_ref, expand_ref, o_ref,
                  qpos0=pos0, win_pos0=win_pos0, n_slc=n_slc, n_sel=n_sel)


def _nsa_attention(q, tail, kc, vc, slc_rows, win_kv, pos0, win_pos0, n_c, paged=None):
    b, t_len, _ = q.shape
    seq = slc_rows.shape[1] // KV_PLANES + (pos0 if paged else 0)
    tq = Q_TILE if t_len % Q_TILE == 0 else _round_up(t_len, SUBLANES)
    tp = _round_up(t_len, tq)
    lp = _round_up(seq, LANES)
    n_cp = kc.shape[2]
    n_slc = -(-seq // SLC_BLOCK)
    n_sp = _round_up(n_slc, LANES)
    start = jnp.arange(n_cp)[:, None] * CMP_STRIDE
    lo = jnp.arange(n_sp)[None, :] * SLC_BLOCK
    cover = ((start < lo + SLC_BLOCK) & (start + CMP_LEN > lo) & (jnp.arange(n_cp)[:, None] < n_c)
             & (jnp.arange(n_sp)[None, :] < n_slc)).astype(jnp.bfloat16)
    expand = (jnp.arange(lp)[None, :] // SLC_BLOCK == jnp.arange(n_sp)[:, None]).astype(jnp.bfloat16)
    static = dict(pos0=pos0, win_pos0=win_pos0, n_slc=n_slc, n_sel=min(N_SEL, n_slc))
    width = NSA_HEADS * HEAD_DIM
    kv_w = KV_PLANES * HEAD_DIM
    out_shape = jax.ShapeDtypeStruct((b, tp, width), jnp.float32)
    params = pltpu.CompilerParams(dimension_semantics=("arbitrary", "arbitrary"),
                                  vmem_limit_bytes=VMEM_LIMIT_BYTES)
    kc, vc = kc.astype(jnp.bfloat16), vc.astype(jnp.bfloat16)
    if paged is None:
        assert seq == lp and tp == t_len
        lw = win_kv.shape[1] // KV_PLANES
        assert lw % LANES == 0
        return pl.pallas_call(
            functools.partial(_nsa_kernel, **static),
            grid=(b, tp // tq),
            in_specs=[pl.BlockSpec((1, tq, width), lambda i, j: (i, j, 0)),
                      pl.BlockSpec((1, tq, LANES), lambda i, j: (i, j, 0)),
                      pl.BlockSpec((1, NSA_KV, n_cp, HEAD_DIM), lambda i, j: (i, 0, 0, 0)),
                      pl.BlockSpec((1, NSA_KV, n_cp, HEAD_DIM), lambda i, j: (i, 0, 0, 0)),
                      pl.BlockSpec((1, lp * KV_PLANES, HEAD_DIM), lambda i, j: (i, 0, 0)),
                      pl.BlockSpec((1, lw * KV_PLANES, HEAD_DIM), lambda i, j: (i, 0, 0)),
                      pl.BlockSpec((n_cp, n_sp), lambda i, j: (0, 0)),
                      pl.BlockSpec((n_sp, lp), lambda i, j: (0, 0))],
            out_specs=pl.BlockSpec((1, tq, width), lambda i, j: (i, j, 0)),
            out_shape=out_shape,
            scratch_shapes=[pltpu.VMEM((KV_PLANES, lp, HEAD_DIM), jnp.bfloat16),
                            pltpu.VMEM((KV_PLANES, lw, HEAD_DIM), jnp.bfloat16)],
            compiler_params=params, name="nsa_attention",
        )(q, tail, kc, vc, slc_rows, win_kv, cover, expand)

    pool, layer, page_table = paged
    assert tp == tq
    page_rows = pool.shape[2] // KV_PLANES
    lw = _round_up(win_kv.shape[1], LANES)
    win = _pad_rows(win_kv.reshape(b, win_kv.shape[1], -1), lw).astype(jnp.bfloat16)
    new_rows = _pad_rows(slc_rows, (lp - pos0) * KV_PLANES)
    out = pl.pallas_call(
        functools.partial(_nsa_paged_kernel, page_rows=page_rows, **static),
        grid_spec=pltpu.PrefetchScalarGridSpec(
            num_scalar_prefetch=1,
            grid=(b, page_table.shape[1] // PAGES_PER_STEP),
            in_specs=[pl.BlockSpec((1, tq, width), lambda i, j, pt: (i, 0, 0)),
                      pl.BlockSpec((1, tq, LANES), lambda i, j, pt: (i, 0, 0)),
                      pl.BlockSpec((1, NSA_KV, n_cp, HEAD_DIM), lambda i, j, pt: (i, 0, 0, 0)),
                      pl.BlockSpec((1, NSA_KV, n_cp, HEAD_DIM), lambda i, j, pt: (i, 0, 0, 0)),
                      pl.BlockSpec((1, (lp - pos0) * KV_PLANES, HEAD_DIM), lambda i, j, pt: (i, 0, 0)),
                      pl.BlockSpec((1, lw, kv_w), lambda i, j, pt: (i, 0, 0)),
                      pl.BlockSpec((n_cp, n_sp), lambda i, j, pt: (0, 0)),
                      pl.BlockSpec((n_sp, lp), lambda i, j, pt: (0, 0))]
                     + _page_specs(pool, layer, page_table.shape[1]),
            out_specs=pl.BlockSpec((1, tq, width), lambda i, j, pt: (i, 0, 0)),
            scratch_shapes=[pltpu.VMEM((KV_PLANES, lp, HEAD_DIM), jnp.bfloat16)]),
        out_shape=out_shape, compiler_params=params, name="nsa_attention_paged",
    )(page_table, _pad_rows(q, tp), _pad_rows(tail, tp), kc, vc, new_rows, win, cover, expand,
      *([pool] * PAGES_PER_STEP))
    return out[:, :t_len]


def _compress_mlp(x, w1_ref, pos_ref, w2_ref, b2_ref, s):
    n_sub = x.shape[0]
    half = CMP_STRIDE * HEAD_DIM
    h_lo = jnp.dot(x, w1_ref[s, :half].astype(jnp.bfloat16), preferred_element_type=jnp.float32)
    h_hi = jnp.dot(x, w1_ref[s, half:].astype(jnp.bfloat16), preferred_element_type=jnp.float32)
    h = h_lo + pltpu.roll(h_hi, n_sub - 1, 0) + pos_ref[s]
    return jnp.dot(jax.nn.gelu(h).astype(jnp.bfloat16), w2_ref[s].astype(jnp.bfloat16),
                   preferred_element_type=jnp.float32) + b2_ref[s]


def _compress_kernel(x_ref, w1_ref, pos_ref, w2_ref, b2_ref, o_ref):
    n_sub = o_ref.shape[3]
    for c in range(KV_PLANES):
        s, g = divmod(c, NSA_KV)
        x = jnp.concatenate([x_ref[0, pl.ds(r * KV_PLANES + c, n_sub, stride=CMP_STRIDE * KV_PLANES), :]
                             .astype(jnp.bfloat16) for r in range(CMP_STRIDE)], axis=1)
        o_ref[0, s, g] = _compress_mlp(x, w1_ref, pos_ref, w2_ref, b2_ref, s)


def _compress_paged_kernel(pt_ref, w1_ref, pos_ref, w2_ref, b2_ref, *rest, page_rows):
    page_refs, (o_ref, x_scr) = rest[:PAGES_PER_STEP], rest[PAGES_PER_STEP:]
    j = pl.program_id(1)
    planes = x_scr.shape[0]
    per_page = page_rows // CMP_STRIDE
    for k, page in enumerate(page_refs):
        row0 = pl.multiple_of((j * len(page_refs) + k) * per_page, per_page)
        for c in range(planes):
            for s in range(CMP_STRIDE):
                x_scr[c, pl.ds(row0, per_page), s * HEAD_DIM:(s + 1) * HEAD_DIM] = (
                    page[0, 0, pl.ds(s * planes + c, per_page, stride=CMP_STRIDE * planes), :])

    @pl.when(j == pl.num_programs(1) - 1)
    def _():
        for c in range(planes):
            s, g = divmod(c, NSA_KV)
            o_ref[0, s, g] = _compress_mlp(x_scr[c].astype(jnp.bfloat16), w1_ref, pos_ref, w2_ref, b2_ref, s)


def _compress(rows, w1, pos_emb, w2, b2, paged=None):
    w1 = w1.reshape(2, CMP_LEN * HEAD_DIM, HEAD_DIM)
    pos_bias = jnp.einsum('kld,klde->ke', pos_emb, w1.reshape(2, CMP_LEN, HEAD_DIM, HEAD_DIM))[:, None, :]
    params = dict(vmem_limit_bytes=VMEM_LIMIT_BYTES)
    if paged is None:
        b = rows.shape[0]
        seq = rows.shape[1] // KV_PLANES
        assert seq % CMP_STRIDE == 0
        n_sub = seq // CMP_STRIDE
        const = lambda *shape: pl.BlockSpec(shape, lambda i: (0,) * len(shape))
        return pl.pallas_call(
            _compress_kernel,
            grid=(b,),
            in_specs=[pl.BlockSpec((1, seq * KV_PLANES, HEAD_DIM), lambda i: (i, 0, 0)),
                      const(2, CMP_LEN * HEAD_DIM, HEAD_DIM), const(2, 1, HEAD_DIM),
                      const(2, HEAD_DIM, HEAD_DIM), const(2, 1, HEAD_DIM)],
            out_specs=pl.BlockSpec((1, 2, NSA_KV, n_sub, HEAD_DIM), lambda i: (i, 0, 0, 0, 0)),
            out_shape=jax.ShapeDtypeStruct((b, 2, NSA_KV, n_sub, HEAD_DIM), jnp.float32),
            compiler_params=pltpu.CompilerParams(dimension_semantics=("arbitrary",), **params),
            name="nsa_compress",
        )(rows, w1, pos_bias, w2, b2[:, None, :])

    pool, layer, page_table = paged
    b, n_pages = page_table.shape
    planes = KV_PLANES
    page_rows = pool.shape[2] // planes
    n_sub = n_pages * page_rows // CMP_STRIDE
    const = lambda *shape: pl.BlockSpec(shape, lambda i, j, pt: (0,) * len(shape))
    return pl.pallas_call(
        functools.partial(_compress_paged_kernel, page_rows=page_rows),
        grid_spec=pltpu.PrefetchScalarGridSpec(
            num_scalar_prefetch=1,
            grid=(b, n_pages // PAGES_PER_STEP),
            in_specs=[const(2, CMP_LEN * HEAD_DIM, HEAD_DIM), const(2, 1, HEAD_DIM),
                      const(2, HEAD_DIM, HEAD_DIM), const(2, 1, HEAD_DIM)]
                     + _page_specs(pool, layer, n_pages),
            out_specs=pl.BlockSpec((1, 2, NSA_KV, n_sub, HEAD_DIM), lambda i, j, pt: (i, 0, 0, 0, 0)),
            scratch_shapes=[pltpu.VMEM((planes, n_sub, CMP_STRIDE * HEAD_DIM), jnp.float32)]),
        out_shape=jax.ShapeDtypeStruct((b, 2, NSA_KV, n_sub, HEAD_DIM), jnp.float32),
        compiler_params=pltpu.CompilerParams(dimension_semantics=("arbitrary",) * 2, **params),
        name="nsa_compress_paged",
    )(page_table, w1, pos_bias, w2, b2[:, None, :], *([pool] * PAGES_PER_STEP))


INT_MIN = -2 ** 31
IDX_W_SCALE = (IDX_HEADS * IDX_DIM) ** -0.5


def _count(pred):
    return jnp.sum(jnp.where(pred, 1.0, 0.0), axis=-1, keepdims=True)


def _dsa_body(q_ref, kv, iq_ref, tail_ref, ik, o_ref, *, qpos0, k_top):
    tq = q_ref.shape[1]
    lp = ik.shape[1]
    rows = slice(0, lp)
    qpos = qpos0 + lax.broadcasted_iota(jnp.int32, (tq, 1), 0)
    kpos = lax.broadcasted_iota(jnp.int32, (1, lp), 1)
    causal = kpos <= qpos

    iq = iq_ref[0].astype(jnp.bfloat16)
    iw = tail_ref[0][:, TAIL_IW[0]:TAIL_IW[1]] * IDX_W_SCALE
    score = jnp.zeros((tq, lp), jnp.float32)
    for h in range(IDX_HEADS):
        logits = jnp.dot(iq[:, h * IDX_DIM:(h + 1) * IDX_DIM], ik, preferred_element_type=jnp.float32)
        score = score + iw[:, h:h + 1] * jnp.maximum(logits, 0.0)
    score = jnp.where(causal, score, NEG)
    score = jnp.where(score == 0.0, 0.0, score)
    bits = lax.bitcast_convert_type(score, jnp.int32)
    key = bits ^ (jnp.right_shift(bits, 31) & 0x7FFFFFFF)

    k_f = float(k_top)
    enough = lambda cand: _count(key >= cand) >= k_f
    thr = jnp.where(enough(0), 0, INT_MIN).astype(jnp.int32)
    top_bit = thr | (1 << 30)
    thr = jnp.where(enough(top_bit), top_bit, thr)

    def refine(i, thr):
        low = 28 - 2 * i
        c1, c2, c3 = (thr | jnp.left_shift(jnp.int32(n), low) for n in (1, 2, 3))
        return jnp.where(enough(c3), c3, jnp.where(enough(c2), c2, jnp.where(enough(c1), c1, thr)))

    thr = lax.fori_loop(0, 15, refine, thr)
    above = key > thr
    tied = key == thr
    need = k_f - _count(above)

    few = lambda cand: _count(tied & (kpos < cand)) < need
    surplus = jnp.max(_count(tied) - need) > 0.0
    bound = jnp.where(surplus, jnp.where(few(1 << 14), 1 << 14, 0), (1 << 15) - 1).astype(jnp.int32)

    def widen(i, c):
        low = 12 - 2 * i
        c1, c2, c3 = (c | jnp.left_shift(jnp.int32(n), low) for n in (1, 2, 3))
        return jnp.where(few(c3), c3, jnp.where(few(c2), c2, jnp.where(few(c1), c1, c)))

    bound = lax.fori_loop(0, jnp.where(surplus, 7, 0), widen, bound)
    mask = causal & (above | (tied & (kpos <= bound)))

    q = q_ref[0]
    for g in range(DSA_KV):
        o = _attend(_group_queries(q, g), kv(0, g, rows), kv(1, g, rows), mask)
        for h in range(HEADS_PER_GROUP):
            head = g * HEADS_PER_GROUP + h
            o_ref[0, :, head * HEAD_DIM:(head + 1) * HEAD_DIM] = o[h * tq:(h + 1) * tq]


def _dsa_kernel(q_ref, kv_ref, iq_ref, tail_ref, ik_ref, o_ref, kv_scr, *, pos0, k_top):
    @pl.when(pl.program_id(1) == 0)
    def _():
        _unpack_planes(kv_ref, kv_scr)

    tq = q_ref.shape[1]
    qpos0 = pos0 + pl.program_id(1) * tq
    _for_key_range(qpos0, tq, kv_scr.shape[1], lambda n: _dsa_body(
        q_ref, _kv_planes(kv_scr, DSA_KV), iq_ref, tail_ref, ik_ref[0, :, :n], o_ref, qpos0=qpos0, k_top=k_top))


def _dsa_paged_kernel(pt_ref, q_ref, iq_ref, tail_ref, new_kv_ref, new_ik_ref, *rest, pos0, k_top, page_rows):
    n = PAGES_PER_STEP
    kv_pages, ik_pages, (o_ref, kv_scr, ik_scr) = rest[:n], rest[n:2 * n], rest[2 * n:]
    _load_kv_pages(kv_pages, new_kv_ref, kv_scr, page_rows, pos0)
    j = pl.program_id(1)
    for k, page in enumerate(ik_pages):
        row0 = pl.multiple_of((j * n + k) * page_rows, page_rows)
        ik_scr[:, pl.ds(row0, page_rows)] = page[0, 0].astype(jnp.bfloat16)

    @pl.when(j == 0)
    def _():
        ik_scr[:, pos0:] = new_ik_ref[0].astype(jnp.bfloat16)

    @pl.when(j == pl.num_programs(1) - 1)
    def _():
        _dsa_body(q_ref, _kv_planes(kv_scr, DSA_KV), iq_ref, tail_ref, ik_scr[...], o_ref, qpos0=pos0, k_top=k_top)


def _dsa_attention(q, kv_rows, idx_k, idx_q, tail, pos0, paged=None):
    b, t_len, _ = q.shape
    seq = idx_k.shape[1] + (pos0 if paged else 0)
    assert seq < 2 ** 15
    tq = Q_TILE if t_len % Q_TILE == 0 else _round_up(t_len, SUBLANES)
    tp = _round_up(t_len, tq)
    lp = _round_up(seq, LANES)
    static = dict(pos0=pos0, k_top=min(TOPK_MAX, seq // 4))
    width = DSA_HEADS * HEAD_DIM
    out_shape = jax.ShapeDtypeStruct((b, tp, width), jnp.float32)
    params = pltpu.CompilerParams(dimension_semantics=("arbitrary", "arbitrary"),
                                  vmem_limit_bytes=VMEM_LIMIT_BYTES)
    if paged is None:
        assert seq == lp and tp == t_len
        return pl.pallas_call(
            functools.partial(_dsa_kernel, **static),
            grid=(b, tp // tq),
            in_specs=[pl.BlockSpec((1, tq, width), lambda i, j: (i, j, 0)),
                      pl.BlockSpec((1, lp * KV_PLANES, HEAD_DIM), lambda i, j: (i, 0, 0)),
                      pl.BlockSpec((1, tq, IDX_HEADS * IDX_DIM), lambda i, j: (i, j, 0)),
                      pl.BlockSpec((1, tq, LANES), lambda i, j: (i, j, 0)),
                      pl.BlockSpec((1, IDX_DIM, lp), lambda i, j: (i, 0, 0))],
            out_specs=pl.BlockSpec((1, tq, width), lambda i, j: (i, j, 0)),
            out_shape=out_shape,
            scratch_shapes=[pltpu.VMEM((KV_PLANES, lp, HEAD_DIM), jnp.bfloat16)],
            compiler_params=params, name="dsa_attention",
        )(q, kv_rows, idx_q, tail, jnp.swapaxes(idx_k, 1, 2).astype(jnp.bfloat16))

    kv_pool, idx_pool, layer, page_table = paged
    assert tp == tq
    page_rows = idx_pool.shape[3]
    n_pages = page_table.shape[1]
    out = pl.pallas_call(
        functools.partial(_dsa_paged_kernel, page_rows=page_rows, **static),
        grid_spec=pltpu.PrefetchScalarGridSpec(
            num_scalar_prefetch=1,
            grid=(b, n_pages // PAGES_PER_STEP),
            in_specs=[pl.BlockSpec((1, tq, width), lambda i, j, pt: (i, 0, 0)),
                      pl.BlockSpec((1, tq, IDX_HEADS * IDX_DIM), lambda i, j, pt: (i, 0, 0)),
                      pl.BlockSpec((1, tq, LANES), lambda i, j, pt: (i, 0, 0)),
                      pl.BlockSpec((1, (lp - pos0) * KV_PLANES, HEAD_DIM), lambda i, j, pt: (i, 0, 0)),
                      pl.BlockSpec((1, IDX_DIM, lp - pos0), lambda i, j, pt: (i, 0, 0))]
                     + _page_specs(kv_pool, layer, n_pages) + _page_specs(idx_pool, layer, n_pages),
            out_specs=pl.BlockSpec((1, tq, width), lambda i, j, pt: (i, 0, 0)),
            scratch_shapes=[pltpu.VMEM((KV_PLANES, lp, HEAD_DIM), jnp.bfloat16),
                            pltpu.VMEM((IDX_DIM, lp), jnp.bfloat16)]),
        out_shape=out_shape, compiler_params=params, name="dsa_attention_paged",
    )(page_table, _pad_rows(q, tp), _pad_rows(idx_q, tp), _pad_rows(tail, tp),
      _pad_rows(kv_rows, (lp - pos0) * KV_PLANES), jnp.swapaxes(_pad_rows(idx_k, lp - pos0), 1, 2),
      *([kv_pool] * PAGES_PER_STEP), *([idx_pool] * PAGES_PER_STEP))
    return out[:, :t_len]


def _rms_norm(x, g):
    xf = x.astype(jnp.float32)
    y = xf * lax.rsqrt(jnp.mean(xf * xf, axis=-1, keepdims=True) + NORM_EPS)
    return (y * g.astype(jnp.float32)).astype(x.dtype)


def _rope_partial(x, pos):
    rot = x.shape[-1] // ROT_DIV
    half = rot // 2
    freqs = ROPE_THETA ** (-jnp.arange(half, dtype=jnp.float32) / half)
    ang = pos.astype(jnp.float32)[:, None] * freqs[None, :]
    cos, sin = jnp.cos(ang)[:, None, :], jnp.sin(ang)[:, None, :]
    xr = x[..., :rot].astype(jnp.float32)
    x1, x2 = xr[..., :half], xr[..., half:]
    xr = jnp.concatenate([x1 * cos - x2 * sin, x1 * sin + x2 * cos], axis=-1)
    return jnp.concatenate([xr.astype(x.dtype), x[..., rot:]], axis=-1)


def _nsa_mixer(q, tail, cmp_rows, slc_rows, win_kv, win_pos0, pos0, p, paged=None):
    if paged is None:
        cmp = _compress(cmp_rows, p['cmp_w1'], p['cmp_pos'], p['cmp_w2'], p['cmp_b2'])
    else:
        cmp = _compress(None, p['cmp_w1'], p['cmp_pos'], p['cmp_w2'], p['cmp_b2'], paged=(paged[0],) + paged[2:])
    n_sub = cmp.shape[3]
    c_pos = jnp.arange(n_sub, dtype=jnp.int32) * CMP_STRIDE + (CMP_LEN - 1)
    kc = jnp.swapaxes(_rope_partial(_rms_norm(jnp.swapaxes(cmp[:, 0], 1, 2), p['a_k_norm'][0]), c_pos), 1, 2)
    return _nsa_attention(q, tail, kc, cmp[:, 1], slc_rows, win_kv, pos0, win_pos0, n_sub - 1,
                          paged=None if paged is None else paged[1:])


def _rwkv7_mixer(proj, prev, state0, p, b, T):
    *vectors, gate, bonus = _rwkv_prep(proj, prev, p, T)
    heads = lambda t: t.reshape(b, T, RWKV_HEADS, RWKV_HEAD)
    y, s_t = _rwkv_scan(state0.astype(jnp.float32), *[heads(t) for t in vectors])
    return _rwkv_post(y.reshape(b * T, RWKV_DIM), gate, bonus, p), s_t.astype(state0.dtype)


def _hybrid_layer(x, p, big, layer, pos0, past, ropes):
    b, T, d = x.shape
    xn = _rms_norm(x, p['norm_attn'])
    if past is None:
        proj = _proj(xn, big['w_in'], layer)
        c_prev = None
    else:
        prev = past['shift'][:, None].astype(xn.dtype)
        full = _proj(jnp.concatenate([prev, xn], axis=1), big['w_in'], layer)
        proj = full[:, 1:]
        r0, l0 = COL['rkv'][0], COL['lora'][0]
        c_prev = tuple(full[:, :-1, lo:lo + width].reshape(b * T, width) for lo, width in
                       ((r0, RWKV_DIM), (r0 + RWKV_DIM, RWKV_DIM), (r0 + 2 * RWKV_DIM, RWKV_DIM), (l0, LORA_WIDTH)))

    proj2 = proj.reshape(b * T, PROJ_COLS)
    gains = jnp.stack([p['a_q_norm'], p['a_k_norm'][1], p['a_k_norm'][2], p['b_q_norm'], p['b_k_norm'],
                       jnp.pad(p['idx_k_norm'], (0, LANES - IDX_DIM)),
                       jnp.zeros((LANES,), jnp.float32), jnp.zeros((LANES,), jnp.float32)])
    q_a, q_d, idx_q, cmp_rows, slc_rows, win_rows, dsa_rows, idx_rows = _post_proj(proj2, *ropes, gains)
    per_seq = lambda t: t.reshape((b, -1) + t.shape[1:])
    q_a, q_d, idx_q, idx_rows = per_seq(q_a), per_seq(q_d), per_seq(idx_q), per_seq(idx_rows)
    cmp_rows, slc_rows, win_rows, dsa_rows = per_seq(cmp_rows), per_seq(slc_rows), per_seq(win_rows), per_seq(dsa_rows)
    t0 = COL['tail'][0]
    tail = proj[..., t0:t0 + LANES]
    as_cache = lambda t: t.reshape(b, T, 2, NSA_KV, HEAD_DIM)

    if past is None:
        n_keep = min(WINDOW, T)
        win_state = as_cache(win_rows)[:, -n_keep:]
        rwkv0 = jnp.zeros((b, RWKV_HEADS, RWKV_HEAD, RWKV_HEAD), x.dtype)
        o_a = _nsa_mixer(q_a, tail, cmp_rows, slc_rows, win_rows, pos0, pos0, p)
        o_b = _dsa_attention(q_d, dsa_rows, idx_rows, idx_q, tail, pos0)
    else:
        assert (pos0 + T) // CMP_STRIDE * CMP_STRIDE <= pos0
        win_all = jnp.concatenate([past['nsa_win'], as_cache(win_rows)], axis=1)
        n_keep = past['nsa_win'].shape[1]
        win_state = win_all[:, -n_keep:]
        rwkv0 = past['rwkv']
        page_table = past['page_table']
        o_a = _nsa_mixer(q_a, tail, None, slc_rows, win_all, pos0 - n_keep, pos0, p,
                         paged=(past['cmp_pool'], past['slc_pool'], layer, page_table))
        o_b = _dsa_attention(q_d, dsa_rows, idx_rows, idx_q, tail, pos0,
                             paged=(past['dsa_pool'], past['idx_pool'], layer, page_table))
    o_c, rwkv_new = _rwkv7_mixer(proj2, c_prev, rwkv0, p, b, T)

    mixed = _merge(o_a.reshape(b * T, -1), o_b.reshape(b * T, -1), o_c, proj2, big, layer)
    x = _proj(mixed.reshape(b, T, d), big['w_o'], layer, residual=x)
    x = _ffn(x, p['norm_ffn'], big['w_ffn_in'], big['w_ffn_out'], layer)
    new_state = {'nsa_cmp': as_cache(cmp_rows), 'nsa_slc': as_cache(slc_rows), 'dsa_kv': as_cache(dsa_rows),
                 'dsa_idx': idx_rows, 'nsa_win': win_state, 'rwkv': rwkv_new, 'shift': xn[:, -1]}
    return x, new_state


_PARAM_NAMES = ('norm_attn', 'w_in', 'a_q_norm', 'a_k_norm', 'cmp_w1', 'cmp_pos', 'cmp_w2', 'cmp_b2',
                'b_q_norm', 'b_k_norm', 'idx_k_norm', 'rwkv_mu', 'rwkv_w0', 'rwkv_w2', 'rwkv_a0',
                'rwkv_a2', 'rwkv_g2', 'rwkv_k_k', 'rwkv_k_a', 'rwkv_r_k', 'rwkv_ln_w', 'rwkv_ln_b',
                'p_a', 'p_b', 'p_c', 'w_o', 'norm_ffn', 'w_ffn_in', 'w_ffn_out')


def kernel(x_prompt, x_sample, cache_nsa_cmp, cache_nsa_slc, cache_dsa_kv, cache_dsa_idx, state_nsa_win, state_rwkv, state_shift, page_table, norm_attn, w_in, a_q_norm, a_k_norm, cmp_w1, cmp_pos, cmp_w2, cmp_b2, b_q_norm, b_k_norm, idx_k_norm, rwkv_mu, rwkv_w0, rwkv_w2, rwkv_a0, rwkv_a2, rwkv_g2, rwkv_k_k, rwkv_k_a, rwkv_r_k, rwkv_ln_w, rwkv_ln_b, p_a, p_b, p_c, w_o, norm_ffn, w_ffn_in, w_ffn_out):
    weights = dict(zip(_PARAM_NAMES, (norm_attn, w_in, a_q_norm, a_k_norm, cmp_w1, cmp_pos, cmp_w2, cmp_b2,
                                      b_q_norm, b_k_norm, idx_k_norm, rwkv_mu, rwkv_w0, rwkv_w2, rwkv_a0,
                                      rwkv_a2, rwkv_g2, rwkv_k_k, rwkv_k_a, rwkv_r_k, rwkv_ln_w, rwkv_ln_b,
                                      p_a, p_b, p_c, w_o, norm_ffn, w_ffn_in, w_ffn_out)))
    depth = w_in.shape[0]
    past_len = page_table.shape[1] * cache_nsa_cmp.shape[2]
    yp, ys = x_prompt, x_sample
    st_p, st_s = [], []
    kv_pool = lambda c: c.reshape(c.shape[:2] + (-1, HEAD_DIM))
    big_names = ('w_in', 'p_a', 'p_b', 'p_c', 'w_o', 'w_ffn_in', 'w_ffn_out')
    big = {name: weights[name].astype(jnp.bfloat16) for name in big_names[1:]}
    big['w_in'] = _regroup_w_in(w_in)

    def ropes(pos0, bsz, t_len):
        pos = jnp.tile(pos0 + jnp.arange(t_len, dtype=jnp.int32), bsz)
        return _rope_tables(pos, LANES, HEAD_DIM), _rope_tables(pos, LANES, IDX_DIM)

    ropes_p = ropes(0, *x_prompt.shape[:2])
    ropes_s = ropes(past_len, *x_sample.shape[:2])
    for l in range(depth):
        p = {name: w[l] for name, w in weights.items() if name not in big_names}
        past = {'cmp_pool': kv_pool(cache_nsa_cmp), 'slc_pool': kv_pool(cache_nsa_slc),
                'dsa_pool': kv_pool(cache_dsa_kv), 'idx_pool': jnp.swapaxes(cache_dsa_idx, 2, 3),
                'page_table': page_table,
                'nsa_win': state_nsa_win[l], 'rwkv': state_rwkv[l], 'shift': state_shift[l]}
        yp, sp = _hybrid_layer(yp, p, big, l, 0, None, ropes_p)
        ys, ss = _hybrid_layer(ys, p, big, l, past_len, past, ropes_s)
        st_p.append(sp)
        st_s.append(ss)
    stack = lambda sts, name: jnp.stack([s[name] for s in sts])
    return (yp, ys,
            stack(st_p, 'nsa_cmp'), stack(st_s, 'nsa_cmp'),
            stack(st_p, 'nsa_slc'), stack(st_s, 'nsa_slc'),
            stack(st_p, 'dsa_kv'), stack(st_s, 'dsa_kv'),
            stack(st_p, 'dsa_idx'), stack(st_s, 'dsa_idx'),
            stack(st_p, 'nsa_win'), stack(st_s, 'nsa_win'),
            stack(st_p, 'rwkv'), stack(st_s, 'rwkv'),
            stack(st_p, 'shift'), stack(st_s, 'shift'))
```

```python
import functools
import math

import jax
import jax.numpy as jnp
from jax import lax
from jax.experimental import pallas as pl
from jax.experimental.pallas import tpu as pltpu

D_MODEL = 2048
HEAD_DIM = 128
ROT_DIV = 4
ROPE_THETA = 500000.0
NORM_EPS = 1e-6

NSA_HEADS = 8
NSA_KV = 2
CMP_STRIDE = 16
CMP_LEN = 2 * CMP_STRIDE
SLC_BLOCK = 64
N_SEL = 8
N_LOCAL = 2
WINDOW = 512

DSA_HEADS = 8
DSA_KV = 2
IDX_HEADS = 8
IDX_DIM = 64
TOPK_MAX = 256

RWKV_HEADS = 16
RWKV_HEAD = 64
RWKV_DIM = RWKV_HEADS * RWKV_HEAD
DECAY_LORA = 64
AAA_LORA = 64
GATE_LORA = 160
RWKV_GN_EPS = 64e-5

NEG = -1e30
FORCE = 1e9
LOWEST = -3e38

IN_SPLITS = (
    NSA_HEADS * HEAD_DIM,
    6 * NSA_KV * HEAD_DIM,
    3 * NSA_HEADS,
    DSA_HEADS * HEAD_DIM,
    2 * DSA_KV * HEAD_DIM,
    IDX_HEADS * IDX_DIM,
    IDX_DIM,
    IDX_HEADS,
    3 * RWKV_DIM + DECAY_LORA + AAA_LORA + GATE_LORA,
    3 * D_MODEL,
)

LANES = 128
SUBLANES = 8
VMEM_LIMIT_BYTES = 56 * 1024 * 1024

HEADS_PER_GROUP = NSA_HEADS // NSA_KV
ATTN_SCALE = HEAD_DIM ** -0.5
LOG2_E = math.log2(math.e)
Q_TILE = 128

_NT = (((1,), (1,)), ((), ()))


def _round_up(n, m):
    return -(-n // m) * m


def _mm_kernel(x_ref, w_ref, *rest):
    r_ref = rest[0] if len(rest) == 3 else None
    o_ref, xb_ref = rest[-2:]

    @pl.when(pl.program_id(1) == 0)
    def _():
        xb_ref[...] = x_ref[...].astype(jnp.bfloat16)

    acc = jnp.dot(xb_ref[...], w_ref[0], preferred_element_type=jnp.float32)
    o_ref[...] = acc if r_ref is None else r_ref[...] + acc


def _pick_tile(n, prefs):
    for t in prefs:
        if n % t == 0:
            return t
    return n


def _mm(x, w, layer, residual=None):
    m, k = x.shape
    n = w.shape[2]
    tm = _pick_tile(m, (1024, 512, 256, 128))
    tn = _pick_tile(n, (512, 384, 256, 128))
    tile = pl.BlockSpec((tm, tn), lambda i, j: (i, j))
    return pl.pallas_call(
        _mm_kernel,
        grid=(m // tm, n // tn),
        in_specs=[pl.BlockSpec((tm, k), lambda i, j: (i, 0)),
                  pl.BlockSpec((1, k, tn), lambda i, j: (layer, 0, j))] + ([] if residual is None else [tile]),
        out_specs=tile,
        out_shape=jax.ShapeDtypeStruct((m, n), jnp.float32),
        scratch_shapes=[pltpu.VMEM((tm, k), jnp.bfloat16)],
        compiler_params=pltpu.CompilerParams(
            dimension_semantics=("arbitrary", "arbitrary"),
            vmem_limit_bytes=VMEM_LIMIT_BYTES),
        name="dense_proj",
    )(x, w, *([] if residual is None else [residual]))


def _proj(x, w, layer, residual=None):
    lead = x.shape[:-1]
    res = None if residual is None else residual.reshape(-1, residual.shape[-1])
    return _mm(x.reshape(-1, x.shape[-1]), w, layer, res).reshape(lead + (w.shape[2],))


FFN_TILE = 512


def _ffn_kernel(x_ref, gain_ref, wg_ref, wu_ref, wo_ref, o_ref, hb_ref, acc_ref):
    f = pl.program_id(1)

    @pl.when(f == 0)
    def _():
        x = x_ref[...]
        y = x * lax.rsqrt(jnp.mean(x * x, axis=-1, keepdims=True) + NORM_EPS)
        hb_ref[...] = (y * gain_ref[...]).astype(jnp.bfloat16)
        acc_ref[...] = jnp.zeros_like(acc_ref)

    hb = hb_ref[...]
    gate = jnp.dot(hb, wg_ref[0], preferred_element_type=jnp.float32)
    up = jnp.dot(hb, wu_ref[0], preferred_element_type=jnp.float32)
    act = (jax.nn.silu(gate) * up).astype(jnp.bfloat16)
    acc_ref[...] += jnp.dot(act, wo_ref[0], preferred_element_type=jnp.float32)

    @pl.when(f == pl.num_programs(1) - 1)
    def _():
        o_ref[...] = x_ref[...] + acc_ref[...]


def _ffn(x, gain, w_in, w_out, layer):
    lead = x.shape[:-1]
    x2 = x.reshape(-1, x.shape[-1])
    m, d = x2.shape
    n_f = w_out.shape[1] // FFN_TILE
    tm = _pick_tile(m, (512, 256, 128))
    out = pl.pallas_call(
        _ffn_kernel,
        grid=(m // tm, n_f),
        in_specs=[pl.BlockSpec((tm, d), lambda i, f: (i, 0)),
                  pl.BlockSpec((1, d), lambda i, f: (0, 0)),
                  pl.BlockSpec((1, d, FFN_TILE), lambda i, f: (layer, 0, f)),
                  pl.BlockSpec((1, d, FFN_TILE), lambda i, f: (layer, 0, n_f + f)),
                  pl.BlockSpec((1, FFN_TILE, d), lambda i, f: (layer, f, 0))],
        out_specs=pl.BlockSpec((tm, d), lambda i, f: (i, 0)),
        out_shape=jax.ShapeDtypeStruct((m, d), jnp.float32),
        scratch_shapes=[pltpu.VMEM((tm, d), jnp.bfloat16), pltpu.VMEM((tm, d), jnp.float32)],
        compiler_params=pltpu.CompilerParams(
            dimension_semantics=("arbitrary", "arbitrary"),
            vmem_limit_bytes=VMEM_LIMIT_BYTES),
        name="ffn_block",
    )(x2, gain[None, :], w_in, w_in, w_out)
    return out.reshape(lead + (d,))


RWKV_HEADS_PER_BLOCK = LANES // 2
RWKV_V_HALF = RWKV_HEAD // 2


def _rwkv_scan_kernel(r_ref, w_ref, k_ref, a_ref, b_ref, v_ref, s0_ref, y_ref, s_out_ref, s_ref, rows_ref):
    c = pl.program_id(1)

    @pl.when(c == 0)
    def _():
        s_ref[...] = s0_ref[0]

    steps = r_ref.shape[1]
    low_half = lax.broadcasted_iota(jnp.int32, (RWKV_V_HALF, LANES), 1) < RWKV_HEADS_PER_BLOCK

    def step(t, carry):
        for n, ref in enumerate((r_ref, w_ref, k_ref, a_ref, b_ref)):
            tile = ref[0, t]
            swapped = pltpu.roll(tile, RWKV_HEADS_PER_BLOCK, 1)
            rows_ref[n, :RWKV_V_HALF] = jnp.where(low_half, tile, swapped)
            rows_ref[n, RWKV_V_HALF:] = jnp.where(low_half, swapped, tile)
        parts = [jnp.zeros((RWKV_V_HALF, LANES), jnp.float32) for _ in range(4)]
        for k in range(RWKV_HEAD):
            parts[k % 4] = parts[k % 4] + s_ref[k] * rows_ref[3, k:k + 1, :]
        sa = (parts[0] + parts[1]) + (parts[2] + parts[3])
        vt = v_ref[0, t]
        ys = [jnp.zeros((RWKV_V_HALF, LANES), jnp.float32) for _ in range(4)]
        for k in range(RWKV_HEAD):
            s_new = (s_ref[k] * rows_ref[1, k:k + 1, :] + sa * rows_ref[4, k:k + 1, :]
                     + vt * rows_ref[2, k:k + 1, :])
            s_ref[k] = s_new
            ys[k % 4] = ys[k % 4] + s_new * rows_ref[0, k:k + 1, :]
        y_ref[0, t] = (ys[0] + ys[1]) + (ys[2] + ys[3])
        return carry

    lax.fori_loop(0, steps, step, 0)

    @pl.when(c == pl.num_programs(1) - 1)
    def _():
        s_out_ref[0] = s_ref[...]


def _rwkv_scan(state0, r, w, k, v, a, b):
    bsz, t_len, n_h, n = r.shape
    heads = bsz * n_h
    hb = RWKV_HEADS_PER_BLOCK
    nblk = heads // hb
    tc = _pick_tile(t_len, (128, 64, 32, 16, 8, 4))

    def pack(x):
        x = jnp.transpose(x.reshape(bsz, t_len, n_h, 2, RWKV_V_HALF), (1, 4, 3, 0, 2))
        x = x.reshape(t_len, RWKV_V_HALF, 2, nblk, hb)
        return jnp.transpose(x, (3, 0, 1, 2, 4)).reshape(nblk, t_len, RWKV_V_HALF, LANES)

    s0 = state0.reshape(nblk, hb, 2, RWKV_V_HALF, n)
    s0 = jnp.transpose(s0, (0, 4, 3, 2, 1)).reshape(nblk, n, RWKV_V_HALF, LANES)

    val_spec = pl.BlockSpec((1, tc, RWKV_V_HALF, LANES), lambda i, c: (i, c, 0, 0))
    st_spec = pl.BlockSpec((1, n, RWKV_V_HALF, LANES), lambda i, c: (i, 0, 0, 0))
    y, s_out = pl.pallas_call(
        _rwkv_scan_kernel,
        grid=(nblk, t_len // tc),
        in_specs=[val_spec] * 6 + [st_spec],
        out_specs=[val_spec, st_spec],
        out_shape=[jax.ShapeDtypeStruct((nblk, t_len, RWKV_V_HALF, LANES), jnp.float32),
                   jax.ShapeDtypeStruct((nblk, n, RWKV_V_HALF, LANES), jnp.float32)],
        scratch_shapes=[pltpu.VMEM((n, RWKV_V_HALF, LANES), jnp.float32),
                        pltpu.VMEM((5, n, LANES), jnp.float32)],
        compiler_params=pltpu.CompilerParams(
            dimension_semantics=("arbitrary", "arbitrary"),
            vmem_limit_bytes=VMEM_LIMIT_BYTES),
        name="rwkv7_scan",
    )(pack(r), pack(w), pack(k), pack(a), pack(b), pack(v), s0)

    y = y.reshape(nblk, t_len, RWKV_V_HALF, 2, hb)
    y = jnp.transpose(y, (0, 4, 1, 3, 2)).reshape(bsz, n_h, t_len, n)
    y = jnp.transpose(y, (0, 2, 1, 3))
    s_out = s_out.reshape(nblk, n, RWKV_V_HALF, 2, hb)
    s_out = jnp.transpose(s_out, (0, 4, 3, 2, 1)).reshape(bsz, n_h, n, n)
    return y, s_out


_ORIG = dict(zip(('a_q', 'a_kv', 'a_gate', 'd_q', 'd_kv', 'i_q', 'i_k', 'i_w', 'c', 'merge'),
                 zip([sum(IN_SPLITS[:i]) for i in range(len(IN_SPLITS))], IN_SPLITS)))
LORA_COLS = DECAY_LORA + AAA_LORA + GATE_LORA
LORA_WIDTH = _round_up(LORA_COLS, LANES)
TAIL_PAD = LANES - (IDX_DIM + 3 * NSA_HEADS + IDX_HEADS)
COL = {}
_off = 0
for _name, _width in (('merge', 3 * D_MODEL), ('a_q', NSA_HEADS * HEAD_DIM), ('d_q', DSA_HEADS * HEAD_DIM),
                      ('rkv', 3 * RWKV_DIM), ('a_kv', 6 * NSA_KV * HEAD_DIM), ('d_kv', 2 * DSA_KV * HEAD_DIM),
                      ('i_q', IDX_HEADS * IDX_DIM), ('lora', LORA_WIDTH), ('tail', LANES)):
    COL[_name] = (_off, _width)
    _off += _width
PROJ_COLS = _off
TAIL_IK = (0, IDX_DIM)
TAIL_GATE = (IDX_DIM, IDX_DIM + 3 * NSA_HEADS)
TAIL_IW = (IDX_DIM + 3 * NSA_HEADS, IDX_DIM + 3 * NSA_HEADS + IDX_HEADS)


def _regroup_w_in(w_in):
    cut = lambda name: w_in[..., _ORIG[name][0]:_ORIG[name][0] + _ORIG[name][1]]
    zeros = lambda n: jnp.zeros(w_in.shape[:2] + (n,), w_in.dtype)
    c0 = _ORIG['c'][0]
    parts = [cut('merge'), cut('a_q'), cut('d_q'), w_in[..., c0:c0 + 3 * RWKV_DIM], cut('a_kv'), cut('d_kv'),
             cut('i_q'), w_in[..., c0 + 3 * RWKV_DIM:c0 + 3 * RWKV_DIM + LORA_COLS], zeros(LORA_WIDTH - LORA_COLS),
             cut('i_k'), cut('a_gate'), cut('i_w'), zeros(TAIL_PAD)]
    return jnp.concatenate(parts, axis=-1).astype(jnp.bfloat16)


def _rope_tables(pos, width, head_dim):
    half = head_dim // ROT_DIV // 2
    freqs = ROPE_THETA ** (-jnp.arange(half, dtype=jnp.float32) / half)
    ang = pos.astype(jnp.float32)[:, None] * freqs[None, :]
    cos, sin = jnp.cos(ang), jnp.sin(ang)
    rest = head_dim - 2 * half
    ones, zeros = jnp.ones((pos.shape[0], rest), jnp.float32), jnp.zeros((pos.shape[0], rest), jnp.float32)
    z_half = jnp.zeros_like(sin)
    reps = width // head_dim
    c = jnp.tile(jnp.concatenate([cos, cos, ones], axis=1), (1, reps))
    s_up = jnp.tile(jnp.concatenate([z_half, sin, zeros], axis=1), (1, reps))
    s_down = jnp.tile(jnp.concatenate([-sin, z_half, zeros], axis=1), (1, reps))
    return c, s_up, s_down


def _rotate(y, tables, half):
    c, s_up, s_down = tables
    width = y.shape[-1]
    return y * c + pltpu.roll(y, half, 1) * s_up + pltpu.roll(y, width - half, 1) * s_down


def _head_norm(x, gain):
    return x * lax.rsqrt(jnp.mean(x * x, axis=-1, keepdims=True) + NORM_EPS) * gain


ROPE_HALF = HEAD_DIM // ROT_DIV // 2
IDX_ROPE_HALF = IDX_DIM // ROT_DIV // 2
KV_PLANES = 2 * NSA_KV


def _post_proj_kernel(aq_ref, dq_ref, cmp_ref, slc_ref, win_ref, dkv_ref, iq_ref, tail_ref,
                      c_ref, su_ref, sd_ref, ic_ref, isu_ref, isd_ref, gains_ref,
                      qa_ref, qd_ref, iqo_ref, cmp_o, slc_o, win_o, dsa_o, ik_o):
    tm = aq_ref.shape[0]
    rope = (c_ref[...], su_ref[...], sd_ref[...])
    irope = (ic_ref[...], isu_ref[...], isd_ref[...])
    gain = lambda i: gains_ref[i:i + 1, :]

    def head(ref, h):
        return ref[:, h * HEAD_DIM:(h + 1) * HEAD_DIM]

    for h in range(NSA_HEADS):
        qa_ref[:, h * HEAD_DIM:(h + 1) * HEAD_DIM] = _rotate(
            _head_norm(head(aq_ref, h), gain(0)), rope, ROPE_HALF).astype(qa_ref.dtype)
    for h in range(DSA_HEADS):
        qd_ref[:, h * HEAD_DIM:(h + 1) * HEAD_DIM] = _rotate(
            _head_norm(head(dq_ref, h), gain(3)), rope, ROPE_HALF).astype(qd_ref.dtype)
    for pair in range(IDX_HEADS * IDX_DIM // LANES):
        iqo_ref[:, pair * LANES:(pair + 1) * LANES] = _rotate(
            iq_ref[:, pair * LANES:(pair + 1) * LANES], irope, IDX_ROPE_HALF).astype(iqo_ref.dtype)

    def planes(src, dst, key_gain):
        for c in range(KV_PLANES):
            x = head(src, c)
            if key_gain is not None and c < NSA_KV:
                x = _rotate(_head_norm(x, gain(key_gain)), rope, ROPE_HALF)
            dst[pl.ds(c, tm, stride=KV_PLANES), :] = x

    planes(cmp_ref, cmp_o, None)
    planes(slc_ref, slc_o, 1)
    planes(win_ref, win_o, 2)
    planes(dkv_ref, dsa_o, 4)

    lane = lax.broadcasted_iota(jnp.int32, (tm, LANES), 1)
    ik = jnp.where(lane < IDX_DIM, tail_ref[...], 0.0)
    ik = ik * lax.rsqrt(jnp.sum(ik * ik, axis=-1, keepdims=True) * (1.0 / IDX_DIM) + NORM_EPS) * gain(5)
    ik_o[...] = _rotate(ik, irope, IDX_ROPE_HALF)[:, :IDX_DIM]


def _post_proj(proj, tables, itables, gains):
    m = proj.shape[0]
    tm = _pick_tile(m, (256, 128))
    col = lambda name, width: pl.BlockSpec((tm, width), functools.partial(
        lambda i, blk: (i, blk), blk=COL[name][0] // width))
    kv_w = KV_PLANES * HEAD_DIM
    akv0 = COL['a_kv'][0] // kv_w
    kv_col = lambda n: pl.BlockSpec((tm, kv_w), functools.partial(lambda i, blk: (i, blk), blk=akv0 + n))
    row = lambda width: pl.BlockSpec((tm, width), lambda i: (i, 0))
    planes = pl.BlockSpec((tm * KV_PLANES, HEAD_DIM), lambda i: (i, 0))
    f32 = jnp.float32
    return pl.pallas_call(
        _post_proj_kernel,
        grid=(m // tm,),
        in_specs=[col('a_q', NSA_HEADS * HEAD_DIM), col('d_q', DSA_HEADS * HEAD_DIM),
                  kv_col(0), kv_col(1), kv_col(2), col('d_kv', kv_w), col('i_q', IDX_HEADS * IDX_DIM),
                  col('tail', LANES)] + [row(LANES)] * 6 + [pl.BlockSpec((8, LANES), lambda i: (0, 0))],
        out_specs=[row(NSA_HEADS * HEAD_DIM), row(DSA_HEADS * HEAD_DIM), row(IDX_HEADS * IDX_DIM),
                   planes, planes, planes, planes, row(IDX_DIM)],
        out_shape=[jax.ShapeDtypeStruct((m, NSA_HEADS * HEAD_DIM), jnp.bfloat16),
                   jax.ShapeDtypeStruct((m, DSA_HEADS * HEAD_DIM), jnp.bfloat16),
                   jax.ShapeDtypeStruct((m, IDX_HEADS * IDX_DIM), jnp.bfloat16)]
                  + [jax.ShapeDtypeStruct((m * KV_PLANES, HEAD_DIM), f32)] * 4
                  + [jax.ShapeDtypeStruct((m, IDX_DIM), f32)],
        compiler_params=pltpu.CompilerParams(dimension_semantics=("arbitrary",),
                                             vmem_limit_bytes=VMEM_LIMIT_BYTES),
        name="post_proj",
    )(*([proj] * 8), *tables, *itables, gains)


def _merge_kernel(oa_ref, ob_ref, oc_ref, pa_ref, pb_ref, pc_ref, ga_ref, gb_ref, gc_ref, o_ref, xb_ref):
    @pl.when(pl.program_id(1) == 0)
    def _():
        for n, ref in enumerate((oa_ref, ob_ref, oc_ref)):
            xb_ref[n] = ref[...].astype(jnp.bfloat16)

    acc = None
    for n, (w_ref, g_ref) in enumerate(((pa_ref, ga_ref), (pb_ref, gb_ref), (pc_ref, gc_ref))):
        term = jax.nn.sigmoid(g_ref[...]) * jnp.dot(xb_ref[n], w_ref[0], preferred_element_type=jnp.float32)
        acc = term if acc is None else acc + term
    o_ref[...] = acc


def _merge(o_a, o_b, o_c, proj, big, layer):
    m, k = o_a.shape
    tm = _pick_tile(m, (512, 256, 128))
    tn = 512
    n_blk = D_MODEL // tn
    x_spec = pl.BlockSpec((tm, k), lambda i, j: (i, 0))
    w_spec = pl.BlockSpec((1, k, tn), lambda i, j: (layer, 0, j))
    gate = lambda n: pl.BlockSpec((tm, tn), functools.partial(lambda i, j, n: (i, n * n_blk + j), n=n))
    assert COL['merge'][0] == 0
    return pl.pallas_call(
        _merge_kernel,
        grid=(m // tm, n_blk),
        in_specs=[x_spec] * 3 + [w_spec] * 3 + [gate(0), gate(1), gate(2)],
        out_specs=pl.BlockSpec((tm, tn), lambda i, j: (i, j)),
        out_shape=jax.ShapeDtypeStruct((m, D_MODEL), jnp.float32),
        scratch_shapes=[pltpu.VMEM((3, tm, k), jnp.bfloat16)],
        compiler_params=pltpu.CompilerParams(dimension_semantics=("arbitrary", "arbitrary"),
                                             vmem_limit_bytes=VMEM_LIMIT_BYTES),
        name="mixer_merge",
    )(o_a, o_b, o_c, big['p_a'], big['p_b'], big['p_c'], proj, proj, proj)


def _head_sums(x, ones_ref):
    hi = x.astype(jnp.bfloat16)
    lo = (x - hi.astype(jnp.float32)).astype(jnp.bfloat16)
    ones = ones_ref[...]
    cols = [jnp.dot(hi[:, j:j + LANES], ones, preferred_element_type=jnp.float32)
            + jnp.dot(lo[:, j:j + LANES], ones, preferred_element_type=jnp.float32)
            for j in range(0, x.shape[1], LANES)]
    return jnp.concatenate(cols, axis=1)


def _softplus(z):
    return jnp.maximum(z, 0.0) + jnp.log(1.0 + jnp.exp(-jnp.abs(z)))


def _rwkv_prep_kernel(*refs, carry_rows):
    cur_refs, rest = refs[:4], refs[4:]
    prev_refs, rest = (None, rest) if carry_rows else (rest[:4], rest[4:])
    (mu_ref, mul_ref, w0_ref, a0_ref, kk_ref, ka_ref, rk_ref, w2_ref, a2_ref, g2_ref, ones_ref,
     r_o, w_o, k_o, v_o, a_o, b_o, g_o, bonus_o) = rest[:19]
    last_refs = rest[19:]
    tm = r_o.shape[0]
    first = lax.broadcasted_iota(jnp.int32, (tm, 1), 0) == 0

    @pl.when(pl.program_id(0) == 0)
    def _():
        for ref in last_refs:
            ref[...] = jnp.zeros_like(ref)

    def shifted(n, cur):
        if prev_refs is not None:
            return prev_refs[n][...]
        at_start = (pl.program_id(0) * tm) % carry_rows == 0
        head = jnp.where(at_start, 0.0, last_refs[n][...])
        last_refs[n][...] = cur[tm - 1:tm, :]
        return jnp.where(first, head, pltpu.roll(cur, 1, 0))

    def mixed(n, mu):
        cur = cur_refs[n][...]
        return cur + (shifted(n, cur) - cur) * mu

    r = mixed(0, mu_ref[0:1, :])
    k = mixed(1, mu_ref[1:2, :])
    v = mixed(2, mu_ref[2:3, :])
    lora = mixed(3, mul_ref[...])
    bf = jnp.bfloat16
    w_lo = jnp.tanh(lora[:, :DECAY_LORA]).astype(bf)
    a_lo = lora[:, DECAY_LORA:DECAY_LORA + AAA_LORA].astype(bf)
    g_lo = jax.nn.sigmoid(lora[:, DECAY_LORA + AAA_LORA:LORA_COLS]).astype(bf)
    w = -_softplus(-(w0_ref[...] + jnp.dot(w_lo, w2_ref[...].astype(bf), preferred_element_type=jnp.float32))) - 0.5
    a = jax.nn.sigmoid(a0_ref[...] + jnp.dot(a_lo, a2_ref[...].astype(bf), preferred_element_type=jnp.float32))
    g_o[...] = jnp.dot(g_lo, g2_ref[...].astype(bf), preferred_element_type=jnp.float32)
    kk = k * kk_ref[...]
    kk = kk * lax.rsqrt(_head_sums(kk * kk, ones_ref) + 1e-12)
    kh = k * (1.0 + (a - 1.0) * ka_ref[...])
    r_o[...] = r
    w_o[...] = jnp.exp(-jnp.exp(w))
    k_o[...] = kh
    v_o[...] = v
    a_o[...] = -kk
    b_o[...] = kk * a
    bonus_o[...] = _head_sums(r * kh * rk_ref[...], ones_ref) * v


def _head_ones():
    seg = jnp.arange(LANES) // RWKV_HEAD
    return (seg[:, None] == seg[None, :]).astype(jnp.bfloat16)


def _rwkv_prep(proj, prev, p, seq_len):
    m = proj.shape[0]
    tm = _pick_tile(m, (256, 128))
    if prev is None:
        assert seq_len % tm == 0
    blk = lambda off, width: pl.BlockSpec((tm, width), functools.partial(lambda i, b: (i, b), b=off // width))
    r0 = COL['rkv'][0]
    cur_specs = [blk(r0, RWKV_DIM), blk(r0 + RWKV_DIM, RWKV_DIM), blk(r0 + 2 * RWKV_DIM, RWKV_DIM),
                 blk(COL['lora'][0], LORA_WIDTH)]
    row = lambda width: pl.BlockSpec((tm, width), lambda i: (i, 0))
    const = lambda *shape: pl.BlockSpec(shape, lambda i: (0,) * len(shape))
    mu = p['rwkv_mu']
    mu_rkv = mu[:3 * RWKV_DIM].reshape(3, RWKV_DIM)
    mu_lora = jnp.pad(mu[3 * RWKV_DIM:], (0, LORA_WIDTH - LORA_COLS))[None, :]
    vec = lambda name: p[name].reshape(1, RWKV_DIM)
    small = [mu_rkv, mu_lora, vec('rwkv_w0'), vec('rwkv_a0'), vec('rwkv_k_k'), vec('rwkv_k_a'), vec('rwkv_r_k'),
             p['rwkv_w2'], p['rwkv_a2'], p['rwkv_g2'], _head_ones()]
    small_specs = [const(*s.shape) for s in small]
    out = jax.ShapeDtypeStruct((m, RWKV_DIM), jnp.float32)
    if prev is None:
        ins, specs = [proj] * 4, cur_specs
        scratch = [pltpu.VMEM((1, RWKV_DIM), jnp.float32)] * 3 + [pltpu.VMEM((1, LORA_WIDTH), jnp.float32)]
    else:
        ins = [proj] * 4 + list(prev)
        specs = cur_specs + [row(RWKV_DIM)] * 3 + [row(LORA_WIDTH)]
        scratch = []
    return pl.pallas_call(
        functools.partial(_rwkv_prep_kernel, carry_rows=seq_len if prev is None else None),
        grid=(m // tm,),
        in_specs=specs + small_specs,
        out_specs=[row(RWKV_DIM)] * 8,
        out_shape=[out] * 8,
        scratch_shapes=scratch,
        compiler_params=pltpu.CompilerParams(dimension_semantics=("arbitrary",),
                                             vmem_limit_bytes=VMEM_LIMIT_BYTES),
        name="rwkv_prep",
    )(*ins, *small)


def _rwkv_post_kernel(y_ref, g_ref, bonus_ref, lnw_ref, lnb_ref, ones_ref, o_ref):
    y = y_ref[...]
    mean = _head_sums(y, ones_ref) * (1.0 / RWKV_HEAD)
    dev = y - mean
    var = _head_sums(dev * dev, ones_ref) * (1.0 / RWKV_HEAD)
    out = dev * lax.rsqrt(var + RWKV_GN_EPS) * lnw_ref[...] + lnb_ref[...] + bonus_ref[...]
    o_ref[...] = out * g_ref[...]


def _rwkv_post(y, gate, bonus, p):
    m = y.shape[0]
    tm = _pick_tile(m, (512, 256, 128))
    row = pl.BlockSpec((tm, RWKV_DIM), lambda i: (i, 0))
    vec = pl.BlockSpec((1, RWKV_DIM), lambda i: (0, 0))
    return pl.pallas_call(
        _rwkv_post_kernel,
        grid=(m // tm,),
        in_specs=[row, row, row, vec, vec, pl.BlockSpec((LANES, LANES), lambda i: (0, 0))],
        out_specs=row,
        out_shape=jax.ShapeDtypeStruct((m, RWKV_DIM), jnp.float32),
        compiler_params=pltpu.CompilerParams(dimension_semantics=("arbitrary",),
                                             vmem_limit_bytes=VMEM_LIMIT_BYTES),
        name="rwkv_post",
    )(y, gate, bonus, p['rwkv_ln_w'][None, :], p['rwkv_ln_b'][None, :], _head_ones())


def _group_queries(q, g):
    h0 = g * HEADS_PER_GROUP
    return jnp.concatenate([q[:, (h0 + h) * HEAD_DIM:(h0 + h + 1) * HEAD_DIM] for h in range(HEADS_PER_GROUP)],
                           axis=0).astype(jnp.bfloat16)


def _attend(qg, k, v, mask, want_psum=False):
    tq = mask.shape[0]
    s = lax.dot_general(qg, k, _NT, preferred_element_type=jnp.float32)
    ps = []
    for h in range(HEADS_PER_GROUP):
        sh = jnp.where(mask, s[h * tq:(h + 1) * tq], NEG)
        top = jnp.max(sh, axis=-1, keepdims=True)
        e = jnp.exp2((sh - top) * (ATTN_SCALE * LOG2_E))
        den = jnp.sum(e, axis=-1, keepdims=True)
        ps.append(e * jnp.where(top > 0.5 * NEG, 1.0 / den, 0.0))
    o = jnp.dot(jnp.concatenate(ps, axis=0).astype(jnp.bfloat16), v, preferred_element_type=jnp.float32)
    if want_psum:
        return o, (ps[0] + ps[1]) + (ps[2] + ps[3])
    return o


def _kv_columns(ref, n_kv):
    def get(s, g, rows):
        c0 = (s * n_kv + g) * HEAD_DIM
        return ref[0, rows, c0:c0 + HEAD_DIM]
    return get


def _kv_planes(ref, n_kv):
    return lambda s, g, rows: ref[s * n_kv + g, rows, :]


def _unpack_planes(rows_ref, scr):
    planes, length = scr.shape[0], scr.shape[1]
    for c in range(planes):
        scr[c] = rows_ref[0, pl.ds(c, length, stride=planes), :].astype(jnp.bfloat16)


KEY_GROUP = 512


def _for_key_range(qpos0, tq, n_keys, body):
    lengths = list(range(KEY_GROUP, n_keys, KEY_GROUP)) + [n_keys]
    group = (qpos0 + tq - 1) // KEY_GROUP
    for i, n in enumerate(lengths):
        @pl.when((group == i) if i + 1 < len(lengths) else (group >= i))
        def _(n=n):
            body(n)


PAGES_PER_STEP = 32


def _page_specs(pool, layer, pages_per_seq):
    assert pages_per_seq % PAGES_PER_STEP == 0
    block = (1, 1) + pool.shape[2:]
    return [pl.BlockSpec(block, functools.partial(
        lambda i, j, pt, k: (layer, pt[i, j * PAGES_PER_STEP + k], 0, 0), k=k)) for k in range(PAGES_PER_STEP)]


def _load_kv_pages(page_refs, new_ref, kv_scr, page_rows, past_len):
    j = pl.program_id(1)
    planes = kv_scr.shape[0]
    for k, page in enumerate(page_refs):
        row0 = pl.multiple_of((j * len(page_refs) + k) * page_rows, page_rows)
        for c in range(planes):
            kv_scr[c, pl.ds(row0, page_rows), :] = (
                page[0, 0, pl.ds(c, page_rows, stride=planes), :].astype(jnp.bfloat16))

    @pl.when(j == 0)
    def _():
        tail = kv_scr.shape[1] - past_len
        for c in range(planes):
            kv_scr[c, past_len:, :] = new_ref[0, pl.ds(c, tail, stride=planes), :].astype(jnp.bfloat16)


def _pad_rows(x, n):
    return jnp.pad(x, ((0, 0), (0, n - x.shape[1])) + ((0, 0),) * (x.ndim - 2))


def _nsa_body(q_ref, tail_ref, kc_ref, vc_ref, slc, slc_len, win, win_total, cover_ref, expand_ref, o_ref,
              *, qpos0, win_pos0, n_slc, n_sel):
    tq = q_ref.shape[1]
    qpos = qpos0 + lax.broadcasted_iota(jnp.int32, (tq, 1), 0)
    q = q_ref[0]
    gates = jax.nn.sigmoid(tail_ref[0][:, TAIL_GATE[0]:TAIL_GATE[1]])
    win_len = min(win_total, _round_up(WINDOW + tq, LANES))
    win0 = jnp.clip(qpos0 - win_pos0 - WINDOW, 0, win_total - win_len)
    win_rows = pl.ds(pl.multiple_of(win0, LANES), win_len)
    slc_rows = slice(0, slc_len)

    cpos = lax.broadcasted_iota(jnp.int32, (1, kc_ref.shape[2]), 1) * CMP_STRIDE + (CMP_LEN - 1)
    cmask = cpos <= qpos
    kpos = lax.broadcasted_iota(jnp.int32, (1, slc_len), 1)
    causal = kpos <= qpos
    wpos = win_pos0 + win0 + lax.broadcasted_iota(jnp.int32, (1, win_len), 1)
    wmask = (wpos <= qpos) & (qpos - wpos < WINDOW)
    blk = lax.broadcasted_iota(jnp.int32, (1, cover_ref.shape[1]), 1)
    cover = cover_ref[...]

    qgs = [_group_queries(q, g) for g in range(NSA_KV)]
    cmp_out = [_attend(qgs[g], kc_ref[0, g], vc_ref[0, g], cmask, want_psum=True) for g in range(NSA_KV)]
    p_sum = jnp.concatenate([p for _, p in cmp_out], axis=0)
    p_hi = p_sum.astype(jnp.bfloat16)
    p_lo = (p_sum - p_hi.astype(jnp.float32)).astype(jnp.bfloat16)
    imp = (jnp.dot(p_hi, cover, preferred_element_type=jnp.float32)
           + jnp.dot(p_lo, cover, preferred_element_type=jnp.float32))
    dist = jnp.right_shift(jnp.concatenate([qpos] * NSA_KV, axis=0), int(math.log2(SLC_BLOCK))) - blk
    forced = (blk == 0) | ((dist >= 0) & (dist < N_LOCAL))
    score = jnp.where(forced, FORCE, jnp.where(dist < 0, -FORCE, imp))
    score = jnp.where(blk < n_slc, score, LOWEST)
    sel = jnp.zeros(score.shape, jnp.float32)
    for _ in range(n_sel):
        best = jnp.max(score, axis=-1, keepdims=True)
        first = jnp.min(jnp.where(score == best, blk, cover_ref.shape[1]), axis=-1, keepdims=True)
        hit = blk == first
        sel = jnp.where(hit, 1.0, sel)
        score = jnp.where(hit, LOWEST, score)

    for g in range(NSA_KV):
        qg, o_cmp = qgs[g], cmp_out[g][0]
        smask = jnp.dot(sel[g * tq:(g + 1) * tq].astype(jnp.bfloat16), expand_ref[:, :slc_len],
                        preferred_element_type=jnp.float32) > 0.5
        o_slc = _attend(qg, slc(0, g, slc_rows), slc(1, g, slc_rows), smask & causal)
        o_win = _attend(qg, win(0, g, win_rows), win(1, g, win_rows), wmask)
        for h in range(HEADS_PER_GROUP):
            head = g * HEADS_PER_GROUP + h
            rows = slice(h * tq, (h + 1) * tq)
            o_ref[0, :, head * HEAD_DIM:(head + 1) * HEAD_DIM] = (
                gates[:, 3 * head:3 * head + 1] * o_cmp[rows]
                + gates[:, 3 * head + 1:3 * head + 2] * o_slc[rows]
                + gates[:, 3 * head + 2:3 * head + 3] * o_win[rows])


def _nsa_kernel(q_ref, tail_ref, kc_ref, vc_ref, slc_ref, win_ref, cover_ref, expand_ref, o_ref, slc_scr, win_scr,
                *, pos0, win_pos0, n_slc, n_sel):
    @pl.when(pl.program_id(1) == 0)
    def _():
        _unpack_planes(slc_ref, slc_scr)
        _unpack_planes(win_ref, win_scr)

    tq = q_ref.shape[1]
    qpos0 = pos0 + pl.program_id(1) * tq
    _for_key_range(qpos0, tq, slc_scr.shape[1], lambda n: _nsa_body(
        q_ref, tail_ref, kc_ref, vc_ref, _kv_planes(slc_scr, NSA_KV), n, _kv_planes(win_scr, NSA_KV),
        win_scr.shape[1], cover_ref, expand_ref, o_ref, qpos0=qpos0, win_pos0=win_pos0, n_slc=n_slc, n_sel=n_sel))


def _nsa_paged_kernel(pt_ref, q_ref, tail_ref, kc_ref, vc_ref, new_ref, win_ref, cover_ref, expand_ref, *rest,
                      pos0, win_pos0, n_slc, n_sel, page_rows):
    page_refs, (o_ref, kv_scr) = rest[:PAGES_PER_STEP], rest[PAGES_PER_STEP:]
    _load_kv_pages(page_refs, new_ref, kv_scr, page_rows, pos0)

    @pl.when(pl.program_id(1) == pl.num_programs(1) - 1)
    def _():
        _nsa_body(q_ref, tail_ref, kc_ref, vc_ref, _kv_planes(kv_scr, NSA_KV), kv_scr.shape[1],
                  _kv_columns(win_ref, NSA_KV), win_ref.shape[1], cover_ref, expand_ref, o_ref,
                  qpos0=pos0, win_pos0=win_pos0, n_slc=n_slc, n_sel=n_sel)


def _nsa_attention(q, tail, kc, vc, slc_rows, win_kv, pos0, win_pos0, n_c, paged=None):
    b, t_len, _ = q.shape
    seq = slc_rows.shape[1] // KV_PLANES + (pos0 if paged else 0)
    tq = Q_TILE if t_len % Q_TILE == 0 else _round_up(t_len, SUBLANES)
    tp = _round_up(t_len, tq)
    lp = _round_up(seq, LANES)
    n_cp = kc.shape[2]
    n_slc = -(-seq // SLC_BLOCK)
    n_sp = _round_up(n_slc, LANES)
    start = jnp.arange(n_cp)[:, None] * CMP_STRIDE
    lo = jnp.arange(n_sp)[None, :] * SLC_BLOCK
    cover = ((start < lo + SLC_BLOCK) & (start + CMP_LEN > lo) & (jnp.arange(n_cp)[:, None] < n_c)
             & (jnp.arange(n_sp)[None, :] < n_slc)).astype(jnp.bfloat16)
    expand = (jnp.arange(lp)[None, :] // SLC_BLOCK == jnp.arange(n_sp)[:, None]).astype(jnp.bfloat16)
    static = dict(pos0=pos0, win_pos0=win_pos0, n_slc=n_slc, n_sel=min(N_SEL, n_slc))
    width = NSA_HEADS * HEAD_DIM
    kv_w = KV_PLANES * HEAD_DIM
    out_shape = jax.ShapeDtypeStruct((b, tp, width), jnp.float32)
    params = pltpu.CompilerParams(dimension_semantics=("arbitrary", "arbitrary"),
                                  vmem_limit_bytes=VMEM_LIMIT_BYTES)
    kc, vc = kc.astype(jnp.bfloat16), vc.astype(jnp.bfloat16)
    if paged is None:
        assert seq == lp and tp == t_len
        lw = win_kv.shape[1] // KV_PLANES
        assert lw % LANES == 0
        return pl.pallas_call(
            functools.partial(_nsa_kernel, **static),
            grid=(b, tp // tq),
            in_specs=[pl.BlockSpec((1, tq, width), lambda i, j: (i, j, 0)),
                      pl.BlockSpec((1, tq, LANES), lambda i, j: (i, j, 0)),
                      pl.BlockSpec((1, NSA_KV, n_cp, HEAD_DIM), lambda i, j: (i, 0, 0, 0)),
                      pl.BlockSpec((1, NSA_KV, n_cp, HEAD_DIM), lambda i, j: (i, 0, 0, 0)),
                      pl.BlockSpec((1, lp * KV_PLANES, HEAD_DIM), lambda i, j: (i, 0, 0)),
                      pl.BlockSpec((1, lw * KV_PLANES, HEAD_DIM), lambda i, j: (i, 0, 0)),
                      pl.BlockSpec((n_cp, n_sp), lambda i, j: (0, 0)),
                      pl.BlockSpec((n_sp, lp), lambda i, j: (0, 0))],
            out_specs=pl.BlockSpec((1, tq, width), lambda i, j: (i, j, 0)),
            out_shape=out_shape,
            scratch_shapes=[pltpu.VMEM((KV_PLANES, lp, HEAD_DIM), jnp.bfloat16),
                            pltpu.VMEM((KV_PLANES, lw, HEAD_DIM), jnp.bfloat16)],
            compiler_params=params, name="nsa_attention",
        )(q, tail, kc, vc, slc_rows, win_kv, cover, expand)

    pool, layer, page_table = paged
    assert tp == tq
    page_rows = pool.shape[2] // KV_PLANES
    lw = _round_up(win_kv.shape[1], LANES)
    win = _pad_rows(win_kv.reshape(b, win_kv.shape[1], -1), lw).astype(jnp.bfloat16)
    new_rows = _pad_rows(slc_rows, (lp - pos0) * KV_PLANES)
    out = pl.pallas_call(
        functools.partial(_nsa_paged_kernel, page_rows=page_rows, **static),
        grid_spec=pltpu.PrefetchScalarGridSpec(
            num_scalar_prefetch=1,
            grid=(b, page_table.shape[1] // PAGES_PER_STEP),
            in_specs=[pl.BlockSpec((1, tq, width), lambda i, j, pt: (i, 0, 0)),
                      pl.BlockSpec((1, tq, LANES), lambda i, j, pt: (i, 0, 0)),
                      pl.BlockSpec((1, NSA_KV, n_cp, HEAD_DIM), lambda i, j, pt: (i, 0, 0, 0)),
                      pl.BlockSpec((1, NSA_KV, n_cp, HEAD_DIM), lambda i, j, pt: (i, 0, 0, 0)),
                      pl.BlockSpec((1, (lp - pos0) * KV_PLANES, HEAD_DIM), lambda i, j, pt: (i, 0, 0)),
                      pl.BlockSpec((1, lw, kv_w), lambda i, j, pt: (i, 0, 0)),
                      pl.BlockSpec((n_cp, n_sp), lambda i, j, pt: (0, 0)),
                      pl.BlockSpec((n_sp, lp), lambda i, j, pt: (0, 0))]
                     + _page_specs(pool, layer, page_table.shape[1]),
            out_specs=pl.BlockSpec((1, tq, width), lambda i, j, pt: (i, 0, 0)),
            scratch_shapes=[pltpu.VMEM((KV_PLANES, lp, HEAD_DIM), jnp.bfloat16)]),
        out_shape=out_shape, compiler_params=params, name="nsa_attention_paged",
    )(page_table, _pad_rows(q, tp), _pad_rows(tail, tp), kc, vc, new_rows, win, cover, expand,
      *([pool] * PAGES_PER_STEP))
    return out[:, :t_len]


def _compress_mlp(x, w1_ref, pos_ref, w2_ref, b2_ref, s):
    n_sub = x.shape[0]
    half = CMP_STRIDE * HEAD_DIM
    h_lo = jnp.dot(x, w1_ref[s, :half].astype(jnp.bfloat16), preferred_element_type=jnp.float32)
    h_hi = jnp.dot(x, w1_ref[s, half:].astype(jnp.bfloat16), preferred_element_type=jnp.float32)
    h = h_lo + pltpu.roll(h_hi, n_sub - 1, 0) + pos_ref[s]
    return jnp.dot(jax.nn.gelu(h).astype(jnp.bfloat16), w2_ref[s].astype(jnp.bfloat16),
                   preferred_element_type=jnp.float32) + b2_ref[s]


def _compress_kernel(x_ref, w1_ref, pos_ref, w2_ref, b2_ref, o_ref):
    n_sub = o_ref.shape[3]
    for c in range(KV_PLANES):
        s, g = divmod(c, NSA_KV)
        x = jnp.concatenate([x_ref[0, pl.ds(r * KV_PLANES + c, n_sub, stride=CMP_STRIDE * KV_PLANES), :]
                             .astype(jnp.bfloat16) for r in range(CMP_STRIDE)], axis=1)
        o_ref[0, s, g] = _compress_mlp(x, w1_ref, pos_ref, w2_ref, b2_ref, s)


def _compress_paged_kernel(pt_ref, w1_ref, pos_ref, w2_ref, b2_ref, *rest, page_rows):
    page_refs, (o_ref, x_scr) = rest[:PAGES_PER_STEP], rest[PAGES_PER_STEP:]
    j = pl.program_id(1)
    planes = x_scr.shape[0]
    per_page = page_rows // CMP_STRIDE
    for k, page in enumerate(page_refs):
        row0 = pl.multiple_of((j * len(page_refs) + k) * per_page, per_page)
        for c in range(planes):
            for s in range(CMP_STRIDE):
                x_scr[c, pl.ds(row0, per_page), s * HEAD_DIM:(s + 1) * HEAD_DIM] = (
                    page[0, 0, pl.ds(s * planes + c, per_page, stride=CMP_STRIDE * planes), :])

    @pl.when(j == pl.num_programs(1) - 1)
    def _():
        for c in range(planes):
            s, g = divmod(c, NSA_KV)
            o_ref[0, s, g] = _compress_mlp(x_scr[c].astype(jnp.bfloat16), w1_ref, pos_ref, w2_ref, b2_ref, s)


def _compress(rows, w1, pos_emb, w2, b2, paged=None):
    w1 = w1.reshape(2, CMP_LEN * HEAD_DIM, HEAD_DIM)
    pos_bias = jnp.einsum('kld,klde->ke', pos_emb, w1.reshape(2, CMP_LEN, HEAD_DIM, HEAD_DIM))[:, None, :]
    params = dict(vmem_limit_bytes=VMEM_LIMIT_BYTES)
    if paged is None:
        b = rows.shape[0]
        seq = rows.shape[1] // KV_PLANES
        assert seq % CMP_STRIDE == 0
        n_sub = seq // CMP_STRIDE
        const = lambda *shape: pl.BlockSpec(shape, lambda i: (0,) * len(shape))
        return pl.pallas_call(
            _compress_kernel,
            grid=(b,),
            in_specs=[pl.BlockSpec((1, seq * KV_PLANES, HEAD_DIM), lambda i: (i, 0, 0)),
                      const(2, CMP_LEN * HEAD_DIM, HEAD_DIM), const(2, 1, HEAD_DIM),
                      const(2, HEAD_DIM, HEAD_DIM), const(2, 1, HEAD_DIM)],
            out_specs=pl.BlockSpec((1, 2, NSA_KV, n_sub, HEAD_DIM), lambda i: (i, 0, 0, 0, 0)),
            out_shape=jax.ShapeDtypeStruct((b, 2, NSA_KV, n_sub, HEAD_DIM), jnp.float32),
            compiler_params=pltpu.CompilerParams(dimension_semantics=("arbitrary",), **params),
            name="nsa_compress",
        )(rows, w1, pos_bias, w2, b2[:, None, :])

    pool, layer, page_table = paged
    b, n_pages = page_table.shape
    planes = KV_PLANES
    page_rows = pool.shape[2] // planes
    n_sub = n_pages * page_rows // CMP_STRIDE
    const = lambda *shape: pl.BlockSpec(shape, lambda i, j, pt: (0,) * len(shape))
    return pl.pallas_call(
        functools.partial(_compress_paged_kernel, page_rows=page_rows),
        grid_spec=pltpu.PrefetchScalarGridSpec(
            num_scalar_prefetch=1,
            grid=(b, n_pages // PAGES_PER_STEP),
            in_specs=[const(2, CMP_LEN * HEAD_DIM, HEAD_DIM), const(2, 1, HEAD_DIM),
                      const(2, HEAD_DIM, HEAD_DIM), const(2, 1, HEAD_DIM)]
                     + _page_specs(pool, layer, n_pages),
            out_specs=pl.BlockSpec((1, 2, NSA_KV, n_sub, HEAD_DIM), lambda i, j, pt: (i, 0, 0, 0, 0)),
            scratch_shapes=[pltpu.VMEM((planes, n_sub, CMP_STRIDE * HEAD_DIM), jnp.float32)]),
        out_shape=jax.ShapeDtypeStruct((b, 2, NSA_KV, n_sub, HEAD_DIM), jnp.float32),
        compiler_params=pltpu.CompilerParams(dimension_semantics=("arbitrary",) * 2, **params),
        name="nsa_compress_paged",
    )(page_table, w1, pos_bias, w2, b2[:, None, :], *([pool] * PAGES_PER_STEP))


INT_MIN = -2 ** 31
IDX_W_SCALE = (IDX_HEADS * IDX_DIM) ** -0.5


def _count(pred):
    return jnp.sum(jnp.where(pred, 1.0, 0.0), axis=-1, keepdims=True)


def _dsa_body(q_ref, kv, iq_ref, tail_ref, ik, o_ref, *, qpos0, k_top):
    tq = q_ref.shape[1]
    lp = ik.shape[1]
    rows = slice(0, lp)
    qpos = qpos0 + lax.broadcasted_iota(jnp.int32, (tq, 1), 0)
    kpos = lax.broadcasted_iota(jnp.int32, (1, lp), 1)
    causal = kpos <= qpos

    iq = iq_ref[0].astype(jnp.bfloat16)
    iw = tail_ref[0][:, TAIL_IW[0]:TAIL_IW[1]] * IDX_W_SCALE
    score = jnp.zeros((tq, lp), jnp.float32)
    for h in range(IDX_HEADS):
        logits = jnp.dot(iq[:, h * IDX_DIM:(h + 1) * IDX_DIM], ik, preferred_element_type=jnp.float32)
        score = score + iw[:, h:h + 1] * jnp.maximum(logits, 0.0)
    score = jnp.where(causal, score, NEG)
    score = jnp.where(score == 0.0, 0.0, score)
    bits = lax.bitcast_convert_type(score, jnp.int32)
    key = bits ^ (jnp.right_shift(bits, 31) & 0x7FFFFFFF)

    k_f = float(k_top)
    enough = lambda cand: _count(key >= cand) >= k_f
    thr = jnp.where(enough(0), 0, INT_MIN).astype(jnp.int32)
    top_bit = thr | (1 << 30)
    thr = jnp.where(enough(top_bit), top_bit, thr)

    def refine(i, thr):
        low = 28 - 2 * i
        c1, c2, c3 = (thr | jnp.left_shift(jnp.int32(n), low) for n in (1, 2, 3))
        return jnp.where(enough(c3), c3, jnp.where(enough(c2), c2, jnp.where(enough(c1), c1, thr)))

    thr = lax.fori_loop(0, 15, refine, thr)
    above = key > thr
    tied = key == thr
    need = k_f - _count(above)

    few = lambda cand: _count(tied & (kpos < cand)) < need
    surplus = jnp.max(_count(tied) - need) > 0.0
    bound = jnp.where(surplus, jnp.where(few(1 << 14), 1 << 14, 0), (1 << 15) - 1).astype(jnp.int32)

    def widen(i, c):
        low = 12 - 2 * i
        c1, c2, c3 = (c | jnp.left_shift(jnp.int32(n), low) for n in (1, 2, 3))
        return jnp.where(few(c3), c3, jnp.where(few(c2), c2, jnp.where(few(c1), c1, c)))

    bound = lax.fori_loop(0, jnp.where(surplus, 7, 0), widen, bound)
    mask = causal & (above | (tied & (kpos <= bound)))

    q = q_ref[0]
    for g in range(DSA_KV):
        o = _attend(_group_queries(q, g), kv(0, g, rows), kv(1, g, rows), mask)
        for h in range(HEADS_PER_GROUP):
            head = g * HEADS_PER_GROUP + h
            o_ref[0, :, head * HEAD_DIM:(head + 1) * HEAD_DIM] = o[h * tq:(h + 1) * tq]


def _dsa_kernel(q_ref, kv_ref, iq_ref, tail_ref, ik_ref, o_ref, kv_scr, *, pos0, k_top):
    @pl.when(pl.program_id(1) == 0)
    def _():
        _unpack_planes(kv_ref, kv_scr)

    tq = q_ref.shape[1]
    qpos0 = pos0 + pl.program_id(1) * tq
    _for_key_range(qpos0, tq, kv_scr.shape[1], lambda n: _dsa_body(
        q_ref, _kv_planes(kv_scr, DSA_KV), iq_ref, tail_ref, ik_ref[0, :, :n], o_ref, qpos0=qpos0, k_top=k_top))


def _dsa_paged_kernel(pt_ref, q_ref, iq_ref, tail_ref, new_kv_ref, new_ik_ref, *rest, pos0, k_top, page_rows):
    n = PAGES_PER_STEP
    kv_pages, ik_pages, (o_ref, kv_scr, ik_scr) = rest[:n], rest[n:2 * n], rest[2 * n:]
    _load_kv_pages(kv_pages, new_kv_ref, kv_scr, page_rows, pos0)
    j = pl.program_id(1)
    for k, page in enumerate(ik_pages):
        row0 = pl.multiple_of((j * n + k) * page_rows, page_rows)
        ik_scr[:, pl.ds(row0, page_rows)] = page[0, 0].astype(jnp.bfloat16)

    @pl.when(j == 0)
    def _():
        ik_scr[:, pos0:] = new_ik_ref[0].astype(jnp.bfloat16)

    @pl.when(j == pl.num_programs(1) - 1)
    def _():
        _dsa_body(q_ref, _kv_planes(kv_scr, DSA_KV), iq_ref, tail_ref, ik_scr[...], o_ref, qpos0=pos0, k_top=k_top)


def _dsa_attention(q, kv_rows, idx_k, idx_q, tail, pos0, paged=None):
    b, t_len, _ = q.shape
    seq = idx_k.shape[1] + (pos0 if paged else 0)
    assert seq < 2 ** 15
    tq = Q_TILE if t_len % Q_TILE == 0 else _round_up(t_len, SUBLANES)
    tp = _round_up(t_len, tq)
    lp = _round_up(seq, LANES)
    static = dict(pos0=pos0, k_top=min(TOPK_MAX, seq // 4))
    width = DSA_HEADS * HEAD_DIM
    out_shape = jax.ShapeDtypeStruct((b, tp, width), jnp.float32)
    params = pltpu.CompilerParams(dimension_semantics=("arbitrary", "arbitrary"),
                                  vmem_limit_bytes=VMEM_LIMIT_BYTES)
    if paged is None:
        assert seq == lp and tp == t_len
        return pl.pallas_call(
            functools.partial(_dsa_kernel, **static),
            grid=(b, tp // tq),
            in_specs=[pl.BlockSpec((1, tq, width), lambda i, j: (i, j, 0)),
                      pl.BlockSpec((1, lp * KV_PLANES, HEAD_DIM), lambda i, j: (i, 0, 0)),
                      pl.BlockSpec((1, tq, IDX_HEADS * IDX_DIM), lambda i, j: (i, j, 0)),
                      pl.BlockSpec((1, tq, LANES), lambda i, j: (i, j, 0)),
                      pl.BlockSpec((1, IDX_DIM, lp), lambda i, j: (i, 0, 0))],
            out_specs=pl.BlockSpec((1, tq, width), lambda i, j: (i, j, 0)),
            out_shape=out_shape,
            scratch_shapes=[pltpu.VMEM((KV_PLANES, lp, HEAD_DIM), jnp.bfloat16)],
            compiler_params=params, name="dsa_attention",
        )(q, kv_rows, idx_q, tail, jnp.swapaxes(idx_k, 1, 2).astype(jnp.bfloat16))

    kv_pool, idx_pool, layer, page_table = paged
    assert tp == tq
    page_rows = idx_pool.shape[3]
    n_pages = page_table.shape[1]
    out = pl.pallas_call(
        functools.partial(_dsa_paged_kernel, page_rows=page_rows, **static),
        grid_spec=pltpu.PrefetchScalarGridSpec(
            num_scalar_prefetch=1,
            grid=(b, n_pages // PAGES_PER_STEP),
            in_specs=[pl.BlockSpec((1, tq, width), lambda i, j, pt: (i, 0, 0)),
                      pl.BlockSpec((1, tq, IDX_HEADS * IDX_DIM), lambda i, j, pt: (i, 0, 0)),
                      pl.BlockSpec((1, tq, LANES), lambda i, j, pt: (i, 0, 0)),
                      pl.BlockSpec((1, (lp - pos0) * KV_PLANES, HEAD_DIM), lambda i, j, pt: (i, 0, 0)),
                      pl.BlockSpec((1, IDX_DIM, lp - pos0), lambda i, j, pt: (i, 0, 0))]
                     + _page_specs(kv_pool, layer, n_pages) + _page_specs(idx_pool, layer, n_pages),
            out_specs=pl.BlockSpec((1, tq, width), lambda i, j, pt: (i, 0, 0)),
            scratch_shapes=[pltpu.VMEM((KV_PLANES, lp, HEAD_DIM), jnp.bfloat16),
                            pltpu.VMEM((IDX_DIM, lp), jnp.bfloat16)]),
        out_shape=out_shape, compiler_params=params, name="dsa_attention_paged",
    )(page_table, _pad_rows(q, tp), _pad_rows(idx_q, tp), _pad_rows(tail, tp),
      _pad_rows(kv_rows, (lp - pos0) * KV_PLANES), jnp.swapaxes(_pad_rows(idx_k, lp - pos0), 1, 2),
      *([kv_pool] * PAGES_PER_STEP), *([idx_pool] * PAGES_PER_STEP))
    return out[:, :t_len]


def _rms_norm(x, g):
    xf = x.astype(jnp.float32)
    y = xf * lax.rsqrt(jnp.mean(xf * xf, axis=-1, keepdims=True) + NORM_EPS)
    return (y * g.astype(jnp.float32)).astype(x.dtype)


def _rope_partial(x, pos):
    rot = x.shape[-1] // ROT_DIV
    half = rot // 2
    freqs = ROPE_THETA ** (-jnp.arange(half, dtype=jnp.float32) / half)
    ang = pos.astype(jnp.float32)[:, None] * freqs[None, :]
    cos, sin = jnp.cos(ang)[:, None, :], jnp.sin(ang)[:, None, :]
    xr = x[..., :rot].astype(jnp.float32)
    x1, x2 = xr[..., :half], xr[..., half:]
    xr = jnp.concatenate([x1 * cos - x2 * sin, x1 * sin + x2 * cos], axis=-1)
    return jnp.concatenate([xr.astype(x.dtype), x[..., rot:]], axis=-1)


def _nsa_mixer(q, tail, cmp_rows, slc_rows, win_kv, win_pos0, pos0, p, paged=None):
    if paged is None:
        cmp = _compress(cmp_rows, p['cmp_w1'], p['cmp_pos'], p['cmp_w2'], p['cmp_b2'])
    else:
        cmp = _compress(None, p['cmp_w1'], p['cmp_pos'], p['cmp_w2'], p['cmp_b2'], paged=(paged[0],) + paged[2:])
    n_sub = cmp.shape[3]
    c_pos = jnp.arange(n_sub, dtype=jnp.int32) * CMP_STRIDE + (CMP_LEN - 1)
    kc = jnp.swapaxes(_rope_partial(_rms_norm(jnp.swapaxes(cmp[:, 0], 1, 2), p['a_k_norm'][0]), c_pos), 1, 2)
    return _nsa_attention(q, tail, kc, cmp[:, 1], slc_rows, win_kv, pos0, win_pos0, n_sub - 1,
                          paged=None if paged is None else paged[1:])


def _rwkv7_mixer(proj, prev, state0, p, b, T):
    *vectors, gate, bonus = _rwkv_prep(proj, prev, p, T)
    heads = lambda t: t.reshape(b, T, RWKV_HEADS, RWKV_HEAD)
    y, s_t = _rwkv_scan(state0.astype(jnp.float32), *[heads(t) for t in vectors])
    return _rwkv_post(y.reshape(b * T, RWKV_DIM), gate, bonus, p), s_t.astype(state0.dtype)


def _hybrid_layer(x, p, big, layer, pos0, past, ropes):
    b, T, d = x.shape
    xn = _rms_norm(x, p['norm_attn'])
    if past is None:
        proj = _proj(xn, big['w_in'], layer)
        c_prev = None
    else:
        prev = past['shift'][:, None].astype(xn.dtype)
        full = _proj(jnp.concatenate([prev, xn], axis=1), big['w_in'], layer)
        proj = full[:, 1:]
        r0, l0 = COL['rkv'][0], COL['lora'][0]
        c_prev = tuple(full[:, :-1, lo:lo + width].reshape(b * T, width) for lo, width in
                       ((r0, RWKV_DIM), (r0 + RWKV_DIM, RWKV_DIM), (r0 + 2 * RWKV_DIM, RWKV_DIM), (l0, LORA_WIDTH)))

    proj2 = proj.reshape(b * T, PROJ_COLS)
    gains = jnp.stack([p['a_q_norm'], p['a_k_norm'][1], p['a_k_norm'][2], p['b_q_norm'], p['b_k_norm'],
                       jnp.pad(p['idx_k_norm'], (0, LANES - IDX_DIM)),
                       jnp.zeros((LANES,), jnp.float32), jnp.zeros((LANES,), jnp.float32)])
    q_a, q_d, idx_q, cmp_rows, slc_rows, win_rows, dsa_rows, idx_rows = _post_proj(proj2, *ropes, gains)
    per_seq = lambda t: t.reshape((b, -1) + t.shape[1:])
    q_a, q_d, idx_q, idx_rows = per_seq(q_a), per_seq(q_d), per_seq(idx_q), per_seq(idx_rows)
    cmp_rows, slc_rows, win_rows, dsa_rows = per_seq(cmp_rows), per_seq(slc_rows), per_seq(win_rows), per_seq(dsa_rows)
    t0 = COL['tail'][0]
    tail = proj[..., t0:t0 + LANES]
    as_cache = lambda t: t.reshape(b, T, 2, NSA_KV, HEAD_DIM)

    if past is None:
        n_keep = min(WINDOW, T)
        win_state = as_cache(win_rows)[:, -n_keep:]
        rwkv0 = jnp.zeros((b, RWKV_HEADS, RWKV_HEAD, RWKV_HEAD), x.dtype)
        o_a = _nsa_mixer(q_a, tail, cmp_rows, slc_rows, win_rows, pos0, pos0, p)
        o_b = _dsa_attention(q_d, dsa_rows, idx_rows, idx_q, tail, pos0)
    else:
        assert (pos0 + T) // CMP_STRIDE * CMP_STRIDE <= pos0
        win_all = jnp.concatenate([past['nsa_win'], as_cache(win_rows)], axis=1)
        n_keep = past['nsa_win'].shape[1]
        win_state = win_all[:, -n_keep:]
        rwkv0 = past['rwkv']
        page_table = past['page_table']
        o_a = _nsa_mixer(q_a, tail, None, slc_rows, win_all, pos0 - n_keep, pos0, p,
                         paged=(past['cmp_pool'], past['slc_pool'], layer, page_table))
        o_b = _dsa_attention(q_d, dsa_rows, idx_rows, idx_q, tail, pos0,
                             paged=(past['dsa_pool'], past['idx_pool'], layer, page_table))
    o_c, rwkv_new = _rwkv7_mixer(proj2, c_prev, rwkv0, p, b, T)

    mixed = _merge(o_a.reshape(b * T, -1), o_b.reshape(b * T, -1), o_c, proj2, big, layer)
    x = _proj(mixed.reshape(b, T, d), big['w_o'], layer, residual=x)
    x = _ffn(x, p['norm_ffn'], big['w_ffn_in'], big['w_ffn_out'], layer)
    new_state = {'nsa_cmp': as_cache(cmp_rows), 'nsa_slc': as_cache(slc_rows), 'dsa_kv': as_cache(dsa_rows),
                 'dsa_idx': idx_rows, 'nsa_win': win_state, 'rwkv': rwkv_new, 'shift': xn[:, -1]}
    return x, new_state


_PARAM_NAMES = ('norm_attn', 'w_in', 'a_q_norm', 'a_k_norm', 'cmp_w1', 'cmp_pos', 'cmp_w2', 'cmp_b2',
                'b_q_norm', 'b_k_norm', 'idx_k_norm', 'rwkv_mu', 'rwkv_w0', 'rwkv_w2', 'rwkv_a0',
                'rwkv_a2', 'rwkv_g2', 'rwkv_k_k', 'rwkv_k_a', 'rwkv_r_k', 'rwkv_ln_w', 'rwkv_ln_b',
                'p_a', 'p_b', 'p_c', 'w_o', 'norm_ffn', 'w_ffn_in', 'w_ffn_out')


def kernel(x_prompt, x_sample, cache_nsa_cmp, cache_nsa_slc, cache_dsa_kv, cache_dsa_idx, state_nsa_win, state_rwkv, state_shift, page_table, norm_attn, w_in, a_q_norm, a_k_norm, cmp_w1, cmp_pos, cmp_w2, cmp_b2, b_q_norm, b_k_norm, idx_k_norm, rwkv_mu, rwkv_w0, rwkv_w2, rwkv_a0, rwkv_a2, rwkv_g2, rwkv_k_k, rwkv_k_a, rwkv_r_k, rwkv_ln_w, rwkv_ln_b, p_a, p_b, p_c, w_o, norm_ffn, w_ffn_in, w_ffn_out):
    weights = dict(zip(_PARAM_NAMES, (norm_attn, w_in, a_q_norm, a_k_norm, cmp_w1, cmp_pos, cmp_w2, cmp_b2,
                                      b_q_norm, b_k_norm, idx_k_norm, rwkv_mu, rwkv_w0, rwkv_w2, rwkv_a0,
                                      rwkv_a2, rwkv_g2, rwkv_k_k, rwkv_k_a, rwkv_r_k, rwkv_ln_w, rwkv_ln_b,
                                      p_a, p_b, p_c, w_o, norm_ffn, w_ffn_in, w_ffn_out)))
    depth = w_in.shape[0]
    past_len = page_table.shape[1] * cache_nsa_cmp.shape[2]
    yp, ys = x_prompt, x_sample
    st_p, st_s = [], []
    kv_pool = lambda c: c.reshape(c.shape[:2] + (-1, HEAD_DIM))
    big_names = ('w_in', 'p_a', 'p_b', 'p_c', 'w_o', 'w_ffn_in', 'w_ffn_out')
    big = {name: weights[name].astype(jnp.bfloat16) for name in big_names[1:]}
    big['w_in'] = _regroup_w_in(w_in)

    def ropes(pos0, bsz, t_len):
        pos = jnp.tile(pos0 + jnp.arange(t_len, dtype=jnp.int32), bsz)
        return _rope_tables(pos, LANES, HEAD_DIM), _rope_tables(pos, LANES, IDX_DIM)

    ropes_p = ropes(0, *x_prompt.shape[:2])
    ropes_s = ropes(past_len, *x_sample.shape[:2])
    for l in range(depth):
        p = {name: w[l] for name, w in weights.items() if name not in big_names}
        past = {'cmp_pool': kv_pool(cache_nsa_cmp), 'slc_pool': kv_pool(cache_nsa_slc),
                'dsa_pool': kv_pool(cache_dsa_kv), 'idx_pool': jnp.swapaxes(cache_dsa_idx, 2, 3),
                'page_table': page_table,
                'nsa_win': state_nsa_win[l], 'rwkv': state_rwkv[l], 'shift': state_shift[l]}
        yp, sp = _hybrid_layer(yp, p, big, l, 0, None, ropes_p)
        ys, ss = _hybrid_layer(ys, p, big, l, past_len, past, ropes_s)
        st_p.append(sp)
        st_s.append(ss)
    stack = lambda sts, name: jnp.stack([s[name] for s in sts])
    return (yp, ys,
            stack(st_p, 'nsa_cmp'), stack(st_s, 'nsa_cmp'),
            stack(st_p, 'nsa_slc'), stack(st_s, 'nsa_slc'),
            stack(st_p, 'dsa_kv'), stack(st_s, 'dsa_kv'),
            stack(st_p, 'dsa_idx'), stack(st_s, 'dsa_idx'),
            stack(st_p, 'nsa_win'), stack(st_s, 'nsa_win'),
            stack(st_p, 'rwkv'), stack(st_s, 'rwkv'),
            stack(st_p, 'shift'), stack(st_s, 'shift'))
```

```python
import functools
import math

import jax
import jax.numpy as jnp
from jax import lax
from jax.experimental import pallas as pl
from jax.experimental.pallas import tpu as pltpu

D_MODEL = 2048
HEAD_DIM = 128
ROT_DIV = 4
ROPE_THETA = 500000.0
NORM_EPS = 1e-6

NSA_HEADS = 8
NSA_KV = 2
CMP_STRIDE = 16
CMP_LEN = 2 * CMP_STRIDE
SLC_BLOCK = 64
N_SEL = 8
N_LOCAL = 2
WINDOW = 512

DSA_HEADS = 8
DSA_KV = 2
IDX_HEADS = 8
IDX_DIM = 64
TOPK_MAX = 256

RWKV_HEADS = 16
RWKV_HEAD = 64
RWKV_DIM = RWKV_HEADS * RWKV_HEAD
DECAY_LORA = 64
AAA_LORA = 64
GATE_LORA = 160
RWKV_GN_EPS = 64e-5

NEG = -1e30
FORCE = 1e9
LOWEST = -3e38

IN_SPLITS = (
    NSA_HEADS * HEAD_DIM,
    6 * NSA_KV * HEAD_DIM,
    3 * NSA_HEADS,
    DSA_HEADS * HEAD_DIM,
    2 * DSA_KV * HEAD_DIM,
    IDX_HEADS * IDX_DIM,
    IDX_DIM,
    IDX_HEADS,
    3 * RWKV_DIM + DECAY_LORA + AAA_LORA + GATE_LORA,
    3 * D_MODEL,
)

LANES = 128
SUBLANES = 8
VMEM_LIMIT_BYTES = 56 * 1024 * 1024

HEADS_PER_GROUP = NSA_HEADS // NSA_KV
ATTN_SCALE = HEAD_DIM ** -0.5
LOG2_E = math.log2(math.e)
Q_TILE = 128

_NT = (((1,), (1,)), ((), ()))


def _round_up(n, m):
    return -(-n // m) * m


def _mm_kernel(x_ref, w_ref, *rest):
    r_ref = rest[0] if len(rest) == 3 else None
    o_ref, xb_ref = rest[-2:]

    @pl.when(pl.program_id(1) == 0)
    def _():
        xb_ref[...] = x_ref[...].astype(jnp.bfloat16)

    acc = jnp.dot(xb_ref[...], w_ref[0], preferred_element_type=jnp.float32)
    o_ref[...] = acc if r_ref is None else r_ref[...] + acc


def _pick_tile(n, prefs):
    for t in prefs:
        if n % t == 0:
            return t
    return n


def _mm(x, w, layer, residual=None):
    m, k = x.shape
    n = w.shape[2]
    tm = _pick_tile(m, (1024, 512, 256, 128))
    tn = _pick_tile(n, (512, 384, 256, 128))
    tile = pl.BlockSpec((tm, tn), lambda i, j: (i, j))
    return pl.pallas_call(
        _mm_kernel,
        grid=(m // tm, n // tn),
        in_specs=[pl.BlockSpec((tm, k), lambda i, j: (i, 0)),
                  pl.BlockSpec((1, k, tn), lambda i, j: (layer, 0, j))] + ([] if residual is None else [tile]),
        out_specs=tile,
        out_shape=jax.ShapeDtypeStruct((m, n), jnp.float32),
        scratch_shapes=[pltpu.VMEM((tm, k), jnp.bfloat16)],
        compiler_params=pltpu.CompilerParams(
            dimension_semantics=("arbitrary", "arbitrary"),
            vmem_limit_bytes=VMEM_LIMIT_BYTES),
        name="dense_proj",
    )(x, w, *([] if residual is None else [residual]))


def _proj(x, w, layer, residual=None):
    lead = x.shape[:-1]
    res = None if residual is None else residual.reshape(-1, residual.shape[-1])
    return _mm(x.reshape(-1, x.shape[-1]), w, layer, res).reshape(lead + (w.shape[2],))


FFN_TILE = 512


def _ffn_kernel(x_ref, gain_ref, wg_ref, wu_ref, wo_ref, o_ref, hb_ref, acc_ref):
    f = pl.program_id(1)

    @pl.when(f == 0)
    def _():
        x = x_ref[...]
        y = x * lax.rsqrt(jnp.mean(x * x, axis=-1, keepdims=True) + NORM_EPS)
        hb_ref[...] = (y * gain_ref[...]).astype(jnp.bfloat16)
        acc_ref[...] = jnp.zeros_like(acc_ref)

    hb = hb_ref[...]
    gate = jnp.dot(hb, wg_ref[0], preferred_element_type=jnp.float32)
    up = jnp.dot(hb, wu_ref[0], preferred_element_type=jnp.float32)
    act = (jax.nn.silu(gate) * up).astype(jnp.bfloat16)
    acc_ref[...] += jnp.dot(act, wo_ref[0], preferred_element_type=jnp.float32)

    @pl.when(f == pl.num_programs(1) - 1)
    def _():
        o_ref[...] = x_ref[...] + acc_ref[...]


def _ffn(x, gain, w_in, w_out, layer):
    lead = x.shape[:-1]
    x2 = x.reshape(-1, x.shape[-1])
    m, d = x2.shape
    n_f = w_out.shape[1] // FFN_TILE
    tm = _pick_tile(m, (512, 256, 128))
    out = pl.pallas_call(
        _ffn_kernel,
        grid=(m // tm, n_f),
        in_specs=[pl.BlockSpec((tm, d), lambda i, f: (i, 0)),
                  pl.BlockSpec((1, d), lambda i, f: (0, 0)),
                  pl.BlockSpec((1, d, FFN_TILE), lambda i, f: (layer, 0, f)),
                  pl.BlockSpec((1, d, FFN_TILE), lambda i, f: (layer, 0, n_f + f)),
                  pl.BlockSpec((1, FFN_TILE, d), lambda i, f: (layer, f, 0))],
        out_specs=pl.BlockSpec((tm, d), lambda i, f: (i, 0)),
        out_shape=jax.ShapeDtypeStruct((m, d), jnp.float32),
        scratch_shapes=[pltpu.VMEM((tm, d), jnp.bfloat16), pltpu.VMEM((tm, d), jnp.float32)],
        compiler_params=pltpu.CompilerParams(
            dimension_semantics=("arbitrary", "arbitrary"),
            vmem_limit_bytes=VMEM_LIMIT_BYTES),
        name="ffn_block",
    )(x2, gain[None, :], w_in, w_in, w_out)
    return out.reshape(lead + (d,))


RWKV_HEADS_PER_BLOCK = LANES // 2
RWKV_V_HALF = RWKV_HEAD // 2


def _rwkv_scan_kernel(r_ref, w_ref, k_ref, a_ref, b_ref, v_ref, s0_ref, y_ref, s_out_ref, s_ref, rows_ref):
    c = pl.program_id(1)

    @pl.when(c == 0)
    def _():
        s_ref[...] = s0_ref[0]

    steps = r_ref.shape[1]
    low_half = lax.broadcasted_iota(jnp.int32, (RWKV_V_HALF, LANES), 1) < RWKV_HEADS_PER_BLOCK

    def step(t, carry):
        for n, ref in enumerate((r_ref, w_ref, k_ref, a_ref, b_ref)):
            tile = ref[0, t]
            swapped = pltpu.roll(tile, RWKV_HEADS_PER_BLOCK, 1)
            rows_ref[n, :RWKV_V_HALF] = jnp.where(low_half, tile, swapped)
            rows_ref[n, RWKV_V_HALF:] = jnp.where(low_half, swapped, tile)
        parts = [jnp.zeros((RWKV_V_HALF, LANES), jnp.float32) for _ in range(4)]
        for k in range(RWKV_HEAD):
            parts[k % 4] = parts[k % 4] + s_ref[k] * rows_ref[3, k:k + 1, :]
        sa = (parts[0] + parts[1]) + (parts[2] + parts[3])
        vt = v_ref[0, t]
        ys = [jnp.zeros((RWKV_V_HALF, LANES), jnp.float32) for _ in range(4)]
        for k in range(RWKV_HEAD):
            s_new = (s_ref[k] * rows_ref[1, k:k + 1, :] + sa * rows_ref[4, k:k + 1, :]
                     + vt * rows_ref[2, k:k + 1, :])
            s_ref[k] = s_new
            ys[k % 4] = ys[k % 4] + s_new * rows_ref[0, k:k + 1, :]
        y_ref[0, t] = (ys[0] + ys[1]) + (ys[2] + ys[3])
        return carry

    lax.fori_loop(0, steps, step, 0, unroll=4)

    @pl.when(c == pl.num_programs(1) - 1)
    def _():
        s_out_ref[0] = s_ref[...]


def _rwkv_scan(state0, r, w, k, v, a, b):
    bsz, t_len, n_h, n = r.shape
    heads = bsz * n_h
    hb = RWKV_HEADS_PER_BLOCK
    nblk = heads // hb
    tc = _pick_tile(t_len, (128, 64, 32, 16, 8, 4))

    def pack(x):
        x = jnp.transpose(x.reshape(bsz, t_len, n_h, 2, RWKV_V_HALF), (1, 4, 3, 0, 2))
        x = x.reshape(t_len, RWKV_V_HALF, 2, nblk, hb)
        return jnp.transpose(x, (3, 0, 1, 2, 4)).reshape(nblk, t_len, RWKV_V_HALF, LANES)

    s0 = state0.reshape(nblk, hb, 2, RWKV_V_HALF, n)
    s0 = jnp.transpose(s0, (0, 4, 3, 2, 1)).reshape(nblk, n, RWKV_V_HALF, LANES)

    val_spec = pl.BlockSpec((1, tc, RWKV_V_HALF, LANES), lambda i, c: (i, c, 0, 0))
    st_spec = pl.BlockSpec((1, n, RWKV_V_HALF, LANES), lambda i, c: (i, 0, 0, 0))
    y, s_out = pl.pallas_call(
        _rwkv_scan_kernel,
        grid=(nblk, t_len // tc),
        in_specs=[val_spec] * 6 + [st_spec],
        out_specs=[val_spec, st_spec],
        out_shape=[jax.ShapeDtypeStruct((nblk, t_len, RWKV_V_HALF, LANES), jnp.float32),
                   jax.ShapeDtypeStruct((nblk, n, RWKV_V_HALF, LANES), jnp.float32)],
        scratch_shapes=[pltpu.VMEM((n, RWKV_V_HALF, LANES), jnp.float32),
                        pltpu.VMEM((5, n, LANES), jnp.float32)],
        compiler_params=pltpu.CompilerParams(
            dimension_semantics=("arbitrary", "arbitrary"),
            vmem_limit_bytes=VMEM_LIMIT_BYTES),
        name="rwkv7_scan",
    )(pack(r), pack(w), pack(k), pack(a), pack(b), pack(v), s0)

    y = y.reshape(nblk, t_len, RWKV_V_HALF, 2, hb)
    y = jnp.transpose(y, (0, 4, 1, 3, 2)).reshape(bsz, n_h, t_len, n)
    y = jnp.transpose(y, (0, 2, 1, 3))
    s_out = s_out.reshape(nblk, n, RWKV_V_HALF, 2, hb)
    s_out = jnp.transpose(s_out, (0, 4, 3, 2, 1)).reshape(bsz, n_h, n, n)
    return y, s_out


_ORIG = dict(zip(('a_q', 'a_kv', 'a_gate', 'd_q', 'd_kv', 'i_q', 'i_k', 'i_w', 'c', 'merge'),
                 zip([sum(IN_SPLITS[:i]) for i in range(len(IN_SPLITS))], IN_SPLITS)))
LORA_COLS = DECAY_LORA + AAA_LORA + GATE_LORA
LORA_WIDTH = _round_up(LORA_COLS, LANES)
TAIL_PAD = LANES - (IDX_DIM + 3 * NSA_HEADS + IDX_HEADS)
COL = {}
_off = 0
for _name, _width in (('merge', 3 * D_MODEL), ('a_q', NSA_HEADS * HEAD_DIM), ('d_q', DSA_HEADS * HEAD_DIM),
                      ('rkv', 3 * RWKV_DIM), ('a_kv', 6 * NSA_KV * HEAD_DIM), ('d_kv', 2 * DSA_KV * HEAD_DIM),
                      ('i_q', IDX_HEADS * IDX_DIM), ('lora', LORA_WIDTH), ('tail', LANES)):
    COL[_name] = (_off, _width)
    _off += _width
PROJ_COLS = _off
TAIL_IK = (0, IDX_DIM)
TAIL_GATE = (IDX_DIM, IDX_DIM + 3 * NSA_HEADS)
TAIL_IW = (IDX_DIM + 3 * NSA_HEADS, IDX_DIM + 3 * NSA_HEADS + IDX_HEADS)


def _regroup_w_in(w_in):
    cut = lambda name: w_in[..., _ORIG[name][0]:_ORIG[name][0] + _ORIG[name][1]]
    zeros = lambda n: jnp.zeros(w_in.shape[:2] + (n,), w_in.dtype)
    c0 = _ORIG['c'][0]
    parts = [cut('merge'), cut('a_q'), cut('d_q'), w_in[..., c0:c0 + 3 * RWKV_DIM], cut('a_kv'), cut('d_kv'),
             cut('i_q'), w_in[..., c0 + 3 * RWKV_DIM:c0 + 3 * RWKV_DIM + LORA_COLS], zeros(LORA_WIDTH - LORA_COLS),
             cut('i_k'), cut('a_gate'), cut('i_w'), zeros(TAIL_PAD)]
    return jnp.concatenate(parts, axis=-1).astype(jnp.bfloat16)


def _rope_tables(pos, width, head_dim):
    half = head_dim // ROT_DIV // 2
    freqs = ROPE_THETA ** (-jnp.arange(half, dtype=jnp.float32) / half)
    ang = pos.astype(jnp.float32)[:, None] * freqs[None, :]
    cos, sin = jnp.cos(ang), jnp.sin(ang)
    rest = head_dim - 2 * half
    ones, zeros = jnp.ones((pos.shape[0], rest), jnp.float32), jnp.zeros((pos.shape[0], rest), jnp.float32)
    z_half = jnp.zeros_like(sin)
    reps = width // head_dim
    c = jnp.tile(jnp.concatenate([cos, cos, ones], axis=1), (1, reps))
    s_up = jnp.tile(jnp.concatenate([z_half, sin, zeros], axis=1), (1, reps))
    s_down = jnp.tile(jnp.concatenate([-sin, z_half, zeros], axis=1), (1, reps))
    return c, s_up, s_down


def _rotate(y, tables, half):
    c, s_up, s_down = tables
    width = y.shape[-1]
    return y * c + pltpu.roll(y, half, 1) * s_up + pltpu.roll(y, width - half, 1) * s_down


def _head_norm(x, gain):
    return x * lax.rsqrt(jnp.mean(x * x, axis=-1, keepdims=True) + NORM_EPS) * gain


ROPE_HALF = HEAD_DIM // ROT_DIV // 2
IDX_ROPE_HALF = IDX_DIM // ROT_DIV // 2
KV_PLANES = 2 * NSA_KV


def _post_proj_kernel(aq_ref, dq_ref, cmp_ref, slc_ref, win_ref, dkv_ref, iq_ref, tail_ref,
                      c_ref, su_ref, sd_ref, ic_ref, isu_ref, isd_ref, gains_ref,
                      qa_ref, qd_ref, iqo_ref, cmp_o, slc_o, win_o, dsa_o, ik_o):
    tm = aq_ref.shape[0]
    rope = (c_ref[...], su_ref[...], sd_ref[...])
    irope = (ic_ref[...], isu_ref[...], isd_ref[...])
    gain = lambda i: gains_ref[i:i + 1, :]

    def head(ref, h):
        return ref[:, h * HEAD_DIM:(h + 1) * HEAD_DIM]

    for h in range(NSA_HEADS):
        qa_ref[:, h * HEAD_DIM:(h + 1) * HEAD_DIM] = _rotate(
            _head_norm(head(aq_ref, h), gain(0)), rope, ROPE_HALF).astype(qa_ref.dtype)
    for h in range(DSA_HEADS):
        qd_ref[:, h * HEAD_DIM:(h + 1) * HEAD_DIM] = _rotate(
            _head_norm(head(dq_ref, h), gain(3)), rope, ROPE_HALF).astype(qd_ref.dtype)
    for pair in range(IDX_HEADS * IDX_DIM // LANES):
        iqo_ref[:, pair * LANES:(pair + 1) * LANES] = _rotate(
            iq_ref[:, pair * LANES:(pair + 1) * LANES], irope, IDX_ROPE_HALF).astype(iqo_ref.dtype)

    def planes(src, dst, key_gain):
        for c in range(KV_PLANES):
            x = head(src, c)
            if key_gain is not None and c < NSA_KV:
                x = _rotate(_head_norm(x, gain(key_gain)), rope, ROPE_HALF)
            dst[pl.ds(c, tm, stride=KV_PLANES), :] = x

    planes(cmp_ref, cmp_o, None)
    planes(slc_ref, slc_o, 1)
    planes(win_ref, win_o, 2)
    planes(dkv_ref, dsa_o, 4)

    lane = lax.broadcasted_iota(jnp.int32, (tm, LANES), 1)
    ik = jnp.where(lane < IDX_DIM, tail_ref[...], 0.0)
    ik = ik * lax.rsqrt(jnp.sum(ik * ik, axis=-1, keepdims=True) * (1.0 / IDX_DIM) + NORM_EPS) * gain(5)
    ik_o[...] = _rotate(ik, irope, IDX_ROPE_HALF)[:, :IDX_DIM]


def _post_proj(proj, tables, itables, gains):
    m = proj.shape[0]
    tm = _pick_tile(m, (256, 128))
    col = lambda name, width: pl.BlockSpec((tm, width), functools.partial(
        lambda i, blk: (i, blk), blk=COL[name][0] // width))
    kv_w = KV_PLANES * HEAD_DIM
    akv0 = COL['a_kv'][0] // kv_w
    kv_col = lambda n: pl.BlockSpec((tm, kv_w), functools.partial(lambda i, blk: (i, blk), blk=akv0 + n))
    row = lambda width: pl.BlockSpec((tm, width), lambda i: (i, 0))
    planes = pl.BlockSpec((tm * KV_PLANES, HEAD_DIM), lambda i: (i, 0))
    f32 = jnp.float32
    return pl.pallas_call(
        _post_proj_kernel,
        grid=(m // tm,),
        in_specs=[col('a_q', NSA_HEADS * HEAD_DIM), col('d_q', DSA_HEADS * HEAD_DIM),
                  kv_col(0), kv_col(1), kv_col(2), col('d_kv', kv_w), col('i_q', IDX_HEADS * IDX_DIM),
                  col('tail', LANES)] + [row(LANES)] * 6 + [pl.BlockSpec((8, LANES), lambda i: (0, 0))],
        out_specs=[row(NSA_HEADS * HEAD_DIM), row(DSA_HEADS * HEAD_DIM), row(IDX_HEADS * IDX_DIM),
                   planes, planes, planes, planes, row(IDX_DIM)],
        out_shape=[jax.ShapeDtypeStruct((m, NSA_HEADS * HEAD_DIM), jnp.bfloat16),
                   jax.ShapeDtypeStruct((m, DSA_HEADS * HEAD_DIM), jnp.bfloat16),
                   jax.ShapeDtypeStruct((m, IDX_HEADS * IDX_DIM), jnp.bfloat16)]
                  + [jax.ShapeDtypeStruct((m * KV_PLANES, HEAD_DIM), f32)] * 4
                  + [jax.ShapeDtypeStruct((m, IDX_DIM), f32)],
        compiler_params=pltpu.CompilerParams(dimension_semantics=("arbitrary",),
                                             vmem_limit_bytes=VMEM_LIMIT_BYTES),
        name="post_proj",
    )(*([proj] * 8), *tables, *itables, gains)


def _merge_kernel(oa_ref, ob_ref, oc_ref, pa_ref, pb_ref, pc_ref, ga_ref, gb_ref, gc_ref, o_ref, xb_ref):
    @pl.when(pl.program_id(1) == 0)
    def _():
        for n, ref in enumerate((oa_ref, ob_ref, oc_ref)):
            xb_ref[n] = ref[...].astype(jnp.bfloat16)

    acc = None
    for n, (w_ref, g_ref) in enumerate(((pa_ref, ga_ref), (pb_ref, gb_ref), (pc_ref, gc_ref))):
        term = jax.nn.sigmoid(g_ref[...]) * jnp.dot(xb_ref[n], w_ref[0], preferred_element_type=jnp.float32)
        acc = term if acc is None else acc + term
    o_ref[...] = acc


def _merge(o_a, o_b, o_c, proj, big, layer):
    m, k = o_a.shape
    tm = _pick_tile(m, (512, 256, 128))
    tn = 512
    n_blk = D_MODEL // tn
    x_spec = pl.BlockSpec((tm, k), lambda i, j: (i, 0))
    w_spec = pl.BlockSpec((1, k, tn), lambda i, j: (layer, 0, j))
    gate = lambda n: pl.BlockSpec((tm, tn), functools.partial(lambda i, j, n: (i, n * n_blk + j), n=n))
    assert COL['merge'][0] == 0
    return pl.pallas_call(
        _merge_kernel,
        grid=(m // tm, n_blk),
        in_specs=[x_spec] * 3 + [w_spec] * 3 + [gate(0), gate(1), gate(2)],
        out_specs=pl.BlockSpec((tm, tn), lambda i, j: (i, j)),
        out_shape=jax.ShapeDtypeStruct((m, D_MODEL), jnp.float32),
        scratch_shapes=[pltpu.VMEM((3, tm, k), jnp.bfloat16)],
        compiler_params=pltpu.CompilerParams(dimension_semantics=("arbitrary", "arbitrary"),
                                             vmem_limit_bytes=VMEM_LIMIT_BYTES),
        name="mixer_merge",
    )(o_a, o_b, o_c, big['p_a'], big['p_b'], big['p_c'], proj, proj, proj)


def _head_sums(x, ones_ref):
    hi = x.astype(jnp.bfloat16)
    lo = (x - hi.astype(jnp.float32)).astype(jnp.bfloat16)
    ones = ones_ref[...]
    cols = [jnp.dot(hi[:, j:j + LANES], ones, preferred_element_type=jnp.float32)
            + jnp.dot(lo[:, j:j + LANES], ones, preferred_element_type=jnp.float32)
            for j in range(0, x.shape[1], LANES)]
    return jnp.concatenate(cols, axis=1)


def _softplus(z):
    return jnp.maximum(z, 0.0) + jnp.log(1.0 + jnp.exp(-jnp.abs(z)))


def _rwkv_prep_kernel(*refs, carry_rows):
    cur_refs, rest = refs[:4], refs[4:]
    prev_refs, rest = (None, rest) if carry_rows else (rest[:4], rest[4:])
    (mu_ref, mul_ref, w0_ref, a0_ref, kk_ref, ka_ref, rk_ref, w2_ref, a2_ref, g2_ref, ones_ref,
     r_o, w_o, k_o, v_o, a_o, b_o, g_o, bonus_o) = rest[:19]
    last_refs = rest[19:]
    tm = r_o.shape[0]
    first = lax.broadcasted_iota(jnp.int32, (tm, 1), 0) == 0

    @pl.when(pl.program_id(0) == 0)
    def _():
        for ref in last_refs:
            ref[...] = jnp.zeros_like(ref)

    def shifted(n, cur):
        if prev_refs is not None:
            return prev_refs[n][...]
        at_start = (pl.program_id(0) * tm) % carry_rows == 0
        head = jnp.where(at_start, 0.0, last_refs[n][...])
        last_refs[n][...] = cur[tm - 1:tm, :]
        return jnp.where(first, head, pltpu.roll(cur, 1, 0))

    def mixed(n, mu):
        cur = cur_refs[n][...]
        return cur + (shifted(n, cur) - cur) * mu

    r = mixed(0, mu_ref[0:1, :])
    k = mixed(1, mu_ref[1:2, :])
    v = mixed(2, mu_ref[2:3, :])
    lora = mixed(3, mul_ref[...])
    bf = jnp.bfloat16
    w_lo = jnp.tanh(lora[:, :DECAY_LORA]).astype(bf)
    a_lo = lora[:, DECAY_LORA:DECAY_LORA + AAA_LORA].astype(bf)
    g_lo = jax.nn.sigmoid(lora[:, DECAY_LORA + AAA_LORA:LORA_COLS]).astype(bf)
    w = -_softplus(-(w0_ref[...] + jnp.dot(w_lo, w2_ref[...].astype(bf), preferred_element_type=jnp.float32))) - 0.5
    a = jax.nn.sigmoid(a0_ref[...] + jnp.dot(a_lo, a2_ref[...].astype(bf), preferred_element_type=jnp.float32))
    g_o[...] = jnp.dot(g_lo, g2_ref[...].astype(bf), preferred_element_type=jnp.float32)
    kk = k * kk_ref[...]
    kk = kk * lax.rsqrt(_head_sums(kk * kk, ones_ref) + 1e-12)
    kh = k * (1.0 + (a - 1.0) * ka_ref[...])
    r_o[...] = r
    w_o[...] = jnp.exp(-jnp.exp(w))
    k_o[...] = kh
    v_o[...] = v
    a_o[...] = -kk
    b_o[...] = kk * a
    bonus_o[...] = _head_sums(r * kh * rk_ref[...], ones_ref) * v


def _head_ones():
    seg = jnp.arange(LANES) // RWKV_HEAD
    return (seg[:, None] == seg[None, :]).astype(jnp.bfloat16)


def _rwkv_prep(proj, prev, p, seq_len):
    m = proj.shape[0]
    tm = _pick_tile(m, (256, 128))
    if prev is None:
        assert seq_len % tm == 0
    blk = lambda off, width: pl.BlockSpec((tm, width), functools.partial(lambda i, b: (i, b), b=off // width))
    r0 = COL['rkv'][0]
    cur_specs = [blk(r0, RWKV_DIM), blk(r0 + RWKV_DIM, RWKV_DIM), blk(r0 + 2 * RWKV_DIM, RWKV_DIM),
                 blk(COL['lora'][0], LORA_WIDTH)]
    row = lambda width: pl.BlockSpec((tm, width), lambda i: (i, 0))
    const = lambda *shape: pl.BlockSpec(shape, lambda i: (0,) * len(shape))
    mu = p['rwkv_mu']
    mu_rkv = mu[:3 * RWKV_DIM].reshape(3, RWKV_DIM)
    mu_lora = jnp.pad(mu[3 * RWKV_DIM:], (0, LORA_WIDTH - LORA_COLS))[None, :]
    vec = lambda name: p[name].reshape(1, RWKV_DIM)
    small = [mu_rkv, mu_lora, vec('rwkv_w0'), vec('rwkv_a0'), vec('rwkv_k_k'), vec('rwkv_k_a'), vec('rwkv_r_k'),
             p['rwkv_w2'], p['rwkv_a2'], p['rwkv_g2'], _head_ones()]
    small_specs = [const(*s.shape) for s in small]
    out = jax.ShapeDtypeStruct((m, RWKV_DIM), jnp.float32)
    if prev is None:
        ins, specs = [proj] * 4, cur_specs
        scratch = [pltpu.VMEM((1, RWKV_DIM), jnp.float32)] * 3 + [pltpu.VMEM((1, LORA_WIDTH), jnp.float32)]
    else:
        ins = [proj] * 4 + list(prev)
        specs = cur_specs + [row(RWKV_DIM)] * 3 + [row(LORA_WIDTH)]
        scratch = []
    return pl.pallas_call(
        functools.partial(_rwkv_prep_kernel, carry_rows=seq_len if prev is None else None),
        grid=(m // tm,),
        in_specs=specs + small_specs,
        out_specs=[row(RWKV_DIM)] * 8,
        out_shape=[out] * 8,
        scratch_shapes=scratch,
        compiler_params=pltpu.CompilerParams(dimension_semantics=("arbitrary",),
                                             vmem_limit_bytes=VMEM_LIMIT_BYTES),
        name="rwkv_prep",
    )(*ins, *small)


def _rwkv_post_kernel(y_ref, g_ref, bonus_ref, lnw_ref, lnb_ref, ones_ref, o_ref):
    y = y_ref[...]
    mean = _head_sums(y, ones_ref) * (1.0 / RWKV_HEAD)
    dev = y - mean
    var = _head_sums(dev * dev, ones_ref) * (1.0 / RWKV_HEAD)
    out = dev * lax.rsqrt(var + RWKV_GN_EPS) * lnw_ref[...] + lnb_ref[...] + bonus_ref[...]
    o_ref[...] = out * g_ref[...]


def _rwkv_post(y, gate, bonus, p):
    m = y.shape[0]
    tm = _pick_tile(m, (512, 256, 128))
    row = pl.BlockSpec((tm, RWKV_DIM), lambda i: (i, 0))
    vec = pl.BlockSpec((1, RWKV_DIM), lambda i: (0, 0))
    return pl.pallas_call(
        _rwkv_post_kernel,
        grid=(m // tm,),
        in_specs=[row, row, row, vec, vec, pl.BlockSpec((LANES, LANES), lambda i: (0, 0))],
        out_specs=row,
        out_shape=jax.ShapeDtypeStruct((m, RWKV_DIM), jnp.float32),
        compiler_params=pltpu.CompilerParams(dimension_semantics=("arbitrary",),
                                             vmem_limit_bytes=VMEM_LIMIT_BYTES),
        name="rwkv_post",
    )(y, gate, bonus, p['rwkv_ln_w'][None, :], p['rwkv_ln_b'][None, :], _head_ones())


def _group_queries(q, g):
    h0 = g * HEADS_PER_GROUP
    return jnp.concatenate([q[:, (h0 + h) * HEAD_DIM:(h0 + h + 1) * HEAD_DIM] for h in range(HEADS_PER_GROUP)],
                           axis=0).astype(jnp.bfloat16)


def _attend(qg, k, v, mask, want_psum=False):
    tq = mask.shape[0]
    s = lax.dot_general(qg, k, _NT, preferred_element_type=jnp.float32)
    ps = []
    for h in range(HEADS_PER_GROUP):
        sh = jnp.where(mask, s[h * tq:(h + 1) * tq], NEG)
        top = jnp.max(sh, axis=-1, keepdims=True)
        e = jnp.exp2((sh - top) * (ATTN_SCALE * LOG2_E))
        den = jnp.sum(e, axis=-1, keepdims=True)
        ps.append(e * jnp.where(top > 0.5 * NEG, 1.0 / den, 0.0))
    o = jnp.dot(jnp.concatenate(ps, axis=0).astype(jnp.bfloat16), v, preferred_element_type=jnp.float32)
    if want_psum:
        return o, (ps[0] + ps[1]) + (ps[2] + ps[3])
    return o


def _kv_columns(ref, n_kv):
    def get(s, g, rows):
        c0 = (s * n_kv + g) * HEAD_DIM
        return ref[0, rows, c0:c0 + HEAD_DIM]
    return get


def _kv_planes(ref, n_kv):
    return lambda s, g, rows: ref[s * n_kv + g, rows, :]


def _unpack_planes(rows_ref, scr):
    planes, length = scr.shape[0], scr.shape[1]
    for c in range(planes):
        scr[c] = rows_ref[0, pl.ds(c, length, stride=planes), :].astype(jnp.bfloat16)


KEY_GROUP = 512


def _for_key_range(qpos0, tq, n_keys, body):
    lengths = list(range(KEY_GROUP, n_keys, KEY_GROUP)) + [n_keys]
    group = (qpos0 + tq - 1) // KEY_GROUP
    for i, n in enumerate(lengths):
        @pl.when((group == i) if i + 1 < len(lengths) else (group >= i))
        def _(n=n):
            body(n)


PAGES_PER_STEP = 32


def _page_specs(pool, layer, pages_per_seq):
    assert pages_per_seq % PAGES_PER_STEP == 0
    block = (1, 1) + pool.shape[2:]
    return [pl.BlockSpec(block, functools.partial(
        lambda i, j, pt, k: (layer, pt[i, j * PAGES_PER_STEP + k], 0, 0), k=k)) for k in range(PAGES_PER_STEP)]


def _load_kv_pages(page_refs, new_ref, kv_scr, page_rows, past_len):
    j = pl.program_id(1)
    planes = kv_scr.shape[0]
    for k, page in enumerate(page_refs):
        row0 = pl.multiple_of((j * len(page_refs) + k) * page_rows, page_rows)
        for c in range(planes):
            kv_scr[c, pl.ds(row0, page_rows), :] = (
                page[0, 0, pl.ds(c, page_rows, stride=planes), :].astype(jnp.bfloat16))

    @pl.when(j == 0)
    def _():
        tail = kv_scr.shape[1] - past_len
        for c in range(planes):
            kv_scr[c, past_len:, :] = new_ref[0, pl.ds(c, tail, stride=planes), :].astype(jnp.bfloat16)


def _pad_rows(x, n):
    return jnp.pad(x, ((0, 0), (0, n - x.shape[1])) + ((0, 0),) * (x.ndim - 2))


def _nsa_body(q_ref, tail_ref, kc_ref, vc_ref, slc, slc_len, win, win_total, cover_ref, expand_ref, o_ref,
              *, qpos0, win_pos0, n_slc, n_sel):
    tq = q_ref.shape[1]
    qpos = qpos0 + lax.broadcasted_iota(jnp.int32, (tq, 1), 0)
    q = q_ref[0]
    gates = jax.nn.sigmoid(tail_ref[0][:, TAIL_GATE[0]:TAIL_GATE[1]])
    win_len = min(win_total, _round_up(WINDOW + tq, LANES))
    win0 = jnp.clip(qpos0 - win_pos0 - WINDOW, 0, win_total - win_len)
    win_rows = pl.ds(pl.multiple_of(win0, LANES), win_len)
    slc_rows = slice(0, slc_len)

    cpos = lax.broadcasted_iota(jnp.int32, (1, kc_ref.shape[2]), 1) * CMP_STRIDE + (CMP_LEN - 1)
    cmask = cpos <= qpos
    kpos = lax.broadcasted_iota(jnp.int32, (1, slc_len), 1)
    causal = kpos <= qpos
    wpos = win_pos0 + win0 + lax.broadcasted_iota(jnp.int32, (1, win_len), 1)
    wmask = (wpos <= qpos) & (qpos - wpos < WINDOW)
    blk = lax.broadcasted_iota(jnp.int32, (1, cover_ref.shape[1]), 1)
    cover = cover_ref[...]

    qgs = [_group_queries(q, g) for g in range(NSA_KV)]
    cmp_out = [_attend(qgs[g], kc_ref[0, g], vc_ref[0, g], cmask, want_psum=True) for g in range(NSA_KV)]
    p_sum = jnp.concatenate([p for _, p in cmp_out], axis=0)
    p_hi = p_sum.astype(jnp.bfloat16)
    p_lo = (p_sum - p_hi.astype(jnp.float32)).astype(jnp.bfloat16)
    imp = (jnp.dot(p_hi, cover, preferred_element_type=jnp.float32)
           + jnp.dot(p_lo, cover, preferred_element_type=jnp.float32))
    dist = jnp.right_shift(jnp.concatenate([qpos] * NSA_KV, axis=0), int(math.log2(SLC_BLOCK))) - blk
    forced = (blk == 0) | ((dist >= 0) & (dist < N_LOCAL))
    score = jnp.where(forced, FORCE, jnp.where(dist < 0, -FORCE, imp))
    score = jnp.where(blk < n_slc, score, LOWEST)
    sel = jnp.zeros(score.shape, jnp.float32)
    for _ in range(n_sel):
        best = jnp.max(score, axis=-1, keepdims=True)
        first = jnp.min(jnp.where(score == best, blk, cover_ref.shape[1]), axis=-1, keepdims=True)
        hit = blk == first
        sel = jnp.where(hit, 1.0, sel)
        score = jnp.where(hit, LOWEST, score)

    for g in range(NSA_KV):
        qg, o_cmp = qgs[g], cmp_out[g][0]
        smask = jnp.dot(sel[g * tq:(g + 1) * tq].astype(jnp.bfloat16), expand_ref[:, :slc_len],
                        preferred_element_type=jnp.float32) > 0.5
        o_slc = _attend(qg, slc(0, g, slc_rows), slc(1, g, slc_rows), smask & causal)
        o_win = _attend(qg, win(0, g, win_rows), win(1, g, win_rows), wmask)
        for h in range(HEADS_PER_GROUP):
            head = g * HEADS_PER_GROUP + h
            rows = slice(h * tq, (h + 1) * tq)
            o_ref[0, :, head * HEAD_DIM:(head + 1) * HEAD_DIM] = (
                gates[:, 3 * head:3 * head + 1] * o_cmp[rows]
                + gates[:, 3 * head + 1:3 * head + 2] * o_slc[rows]
                + gates[:, 3 * head + 2:3 * head + 3] * o_win[rows])


def _nsa_kernel(q_ref, tail_ref, kc_ref, vc_ref, slc_ref, win_ref, cover_ref, expand_ref, o_ref, slc_scr, win_scr,
                *, pos0, win_pos0, n_slc, n_sel):
    @pl.when(pl.program_id(1) == 0)
    def _():
        _unpack_planes(slc_ref, slc_scr)
        _unpack_planes(win_ref, win_scr)

    tq = q_ref.shape[1]
    qpos0 = pos0 + pl.program_id(1) * tq
    _for_key_range(qpos0, tq, slc_scr.shape[1], lambda n: _nsa_body(
        q_ref, tail_ref, kc_ref, vc_ref, _kv_planes(slc_scr, NSA_KV), n, _kv_planes(win_scr, NSA_KV),
        win_scr.shape[1], cover_ref, expand_ref, o_ref, qpos0=qpos0, win_pos0=win_pos0, n_slc=n_slc, n_sel=n_sel))


def _nsa_paged_kernel(pt_ref, q_ref, tail_ref, kc_ref, vc_ref, new_ref, win_ref, cover_ref, expand_ref, *rest,
                      pos0, win_pos0, n_slc, n_sel, page_rows):
    page_refs, (o_ref, kv_scr) = rest[:PAGES_PER_STEP], rest[PAGES_PER_STEP:]
    _load_kv_pages(page_refs, new_ref, kv_scr, page_rows, pos0)

    @pl.when(pl.program_id(1) == pl.num_programs(1) - 1)
    def _():
        _nsa_body(q_ref, tail_ref, kc_ref, vc_ref, _kv_planes(kv_scr, NSA_KV), kv_scr.shape[1],
                  _kv_columns(win_ref, NSA_KV), win_ref.shape[1], cover_ref, expand_ref, o_ref,
                  qpos0=pos0, win_pos0=win_pos0, n_slc=n_slc, n_sel=n_sel)


def _nsa_attention(q, tail, kc, vc, slc_rows, win_kv, pos0, win_pos0, n_c, paged=None):
    b, t_len, _ = q.shape
    seq = slc_rows.shape[1] // KV_PLANES + (pos0 if paged else 0)
    tq = Q_TILE if t_len % Q_TILE == 0 else _round_up(t_len, SUBLANES)
    tp = _round_up(t_len, tq)
    lp = _round_up(seq, LANES)
    n_cp = kc.shape[2]
    n_slc = -(-seq // SLC_BLOCK)
    n_sp = _round_up(n_slc, LANES)
    start = jnp.arange(n_cp)[:, None] * CMP_STRIDE
    lo = jnp.arange(n_sp)[None, :] * SLC_BLOCK
    cover = ((start < lo + SLC_BLOCK) & (start + CMP_LEN > lo) & (jnp.arange(n_cp)[:, None] < n_c)
             & (jnp.arange(n_sp)[None, :] < n_slc)).astype(jnp.bfloat16)
    expand = (jnp.arange(lp)[None, :] // SLC_BLOCK == jnp.arange(n_sp)[:, None]).astype(jnp.bfloat16)
    static = dict(pos0=pos0, win_pos0=win_pos0, n_slc=n_slc, n_sel=min(N_SEL, n_slc))
    width = NSA_HEADS * HEAD_DIM
    kv_w = KV_PLANES * HEAD_DIM
    out_shape = jax.ShapeDtypeStruct((b, tp, width), jnp.float32)
    params = pltpu.CompilerParams(dimension_semantics=("arbitrary", "arbitrary"),
                                  vmem_limit_bytes=VMEM_LIMIT_BYTES)
    kc, vc = kc.astype(jnp.bfloat16), vc.astype(jnp.bfloat16)
    if paged is None:
        assert seq == lp and tp == t_len
        lw = win_kv.shape[1] // KV_PLANES
        assert lw % LANES == 0
        return pl.pallas_call(
            functools.partial(_nsa_kernel, **static),
            grid=(b, tp // tq),
            in_specs=[pl.BlockSpec((1, tq, width), lambda i, j: (i, j, 0)),
                      pl.BlockSpec((1, tq, LANES), lambda i, j: (i, j, 0)),
                      pl.BlockSpec((1, NSA_KV, n_cp, HEAD_DIM), lambda i, j: (i, 0, 0, 0)),
                      pl.BlockSpec((1, NSA_KV, n_cp, HEAD_DIM), lambda i, j: (i, 0, 0, 0)),
                      pl.BlockSpec((1, lp * KV_PLANES, HEAD_DIM), lambda i, j: (i, 0, 0)),
                      pl.BlockSpec((1, lw * KV_PLANES, HEAD_DIM), lambda i, j: (i, 0, 0)),
                      pl.BlockSpec((n_cp, n_sp), lambda i, j: (0, 0)),
                      pl.BlockSpec((n_sp, lp), lambda i, j: (0, 0))],
            out_specs=pl.BlockSpec((1, tq, width), lambda i, j: (i, j, 0)),
            out_shape=out_shape,
            scratch_shapes=[pltpu.VMEM((KV_PLANES, lp, HEAD_DIM), jnp.bfloat16),
                            pltpu.VMEM((KV_PLANES, lw, HEAD_DIM), jnp.bfloat16)],
            compiler_params=params, name="nsa_attention",
        )(q, tail, kc, vc, slc_rows, win_kv, cover, expand)

    pool, layer, page_table = paged
    assert tp == tq
    page_rows = pool.shape[2] // KV_PLANES
    lw = _round_up(win_kv.shape[1], LANES)
    win = _pad_rows(win_kv.reshape(b, win_kv.shape[1], -1), lw).astype(jnp.bfloat16)
    new_rows = _pad_rows(slc_rows, (lp - pos0) * KV_PLANES)
    out = pl.pallas_call(
        functools.partial(_nsa_paged_kernel, page_rows=page_rows, **static),
        grid_spec=pltpu.PrefetchScalarGridSpec(
            num_scalar_prefetch=1,
            grid=(b, page_table.shape[1] // PAGES_PER_STEP),
            in_specs=[pl.BlockSpec((1, tq, width), lambda i, j, pt: (i, 0, 0)),
                      pl.BlockSpec((1, tq, LANES), lambda i, j, pt: (i, 0, 0)),
                      pl.BlockSpec((1, NSA_KV, n_cp, HEAD_DIM), lambda i, j, pt: (i, 0, 0, 0)),
                      pl.BlockSpec((1, NSA_KV, n_cp, HEAD_DIM), lambda i, j, pt: (i, 0, 0, 0)),
                      pl.BlockSpec((1, (lp - pos0) * KV_PLANES, HEAD_DIM), lambda i, j, pt: (i, 0, 0)),
                      pl.BlockSpec((1, lw, kv_w), lambda i, j, pt: (i, 0, 0)),
                      pl.BlockSpec((n_cp, n_sp), lambda i, j, pt: (0, 0)),
                      pl.BlockSpec((n_sp, lp), lambda i, j, pt: (0, 0))]
                     + _page_specs(pool, layer, page_table.shape[1]),
            out_specs=pl.BlockSpec((1, tq, width), lambda i, j, pt: (i, 0, 0)),
            scratch_shapes=[pltpu.VMEM((KV_PLANES, lp, HEAD_DIM), jnp.bfloat16)]),
        out_shape=out_shape, compiler_params=params, name="nsa_attention_paged",
    )(page_table, _pad_rows(q, tp), _pad_rows(tail, tp), kc, vc, new_rows, win, cover, expand,
      *([pool] * PAGES_PER_STEP))
    return out[:, :t_len]


def _compress_mlp(x, w1_ref, pos_ref, w2_ref, b2_ref, s):
    n_sub = x.shape[0]
    half = CMP_STRIDE * HEAD_DIM
    h_lo = jnp.dot(x, w1_ref[s, :half].astype(jnp.bfloat16), preferred_element_type=jnp.float32)
    h_hi = jnp.dot(x, w1_ref[s, half:].astype(jnp.bfloat16), preferred_element_type=jnp.float32)
    h = h_lo + pltpu.roll(h_hi, n_sub - 1, 0) + pos_ref[s]
    return jnp.dot(jax.nn.gelu(h).astype(jnp.bfloat16), w2_ref[s].astype(jnp.bfloat16),
                   preferred_element_type=jnp.float32) + b2_ref[s]


def _compress_kernel(x_ref, w1_ref, pos_ref, w2_ref, b2_ref, o_ref):
    n_sub = o_ref.shape[3]
    for c in range(KV_PLANES):
        s, g = divmod(c, NSA_KV)
        x = jnp.concatenate([x_ref[0, pl.ds(r * KV_PLANES + c, n_sub, stride=CMP_STRIDE * KV_PLANES), :]
                             .astype(jnp.bfloat16) for r in range(CMP_STRIDE)], axis=1)
        o_ref[0, s, g] = _compress_mlp(x, w1_ref, pos_ref, w2_ref, b2_ref, s)


def _compress_paged_kernel(pt_ref, w1_ref, pos_ref, w2_ref, b2_ref, *rest, page_rows):
    page_refs, (o_ref, x_scr) = rest[:PAGES_PER_STEP], rest[PAGES_PER_STEP:]
    j = pl.program_id(1)
    planes = x_scr.shape[0]
    per_page = page_rows // CMP_STRIDE
    for k, page in enumerate(page_refs):
        row0 = pl.multiple_of((j * len(page_refs) + k) * per_page, per_page)
        for c in range(planes):
            for s in range(CMP_STRIDE):
                x_scr[c, pl.ds(row0, per_page), s * HEAD_DIM:(s + 1) * HEAD_DIM] = (
                    page[0, 0, pl.ds(s * planes + c, per_page, stride=CMP_STRIDE * planes), :])

    @pl.when(j == pl.num_programs(1) - 1)
    def _():
        for c in range(planes):
            s, g = divmod(c, NSA_KV)
            o_ref[0, s, g] = _compress_mlp(x_scr[c].astype(jnp.bfloat16), w1_ref, pos_ref, w2_ref, b2_ref, s)


def _compress(rows, w1, pos_emb, w2, b2, paged=None):
    w1 = w1.reshape(2, CMP_LEN * HEAD_DIM, HEAD_DIM)
    pos_bias = jnp.einsum('kld,klde->ke', pos_emb, w1.reshape(2, CMP_LEN, HEAD_DIM, HEAD_DIM))[:, None, :]
    params = dict(vmem_limit_bytes=VMEM_LIMIT_BYTES)
    if paged is None:
        b = rows.shape[0]
        seq = rows.shape[1] // KV_PLANES
        assert seq % CMP_STRIDE == 0
        n_sub = seq // CMP_STRIDE
        const = lambda *shape: pl.BlockSpec(shape, lambda i: (0,) * len(shape))
        return pl.pallas_call(
            _compress_kernel,
            grid=(b,),
            in_specs=[pl.BlockSpec((1, seq * KV_PLANES, HEAD_DIM), lambda i: (i, 0, 0)),
                      const(2, CMP_LEN * HEAD_DIM, HEAD_DIM), const(2, 1, HEAD_DIM),
                      const(2, HEAD_DIM, HEAD_DIM), const(2, 1, HEAD_DIM)],
            out_specs=pl.BlockSpec((1, 2, NSA_KV, n_sub, HEAD_DIM), lambda i: (i, 0, 0, 0, 0)),
            out_shape=jax.ShapeDtypeStruct((b, 2, NSA_KV, n_sub, HEAD_DIM), jnp.float32),
            compiler_params=pltpu.CompilerParams(dimension_semantics=("arbitrary",), **params),
            name="nsa_compress",
        )(rows, w1, pos_bias, w2, b2[:, None, :])

    pool, layer, page_table = paged
    b, n_pages = page_table.shape
    planes = KV_PLANES
    page_rows = pool.shape[2] // planes
    n_sub = n_pages * page_rows // CMP_STRIDE
    const = lambda *shape: pl.BlockSpec(shape, lambda i, j, pt: (0,) * len(shape))
    return pl.pallas_call(
        functools.partial(_compress_paged_kernel, page_rows=page_rows),
        grid_spec=pltpu.PrefetchScalarGridSpec(
            num_scalar_prefetch=1,
            grid=(b, n_pages // PAGES_PER_STEP),
            in_specs=[const(2, CMP_LEN * HEAD_DIM, HEAD_DIM), const(2, 1, HEAD_DIM),
                      const(2, HEAD_DIM, HEAD_DIM), const(2, 1, HEAD_DIM)]
                     + _page_specs(pool, layer, n_pages),
            out_specs=pl.BlockSpec((1, 2, NSA_KV, n_sub, HEAD_DIM), lambda i, j, pt: (i, 0, 0, 0, 0)),
            scratch_shapes=[pltpu.VMEM((planes, n_sub, CMP_STRIDE * HEAD_DIM), jnp.float32)]),
        out_shape=jax.ShapeDtypeStruct((b, 2, NSA_KV, n_sub, HEAD_DIM), jnp.float32),
        compiler_params=pltpu.CompilerParams(dimension_semantics=("arbitrary",) * 2, **params),
        name="nsa_compress_paged",
    )(page_table, w1, pos_bias, w2, b2[:, None, :], *([pool] * PAGES_PER_STEP))


INT_MIN = -2 ** 31
IDX_W_SCALE = (IDX_HEADS * IDX_DIM) ** -0.5


def _count(pred):
    return jnp.sum(jnp.where(pred, 1.0, 0.0), axis=-1, keepdims=True)


def _dsa_body(q_ref, kv, iq_ref, tail_ref, ik, o_ref, *, qpos0, k_top):
    tq = q_ref.shape[1]
    lp = ik.shape[1]
    rows = slice(0, lp)
    qpos = qpos0 + lax.broadcasted_iota(jnp.int32, (tq, 1), 0)
    kpos = lax.broadcasted_iota(jnp.int32, (1, lp), 1)
    causal = kpos <= qpos

    iq = iq_ref[0].astype(jnp.bfloat16)
    iw = tail_ref[0][:, TAIL_IW[0]:TAIL_IW[1]] * IDX_W_SCALE
    score = jnp.zeros((tq, lp), jnp.float32)
    for h in range(IDX_HEADS):
        logits = jnp.dot(iq[:, h * IDX_DIM:(h + 1) * IDX_DIM], ik, preferred_element_type=jnp.float32)
        score = score + iw[:, h:h + 1] * jnp.maximum(logits, 0.0)
    score = jnp.where(causal, score, NEG)
    score = jnp.where(score == 0.0, 0.0, score)
    bits = lax.bitcast_convert_type(score, jnp.int32)
    key = bits ^ (jnp.right_shift(bits, 31) & 0x7FFFFFFF)

    k_f = float(k_top)
    enough = lambda cand: _count(key >= cand) >= k_f
    thr = jnp.where(enough(0), 0, INT_MIN).astype(jnp.int32)
    top_bit = thr | (1 << 30)
    thr = jnp.where(enough(top_bit), top_bit, thr)

    def refine(i, thr):
        low = 28 - 2 * i
        c1, c2, c3 = (thr | jnp.left_shift(jnp.int32(n), low) for n in (1, 2, 3))
        return jnp.where(enough(c3), c3, jnp.where(enough(c2), c2, jnp.where(enough(c1), c1, thr)))

    thr = lax.fori_loop(0, 15, refine, thr)
    above = key > thr
    tied = key == thr
    need = k_f - _count(above)

    few = lambda cand: _count(tied & (kpos < cand)) < need
    surplus = jnp.max(_count(tied) - need) > 0.0
    bound = jnp.where(surplus, jnp.where(few(1 << 14), 1 << 14, 0), (1 << 15) - 1).astype(jnp.int32)

    def widen(i, c):
        low = 12 - 2 * i
        c1, c2, c3 = (c | jnp.left_shift(jnp.int32(n), low) for n in (1, 2, 3))
        return jnp.where(few(c3), c3, jnp.where(few(c2), c2, jnp.where(few(c1), c1, c)))

    bound = lax.fori_loop(0, jnp.where(surplus, 7, 0), widen, bound)
    mask = causal & (above | (tied & (kpos <= bound)))

    q = q_ref[0]
    for g in range(DSA_KV):
        o = _attend(_group_queries(q, g), kv(0, g, rows), kv(1, g, rows), mask)
        for h in range(HEADS_PER_GROUP):
            head = g * HEADS_PER_GROUP + h
            o_ref[0, :, head * HEAD_DIM:(head + 1) * HEAD_DIM] = o[h * tq:(h + 1) * tq]


def _dsa_kernel(q_ref, kv_ref, iq_ref, tail_ref, ik_ref, o_ref, kv_scr, *, pos0, k_top):
    @pl.when(pl.program_id(1) == 0)
    def _():
        _unpack_planes(kv_ref, kv_scr)

    tq = q_ref.shape[1]
    qpos0 = pos0 + pl.program_id(1) * tq
    _for_key_range(qpos0, tq, kv_scr.shape[1], lambda n: _dsa_body(
        q_ref, _kv_planes(kv_scr, DSA_KV), iq_ref, tail_ref, ik_ref[0, :, :n], o_ref, qpos0=qpos0, k_top=k_top))


def _dsa_paged_kernel(pt_ref, q_ref, iq_ref, tail_ref, new_kv_ref, new_ik_ref, *rest, pos0, k_top, page_rows):
    n = PAGES_PER_STEP
    kv_pages, ik_pages, (o_ref, kv_scr, ik_scr) = rest[:n], rest[n:2 * n], rest[2 * n:]
    _load_kv_pages(kv_pages, new_kv_ref, kv_scr, page_rows, pos0)
    j = pl.program_id(1)
    for k, page in enumerate(ik_pages):
        row0 = pl.multiple_of((j * n + k) * page_rows, page_rows)
        ik_scr[:, pl.ds(row0, page_rows)] = page[0, 0].astype(jnp.bfloat16)

    @pl.when(j == 0)
    def _():
        ik_scr[:, pos0:] = new_ik_ref[0].astype(jnp.bfloat16)

    @pl.when(j == pl.num_programs(1) - 1)
    def _():
        _dsa_body(q_ref, _kv_planes(kv_scr, DSA_KV), iq_ref, tail_ref, ik_scr[...], o_ref, qpos0=pos0, k_top=k_top)


def _dsa_attention(q, kv_rows, idx_k, idx_q, tail, pos0, paged=None):
    b, t_len, _ = q.shape
    seq = idx_k.shape[1] + (pos0 if paged else 0)
    assert seq < 2 ** 15
    tq = Q_TILE if t_len % Q_TILE == 0 else _round_up(t_len, SUBLANES)
    tp = _round_up(t_len, tq)
    lp = _round_up(seq, LANES)
    static = dict(pos0=pos0, k_top=min(TOPK_MAX, seq // 4))
    width = DSA_HEADS * HEAD_DIM
    out_shape = jax.ShapeDtypeStruct((b, tp, width), jnp.float32)
    params = pltpu.CompilerParams(dimension_semantics=("arbitrary", "arbitrary"),
                                  vmem_limit_bytes=VMEM_LIMIT_BYTES)
    if paged is None:
        assert seq == lp and tp == t_len
        return pl.pallas_call(
            functools.partial(_dsa_kernel, **static),
            grid=(b, tp // tq),
            in_specs=[pl.BlockSpec((1, tq, width), lambda i, j: (i, j, 0)),
                      pl.BlockSpec((1, lp * KV_PLANES, HEAD_DIM), lambda i, j: (i, 0, 0)),
                      pl.BlockSpec((1, tq, IDX_HEADS * IDX_DIM), lambda i, j: (i, j, 0)),
                      pl.BlockSpec((1, tq, LANES), lambda i, j: (i, j, 0)),
                      pl.BlockSpec((1, IDX_DIM, lp), lambda i, j: (i, 0, 0))],
            out_specs=pl.BlockSpec((1, tq, width), lambda i, j: (i, j, 0)),
            out_shape=out_shape,
            scratch_shapes=[pltpu.VMEM((KV_PLANES, lp, HEAD_DIM), jnp.bfloat16)],
            compiler_params=params, name="dsa_attention",
        )(q, kv_rows, idx_q, tail, jnp.swapaxes(idx_k, 1, 2).astype(jnp.bfloat16))

    kv_pool, idx_pool, layer, page_table = paged
    assert tp == tq
    page_rows = idx_pool.shape[3]
    n_pages = page_table.shape[1]
    out = pl.pallas_call(
        functools.partial(_dsa_paged_kernel, page_rows=page_rows, **static),
        grid_spec=pltpu.PrefetchScalarGridSpec(
            num_scalar_prefetch=1,
            grid=(b, n_pages // PAGES_PER_STEP),
            in_specs=[pl.BlockSpec((1, tq, width), lambda i, j, pt: (i, 0, 0)),
                      pl.BlockSpec((1, tq, IDX_HEADS * IDX_DIM), lambda i, j, pt: (i, 0, 0)),
                      pl.BlockSpec((1, tq, LANES), lambda i, j, pt: (i, 0, 0)),
                      pl.BlockSpec((1, (lp - pos0) * KV_PLANES, HEAD_DIM), lambda i, j, pt: (i, 0, 0)),
                      pl.BlockSpec((1, IDX_DIM, lp - pos0), lambda i, j, pt: (i, 0, 0))]
                     + _page_specs(kv_pool, layer, n_pages) + _page_specs(idx_pool, layer, n_pages),
            out_specs=pl.BlockSpec((1, tq, width), lambda i, j, pt: (i, 0, 0)),
            scratch_shapes=[pltpu.VMEM((KV_PLANES, lp, HEAD_DIM), jnp.bfloat16),
                            pltpu.VMEM((IDX_DIM, lp), jnp.bfloat16)]),
        out_shape=out_shape, compiler_params=params, name="dsa_attention_paged",
    )(page_table, _pad_rows(q, tp), _pad_rows(idx_q, tp), _pad_rows(tail, tp),
      _pad_rows(kv_rows, (lp - pos0) * KV_PLANES), jnp.swapaxes(_pad_rows(idx_k, lp - pos0), 1, 2),
      *([kv_pool] * PAGES_PER_STEP), *([idx_pool] * PAGES_PER_STEP))
    return out[:, :t_len]


def _rms_norm(x, g):
    xf = x.astype(jnp.float32)
    y = xf * lax.rsqrt(jnp.mean(xf * xf, axis=-1, keepdims=True) + NORM_EPS)
    return (y * g.astype(jnp.float32)).astype(x.dtype)


def _rope_partial(x, pos):
    rot = x.shape[-1] // ROT_DIV
    half = rot // 2
    freqs = ROPE_THETA ** (-jnp.arange(half, dtype=jnp.float32) / half)
    ang = pos.astype(jnp.float32)[:, None] * freqs[None, :]
    cos, sin = jnp.cos(ang)[:, None, :], jnp.sin(ang)[:, None, :]
    xr = x[..., :rot].astype(jnp.float32)
    x1, x2 = xr[..., :half], xr[..., half:]
    xr = jnp.concatenate([x1 * cos - x2 * sin, x1 * sin + x2 * cos], axis=-1)
    return jnp.concatenate([xr.astype(x.dtype), x[..., rot:]], axis=-1)


def _nsa_mixer(q, tail, cmp_rows, slc_rows, win_kv, win_pos0, pos0, p, paged=None):
    if paged is None:
        cmp = _compress(cmp_rows, p['cmp_w1'], p['cmp_pos'], p['cmp_w2'], p['cmp_b2'])
    else:
        cmp = _compress(None, p['cmp_w1'], p['cmp_pos'], p['cmp_w2'], p['cmp_b2'], paged=(paged[0],) + paged[2:])
    n_sub = cmp.shape[3]
    c_pos = jnp.arange(n_sub, dtype=jnp.int32) * CMP_STRIDE + (CMP_LEN - 1)
    kc = jnp.swapaxes(_rope_partial(_rms_norm(jnp.swapaxes(cmp[:, 0], 1, 2), p['a_k_norm'][0]), c_pos), 1, 2)
    return _nsa_attention(q, tail, kc, cmp[:, 1], slc_rows, win_kv, pos0, win_pos0, n_sub - 1,
                          paged=None if paged is None else paged[1:])


def _rwkv7_mixer(proj, prev, state0, p, b, T):
    *vectors, gate, bonus = _rwkv_prep(proj, prev, p, T)
    heads = lambda t: t.reshape(b, T, RWKV_HEADS, RWKV_HEAD)
    y, s_t = _rwkv_scan(state0.astype(jnp.float32), *[heads(t) for t in vectors])
    return _rwkv_post(y.reshape(b * T, RWKV_DIM), gate, bonus, p), s_t.astype(state0.dtype)


def _hybrid_layer(x, p, big, layer, pos0, past, ropes):
    b, T, d = x.shape
    xn = _rms_norm(x, p['norm_attn'])
    if past is None:
        proj = _proj(xn, big['w_in'], layer)
        c_prev = None
    else:
        prev = past['shift'][:, None].astype(xn.dtype)
        full = _proj(jnp.concatenate([prev, xn], axis=1), big['w_in'], layer)
        proj = full[:, 1:]
        r0, l0 = COL['rkv'][0], COL['lora'][0]
        c_prev = tuple(full[:, :-1, lo:lo + width].reshape(b * T, width) for lo, width in
                       ((r0, RWKV_DIM), (r0 + RWKV_DIM, RWKV_DIM), (r0 + 2 * RWKV_DIM, RWKV_DIM), (l0, LORA_WIDTH)))

    proj2 = proj.reshape(b * T, PROJ_COLS)
    gains = jnp.stack([p['a_q_norm'], p['a_k_norm'][1], p['a_k_norm'][2], p['b_q_norm'], p['b_k_norm'],
                       jnp.pad(p['idx_k_norm'], (0, LANES - IDX_DIM)),
                       jnp.zeros((LANES,), jnp.float32), jnp.zeros((LANES,), jnp.float32)])
    q_a, q_d, idx_q, cmp_rows, slc_rows, win_rows, dsa_rows, idx_rows = _post_proj(proj2, *ropes, gains)
    per_seq = lambda t: t.reshape((b, -1) + t.shape[1:])
    q_a, q_d, idx_q, idx_rows = per_seq(q_a), per_seq(q_d), per_seq(idx_q), per_seq(idx_rows)
    cmp_rows, slc_rows, win_rows, dsa_rows = per_seq(cmp_rows), per_seq(slc_rows), per_seq(win_rows), per_seq(dsa_rows)
    t0 = COL['tail'][0]
    tail = proj[..., t0:t0 + LANES]
    as_cache = lambda t: t.reshape(b, T, 2, NSA_KV, HEAD_DIM)

    if past is None:
        n_keep = min(WINDOW, T)
        win_state = as_cache(win_rows)[:, -n_keep:]
        rwkv0 = jnp.zeros((b, RWKV_HEADS, RWKV_HEAD, RWKV_HEAD), x.dtype)
        o_a = _nsa_mixer(q_a, tail, cmp_rows, slc_rows, win_rows, pos0, pos0, p)
        o_b = _dsa_attention(q_d, dsa_rows, idx_rows, idx_q, tail, pos0)
    else:
        assert (pos0 + T) // CMP_STRIDE * CMP_STRIDE <= pos0
        win_all = jnp.concatenate([past['nsa_win'], as_cache(win_rows)], axis=1)
        n_keep = past['nsa_win'].shape[1]
        win_state = win_all[:, -n_keep:]
        rwkv0 = past['rwkv']
        page_table = past['page_table']
        o_a = _nsa_mixer(q_a, tail, None, slc_rows, win_all, pos0 - n_keep, pos0, p,
                         paged=(past['cmp_pool'], past['slc_pool'], layer, page_table))
        o_b = _dsa_attention(q_d, dsa_rows, idx_rows, idx_q, tail, pos0,
                             paged=(past['dsa_pool'], past['idx_pool'], layer, page_table))
    o_c, rwkv_new = _rwkv7_mixer(proj2, c_prev, rwkv0, p, b, T)

    mixed = _merge(o_a.reshape(b * T, -1), o_b.reshape(b * T, -1), o_c, proj2, big, layer)
    x = _proj(mixed.reshape(b, T, d), big['w_o'], layer, residual=x)
    x = _ffn(x, p['norm_ffn'], big['w_ffn_in'], big['w_ffn_out'], layer)
    new_state = {'nsa_cmp': as_cache(cmp_rows), 'nsa_slc': as_cache(slc_rows), 'dsa_kv': as_cache(dsa_rows),
                 'dsa_idx': idx_rows, 'nsa_win': win_state, 'rwkv': rwkv_new, 'shift': xn[:, -1]}
    return x, new_state


_PARAM_NAMES = ('norm_attn', 'w_in', 'a_q_norm', 'a_k_norm', 'cmp_w1', 'cmp_pos', 'cmp_w2', 'cmp_b2',
                'b_q_norm', 'b_k_norm', 'idx_k_norm', 'rwkv_mu', 'rwkv_w0', 'rwkv_w2', 'rwkv_a0',
                'rwkv_a2', 'rwkv_g2', 'rwkv_k_k', 'rwkv_k_a', 'rwkv_r_k', 'rwkv_ln_w', 'rwkv_ln_b',
                'p_a', 'p_b', 'p_c', 'w_o', 'norm_ffn', 'w_ffn_in', 'w_ffn_out')


def kernel(x_prompt, x_sample, cache_nsa_cmp, cache_nsa_slc, cache_dsa_kv, cache_dsa_idx, state_nsa_win, state_rwkv, state_shift, page_table, norm_attn, w_in, a_q_norm, a_k_norm, cmp_w1, cmp_pos, cmp_w2, cmp_b2, b_q_norm, b_k_norm, idx_k_norm, rwkv_mu, rwkv_w0, rwkv_w2, rwkv_a0, rwkv_a2, rwkv_g2, rwkv_k_k, rwkv_k_a, rwkv_r_k, rwkv_ln_w, rwkv_ln_b, p_a, p_b, p_c, w_o, norm_ffn, w_ffn_in, w_ffn_out):
    weights = dict(zip(_PARAM_NAMES, (norm_attn, w_in, a_q_norm, a_k_norm, cmp_w1, cmp_pos, cmp_w2, cmp_b2,
                                      b_q_norm, b_k_norm, idx_k_norm, rwkv_mu, rwkv_w0, rwkv_w2, rwkv_a0,
                                      rwkv_a2, rwkv_g2, rwkv_k_k, rwkv_k_a, rwkv_r_k, rwkv_ln_w, rwkv_ln_b,
                                      p_a, p_b, p_c, w_o, norm_ffn, w_ffn_in, w_ffn_out)))
    depth = w_in.shape[0]
    past_len = page_table.shape[1] * cache_nsa_cmp.shape[2]
    yp, ys = x_prompt, x_sample
    st_p, st_s = [], []
    kv_pool = lambda c: c.reshape(c.shape[:2] + (-1, HEAD_DIM))
    big_names = ('w_in', 'p_a', 'p_b', 'p_c', 'w_o', 'w_ffn_in', 'w_ffn_out')
    big = {name: weights[name].astype(jnp.bfloat16) for name in big_names[1:]}
    big['w_in'] = _regroup_w_in(w_in)

    def ropes(pos0, bsz, t_len):
        pos = jnp.tile(pos0 + jnp.arange(t_len, dtype=jnp.int32), bsz)
        return _rope_tables(pos, LANES, HEAD_DIM), _rope_tables(pos, LANES, IDX_DIM)

    ropes_p = ropes(0, *x_prompt.shape[:2])
    ropes_s = ropes(past_len, *x_sample.shape[:2])
    for l in range(depth):
        p = {name: w[l] for name, w in weights.items() if name not in big_names}
        past = {'cmp_pool': kv_pool(cache_nsa_cmp), 'slc_pool': kv_pool(cache_nsa_slc),
                'dsa_pool': kv_pool(cache_dsa_kv), 'idx_pool': jnp.swapaxes(cache_dsa_idx, 2, 3),
                'page_table': page_table,
                'nsa_win': state_nsa_win[l], 'rwkv': state_rwkv[l], 'shift': state_shift[l]}
        yp, sp = _hybrid_layer(yp, p, big, l, 0, None, ropes_p)
        ys, ss = _hybrid_layer(ys, p, big, l, past_len, past, ropes_s)
        st_p.append(sp)
        st_s.append(ss)
    stack = lambda sts, name: jnp.stack([s[name] for s in sts])
    return (yp, ys,
            stack(st_p, 'nsa_cmp'), stack(st_s, 'nsa_cmp'),
            stack(st_p, 'nsa_slc'), stack(st_s, 'nsa_slc'),
            stack(st_p, 'dsa_kv'), stack(st_s, 'dsa_kv'),
            stack(st_p, 'dsa_idx'), stack(st_s, 'dsa_idx'),
            stack(st_p, 'nsa_win'), stack(st_s, 'nsa_win'),
            stack(st_p, 'rwkv'), stack(st_s, 'rwkv'),
            stack(st_p, 'shift'), stack(st_s, 'shift'))
```

```python
import functools
import math

import jax
import jax.numpy as jnp
from jax import lax
from jax.experimental import pallas as pl
from jax.experimental.pallas import tpu as pltpu

D_MODEL = 2048
HEAD_DIM = 128
ROT_DIV = 4
ROPE_THETA = 500000.0
NORM_EPS = 1e-6

NSA_HEADS = 8
NSA_KV = 2
CMP_STRIDE = 16
CMP_LEN = 2 * CMP_STRIDE
SLC_BLOCK = 64
N_SEL = 8
N_LOCAL = 2
WINDOW = 512

DSA_HEADS = 8
DSA_KV = 2
IDX_HEADS = 8
IDX_DIM = 64
TOPK_MAX = 256

RWKV_HEADS = 16
RWKV_HEAD = 64
RWKV_DIM = RWKV_HEADS * RWKV_HEAD
DECAY_LORA = 64
AAA_LORA = 64
GATE_LORA = 160
RWKV_GN_EPS = 64e-5

NEG = -1e30
FORCE = 1e9
LOWEST = -3e38

IN_SPLITS = (
    NSA_HEADS * HEAD_DIM,
    6 * NSA_KV * HEAD_DIM,
    3 * NSA_HEADS,
    DSA_HEADS * HEAD_DIM,
    2 * DSA_KV * HEAD_DIM,
    IDX_HEADS * IDX_DIM,
    IDX_DIM,
    IDX_HEADS,
    3 * RWKV_DIM + DECAY_LORA + AAA_LORA + GATE_LORA,
    3 * D_MODEL,
)

LANES = 128
SUBLANES = 8
VMEM_LIMIT_BYTES = 56 * 1024 * 1024

HEADS_PER_GROUP = NSA_HEADS // NSA_KV
ATTN_SCALE = HEAD_DIM ** -0.5
LOG2_E = math.log2(math.e)
Q_TILE = 128

_NT = (((1,), (1,)), ((), ()))


def _round_up(n, m):
    return -(-n // m) * m


def _mm_kernel(x_ref, w_ref, *rest):
    r_ref = rest[0] if len(rest) == 3 else None
    o_ref, xb_ref = rest[-2:]

    @pl.when(pl.program_id(1) == 0)
    def _():
        xb_ref[...] = x_ref[...].astype(jnp.bfloat16)

    acc = jnp.dot(xb_ref[...], w_ref[0], preferred_element_type=jnp.float32)
    o_ref[...] = acc if r_ref is None else r_ref[...] + acc


def _pick_tile(n, prefs):
    for t in prefs:
        if n % t == 0:
            return t
    return n


def _mm(x, w, layer, residual=None):
    m, k = x.shape
    n = w.shape[2]
    tm = _pick_tile(m, (1024, 512, 256, 128))
    tn = _pick_tile(n, (1024, 512, 384, 256, 128))
    tile = pl.BlockSpec((tm, tn), lambda i, j: (i, j))
    return pl.pallas_call(
        _mm_kernel,
        grid=(m // tm, n // tn),
        in_specs=[pl.BlockSpec((tm, k), lambda i, j: (i, 0)),
                  pl.BlockSpec((1, k, tn), lambda i, j: (layer, 0, j))] + ([] if residual is None else [tile]),
        out_specs=tile,
        out_shape=jax.ShapeDtypeStruct((m, n), jnp.float32),
        scratch_shapes=[pltpu.VMEM((tm, k), jnp.bfloat16)],
        compiler_params=pltpu.CompilerParams(
            dimension_semantics=("arbitrary", "arbitrary"),
            vmem_limit_bytes=VMEM_LIMIT_BYTES),
        name="dense_proj",
    )(x, w, *([] if residual is None else [residual]))


def _proj(x, w, layer, residual=None):
    lead = x.shape[:-1]
    res = None if residual is None else residual.reshape(-1, residual.shape[-1])
    return _mm(x.reshape(-1, x.shape[-1]), w, layer, res).reshape(lead + (w.shape[2],))


FFN_TILE = 512


def _ffn_kernel(x_ref, gain_ref, wg_ref, wu_ref, wo_ref, o_ref, hb_ref, acc_ref):
    f = pl.program_id(1)

    @pl.when(f == 0)
    def _():
        x = x_ref[...]
        y = x * lax.rsqrt(jnp.mean(x * x, axis=-1, keepdims=True) + NORM_EPS)
        hb_ref[...] = (y * gain_ref[...]).astype(jnp.bfloat16)
        acc_ref[...] = jnp.zeros_like(acc_ref)

    hb = hb_ref[...]
    gate = jnp.dot(hb, wg_ref[0], preferred_element_type=jnp.float32)
    up = jnp.dot(hb, wu_ref[0], preferred_element_type=jnp.float32)
    act = (jax.nn.silu(gate) * up).astype(jnp.bfloat16)
    acc_ref[...] += jnp.dot(act, wo_ref[0], preferred_element_type=jnp.float32)

    @pl.when(f == pl.num_programs(1) - 1)
    def _():
        o_ref[...] = x_ref[...] + acc_ref[...]


def _ffn(x, gain, w_in, w_out, layer):
    lead = x.shape[:-1]
    x2 = x.reshape(-1, x.shape[-1])
    m, d = x2.shape
    n_f = w_out.shape[1] // FFN_TILE
    tm = _pick_tile(m, (512, 256, 128))
    out = pl.pallas_call(
        _ffn_kernel,
        grid=(m // tm, n_f),
        in_specs=[pl.BlockSpec((tm, d), lambda i, f: (i, 0)),
                  pl.BlockSpec((1, d), lambda i, f: (0, 0)),
                  pl.BlockSpec((1, d, FFN_TILE), lambda i, f: (layer, 0, f)),
                  pl.BlockSpec((1, d, FFN_TILE), lambda i, f: (layer, 0, n_f + f)),
                  pl.BlockSpec((1, FFN_TILE, d), lambda i, f: (layer, f, 0))],
        out_specs=pl.BlockSpec((tm, d), lambda i, f: (i, 0)),
        out_shape=jax.ShapeDtypeStruct((m, d), jnp.float32),
        scratch_shapes=[pltpu.VMEM((tm, d), jnp.bfloat16), pltpu.VMEM((tm, d), jnp.float32)],
        compiler_params=pltpu.CompilerParams(
            dimension_semantics=("arbitrary", "arbitrary"),
            vmem_limit_bytes=VMEM_LIMIT_BYTES),
        name="ffn_block",
    )(x2, gain[None, :], w_in, w_in, w_out)
    return out.reshape(lead + (d,))


RWKV_HEADS_PER_BLOCK = LANES // 2
RWKV_V_HALF = RWKV_HEAD // 2


def _rwkv_scan_kernel(r_ref, w_ref, k_ref, a_ref, b_ref, v_ref, s0_ref, y_ref, s_out_ref, s_ref, rows_ref):
    c = pl.program_id(1)

    @pl.when(c == 0)
    def _():
        s_ref[...] = s0_ref[0]

    steps = r_ref.shape[1]
    low_half = lax.broadcasted_iota(jnp.int32, (RWKV_V_HALF, LANES), 1) < RWKV_HEADS_PER_BLOCK

    def step(t, carry):
        for n, ref in enumerate((r_ref, w_ref, k_ref, a_ref, b_ref)):
            tile = ref[0, t]
            swapped = pltpu.roll(tile, RWKV_HEADS_PER_BLOCK, 1)
            rows_ref[n, :RWKV_V_HALF] = jnp.where(low_half, tile, swapped)
            rows_ref[n, RWKV_V_HALF:] = jnp.where(low_half, swapped, tile)
        parts = [jnp.zeros((RWKV_V_HALF, LANES), jnp.float32) for _ in range(4)]
        for k in range(RWKV_HEAD):
            parts[k % 4] = parts[k % 4] + s_ref[k] * rows_ref[3, k:k + 1, :]
        sa = (parts[0] + parts[1]) + (parts[2] + parts[3])
        vt = v_ref[0, t]
        ys = [jnp.zeros((RWKV_V_HALF, LANES), jnp.float32) for _ in range(4)]
        for k in range(RWKV_HEAD):
            s_new = (s_ref[k] * rows_ref[1, k:k + 1, :] + sa * rows_ref[4, k:k + 1, :]
                     + vt * rows_ref[2, k:k + 1, :])
            s_ref[k] = s_new
            ys[k % 4] = ys[k % 4] + s_new * rows_ref[0, k:k + 1, :]
        y_ref[0, t] = (ys[0] + ys[1]) + (ys[2] + ys[3])
        return carry

    lax.fori_loop(0, steps, step, 0, unroll=8)

    @pl.when(c == pl.num_programs(1) - 1)
    def _():
        s_out_ref[0] = s_ref[...]


def _rwkv_scan(state0, r, w, k, v, a, b):
    bsz, t_len, n_h, n = r.shape
    heads = bsz * n_h
    hb = RWKV_HEADS_PER_BLOCK
    nblk = heads // hb
    tc = _pick_tile(t_len, (128, 64, 32, 16, 8, 4))

    def pack(x):
        x = jnp.transpose(x.reshape(bsz, t_len, n_h, 2, RWKV_V_HALF), (1, 4, 3, 0, 2))
        x = x.reshape(t_len, RWKV_V_HALF, 2, nblk, hb)
        return jnp.transpose(x, (3, 0, 1, 2, 4)).reshape(nblk, t_len, RWKV_V_HALF, LANES)

    s0 = state0.reshape(nblk, hb, 2, RWKV_V_HALF, n)
    s0 = jnp.transpose(s0, (0, 4, 3, 2, 1)).reshape(nblk, n, RWKV_V_HALF, LANES)

    val_spec = pl.BlockSpec((1, tc, RWKV_V_HALF, LANES), lambda i, c: (i, c, 0, 0))
    st_spec = pl.BlockSpec((1, n, RWKV_V_HALF, LANES), lambda i, c: (i, 0, 0, 0))
    y, s_out = pl.pallas_call(
        _rwkv_scan_kernel,
        grid=(nblk, t_len // tc),
        in_specs=[val_spec] * 6 + [st_spec],
        out_specs=[val_spec, st_spec],
        out_shape=[jax.ShapeDtypeStruct((nblk, t_len, RWKV_V_HALF, LANES), jnp.float32),
                   jax.ShapeDtypeStruct((nblk, n, RWKV_V_HALF, LANES), jnp.float32)],
        scratch_shapes=[pltpu.VMEM((n, RWKV_V_HALF, LANES), jnp.float32),
                        pltpu.VMEM((5, n, LANES), jnp.float32)],
        compiler_params=pltpu.CompilerParams(
            dimension_semantics=("arbitrary", "arbitrary"),
            vmem_limit_bytes=VMEM_LIMIT_BYTES),
        name="rwkv7_scan",
    )(pack(r), pack(w), pack(k), pack(a), pack(b), pack(v), s0)

    y = y.reshape(nblk, t_len, RWKV_V_HALF, 2, hb)
    y = jnp.transpose(y, (0, 4, 1, 3, 2)).reshape(bsz, n_h, t_len, n)
    y = jnp.transpose(y, (0, 2, 1, 3))
    s_out = s_out.reshape(nblk, n, RWKV_V_HALF, 2, hb)
    s_out = jnp.transpose(s_out, (0, 4, 3, 2, 1)).reshape(bsz, n_h, n, n)
    return y, s_out


_ORIG = dict(zip(('a_q', 'a_kv', 'a_gate', 'd_q', 'd_kv', 'i_q', 'i_k', 'i_w', 'c', 'merge'),
                 zip([sum(IN_SPLITS[:i]) for i in range(len(IN_SPLITS))], IN_SPLITS)))
LORA_COLS = DECAY_LORA + AAA_LORA + GATE_LORA
LORA_WIDTH = _round_up(LORA_COLS, LANES)
TAIL_PAD = LANES - (IDX_DIM + 3 * NSA_HEADS + IDX_HEADS)
COL = {}
_off = 0
for _name, _width in (('merge', 3 * D_MODEL), ('a_q', NSA_HEADS * HEAD_DIM), ('d_q', DSA_HEADS * HEAD_DIM),
                      ('rkv', 3 * RWKV_DIM), ('a_kv', 6 * NSA_KV * HEAD_DIM), ('d_kv', 2 * DSA_KV * HEAD_DIM),
                      ('i_q', IDX_HEADS * IDX_DIM), ('lora', LORA_WIDTH), ('tail', LANES)):
    COL[_name] = (_off, _width)
    _off += _width
PROJ_COLS = _off
TAIL_IK = (0, IDX_DIM)
TAIL_GATE = (IDX_DIM, IDX_DIM + 3 * NSA_HEADS)
TAIL_IW = (IDX_DIM + 3 * NSA_HEADS, IDX_DIM + 3 * NSA_HEADS + IDX_HEADS)


def _regroup_w_in(w_in):
    cut = lambda name: w_in[..., _ORIG[name][0]:_ORIG[name][0] + _ORIG[name][1]]
    zeros = lambda n: jnp.zeros(w_in.shape[:2] + (n,), w_in.dtype)
    c0 = _ORIG['c'][0]
    parts = [cut('merge'), cut('a_q'), cut('d_q'), w_in[..., c0:c0 + 3 * RWKV_DIM], cut('a_kv'), cut('d_kv'),
             cut('i_q'), w_in[..., c0 + 3 * RWKV_DIM:c0 + 3 * RWKV_DIM + LORA_COLS], zeros(LORA_WIDTH - LORA_COLS),
             cut('i_k'), cut('a_gate'), cut('i_w'), zeros(TAIL_PAD)]
    return jnp.concatenate(parts, axis=-1).astype(jnp.bfloat16)


def _rope_tables(pos, width, head_dim):
    half = head_dim // ROT_DIV // 2
    freqs = ROPE_THETA ** (-jnp.arange(half, dtype=jnp.float32) / half)
    ang = pos.astype(jnp.float32)[:, None] * freqs[None, :]
    cos, sin = jnp.cos(ang), jnp.sin(ang)
    rest = head_dim - 2 * half
    ones, zeros = jnp.ones((pos.shape[0], rest), jnp.float32), jnp.zeros((pos.shape[0], rest), jnp.float32)
    z_half = jnp.zeros_like(sin)
    reps = width // head_dim
    c = jnp.tile(jnp.concatenate([cos, cos, ones], axis=1), (1, reps))
    s_up = jnp.tile(jnp.concatenate([z_half, sin, zeros], axis=1), (1, reps))
    s_down = jnp.tile(jnp.concatenate([-sin, z_half, zeros], axis=1), (1, reps))
    return c, s_up, s_down


def _rotate(y, tables, half):
    c, s_up, s_down = tables
    width = y.shape[-1]
    return y * c + pltpu.roll(y, half, 1) * s_up + pltpu.roll(y, width - half, 1) * s_down


def _head_norm(x, gain):
    return x * lax.rsqrt(jnp.mean(x * x, axis=-1, keepdims=True) + NORM_EPS) * gain


ROPE_HALF = HEAD_DIM // ROT_DIV // 2
IDX_ROPE_HALF = IDX_DIM // ROT_DIV // 2
KV_PLANES = 2 * NSA_KV


def _post_proj_kernel(aq_ref, dq_ref, cmp_ref, slc_ref, win_ref, dkv_ref, iq_ref, tail_ref,
                      c_ref, su_ref, sd_ref, ic_ref, isu_ref, isd_ref, gains_ref,
                      qa_ref, qd_ref, iqo_ref, cmp_o, slc_o, win_o, dsa_o, ik_o):
    tm = aq_ref.shape[0]
    rope = (c_ref[...], su_ref[...], sd_ref[...])
    irope = (ic_ref[...], isu_ref[...], isd_ref[...])
    gain = lambda i: gains_ref[i:i + 1, :]

    def head(ref, h):
        return ref[:, h * HEAD_DIM:(h + 1) * HEAD_DIM]

    for h in range(NSA_HEADS):
        qa_ref[:, h * HEAD_DIM:(h + 1) * HEAD_DIM] = _rotate(
            _head_norm(head(aq_ref, h), gain(0)), rope, ROPE_HALF).astype(qa_ref.dtype)
    for h in range(DSA_HEADS):
        qd_ref[:, h * HEAD_DIM:(h + 1) * HEAD_DIM] = _rotate(
            _head_norm(head(dq_ref, h), gain(3)), rope, ROPE_HALF).astype(qd_ref.dtype)
    for pair in range(IDX_HEADS * IDX_DIM // LANES):
        iqo_ref[:, pair * LANES:(pair + 1) * LANES] = _rotate(
            iq_ref[:, pair * LANES:(pair + 1) * LANES], irope, IDX_ROPE_HALF).astype(iqo_ref.dtype)

    def planes(src, dst, key_gain):
        for c in range(KV_PLANES):
            x = head(src, c)
            if key_gain is not None and c < NSA_KV:
                x = _rotate(_head_norm(x, gain(key_gain)), rope, ROPE_HALF)
            dst[pl.ds(c, tm, stride=KV_PLANES), :] = x

    planes(cmp_ref, cmp_o, None)
    planes(slc_ref, slc_o, 1)
    planes(win_ref, win_o, 2)
    planes(dkv_ref, dsa_o, 4)

    lane = lax.broadcasted_iota(jnp.int32, (tm, LANES), 1)
    ik = jnp.where(lane < IDX_DIM, tail_ref[...], 0.0)
    ik = ik * lax.rsqrt(jnp.sum(ik * ik, axis=-1, keepdims=True) * (1.0 / IDX_DIM) + NORM_EPS) * gain(5)
    ik_o[...] = _rotate(ik, irope, IDX_ROPE_HALF)[:, :IDX_DIM]


def _post_proj(proj, tables, itables, gains):
    m = proj.shape[0]
    tm = _pick_tile(m, (256, 128))
    col = lambda name, width: pl.BlockSpec((tm, width), functools.partial(
        lambda i, blk: (i, blk), blk=COL[name][0] // width))
    kv_w = KV_PLANES * HEAD_DIM
    akv0 = COL['a_kv'][0] // kv_w
    kv_col = lambda n: pl.BlockSpec((tm, kv_w), functools.partial(lambda i, blk: (i, blk), blk=akv0 + n))
    row = lambda width: pl.BlockSpec((tm, width), lambda i: (i, 0))
    planes = pl.BlockSpec((tm * KV_PLANES, HEAD_DIM), lambda i: (i, 0))
    f32 = jnp.float32
    return pl.pallas_call(
        _post_proj_kernel,
        grid=(m // tm,),
        in_specs=[col('a_q', NSA_HEADS * HEAD_DIM), col('d_q', DSA_HEADS * HEAD_DIM),
                  kv_col(0), kv_col(1), kv_col(2), col('d_kv', kv_w), col('i_q', IDX_HEADS * IDX_DIM),
                  col('tail', LANES)] + [row(LANES)] * 6 + [pl.BlockSpec((8, LANES), lambda i: (0, 0))],
        out_specs=[row(NSA_HEADS * HEAD_DIM), row(DSA_HEADS * HEAD_DIM), row(IDX_HEADS * IDX_DIM),
                   planes, planes, planes, planes, row(IDX_DIM)],
        out_shape=[jax.ShapeDtypeStruct((m, NSA_HEADS * HEAD_DIM), jnp.bfloat16),
                   jax.ShapeDtypeStruct((m, DSA_HEADS * HEAD_DIM), jnp.bfloat16),
                   jax.ShapeDtypeStruct((m, IDX_HEADS * IDX_DIM), jnp.bfloat16)]
                  + [jax.ShapeDtypeStruct((m * KV_PLANES, HEAD_DIM), f32)] * 4
                  + [jax.ShapeDtypeStruct((m, IDX_DIM), f32)],
        compiler_params=pltpu.CompilerParams(dimension_semantics=("arbitrary",),
                                             vmem_limit_bytes=VMEM_LIMIT_BYTES),
        name="post_proj",
    )(*([proj] * 8), *tables, *itables, gains)


def _merge_kernel(oa_ref, ob_ref, oc_ref, pa_ref, pb_ref, pc_ref, ga_ref, gb_ref, gc_ref, o_ref, xb_ref):
    @pl.when(pl.program_id(1) == 0)
    def _():
        for n, ref in enumerate((oa_ref, ob_ref, oc_ref)):
            xb_ref[n] = ref[...].astype(jnp.bfloat16)

    acc = None
    for n, (w_ref, g_ref) in enumerate(((pa_ref, ga_ref), (pb_ref, gb_ref), (pc_ref, gc_ref))):
        term = jax.nn.sigmoid(g_ref[...]) * jnp.dot(xb_ref[n], w_ref[0], preferred_element_type=jnp.float32)
        acc = term if acc is None else acc + term
    o_ref[...] = acc


def _merge(o_a, o_b, o_c, proj, big, layer):
    m, k = o_a.shape
    tm = _pick_tile(m, (512, 256, 128))
    tn = 512
    n_blk = D_MODEL // tn
    x_spec = pl.BlockSpec((tm, k), lambda i, j: (i, 0))
    w_spec = pl.BlockSpec((1, k, tn), lambda i, j: (layer, 0, j))
    gate = lambda n: pl.BlockSpec((tm, tn), functools.partial(lambda i, j, n: (i, n * n_blk + j), n=n))
    assert COL['merge'][0] == 0
    return pl.pallas_call(
        _merge_kernel,
        grid=(m // tm, n_blk),
        in_specs=[x_spec] * 3 + [w_spec] * 3 + [gate(0), gate(1), gate(2)],
        out_specs=pl.BlockSpec((tm, tn), lambda i, j: (i, j)),
        out_shape=jax.ShapeDtypeStruct((m, D_MODEL), jnp.float32),
        scratch_shapes=[pltpu.VMEM((3, tm, k), jnp.bfloat16)],
        compiler_params=pltpu.CompilerParams(dimension_semantics=("arbitrary", "arbitrary"),
                                             vmem_limit_bytes=VMEM_LIMIT_BYTES),
        name="mixer_merge",
    )(o_a, o_b, o_c, big['p_a'], big['p_b'], big['p_c'], proj, proj, proj)


def _head_sums(x, ones_ref):
    hi = x.astype(jnp.bfloat16)
    lo = (x - hi.astype(jnp.float32)).astype(jnp.bfloat16)
    ones = ones_ref[...]
    cols = [jnp.dot(hi[:, j:j + LANES], ones, preferred_element_type=jnp.float32)
            + jnp.dot(lo[:, j:j + LANES], ones, preferred_element_type=jnp.float32)
            for j in range(0, x.shape[1], LANES)]
    return jnp.concatenate(cols, axis=1)


def _softplus(z):
    return jnp.maximum(z, 0.0) + jnp.log(1.0 + jnp.exp(-jnp.abs(z)))


def _rwkv_prep_kernel(*refs, carry_rows):
    cur_refs, rest = refs[:4], refs[4:]
    prev_refs, rest = (None, rest) if carry_rows else (rest[:4], rest[4:])
    (mu_ref, mul_ref, w0_ref, a0_ref, kk_ref, ka_ref, rk_ref, w2_ref, a2_ref, g2_ref, ones_ref,
     r_o, w_o, k_o, v_o, a_o, b_o, g_o, bonus_o) = rest[:19]
    last_refs = rest[19:]
    tm = r_o.shape[0]
    first = lax.broadcasted_iota(jnp.int32, (tm, 1), 0) == 0

    @pl.when(pl.program_id(0) == 0)
    def _():
        for ref in last_refs:
            ref[...] = jnp.zeros_like(ref)

    def shifted(n, cur):
        if prev_refs is not None:
            return prev_refs[n][...]
        at_start = (pl.program_id(0) * tm) % carry_rows == 0
        head = jnp.where(at_start, 0.0, last_refs[n][...])
        last_refs[n][...] = cur[tm - 1:tm, :]
        return jnp.where(first, head, pltpu.roll(cur, 1, 0))

    def mixed(n, mu):
        cur = cur_refs[n][...]
        return cur + (shifted(n, cur) - cur) * mu

    r = mixed(0, mu_ref[0:1, :])
    k = mixed(1, mu_ref[1:2, :])
    v = mixed(2, mu_ref[2:3, :])
    lora = mixed(3, mul_ref[...])
    bf = jnp.bfloat16
    w_lo = jnp.tanh(lora[:, :DECAY_LORA]).astype(bf)
    a_lo = lora[:, DECAY_LORA:DECAY_LORA + AAA_LORA].astype(bf)
    g_lo = jax.nn.sigmoid(lora[:, DECAY_LORA + AAA_LORA:LORA_COLS]).astype(bf)
    w = -_softplus(-(w0_ref[...] + jnp.dot(w_lo, w2_ref[...].astype(bf), preferred_element_type=jnp.float32))) - 0.5
    a = jax.nn.sigmoid(a0_ref[...] + jnp.dot(a_lo, a2_ref[...].astype(bf), preferred_element_type=jnp.float32))
    g_o[...] = jnp.dot(g_lo, g2_ref[...].astype(bf), preferred_element_type=jnp.float32)
    kk = k * kk_ref[...]
    kk = kk * lax.rsqrt(_head_sums(kk * kk, ones_ref) + 1e-12)
    kh = k * (1.0 + (a - 1.0) * ka_ref[...])
    r_o[...] = r
    w_o[...] = jnp.exp(-jnp.exp(w))
    k_o[...] = kh
    v_o[...] = v
    a_o[...] = -kk
    b_o[...] = kk * a
    bonus_o[...] = _head_sums(r * kh * rk_ref[...], ones_ref) * v


def _head_ones():
    seg = jnp.arange(LANES) // RWKV_HEAD
    return (seg[:, None] == seg[None, :]).astype(jnp.bfloat16)


def _rwkv_prep(proj, prev, p, seq_len):
    m = proj.shape[0]
    tm = _pick_tile(m, (256, 128))
    if prev is None:
        assert seq_len % tm == 0
    blk = lambda off, width: pl.BlockSpec((tm, width), functools.partial(lambda i, b: (i, b), b=off // width))
    r0 = COL['rkv'][0]
    cur_specs = [blk(r0, RWKV_DIM), blk(r0 + RWKV_DIM, RWKV_DIM), blk(r0 + 2 * RWKV_DIM, RWKV_DIM),
                 blk(COL['lora'][0], LORA_WIDTH)]
    row = lambda width: pl.BlockSpec((tm, width), lambda i: (i, 0))
    const = lambda *shape: pl.BlockSpec(shape, lambda i: (0,) * len(shape))
    mu = p['rwkv_mu']
    mu_rkv = mu[:3 * RWKV_DIM].reshape(3, RWKV_DIM)
    mu_lora = jnp.pad(mu[3 * RWKV_DIM:], (0, LORA_WIDTH - LORA_COLS))[None, :]
    vec = lambda name: p[name].reshape(1, RWKV_DIM)
    small = [mu_rkv, mu_lora, vec('rwkv_w0'), vec('rwkv_a0'), vec('rwkv_k_k'), vec('rwkv_k_a'), vec('rwkv_r_k'),
             p['rwkv_w2'], p['rwkv_a2'], p['rwkv_g2'], _head_ones()]
    small_specs = [const(*s.shape) for s in small]
    out = jax.ShapeDtypeStruct((m, RWKV_DIM), jnp.float32)
    if prev is None:
        ins, specs = [proj] * 4, cur_specs
        scratch = [pltpu.VMEM((1, RWKV_DIM), jnp.float32)] * 3 + [pltpu.VMEM((1, LORA_WIDTH), jnp.float32)]
    else:
        ins = [proj] * 4 + list(prev)
        specs = cur_specs + [row(RWKV_DIM)] * 3 + [row(LORA_WIDTH)]
        scratch = []
    return pl.pallas_call(
        functools.partial(_rwkv_prep_kernel, carry_rows=seq_len if prev is None else None),
        grid=(m // tm,),
        in_specs=specs + small_specs,
        out_specs=[row(RWKV_DIM)] * 8,
        out_shape=[out] * 8,
        scratch_shapes=scratch,
        compiler_params=pltpu.CompilerParams(dimension_semantics=("arbitrary",),
                                             vmem_limit_bytes=VMEM_LIMIT_BYTES),
        name="rwkv_prep",
    )(*ins, *small)


def _rwkv_post_kernel(y_ref, g_ref, bonus_ref, lnw_ref, lnb_ref, ones_ref, o_ref):
    y = y_ref[...]
    mean = _head_sums(y, ones_ref) * (1.0 / RWKV_HEAD)
    dev = y - mean
    var = _head_sums(dev * dev, ones_ref) * (1.0 / RWKV_HEAD)
    out = dev * lax.rsqrt(var + RWKV_GN_EPS) * lnw_ref[...] + lnb_ref[...] + bonus_ref[...]
    o_ref[...] = out * g_ref[...]


def _rwkv_post(y, gate, bonus, p):
    m = y.shape[0]
    tm = _pick_tile(m, (512, 256, 128))
    row = pl.BlockSpec((tm, RWKV_DIM), lambda i: (i, 0))
    vec = pl.BlockSpec((1, RWKV_DIM), lambda i: (0, 0))
    return pl.pallas_call(
        _rwkv_post_kernel,
        grid=(m // tm,),
        in_specs=[row, row, row, vec, vec, pl.BlockSpec((LANES, LANES), lambda i: (0, 0))],
        out_specs=row,
        out_shape=jax.ShapeDtypeStruct((m, RWKV_DIM), jnp.float32),
        compiler_params=pltpu.CompilerParams(dimension_semantics=("arbitrary",),
                                             vmem_limit_bytes=VMEM_LIMIT_BYTES),
        name="rwkv_post",
    )(y, gate, bonus, p['rwkv_ln_w'][None, :], p['rwkv_ln_b'][None, :], _head_ones())


def _group_queries(q, g):
    h0 = g * HEADS_PER_GROUP
    return jnp.concatenate([q[:, (h0 + h) * HEAD_DIM:(h0 + h + 1) * HEAD_DIM] for h in range(HEADS_PER_GROUP)],
                           axis=0).astype(jnp.bfloat16)


def _attend(qg, k, v, mask, want_psum=False):
    tq = mask.shape[0]
    s = lax.dot_general(qg, k, _NT, preferred_element_type=jnp.float32)
    ps = []
    for h in range(HEADS_PER_GROUP):
        sh = jnp.where(mask, s[h * tq:(h + 1) * tq], NEG)
        top = jnp.max(sh, axis=-1, keepdims=True)
        e = jnp.exp2((sh - top) * (ATTN_SCALE * LOG2_E))
        den = jnp.sum(e, axis=-1, keepdims=True)
        ps.append(e * jnp.where(top > 0.5 * NEG, 1.0 / den, 0.0))
    o = jnp.dot(jnp.concatenate(ps, axis=0).astype(jnp.bfloat16), v, preferred_element_type=jnp.float32)
    if want_psum:
        return o, (ps[0] + ps[1]) + (ps[2] + ps[3])
    return o


def _kv_columns(ref, n_kv):
    def get(s, g, rows):
        c0 = (s * n_kv + g) * HEAD_DIM
        return ref[0, rows, c0:c0 + HEAD_DIM]
    return get


def _kv_planes(ref, n_kv):
    return lambda s, g, rows: ref[s * n_kv + g, rows, :]


def _unpack_planes(rows_ref, scr):
    planes, length = scr.shape[0], scr.shape[1]
    for c in range(planes):
        scr[c] = rows_ref[0, pl.ds(c, length, stride=planes), :].astype(jnp.bfloat16)


KEY_GROUP = 512


def _for_key_range(qpos0, tq, n_keys, body):
    lengths = list(range(KEY_GROUP, n_keys, KEY_GROUP)) + [n_keys]
    group = (qpos0 + tq - 1) // KEY_GROUP
    for i, n in enumerate(lengths):
        @pl.when((group == i) if i + 1 < len(lengths) else (group >= i))
        def _(n=n):
            body(n)


PAGES_PER_STEP = 32


def _page_specs(pool, layer, pages_per_seq):
    assert pages_per_seq % PAGES_PER_STEP == 0
    block = (1, 1) + pool.shape[2:]
    return [pl.BlockSpec(block, functools.partial(
        lambda i, j, pt, k: (layer, pt[i, j * PAGES_PER_STEP + k], 0, 0), k=k)) for k in range(PAGES_PER_STEP)]


def _load_kv_pages(page_refs, new_ref, kv_scr, page_rows, past_len):
    j = pl.program_id(1)
    planes = kv_scr.shape[0]
    for k, page in enumerate(page_refs):
        row0 = pl.multiple_of((j * len(page_refs) + k) * page_rows, page_rows)
        for c in range(planes):
            kv_scr[c, pl.ds(row0, page_rows), :] = (
                page[0, 0, pl.ds(c, page_rows, stride=planes), :].astype(jnp.bfloat16))

    @pl.when(j == 0)
    def _():
        tail = kv_scr.shape[1] - past_len
        for c in range(planes):
            kv_scr[c, past_len:, :] = new_ref[0, pl.ds(c, tail, stride=planes), :].astype(jnp.bfloat16)


def _pad_rows(x, n):
    return jnp.pad(x, ((0, 0), (0, n - x.shape[1])) + ((0, 0),) * (x.ndim - 2))


def _nsa_body(q_ref, tail_ref, kc_ref, vc_ref, slc, slc_len, win, win_total, cover_ref, expand_ref, o_ref,
              *, qpos0, win_pos0, n_slc, n_sel):
    tq = q_ref.shape[1]
    qpos = qpos0 + lax.broadcasted_iota(jnp.int32, (tq, 1), 0)
    q = q_ref[0]
    gates = jax.nn.sigmoid(tail_ref[0][:, TAIL_GATE[0]:TAIL_GATE[1]])
    win_len = min(win_total, _round_up(WINDOW + tq, LANES))
    win0 = jnp.clip(qpos0 - win_pos0 - WINDOW, 0, win_total - win_len)
    win_rows = pl.ds(pl.multiple_of(win0, LANES), win_len)
    slc_rows = slice(0, slc_len)

    cpos = lax.broadcasted_iota(jnp.int32, (1, kc_ref.shape[2]), 1) * CMP_STRIDE + (CMP_LEN - 1)
    cmask = cpos <= qpos
    kpos = lax.broadcasted_iota(jnp.int32, (1, slc_len), 1)
    causal = kpos <= qpos
    wpos = win_pos0 + win0 + lax.broadcasted_iota(jnp.int32, (1, win_len), 1)
    wmask = (wpos <= qpos) & (qpos - wpos < WINDOW)
    blk = lax.broadcasted_iota(jnp.int32, (1, cover_ref.shape[1]), 1)
    cover = cover_ref[...]

    qgs = [_group_queries(q, g) for g in range(NSA_KV)]
    cmp_out = [_attend(qgs[g], kc_ref[0, g], vc_ref[0, g], cmask, want_psum=True) for g in range(NSA_KV)]
    p_sum = jnp.concatenate([p for _, p in cmp_out], axis=0)
    p_hi = p_sum.astype(jnp.bfloat16)
    p_lo = (p_sum - p_hi.astype(jnp.float32)).astype(jnp.bfloat16)
    imp = (jnp.dot(p_hi, cover, preferred_element_type=jnp.float32)
           + jnp.dot(p_lo, cover, preferred_element_type=jnp.float32))
    dist = jnp.right_shift(jnp.concatenate([qpos] * NSA_KV, axis=0), int(math.log2(SLC_BLOCK))) - blk
    forced = (blk == 0) | ((dist >= 0) & (dist < N_LOCAL))
    score = jnp.where(forced, FORCE, jnp.where(dist < 0, -FORCE, imp))
    score = jnp.where(blk < n_slc, score, LOWEST)
    sel = jnp.zeros(score.shape, jnp.float32)
    for _ in range(n_sel):
        best = jnp.max(score, axis=-1, keepdims=True)
        first = jnp.min(jnp.where(score == best, blk, cover_ref.shape[1]), axis=-1, keepdims=True)
        hit = blk == first
        sel = jnp.where(hit, 1.0, sel)
        score = jnp.where(hit, LOWEST, score)

    for g in range(NSA_KV):
        qg, o_cmp = qgs[g], cmp_out[g][0]
        smask = jnp.dot(sel[g * tq:(g + 1) * tq].astype(jnp.bfloat16), expand_ref[:, :slc_len],
                        preferred_element_type=jnp.float32) > 0.5
        o_slc = _attend(qg, slc(0, g, slc_rows), slc(1, g, slc_rows), smask & causal)
        o_win = _attend(qg, win(0, g, win_rows), win(1, g, win_rows), wmask)
        for h in range(HEADS_PER_GROUP):
            head = g * HEADS_PER_GROUP + h
            rows = slice(h * tq, (h + 1) * tq)
            o_ref[0, :, head * HEAD_DIM:(head + 1) * HEAD_DIM] = (
                gates[:, 3 * head:3 * head + 1] * o_cmp[rows]
                + gates[:, 3 * head + 1:3 * head + 2] * o_slc[rows]
                + gates[:, 3 * head + 2:3 * head + 3] * o_win[rows])


def _nsa_kernel(q_ref, tail_ref, kc_ref, vc_ref, slc_ref, win_ref, cover_ref, expand_ref, o_ref, slc_scr, win_scr,
                *, pos0, win_pos0, n_slc, n_sel):
    @pl.when(pl.program_id(1) == 0)
    def _():
        _unpack_planes(slc_ref, slc_scr)
        _unpack_planes(win_ref, win_scr)

    tq = q_ref.shape[1]
    qpos0 = pos0 + pl.program_id(1) * tq
    _for_key_range(qpos0, tq, slc_scr.shape[1], lambda n: _nsa_body(
        q_ref, tail_ref, kc_ref, vc_ref, _kv_planes(slc_scr, NSA_KV), n, _kv_planes(win_scr, NSA_KV),
        win_scr.shape[1], cover_ref, expand_ref, o_ref, qpos0=qpos0, win_pos0=win_pos0, n_slc=n_slc, n_sel=n_sel))


def _nsa_paged_kernel(pt_ref, q_ref, tail_ref, kc_ref, vc_ref, new_ref, win_ref, cover_ref, expand_ref, *rest,
                      pos0, win_pos0, n_slc, n_sel, page_rows):
    page_refs, (o_ref, kv_scr) = rest[:PAGES_PER_STEP], rest[PAGES_PER_STEP:]
    _load_kv_pages(page_refs, new_ref, kv_scr, page_rows, pos0)

    @pl.when(pl.program_id(1) == pl.num_programs(1) - 1)
    def _():
        _nsa_body(q_ref, tail_ref, kc_ref, vc_ref, _kv_planes(kv_scr, NSA_KV), kv_scr.shape[1],
                  _kv_columns(win_ref, NSA_KV), win_ref.shape[1], cover_ref, expand_ref, o_ref,
                  qpos0=pos0, win_pos0=win_pos0, n_slc=n_slc, n_sel=n_sel)


def _nsa_attention(q, tail, kc, vc, slc_rows, win_kv, pos0, win_pos0, n_c, paged=None):
    b, t_len, _ = q.shape
    seq = slc_rows.shape[1] // KV_PLANES + (pos0 if paged else 0)
    tq = Q_TILE if t_len % Q_TILE == 0 else _round_up(t_len, SUBLANES)
    tp = _round_up(t_len, tq)
    lp = _round_up(seq, LANES)
    n_cp = kc.shape[2]
    n_slc = -(-seq // SLC_BLOCK)
    n_sp = _round_up(n_slc, LANES)
    start = jnp.arange(n_cp)[:, None] * CMP_STRIDE
    lo = jnp.arange(n_sp)[None, :] * SLC_BLOCK
    cover = ((start < lo + SLC_BLOCK) & (start + CMP_LEN > lo) & (jnp.arange(n_cp)[:, None] < n_c)
             & (jnp.arange(n_sp)[None, :] < n_slc)).astype(jnp.bfloat16)
    expand = (jnp.arange(lp)[None, :] // SLC_BLOCK == jnp.arange(n_sp)[:, None]).astype(jnp.bfloat16)
    static = dict(pos0=pos0, win_pos0=win_pos0, n_slc=n_slc, n_sel=min(N_SEL, n_slc))
    width = NSA_HEADS * HEAD_DIM
    kv_w = KV_PLANES * HEAD_DIM
    out_shape = jax.ShapeDtypeStruct((b, tp, width), jnp.float32)
    params = pltpu.CompilerParams(dimension_semantics=("arbitrary", "arbitrary"),
                                  vmem_limit_bytes=VMEM_LIMIT_BYTES)
    kc, vc = kc.astype(jnp.bfloat16), vc.astype(jnp.bfloat16)
    if paged is None:
        assert seq == lp and tp == t_len
        lw = win_kv.shape[1] // KV_PLANES
        assert lw % LANES == 0
        return pl.pallas_call(
            functools.partial(_nsa_kernel, **static),
            grid=(b, tp // tq),
            in_specs=[pl.BlockSpec((1, tq, width), lambda i, j: (i, j, 0)),
                      pl.BlockSpec((1, tq, LANES), lambda i, j: (i, j, 0)),
                      pl.BlockSpec((1, NSA_KV, n_cp, HEAD_DIM), lambda i, j: (i, 0, 0, 0)),
                      pl.BlockSpec((1, NSA_KV, n_cp, HEAD_DIM), lambda i, j: (i, 0, 0, 0)),
                      pl.BlockSpec((1, lp * KV_PLANES, HEAD_DIM), lambda i, j: (i, 0, 0)),
                      pl.BlockSpec((1, lw * KV_PLANES, HEAD_DIM), lambda i, j: (i, 0, 0)),
                      pl.BlockSpec((n_cp, n_sp), lambda i, j: (0, 0)),
                      pl.BlockSpec((n_sp, lp), lambda i, j: (0, 0))],
            out_specs=pl.BlockSpec((1, tq, width), lambda i, j: (i, j, 0)),
            out_shape=out_shape,
            scratch_shapes=[pltpu.VMEM((KV_PLANES, lp, HEAD_DIM), jnp.bfloat16),
                            pltpu.VMEM((KV_PLANES, lw, HEAD_DIM), jnp.bfloat16)],
            compiler_params=params, name="nsa_attention",
        )(q, tail, kc, vc, slc_rows, win_kv, cover, expand)

    pool, layer, page_table = paged
    assert tp == tq
    page_rows = pool.shape[2] // KV_PLANES
    lw = _round_up(win_kv.shape[1], LANES)
    win = _pad_rows(win_kv.reshape(b, win_kv.shape[1], -1), lw).astype(jnp.bfloat16)
    new_rows = _pad_rows(slc_rows, (lp - pos0) * KV_PLANES)
    out = pl.pallas_call(
        functools.partial(_nsa_paged_kernel, page_rows=page_rows, **static),
        grid_spec=pltpu.PrefetchScalarGridSpec(
            num_scalar_prefetch=1,
            grid=(b, page_table.shape[1] // PAGES_PER_STEP),
            in_specs=[pl.BlockSpec((1, tq, width), lambda i, j, pt: (i, 0, 0)),
                      pl.BlockSpec((1, tq, LANES), lambda i, j, pt: (i, 0, 0)),
                      pl.BlockSpec((1, NSA_KV, n_cp, HEAD_DIM), lambda i, j, pt: (i, 0, 0, 0)),
                      pl.BlockSpec((1, NSA_KV, n_cp, HEAD_DIM), lambda i, j, pt: (i, 0, 0, 0)),
                      pl.BlockSpec((1, (lp - pos0) * KV_PLANES, HEAD_DIM), lambda i, j, pt: (i, 0, 0)),
                      pl.BlockSpec((1, lw, kv_w), lambda i, j, pt: (i, 0, 0)),
                      pl.BlockSpec((n_cp, n_sp), lambda i, j, pt: (0, 0)),
                      pl.BlockSpec((n_sp, lp), lambda i, j, pt: (0, 0))]
                     + _page_specs(pool, layer, page_table.shape[1]),
            out_specs=pl.BlockSpec((1, tq, width), lambda i, j, pt: (i, 0, 0)),
            scratch_shapes=[pltpu.VMEM((KV_PLANES, lp, HEAD_DIM), jnp.bfloat16)]),
        out_shape=out_shape, compiler_params=params, name="nsa_attention_paged",
    )(page_table, _pad_rows(q, tp), _pad_rows(tail, tp), kc, vc, new_rows, win, cover, expand,
      *([pool] * PAGES_PER_STEP))
    return out[:, :t_len]


def _compress_mlp(x, w1_ref, pos_ref, w2_ref, b2_ref, s):
    n_sub = x.shape[0]
    half = CMP_STRIDE * HEAD_DIM
    h_lo = jnp.dot(x, w1_ref[s, :half].astype(jnp.bfloat16), preferred_element_type=jnp.float32)
    h_hi = jnp.dot(x, w1_ref[s, half:].astype(jnp.bfloat16), preferred_element_type=jnp.float32)
    h = h_lo + pltpu.roll(h_hi, n_sub - 1, 0) + pos_ref[s]
    return jnp.dot(jax.nn.gelu(h).astype(jnp.bfloat16), w2_ref[s].astype(jnp.bfloat16),
                   preferred_element_type=jnp.float32) + b2_ref[s]


def _compress_kernel(x_ref, w1_ref, pos_ref, w2_ref, b2_ref, o_ref):
    n_sub = o_ref.shape[3]
    for c in range(KV_PLANES):
        s, g = divmod(c, NSA_KV)
        x = jnp.concatenate([x_ref[0, pl.ds(r * KV_PLANES + c, n_sub, stride=CMP_STRIDE * KV_PLANES), :]
                             .astype(jnp.bfloat16) for r in range(CMP_STRIDE)], axis=1)
        o_ref[0, s, g] = _compress_mlp(x, w1_ref, pos_ref, w2_ref, b2_ref, s)


def _compress_paged_kernel(pt_ref, w1_ref, pos_ref, w2_ref, b2_ref, *rest, page_rows):
    page_refs, (o_ref, x_scr) = rest[:PAGES_PER_STEP], rest[PAGES_PER_STEP:]
    j = pl.program_id(1)
    planes = x_scr.shape[0]
    per_page = page_rows // CMP_STRIDE
    for k, page in enumerate(page_refs):
        row0 = pl.multiple_of((j * len(page_refs) + k) * per_page, per_page)
        for c in range(planes):
            for s in range(CMP_STRIDE):
                x_scr[c, pl.ds(row0, per_page), s * HEAD_DIM:(s + 1) * HEAD_DIM] = (
                    page[0, 0, pl.ds(s * planes + c, per_page, stride=CMP_STRIDE * planes), :])

    @pl.when(j == pl.num_programs(1) - 1)
    def _():
        for c in range(planes):
            s, g = divmod(c, NSA_KV)
            o_ref[0, s, g] = _compress_mlp(x_scr[c].astype(jnp.bfloat16), w1_ref, pos_ref, w2_ref, b2_ref, s)


def _compress(rows, w1, pos_emb, w2, b2, paged=None):
    w1 = w1.reshape(2, CMP_LEN * HEAD_DIM, HEAD_DIM)
    pos_bias = jnp.einsum('kld,klde->ke', pos_emb, w1.reshape(2, CMP_LEN, HEAD_DIM, HEAD_DIM))[:, None, :]
    params = dict(vmem_limit_bytes=VMEM_LIMIT_BYTES)
    if paged is None:
        b = rows.shape[0]
        seq = rows.shape[1] // KV_PLANES
        assert seq % CMP_STRIDE == 0
        n_sub = seq // CMP_STRIDE
        const = lambda *shape: pl.BlockSpec(shape, lambda i: (0,) * len(shape))
        return pl.pallas_call(
            _compress_kernel,
            grid=(b,),
            in_specs=[pl.BlockSpec((1, seq * KV_PLANES, HEAD_DIM), lambda i: (i, 0, 0)),
                      const(2, CMP_LEN * HEAD_DIM, HEAD_DIM), const(2, 1, HEAD_DIM),
                      const(2, HEAD_DIM, HEAD_DIM), const(2, 1, HEAD_DIM)],
            out_specs=pl.BlockSpec((1, 2, NSA_KV, n_sub, HEAD_DIM), lambda i: (i, 0, 0, 0, 0)),
            out_shape=jax.ShapeDtypeStruct((b, 2, NSA_KV, n_sub, HEAD_DIM), jnp.float32),
            compiler_params=pltpu.CompilerParams(dimension_semantics=("arbitrary",), **params),
            name="nsa_compress",
        )(rows, w1, pos_bias, w2, b2[:, None, :])

    pool, layer, page_table = paged
    b, n_pages = page_table.shape
    planes = KV_PLANES
    page_rows = pool.shape[2] // planes
    n_sub = n_pages * page_rows // CMP_STRIDE
    const = lambda *shape: pl.BlockSpec(shape, lambda i, j, pt: (0,) * len(shape))
    return pl.pallas_call(
        functools.partial(_compress_paged_kernel, page_rows=page_rows),
        grid_spec=pltpu.PrefetchScalarGridSpec(
            num_scalar_prefetch=1,
            grid=(b, n_pages // PAGES_PER_STEP),
            in_specs=[const(2, CMP_LEN * HEAD_DIM, HEAD_DIM), const(2, 1, HEAD_DIM),
                      const(2, HEAD_DIM, HEAD_DIM), const(2, 1, HEAD_DIM)]
                     + _page_specs(pool, layer, n_pages),
            out_specs=pl.BlockSpec((1, 2, NSA_KV, n_sub, HEAD_DIM), lambda i, j, pt: (i, 0, 0, 0, 0)),
            scratch_shapes=[pltpu.VMEM((planes, n_sub, CMP_STRIDE * HEAD_DIM), jnp.float32)]),
        out_shape=jax.ShapeDtypeStruct((b, 2, NSA_KV, n_sub, HEAD_DIM), jnp.float32),
        compiler_params=pltpu.CompilerParams(dimension_semantics=("arbitrary",) * 2, **params),
        name="nsa_compress_paged",
    )(page_table, w1, pos_bias, w2, b2[:, None, :], *([pool] * PAGES_PER_STEP))


INT_MIN = -2 ** 31
IDX_W_SCALE = (IDX_HEADS * IDX_DIM) ** -0.5


def _count(pred):
    return jnp.sum(jnp.where(pred, 1.0, 0.0), axis=-1, keepdims=True)


def _dsa_body(q_ref, kv, iq_ref, tail_ref, ik, o_ref, *, qpos0, k_top):
    tq = q_ref.shape[1]
    lp = ik.shape[1]
    rows = slice(0, lp)
    qpos = qpos0 + lax.broadcasted_iota(jnp.int32, (tq, 1), 0)
    kpos = lax.broadcasted_iota(jnp.int32, (1, lp), 1)
    causal = kpos <= qpos

    iq = iq_ref[0].astype(jnp.bfloat16)
    iw = tail_ref[0][:, TAIL_IW[0]:TAIL_IW[1]] * IDX_W_SCALE
    score = jnp.zeros((tq, lp), jnp.float32)
    for h in range(IDX_HEADS):
        logits = jnp.dot(iq[:, h * IDX_DIM:(h + 1) * IDX_DIM], ik, preferred_element_type=jnp.float32)
        score = score + iw[:, h:h + 1] * jnp.maximum(logits, 0.0)
    score = jnp.where(causal, score, NEG)
    score = jnp.where(score == 0.0, 0.0, score)
    bits = lax.bitcast_convert_type(score, jnp.int32)
    key = bits ^ (jnp.right_shift(bits, 31) & 0x7FFFFFFF)

    k_f = float(k_top)
    enough = lambda cand: _count(key >= cand) >= k_f
    thr = jnp.where(enough(0), 0, INT_MIN).astype(jnp.int32)
    top_bit = thr | (1 << 30)
    thr = jnp.where(enough(top_bit), top_bit, thr)

    def refine(i, thr):
        low = 28 - 2 * i
        c1, c2, c3 = (thr | jnp.left_shift(jnp.int32(n), low) for n in (1, 2, 3))
        return jnp.where(enough(c3), c3, jnp.where(enough(c2), c2, jnp.where(enough(c1), c1, thr)))

    thr = lax.fori_loop(0, 15, refine, thr)
    above = key > thr
    tied = key == thr
    need = k_f - _count(above)

    few = lambda cand: _count(tied & (kpos < cand)) < need
    surplus = jnp.max(_count(tied) - need) > 0.0
    bound = jnp.where(surplus, jnp.where(few(1 << 14), 1 << 14, 0), (1 << 15) - 1).astype(jnp.int32)

    def widen(i, c):
        low = 12 - 2 * i
        c1, c2, c3 = (c | jnp.left_shift(jnp.int32(n), low) for n in (1, 2, 3))
        return jnp.where(few(c3), c3, jnp.where(few(c2), c2, jnp.where(few(c1), c1, c)))

    bound = lax.fori_loop(0, jnp.where(surplus, 7, 0), widen, bound)
    mask = causal & (above | (tied & (kpos <= bound)))

    q = q_ref[0]
    for g in range(DSA_KV):
        o = _attend(_group_queries(q, g), kv(0, g, rows), kv(1, g, rows), mask)
        for h in range(HEADS_PER_GROUP):
            head = g * HEADS_PER_GROUP + h
            o_ref[0, :, head * HEAD_DIM:(head + 1) * HEAD_DIM] = o[h * tq:(h + 1) * tq]


def _dsa_kernel(q_ref, kv_ref, iq_ref, tail_ref, ik_ref, o_ref, kv_scr, *, pos0, k_top):
    @pl.when(pl.program_id(1) == 0)
    def _():
        _unpack_planes(kv_ref, kv_scr)

    tq = q_ref.shape[1]
    qpos0 = pos0 + pl.program_id(1) * tq
    _for_key_range(qpos0, tq, kv_scr.shape[1], lambda n: _dsa_body(
        q_ref, _kv_planes(kv_scr, DSA_KV), iq_ref, tail_ref, ik_ref[0, :, :n], o_ref, qpos0=qpos0, k_top=k_top))


def _dsa_paged_kernel(pt_ref, q_ref, iq_ref, tail_ref, new_kv_ref, new_ik_ref, *rest, pos0, k_top, page_rows):
    n = PAGES_PER_STEP
    kv_pages, ik_pages, (o_ref, kv_scr, ik_scr) = rest[:n], rest[n:2 * n], rest[2 * n:]
    _load_kv_pages(kv_pages, new_kv_ref, kv_scr, page_rows, pos0)
    j = pl.program_id(1)
    for k, page in enumerate(ik_pages):
        row0 = pl.multiple_of((j * n + k) * page_rows, page_rows)
        ik_scr[:, pl.ds(row0, page_rows)] = page[0, 0].astype(jnp.bfloat16)

    @pl.when(j == 0)
    def _():
        ik_scr[:, pos0:] = new_ik_ref[0].astype(jnp.bfloat16)

    @pl.when(j == pl.num_programs(1) - 1)
    def _():
        _dsa_body(q_ref, _kv_planes(kv_scr, DSA_KV), iq_ref, tail_ref, ik_scr[...], o_ref, qpos0=pos0, k_top=k_top)


def _dsa_attention(q, kv_rows, idx_k, idx_q, tail, pos0, paged=None):
    b, t_len, _ = q.shape
    seq = idx_k.shape[1] + (pos0 if paged else 0)
    assert seq < 2 ** 15
    tq = Q_TILE if t_len % Q_TILE == 0 else _round_up(t_len, SUBLANES)
    tp = _round_up(t_len, tq)
    lp = _round_up(seq, LANES)
    static = dict(pos0=pos0, k_top=min(TOPK_MAX, seq // 4))
    width = DSA_HEADS * HEAD_DIM
    out_shape = jax.ShapeDtypeStruct((b, tp, width), jnp.float32)
    params = pltpu.CompilerParams(dimension_semantics=("arbitrary", "arbitrary"),
                                  vmem_limit_bytes=VMEM_LIMIT_BYTES)
    if paged is None:
        assert seq == lp and tp == t_len
        return pl.pallas_call(
            functools.partial(_dsa_kernel, **static),
            grid=(b, tp // tq),
            in_specs=[pl.BlockSpec((1, tq, width), lambda i, j: (i, j, 0)),
                      pl.BlockSpec((1, lp * KV_PLANES, HEAD_DIM), lambda i, j: (i, 0, 0)),
                      pl.BlockSpec((1, tq, IDX_HEADS * IDX_DIM), lambda i, j: (i, j, 0)),
                      pl.BlockSpec((1, tq, LANES), lambda i, j: (i, j, 0)),
                      pl.BlockSpec((1, IDX_DIM, lp), lambda i, j: (i, 0, 0))],
            out_specs=pl.BlockSpec((1, tq, width), lambda i, j: (i, j, 0)),
            out_shape=out_shape,
            scratch_shapes=[pltpu.VMEM((KV_PLANES, lp, HEAD_DIM), jnp.bfloat16)],
            compiler_params=params, name="dsa_attention",
        )(q, kv_rows, idx_q, tail, jnp.swapaxes(idx_k, 1, 2).astype(jnp.bfloat16))

    kv_pool, idx_pool, layer, page_table = paged
    assert tp == tq
    page_rows = idx_pool.shape[3]
    n_pages = page_table.shape[1]
    out = pl.pallas_call(
        functools.partial(_dsa_paged_kernel, page_rows=page_rows, **static),
        grid_spec=pltpu.PrefetchScalarGridSpec(
            num_scalar_prefetch=1,
            grid=(b, n_pages // PAGES_PER_STEP),
            in_specs=[pl.BlockSpec((1, tq, width), lambda i, j, pt: (i, 0, 0)),
                      pl.BlockSpec((1, tq, IDX_HEADS * IDX_DIM), lambda i, j, pt: (i, 0, 0)),
                      pl.BlockSpec((1, tq, LANES), lambda i, j, pt: (i, 0, 0)),
                      pl.BlockSpec((1, (lp - pos0) * KV_PLANES, HEAD_DIM), lambda i, j, pt: (i, 0, 0)),
                      pl.BlockSpec((1, IDX_DIM, lp - pos0), lambda i, j, pt: (i, 0, 0))]
                     + _page_specs(kv_pool, layer, n_pages) + _page_specs(idx_pool, layer, n_pages),
            out_specs=pl.BlockSpec((1, tq, width), lambda i, j, pt: (i, 0, 0)),
            scratch_shapes=[pltpu.VMEM((KV_PLANES, lp, HEAD_DIM), jnp.bfloat16),
                            pltpu.VMEM((IDX_DIM, lp), jnp.bfloat16)]),
        out_shape=out_shape, compiler_params=params, name="dsa_attention_paged",
    )(page_table, _pad_rows(q, tp), _pad_rows(idx_q, tp), _pad_rows(tail, tp),
      _pad_rows(kv_rows, (lp - pos0) * KV_PLANES), jnp.swapaxes(_pad_rows(idx_k, lp - pos0), 1, 2),
      *([kv_pool] * PAGES_PER_STEP), *([idx_pool] * PAGES_PER_STEP))
    return out[:, :t_len]


def _rms_norm(x, g):
    xf = x.astype(jnp.float32)
    y = xf * lax.rsqrt(jnp.mean(xf * xf, axis=-1, keepdims=True) + NORM_EPS)
    return (y * g.astype(jnp.float32)).astype(x.dtype)


def _rope_partial(x, pos):
    rot = x.shape[-1] // ROT_DIV
    half = rot // 2
    freqs = ROPE_THETA ** (-jnp.arange(half, dtype=jnp.float32) / half)
    ang = pos.astype(jnp.float32)[:, None] * freqs[None, :]
    cos, sin = jnp.cos(ang)[:, None, :], jnp.sin(ang)[:, None, :]
    xr = x[..., :rot].astype(jnp.float32)
    x1, x2 = xr[..., :half], xr[..., half:]
    xr = jnp.concatenate([x1 * cos - x2 * sin, x1 * sin + x2 * cos], axis=-1)
    return jnp.concatenate([xr.astype(x.dtype), x[..., rot:]], axis=-1)


def _nsa_mixer(q, tail, cmp_rows, slc_rows, win_kv, win_pos0, pos0, p, paged=None):
    if paged is None:
        cmp = _compress(cmp_rows, p['cmp_w1'], p['cmp_pos'], p['cmp_w2'], p['cmp_b2'])
    else:
        cmp = _compress(None, p['cmp_w1'], p['cmp_pos'], p['cmp_w2'], p['cmp_b2'], paged=(paged[0],) + paged[2:])
    n_sub = cmp.shape[3]
    c_pos = jnp.arange(n_sub, dtype=jnp.int32) * CMP_STRIDE + (CMP_LEN - 1)
    kc = jnp.swapaxes(_rope_partial(_rms_norm(jnp.swapaxes(cmp[:, 0], 1, 2), p['a_k_norm'][0]), c_pos), 1, 2)
    return _nsa_attention(q, tail, kc, cmp[:, 1], slc_rows, win_kv, pos0, win_pos0, n_sub - 1,
                          paged=None if paged is None else paged[1:])


def _rwkv7_mixer(proj, prev, state0, p, b, T):
    *vectors, gate, bonus = _rwkv_prep(proj, prev, p, T)
    heads = lambda t: t.reshape(b, T, RWKV_HEADS, RWKV_HEAD)
    y, s_t = _rwkv_scan(state0.astype(jnp.float32), *[heads(t) for t in vectors])
    return _rwkv_post(y.reshape(b * T, RWKV_DIM), gate, bonus, p), s_t.astype(state0.dtype)


def _hybrid_layer(x, p, big, layer, pos0, past, ropes):
    b, T, d = x.shape
    xn = _rms_norm(x, p['norm_attn'])
    if past is None:
        proj = _proj(xn, big['w_in'], layer)
        c_prev = None
    else:
        prev = past['shift'][:, None].astype(xn.dtype)
        full = _proj(jnp.concatenate([prev, xn], axis=1), big['w_in'], layer)
        proj = full[:, 1:]
        r0, l0 = COL['rkv'][0], COL['lora'][0]
        c_prev = tuple(full[:, :-1, lo:lo + width].reshape(b * T, width) for lo, width in
                       ((r0, RWKV_DIM), (r0 + RWKV_DIM, RWKV_DIM), (r0 + 2 * RWKV_DIM, RWKV_DIM), (l0, LORA_WIDTH)))

    proj2 = proj.reshape(b * T, PROJ_COLS)
    gains = jnp.stack([p['a_q_norm'], p['a_k_norm'][1], p['a_k_norm'][2], p['b_q_norm'], p['b_k_norm'],
                       jnp.pad(p['idx_k_norm'], (0, LANES - IDX_DIM)),
                       jnp.zeros((LANES,), jnp.float32), jnp.zeros((LANES,), jnp.float32)])
    q_a, q_d, idx_q, cmp_rows, slc_rows, win_rows, dsa_rows, idx_rows = _post_proj(proj2, *ropes, gains)
    per_seq = lambda t: t.reshape((b, -1) + t.shape[1:])
    q_a, q_d, idx_q, idx_rows = per_seq(q_a), per_seq(q_d), per_seq(idx_q), per_seq(idx_rows)
    cmp_rows, slc_rows, win_rows, dsa_rows = per_seq(cmp_rows), per_seq(slc_rows), per_seq(win_rows), per_seq(dsa_rows)
    t0 = COL['tail'][0]
    tail = proj[..., t0:t0 + LANES]
    as_cache = lambda t: t.reshape(b, T, 2, NSA_KV, HEAD_DIM)

    if past is None:
        n_keep = min(WINDOW, T)
        win_state = as_cache(win_rows)[:, -n_keep:]
        rwkv0 = jnp.zeros((b, RWKV_HEADS, RWKV_HEAD, RWKV_HEAD), x.dtype)
        o_a = _nsa_mixer(q_a, tail, cmp_rows, slc_rows, win_rows, pos0, pos0, p)
        o_b = _dsa_attention(q_d, dsa_rows, idx_rows, idx_q, tail, pos0)
    else:
        assert (pos0 + T) // CMP_STRIDE * CMP_STRIDE <= pos0
        win_all = jnp.concatenate([past['nsa_win'], as_cache(win_rows)], axis=1)
        n_keep = past['nsa_win'].shape[1]
        win_state = win_all[:, -n_keep:]
        rwkv0 = past['rwkv']
        page_table = past['page_table']
        o_a = _nsa_mixer(q_a, tail, None, slc_rows, win_all, pos0 - n_keep, pos0, p,
                         paged=(past['cmp_pool'], past['slc_pool'], layer, page_table))
        o_b = _dsa_attention(q_d, dsa_rows, idx_rows, idx_q, tail, pos0,
                             paged=(past['dsa_pool'], past['idx_pool'], layer, page_table))
    o_c, rwkv_new = _rwkv7_mixer(proj2, c_prev, rwkv0, p, b, T)

    mixed = _merge(o_a.reshape(b * T, -1), o_b.reshape(b * T, -1), o_c, proj2, big, layer)
    x = _proj(mixed.reshape(b, T, d), big['w_o'], layer, residual=x)
    x = _ffn(x, p['norm_ffn'], big['w_ffn_in'], big['w_ffn_out'], layer)
    new_state = {'nsa_cmp': as_cache(cmp_rows), 'nsa_slc': as_cache(slc_rows), 'dsa_kv': as_cache(dsa_rows),
                 'dsa_idx': idx_rows, 'nsa_win': win_state, 'rwkv': rwkv_new, 'shift': xn[:, -1]}
    return x, new_state


_PARAM_NAMES = ('norm_attn', 'w_in', 'a_q_norm', 'a_k_norm', 'cmp_w1', 'cmp_pos', 'cmp_w2', 'cmp_b2',
                'b_q_norm', 'b_k_norm', 'idx_k_norm', 'rwkv_mu', 'rwkv_w0', 'rwkv_w2', 'rwkv_a0',
                'rwkv_a2', 'rwkv_g2', 'rwkv_k_k', 'rwkv_k_a', 'rwkv_r_k', 'rwkv_ln_w', 'rwkv_ln_b',
                'p_a', 'p_b', 'p_c', 'w_o', 'norm_ffn', 'w_ffn_in', 'w_ffn_out')


def kernel(x_prompt, x_sample, cache_nsa_cmp, cache_nsa_slc, cache_dsa_kv, cache_dsa_idx, state_nsa_win, state_rwkv, state_shift, page_table, norm_attn, w_in, a_q_norm, a_k_norm, cmp_w1, cmp_pos, cmp_w2, cmp_b2, b_q_norm, b_k_norm, idx_k_norm, rwkv_mu, rwkv_w0, rwkv_w2, rwkv_a0, rwkv_a2, rwkv_g2, rwkv_k_k, rwkv_k_a, rwkv_r_k, rwkv_ln_w, rwkv_ln_b, p_a, p_b, p_c, w_o, norm_ffn, w_ffn_in, w_ffn_out):
    weights = dict(zip(_PARAM_NAMES, (norm_attn, w_in, a_q_norm, a_k_norm, cmp_w1, cmp_pos, cmp_w2, cmp_b2,
                                      b_q_norm, b_k_norm, idx_k_norm, rwkv_mu, rwkv_w0, rwkv_w2, rwkv_a0,
                                      rwkv_a2, rwkv_g2, rwkv_k_k, rwkv_k_a, rwkv_r_k, rwkv_ln_w, rwkv_ln_b,
                                      p_a, p_b, p_c, w_o, norm_ffn, w_ffn_in, w_ffn_out)))
    depth = w_in.shape[0]
    past_len = page_table.shape[1] * cache_nsa_cmp.shape[2]
    yp, ys = x_prompt, x_sample
    st_p, st_s = [], []
    kv_pool = lambda c: c.reshape(c.shape[:2] + (-1, HEAD_DIM))
    big_names = ('w_in', 'p_a', 'p_b', 'p_c', 'w_o', 'w_ffn_in', 'w_ffn_out')
    big = {name: weights[name].astype(jnp.bfloat16) for name in big_names[1:]}
    big['w_in'] = _regroup_w_in(w_in)

    def ropes(pos0, bsz, t_len):
        pos = jnp.tile(pos0 + jnp.arange(t_len, dtype=jnp.int32), bsz)
        return _rope_tables(pos, LANES, HEAD_DIM), _rope_tables(pos, LANES, IDX_DIM)

    ropes_p = ropes(0, *x_prompt.shape[:2])
    ropes_s = ropes(past_len, *x_sample.shape[:2])
    for l in range(depth):
        p = {name: w[l] for name, w in weights.items() if name not in big_names}
        past = {'cmp_pool': kv_pool(cache_nsa_cmp), 'slc_pool': kv_pool(cache_nsa_slc),
                'dsa_pool': kv_pool(cache_dsa_kv), 'idx_pool': jnp.swapaxes(cache_dsa_idx, 2, 3),
                'page_table': page_table,
                'nsa_win': state_nsa_win[l], 'rwkv': state_rwkv[l], 'shift': state_shift[l]}
        yp, sp = _hybrid_layer(yp, p, big, l, 0, None, ropes_p)
        ys, ss = _hybrid_layer(ys, p, big, l, past_len, past, ropes_s)
        st_p.append(sp)
        st_s.append(ss)
    stack = lambda sts, name: jnp.stack([s[name] for s in sts])
    return (yp, ys,
            stack(st_p, 'nsa_cmp'), stack(st_s, 'nsa_cmp'),
            stack(st_p, 'nsa_slc'), stack(st_s, 'nsa_slc'),
            stack(st_p, 'dsa_kv'), stack(st_s, 'dsa_kv'),
            stack(st_p, 'dsa_idx'), stack(st_s, 'dsa_idx'),
            stack(st_p, 'nsa_win'), stack(st_s, 'nsa_win'),
            stack(st_p, 'rwkv'), stack(st_s, 'rwkv'),
            stack(st_p, 'shift'), stack(st_s, 'shift'))
```
